```python
import math
import jax, jax.numpy as jnp
from jax import lax
import numpy as np

D_MODEL = 1024
BATCH = 1
SEQ = 16384
DEPTH = 1
DEC_BATCH = 8
DEC_SEQ = 32
PAST_LEN = 1024

CHUNK = 64
N_HEADS = 8
HEAD_DIM = 64
V_DIM = 2 * HEAD_DIM
ATTN_QK = N_HEADS * 2 * HEAD_DIM
ATTN_V = N_HEADS * V_DIM
N_BUCKETS = 32
MAX_DISTANCE = 128
Q_BLOCK = 128
LRU_WIDTH = D_MODEL
LRU_BLOCKS = 8
LRU_BLOCK_W = LRU_WIDTH // LRU_BLOCKS
CONV_W = 4
LRU_C = 8.0
N_GROUPS = 4
EXPERTS_PER_GROUP = 8
N_EXPERTS = N_GROUPS * EXPERTS_PER_GROUP
TOP_K_INNER = 2
D_EXPERT = 256
PLE_DIM = 256
IN_COLS = 2 * ATTN_QK + ATTN_V + 2 * LRU_WIDTH + 2 * D_MODEL
SPLITS = (ATTN_QK, 2 * ATTN_QK, 2 * ATTN_QK + ATTN_V, 2 * ATTN_QK + ATTN_V + LRU_WIDTH,
          2 * ATTN_QK + ATTN_V + 2 * LRU_WIDTH)
EPS = 1e-6
NEG_INF = -1e30

kernel_name = 'hybrid_diffattn_rglru_hmoe_stream_step'

f32 = jnp.float32


def rmsnorm(x, g):
    xf = x.astype(f32)
    y = xf * lax.rsqrt(jnp.mean(xf * xf, axis=-1, keepdims=True) + EPS)
    return (y * g.astype(f32)).astype(x.dtype)


def rel_bucket(rel):
    half = N_BUCKETS // 2
    max_exact = half // 2
    ret = jnp.where(rel > 0, half, 0)
    n = jnp.abs(rel)
    nf = jnp.maximum(n, 1).astype(f32)
    large = max_exact + (jnp.log(nf / max_exact) / math.log(MAX_DISTANCE / max_exact)
                         * (half - max_exact)).astype(jnp.int32)
    large = jnp.minimum(large, half - 1)
    return ret + jnp.where(n < max_exact, n, large)


def diff_attn_core(q, k, v, qpos, kpos, rel_bias, lam):
    s = jnp.einsum('bqhmd,bkhmd->bmhqk', q.astype(f32), k.astype(f32)) * (HEAD_DIM ** -0.5)
    bias = jnp.transpose(rel_bias.astype(f32)[rel_bucket(kpos[None, :] - qpos[:, None])], (2, 0, 1))
    mask = (kpos[None, :] // CHUNK) <= (qpos[:, None] // CHUNK)
    s = jnp.where(mask, s + bias, NEG_INF)
    p = jax.nn.softmax(s, axis=-1)
    w = p[:, 0] - lam * p[:, 1]
    return jnp.einsum('bhqk,bkhd->bqhd', w, v.astype(f32))


def diff_attn_prompt(q, k, v, pos, rel_bias, lam):
    B, T = q.shape[0], q.shape[1]
    nb = T // Q_BLOCK
    qb = q.reshape(B, nb, Q_BLOCK, N_HEADS, 2, HEAD_DIM).transpose(1, 0, 2, 3, 4, 5)

    def blk(args):
        qi, bi = args
        qpos = bi * Q_BLOCK + jnp.arange(Q_BLOCK)
        return diff_attn_core(qi, k, v, qpos, pos, rel_bias, lam)

    o = lax.map(blk, (qb, jnp.arange(nb)))
    return o.transpose(1, 0, 2, 3, 4).reshape(B, T, N_HEADS, V_DIM)


def causal_conv(xin, prev, w, b):
    full = jnp.concatenate([prev.astype(xin.dtype), xin], axis=1)
    T = xin.shape[1]
    out = full[:, 0:T] * w[0]
    for j in range(1, CONV_W):
        out = out + full[:, j:j + T] * w[j]
    return out + b, full[:, -(CONV_W - 1):]


def block_diag(x, w, b):
    xb = x.reshape(x.shape[:-1] + (LRU_BLOCKS, LRU_BLOCK_W))
    y = jnp.einsum('btni,nij->btnj', xb, w.astype(f32))
    return y.reshape(x.shape) + b.astype(f32)


def rg_lru(xc, pos, h0, w_a, b_a, w_x, b_x, lam_param):
    xf = xc.astype(f32)
    r = jax.nn.sigmoid(block_diag(xf, w_a, b_a))
    i = jax.nn.sigmoid(block_diag(xf, w_x, b_x))
    log_a = -LRU_C * r * jax.nn.softplus(-lam_param.astype(f32))
    a = jnp.exp(log_a)
    mult = jnp.sqrt(-jnp.expm1(2.0 * log_a))
    mult = jnp.where((pos == 0)[None, :, None], 1.0, mult)
    bterm = mult * i * xf

    def step(h, ab):
        at, bt = ab
        h = at * h + bt
        return h, h

    hT, hs = lax.scan(step, h0.astype(f32), (a.transpose(1, 0, 2), bterm.transpose(1, 0, 2)))
    return hs.transpose(1, 0, 2), hT


def hier_moe(h, w_grp, b_grp, w_rt, b_rt, w_e_gate, w_e_up, w_e_down):
    B, T, D = h.shape
    hf = h.reshape(B * T, D)
    N = hf.shape[0]
    gl = (hf @ w_grp).astype(f32) + b_grp.astype(f32)
    gp = jax.nn.softmax(gl, axis=-1)
    g_idx = jnp.argmax(gl, axis=-1)
    g_w = gp[jnp.arange(N), g_idx]
    el = jnp.einsum('nd,gde->nge', hf, w_rt).astype(f32) + b_rt.astype(f32)
    sel = el[jnp.arange(N), g_idx]
    vals, idx = lax.top_k(sel, TOP_K_INNER)
    wts = g_w[:, None] * jax.nn.softmax(vals, axis=-1)
    ids = g_idx[:, None] * EXPERTS_PER_GROUP + idx
    comb = jnp.einsum('nk,nke->ne', wts, jax.nn.one_hot(ids, N_EXPERTS, dtype=f32))
    y = jnp.zeros((N, D), f32)
    for e in range(N_EXPERTS):
        he = jax.nn.silu(hf @ w_e_gate[e]) * (hf @ w_e_up[e])
        y = y + comb[:, e:e + 1] * (he @ w_e_down[e]).astype(f32)
    return y.reshape(B, T, D).astype(h.dtype)


def layer(x, p, pos, k_past, v_past, conv_prev, h_prev, li, rel_bias, lw):
    B, T, _ = x.shape
    h = rmsnorm(x, lw['g_mix'])
    proj = h @ lw['w_in']
    q, k, v, lx, ly, gts = jnp.split(proj, SPLITS, axis=-1)
    q = q.reshape(B, T, N_HEADS, 2, HEAD_DIM)
    k = k.reshape(B, T, N_HEADS, 2, HEAD_DIM)
    v = v.reshape(B, T, N_HEADS, V_DIM)
    lam_init = 0.8 - 0.6 * math.exp(-0.3 * li)
    lam = (jnp.exp(jnp.sum(lw['lam_q1'].astype(f32) * lw['lam_k1'].astype(f32)))
           - jnp.exp(jnp.sum(lw['lam_q2'].astype(f32) * lw['lam_k2'].astype(f32))) + lam_init)
    if k_past is None:
        att = diff_attn_prompt(q, k, v, pos, rel_bias, lam)
    else:
        k_all = jnp.concatenate([k_past.astype(k.dtype), k], axis=1)
        v_all = jnp.concatenate([v_past.astype(v.dtype), v], axis=1)
        att = diff_attn_core(q, k_all, v_all, pos, jnp.arange(k_all.shape[1]), rel_bias, lam)
    att = rmsnorm(att, lw['g_subln']) * (1.0 - lam_init)
    br_a = att.astype(x.dtype).reshape(B, T, ATTN_V) @ lw['w_attn_br']
    xc, conv_new = causal_conv(lx, conv_prev, lw['w_conv'], lw['b_conv'])
    hs, h_new = rg_lru(xc, pos, h_prev, lw['w_rg_a'], lw['b_rg_a'], lw['w_rg_x'], lw['b_rg_x'], lw['lru_lambda'])
    br_b = (hs * jax.nn.gelu(ly.astype(f32))).astype(x.dtype) @ lw['w_lru_br']
    g = jax.nn.sigmoid(gts.astype(f32) + lw['b_merge'].astype(f32))
    m = (g[..., :D_MODEL] * br_a.astype(f32) + g[..., D_MODEL:] * br_b.astype(f32)).astype(x.dtype)
    x = x + m @ lw['w_out']
    x = x + hier_moe(rmsnorm(x, lw['g_ffn']), lw['w_grp'], lw['b_grp'], lw['w_rt'], lw['b_rt'],
                     lw['w_e_gate'], lw['w_e_up'], lw['w_e_down'])
    pg = jax.nn.sigmoid((rmsnorm(x, lw['g_ple']) @ lw['w_ple_gate']).astype(f32) + lw['b_ple_gate'].astype(f32))
    x = x + (pg * (p @ lw['w_ple_in']).astype(f32)).astype(x.dtype)
    return x, k, v, conv_new, h_new


def setup_inputs(seed: int = 0) -> dict:
    key = jax.random.key(seed)
    ks = iter(jax.random.split(key, 64))
    nrm = lambda shape, scale: jax.random.normal(next(ks), shape, f32) * scale
    gain = lambda shape: 1.0 + 0.02 * jax.random.normal(next(ks), shape, f32)
    u = jax.random.uniform(next(ks), (DEPTH, LRU_WIDTH), f32, minval=0.9, maxval=0.999)
    s = u ** (1.0 / LRU_C)
    lru_lambda = jnp.log(s) - jnp.log1p(-s)
    return {
        'x_prompt': nrm((BATCH, SEQ, D_MODEL), 1.0),
        'x_sample': nrm((DEC_BATCH, DEC_SEQ, D_MODEL), 1.0),
        'cache_k': nrm((DEPTH, DEC_BATCH, PAST_LEN, N_HEADS, 2, HEAD_DIM), 1.0),
        'cache_v': nrm((DEPTH, DEC_BATCH, PAST_LEN, N_HEADS, V_DIM), 1.0),
        'state_conv': nrm((DEPTH, DEC_BATCH, CONV_W - 1, LRU_WIDTH), 1.0),
        'state_lru': nrm((DEPTH, DEC_BATCH, LRU_WIDTH), 0.5),
        'p_prompt': nrm((DEPTH, BATCH, SEQ, PLE_DIM), 1.0),
        'p_sample': nrm((DEPTH, DEC_BATCH, DEC_SEQ, PLE_DIM), 1.0),
        'rel_bias': nrm((N_BUCKETS, N_HEADS), 0.5),
        'g_mix': gain((DEPTH, D_MODEL)),
        'w_in': nrm((DEPTH, D_MODEL, IN_COLS), D_MODEL ** -0.5),
        'b_merge': nrm((DEPTH, 2 * D_MODEL), 0.1),
        'lam_q1': nrm((DEPTH, HEAD_DIM), 0.1),
        'lam_k1': nrm((DEPTH, HEAD_DIM), 0.1),
        'lam_q2': nrm((DEPTH, HEAD_DIM), 0.1),
        'lam_k2': nrm((DEPTH, HEAD_DIM), 0.1),
        'g_subln': gain((DEPTH, V_DIM)),
        'w_attn_br': nrm((DEPTH, ATTN_V, D_MODEL), ATTN_V ** -0.5),
        'w_conv': nrm((DEPTH, CONV_W, LRU_WIDTH), CONV_W ** -0.5),
        'b_conv': nrm((DEPTH, LRU_WIDTH), 0.01),
        'w_rg_a': nrm((DEPTH, LRU_BLOCKS, LRU_BLOCK_W, LRU_BLOCK_W), LRU_BLOCK_W ** -0.5),
        'b_rg_a': nrm((DEPTH, LRU_WIDTH), 0.1),
        'w_rg_x': nrm((DEPTH, LRU_BLOCKS, LRU_BLOCK_W, LRU_BLOCK_W), LRU_BLOCK_W ** -0.5),
        'b_rg_x': nrm((DEPTH, LRU_WIDTH), 0.1),
        'lru_lambda': lru_lambda,
        'w_lru_br': nrm((DEPTH, LRU_WIDTH, D_MODEL), LRU_WIDTH ** -0.5),
        'w_out': nrm((DEPTH, D_MODEL, D_MODEL), D_MODEL ** -0.5),
        'g_ffn': gain((DEPTH, D_MODEL)),
        'w_grp': nrm((DEPTH, D_MODEL, N_GROUPS), D_MODEL ** -0.5),
        'b_grp': nrm((DEPTH, N_GROUPS), 0.01),
        'w_rt': nrm((DEPTH, N_GROUPS, D_MODEL, EXPERTS_PER_GROUP), D_MODEL ** -0.5),
        'b_rt': nrm((DEPTH, N_GROUPS, EXPERTS_PER_GROUP), 0.01),
        'w_e_gate': nrm((DEPTH, N_EXPERTS, D_MODEL, D_EXPERT), D_MODEL ** -0.5),
        'w_e_up': nrm((DEPTH, N_EXPERTS, D_MODEL, D_EXPERT), D_MODEL ** -0.5),
        'w_e_down': nrm((DEPTH, N_EXPERTS, D_EXPERT, D_MODEL), D_EXPERT ** -0.5),
        'g_ple': gain((DEPTH, D_MODEL)),
        'w_ple_gate': nrm((DEPTH, D_MODEL, D_MODEL), D_MODEL ** -0.5),
        'b_ple_gate': nrm((DEPTH, D_MODEL), 0.1),
        'w_ple_in': nrm((DEPTH, PLE_DIM, D_MODEL), PLE_DIM ** -0.5),
        'g_final': gain((D_MODEL,)),
    }


def reference(x_prompt, x_sample, cache_k, cache_v, state_conv, state_lru, p_prompt, p_sample,
              rel_bias, g_mix, w_in, b_merge, lam_q1, lam_k1, lam_q2, lam_k2, g_subln, w_attn_br,
              w_conv, b_conv, w_rg_a, b_rg_a, w_rg_x, b_rg_x, lru_lambda, w_lru_br, w_out,
              g_ffn, w_grp, b_grp, w_rt, b_rt, w_e_gate, w_e_up, w_e_down,
              g_ple, w_ple_gate, b_ple_gate, w_ple_in, g_final):
    Bp = x_prompt.shape[0]
    past = cache_k.shape[2]
    pos_p = jnp.arange(x_prompt.shape[1])
    pos_s = past + jnp.arange(x_sample.shape[1])
    xp, xs = x_prompt, x_sample
    kp_l, vp_l, cp_l, hp_l, ks_l, vs_l, cs_l, hs_l = [], [], [], [], [], [], [], []
    for i in range(DEPTH):
        lw = dict(g_mix=g_mix[i], w_in=w_in[i], b_merge=b_merge[i], lam_q1=lam_q1[i], lam_k1=lam_k1[i],
                  lam_q2=lam_q2[i], lam_k2=lam_k2[i], g_subln=g_subln[i], w_attn_br=w_attn_br[i],
                  w_conv=w_conv[i], b_conv=b_conv[i], w_rg_a=w_rg_a[i], b_rg_a=b_rg_a[i],
                  w_rg_x=w_rg_x[i], b_rg_x=b_rg_x[i], lru_lambda=lru_lambda[i], w_lru_br=w_lru_br[i],
                  w_out=w_out[i], g_ffn=g_ffn[i], w_grp=w_grp[i], b_grp=b_grp[i], w_rt=w_rt[i],
                  b_rt=b_rt[i], w_e_gate=w_e_gate[i], w_e_up=w_e_up[i], w_e_down=w_e_down[i],
                  g_ple=g_ple[i], w_ple_gate=w_ple_gate[i], b_ple_gate=b_ple_gate[i], w_ple_in=w_ple_in[i])
        conv0 = jnp.zeros((Bp, CONV_W - 1, LRU_WIDTH), xp.dtype)
        h0 = jnp.zeros((Bp, LRU_WIDTH), f32)
        xp, kp, vp, cp, hp = layer(xp, p_prompt[i], pos_p, None, None, conv0, h0, i, rel_bias, lw)
        xs, ks, vs, cs, hs = layer(xs, p_sample[i], pos_s, cache_k[i], cache_v[i], state_conv[i],
                                   state_lru[i], i, rel_bias, lw)
        kp_l.append(kp); vp_l.append(vp); cp_l.append(cp); hp_l.append(hp)
        ks_l.append(ks); vs_l.append(vs); cs_l.append(cs); hs_l.append(hs)
    y_prompt = rmsnorm(xp, g_final)
    y_sample = rmsnorm(xs, g_final)
    return (y_prompt, y_sample,
            jnp.stack(kp_l), jnp.stack(vp_l), jnp.stack(cp_l), jnp.stack(hp_l),
            jnp.stack(ks_l), jnp.stack(vs_l), jnp.stack(cs_l), jnp.stack(hs_l))
```

```python
import functools
import math

import jax
import jax.numpy as jnp
from jax import lax
from jax.experimental import pallas as pl
from jax.experimental.pallas import tpu as pltpu

f32 = jnp.float32
bf16 = jnp.bfloat16

D_MODEL = 1024
N_HEADS = 8
HEAD_DIM = 64
V_DIM = 128
QK_COLS = N_HEADS * 2 * HEAD_DIM
V_COLS = N_HEADS * V_DIM
LRU_WIDTH = 1024
LRU_BLOCKS = 8
LRU_BLOCK_W = LRU_WIDTH // LRU_BLOCKS
CONV_W = 4
LRU_C = 8.0
CHUNK = 64
N_BUCKETS = 32
MAX_DISTANCE = 128
N_GROUPS = 4
EXPERTS_PER_GROUP = 8
N_EXPERTS = N_GROUPS * EXPERTS_PER_GROUP
D_EXPERT = 256
PLE_DIM = 256
EPS = 1e-6
NEG_INF = -1e30
LANES = 128
SUBLANES = 8
VMEM_LIMIT = 56 * 1024 * 1024

C_Q, C_K, C_V = 0, QK_COLS, 2 * QK_COLS
C_LX = 2 * QK_COLS + V_COLS
C_LY = C_LX + LRU_WIDTH
C_G = C_LY + LRU_WIDTH
IN_COLS = C_G + 2 * D_MODEL


def _cparams(sem):
    return pltpu.CompilerParams(dimension_semantics=sem, vmem_limit_bytes=VMEM_LIMIT)


def _const_spec(shape):
    nd = len(shape)
    return pl.BlockSpec(shape, lambda *_: (0,) * nd, pipeline_mode=pl.Buffered(1))


def _rms(x, g):
    return x * lax.rsqrt(jnp.mean(x * x, axis=-1, keepdims=True) + EPS) * g


def _sigmoid(x):
    return 1.0 / (1.0 + jnp.exp(-x))


def _gelu_tanh(x):
    c = math.sqrt(2.0 / math.pi)
    return 0.5 * x * (1.0 + jnp.tanh(c * (x + 0.044715 * (x * x * x))))


def _in_proj_kernel(x_ref, g_ref, w_ref, bm_ref, q_ref, k_ref, kb_ref, v_ref, vb_ref, lx_ref, gl_ref, gt_ref):
    h = _rms(x_ref[...], g_ref[...]).astype(bf16)

    def proj(lo, width):
        return jnp.dot(h, w_ref[:, lo:lo + width], preferred_element_type=f32)

    q_ref[...] = (proj(C_Q, QK_COLS) * (HEAD_DIM ** -0.5)).astype(bf16)
    k = proj(C_K, QK_COLS)
    k_ref[...] = k
    kb_ref[...] = k.astype(bf16)
    v = proj(C_V, V_COLS)
    v_ref[...] = v
    vb_ref[...] = v.astype(bf16)
    lx_ref[...] = proj(C_LX, LRU_WIDTH)
    gl_ref[...] = _gelu_tanh(proj(C_LY, LRU_WIDTH))
    gt_ref[...] = _sigmoid(proj(C_G, 2 * D_MODEL) + bm_ref[...])


def _in_proj(x, g_mix, w_in_b, b_merge, tm):
    n = x.shape[0]
    row = lambda w: pl.BlockSpec((tm, w), lambda i: (i, 0))
    outs = [
        jax.ShapeDtypeStruct((n, QK_COLS), bf16),
        jax.ShapeDtypeStruct((n, QK_COLS), f32),
        jax.ShapeDtypeStruct((n, QK_COLS), bf16),
        jax.ShapeDtypeStruct((n, V_COLS), f32),
        jax.ShapeDtypeStruct((n, V_COLS), bf16),
        jax.ShapeDtypeStruct((n, LRU_WIDTH), f32),
        jax.ShapeDtypeStruct((n, LRU_WIDTH), f32),
        jax.ShapeDtypeStruct((n, 2 * D_MODEL), f32),
    ]
    return pl.pallas_call(
        _in_proj_kernel,
        grid=(n // tm,),
        in_specs=[row(D_MODEL), _const_spec((1, D_MODEL)), _const_spec((D_MODEL, IN_COLS)),
                  _const_spec((1, 2 * D_MODEL))],
        out_specs=[row(QK_COLS), row(QK_COLS), row(QK_COLS), row(V_COLS), row(V_COLS),
                   row(LRU_WIDTH), row(LRU_WIDTH), row(2 * D_MODEL)],
        out_shape=outs,
        compiler_params=_cparams(("parallel",)),
        name="in_proj",
    )(x, g_mix, w_in_b, b_merge)


def _rel_bucket(rel):
    half = N_BUCKETS // 2
    max_exact = half // 2
    ret = jnp.where(rel > 0, half, 0)
    n = jnp.abs(rel)
    nf = jnp.maximum(n, 1).astype(f32)
    large = max_exact + (jnp.log(nf / max_exact) / math.log(MAX_DISTANCE / max_exact)
                         * (half - max_exact)).astype(jnp.int32)
    large = jnp.minimum(large, half - 1)
    return ret + jnp.where(n < max_exact, n, large)


def _bias_table(rel_bias, qpos, kpos, shift):
    bias = jnp.transpose(rel_bias.astype(f32)[_rel_bucket(kpos[None, :] - qpos[:, None])], (2, 0, 1))
    mask = (kpos[None, :] // CHUNK) <= (qpos[:, None] // CHUNK)
    return jnp.where(mask[None], bias - shift[:, None, None], NEG_INF)


def _lam_value(lq1, lk1, lq2, lk2, lam_init):
    s1 = jnp.sum(lq1[...] * lk1[...], axis=-1, keepdims=True)
    s2 = jnp.sum(lq2[...] * lk2[...], axis=-1, keepdims=True)
    return jnp.exp(s1) - jnp.exp(s2) + lam_init


def _split_maps(q):
    lane = lax.broadcasted_iota(jnp.int32, q.shape, 1)
    zero = jnp.zeros_like(q)
    return jnp.where(lane < HEAD_DIM, q, zero), jnp.where(lane >= HEAD_DIM, q, zero)


def _qk(qz, kt):
    return lax.dot_general(qz, kt, (((1,), (1,)), ((), ())), preferred_element_type=f32)


def _finish_heads(o1, o2, lam, gs, lam_init):
    att = o1 - lam * o2
    return _rms(att, gs) * (1.0 - lam_init)


def _attn_prompt_kernel(q_ref, k_ref, v_ref, nb_ref, lq1, lk1, lq2, lk2, gs_ref, o_ref, m_sc, acc_sc,
                        *, tq, lam_init):
    i = pl.program_id(1)
    qz = _split_maps(q_ref[...])
    ones = jnp.ones((tq, V_DIM), bf16)

    m_sc[...] = jnp.full(m_sc.shape, NEG_INF, f32)
    acc_sc[...] = jnp.zeros(acc_sc.shape, f32)

    def update(mi, s, va):
        m_prev = m_sc[mi]
        m_new = jnp.maximum(m_prev, jnp.max(s, axis=1, keepdims=True))
        alpha = jnp.exp(m_prev - m_new)
        p = jnp.exp(s - jnp.tile(m_new, (1, s.shape[1] // LANES)))
        pv = jnp.dot(p.astype(bf16), va, preferred_element_type=f32)
        acc_sc[mi] = acc_sc[mi] * jnp.tile(alpha, (1, 2)) + pv
        m_sc[mi] = m_new

    def far_step(j, carry):
        start = pl.multiple_of(j * tq, tq)
        kt = k_ref[pl.ds(start, tq), :]
        va = jnp.concatenate([v_ref[pl.ds(start, tq), :], ones], axis=1)
        for mi in range(2):
            update(mi, _qk(qz[mi], kt), va)
        return carry

    lax.fori_loop(0, jnp.maximum(i - 1, 0), far_step, 0)

    for t in range(2):
        start = pl.multiple_of((jnp.maximum(i - 1, 0) + t) * tq, tq)
        kt = k_ref[pl.ds(start, tq), :]
        va = jnp.concatenate([v_ref[pl.ds(start, tq), :], ones], axis=1)
        nb = nb_ref[:, t * tq:(t + 1) * tq]
        for mi in range(2):
            update(mi, _qk(qz[mi], kt) + nb, va)

    a1 = acc_sc[0]
    a2 = acc_sc[1]
    lam = _lam_value(lq1, lk1, lq2, lk2, lam_init)
    o = _finish_heads(a1[:, :V_DIM] / a1[:, V_DIM:], a2[:, :V_DIM] / a2[:, V_DIM:], lam, gs_ref[...], lam_init)
    o_ref[...] = o.astype(o_ref.dtype)


def _attn_prompt(q, kb, vb, rel_bias, lam_vecs, g_subln, lam_init, tq):
    t = q.shape[0]
    nq = t // tq
    qq = jnp.arange(tq)
    kk = jnp.arange(2 * tq)
    far = rel_bias.astype(f32)[_rel_bucket(jnp.asarray(-4 * MAX_DISTANCE))]
    nb = jnp.stack([_bias_table(rel_bias, qq, kk, far), _bias_table(rel_bias, qq, kk - tq, far)])
    vec = _const_spec((1, HEAD_DIM))
    return pl.pallas_call(
        functools.partial(_attn_prompt_kernel, tq=tq, lam_init=lam_init),
        grid=(N_HEADS, nq),
        in_specs=[
            pl.BlockSpec((tq, 2 * HEAD_DIM), lambda h, i: (i, h)),
            pl.BlockSpec((t, 2 * HEAD_DIM), lambda h, i: (0, h)),
            pl.BlockSpec((t, V_DIM), lambda h, i: (0, h)),
            pl.BlockSpec((None, None, tq, 2 * tq), lambda h, i: (jnp.minimum(i, 1), h, 0, 0)),
            vec, vec, vec, vec, _const_spec((1, V_DIM)),
        ],
        out_specs=pl.BlockSpec((tq, V_DIM), lambda h, i: (i, h)),
        out_shape=jax.ShapeDtypeStruct((t, V_COLS), bf16),
        scratch_shapes=[pltpu.VMEM((2, tq, LANES), f32), pltpu.VMEM((2, tq, 2 * V_DIM), f32)],
        compiler_params=_cparams(("parallel", "arbitrary")),
        name="attn_prompt",
    )(q, kb, vb, nb, *lam_vecs, g_subln)


def _attn_sample_kernel(q_ref, kp_ref, vp_ref, kn_ref, vn_ref, nbp_ref, nbn_ref, lq1, lk1, lq2, lk2, gs_ref,
                        o_ref, *, lam_init):
    qz = _split_maps(q_ref[...])
    kp = kp_ref[...].astype(bf16)
    vp = vp_ref[...].astype(bf16)
    kn = kn_ref[...]
    vn = vn_ref[...]
    outs = []
    for mi in range(2):
        sp = _qk(qz[mi], kp) + nbp_ref[...]
        sn = _qk(qz[mi], kn) + nbn_ref[...]
        m = jnp.maximum(jnp.max(sp, axis=1, keepdims=True), jnp.max(sn, axis=1, keepdims=True))
        pp = jnp.exp(sp - m).astype(bf16)
        pn = jnp.exp(sn - m).astype(bf16)
        den = (jnp.sum(pp.astype(f32), axis=1, keepdims=True) + jnp.sum(pn.astype(f32), axis=1, keepdims=True))
        num = jnp.dot(pp, vp, preferred_element_type=f32) + jnp.dot(pn, vn, preferred_element_type=f32)
        outs.append(num / den)
    lam = _lam_value(lq1, lk1, lq2, lk2, lam_init)
    o_ref[...] = _finish_heads(outs[0], outs[1], lam, gs_ref[...], lam_init).astype(o_ref.dtype)


def _attn_sample(q, kb, vb, cache_k, cache_v, rel_bias, lam_vecs, g_subln, lam_init, nb_, ts):
    past = cache_k.shape[1]
    qpos = past + jnp.arange(ts)
    zero = jnp.zeros((N_HEADS,), f32)
    nbp = _bias_table(rel_bias, qpos, jnp.arange(past), zero)
    nbn = _bias_table(rel_bias, qpos, qpos, zero)
    vec = _const_spec((1, HEAD_DIM))
    return pl.pallas_call(
        functools.partial(_attn_sample_kernel, lam_init=lam_init),
        grid=(nb_, N_HEADS),
        in_specs=[
            pl.BlockSpec((ts, 2 * HEAD_DIM), lambda b, h: (b, h)),
            pl.BlockSpec((None, past, 2 * HEAD_DIM), lambda b, h: (b, 0, h)),
            pl.BlockSpec((None, past, V_DIM), lambda b, h: (b, 0, h)),
            pl.BlockSpec((ts, 2 * HEAD_DIM), lambda b, h: (b, h)),
            pl.BlockSpec((ts, V_DIM), lambda b, h: (b, h)),
            pl.BlockSpec((None, ts, past), lambda b, h: (h, 0, 0)),
            pl.BlockSpec((None, ts, ts), lambda b, h: (h, 0, 0)),
            vec, vec, vec, vec, _const_spec((1, V_DIM)),
        ],
        out_specs=pl.BlockSpec((ts, V_DIM), lambda b, h: (b, h)),
        out_shape=jax.ShapeDtypeStruct((nb_ * ts, V_COLS), bf16),
        compiler_params=_cparams(("parallel", "parallel")),
        name="attn_sample",
    )(q, cache_k, cache_v, kb, vb, nbp, nbn, *lam_vecs, g_subln)


def _conv_lru_kernel(lx_ref, gl_ref, c0_ref, h0_ref, wc_ref, bc_ref, wa_ref, ba_ref, wx_ref, bx_ref, lam_ref,
                     hsg_ref, cout_ref, hout_ref, xbuf, a_sc, b_sc, hcar, *, tt, first_pos_is_zero):
    t = pl.program_id(1)

    @pl.when(t == 0)
    def _():
        xbuf[0:SUBLANES] = c0_ref[...]
        hcar[...] = jnp.broadcast_to(h0_ref[...], hcar.shape)

    xbuf[SUBLANES:SUBLANES + tt] = lx_ref[...]
    full = xbuf[...]
    xc = jnp.broadcast_to(bc_ref[...], (tt, LRU_WIDTH))
    for j in range(CONV_W):
        shift = CONV_W - 1 - j
        sh = full if shift == 0 else pltpu.roll(full, shift, axis=0)
        xc = xc + sh[SUBLANES:SUBLANES + tt] * wc_ref[j:j + 1, :]
    xbuf[0:SUBLANES] = full[tt:tt + SUBLANES]
    cout_ref[...] = full[tt:tt + SUBLANES]

    xcb = xc.astype(bf16)

    def block_diag(w_ref, b_ref):
        cols = [jnp.dot(xcb[:, n * LRU_BLOCK_W:(n + 1) * LRU_BLOCK_W], w_ref[n], preferred_element_type=f32)
                for n in range(LRU_BLOCKS)]
        return jnp.concatenate(cols, axis=1) + b_ref[...]

    r = _sigmoid(block_diag(wa_ref, ba_ref))
    ig = _sigmoid(block_diag(wx_ref, bx_ref))
    z = -lam_ref[...]
    softplus = jnp.maximum(z, 0.0) + jnp.log1p(jnp.exp(-jnp.abs(z)))
    log_a = -LRU_C * r * softplus
    a = jnp.exp(log_a)
    th = jnp.tanh(log_a)
    mult = jnp.sqrt(-2.0 * th / (1.0 - th))
    row = lax.broadcasted_iota(jnp.int32, (tt, LRU_WIDTH), 0)
    if first_pos_is_zero:
        mult = jnp.where((row == 0) & (t == 0), 1.0, mult)
    bv = mult * ig * xc

    rowmod = row % SUBLANES
    d = 1
    while d < SUBLANES:
        valid = rowmod >= d
        a_s = pltpu.roll(a, d, axis=0)
        b_s = pltpu.roll(bv, d, axis=0)
        bv = jnp.where(valid, a * b_s + bv, bv)
        a = jnp.where(valid, a * a_s, a)
        d *= 2
    a_sc[...] = a
    b_sc[...] = bv

    def group(g, h):
        sl = pl.ds(pl.multiple_of(g * SUBLANES, SUBLANES), SUBLANES)
        hg = a_sc[sl, :] * h + b_sc[sl, :]
        b_sc[sl, :] = hg
        return jnp.broadcast_to(hg[SUBLANES - 1:SUBLANES, :], hg.shape)

    h = lax.fori_loop(0, tt // SUBLANES, group, hcar[...], unroll=4)
    hcar[...] = h
    hout_ref[...] = h[0:1, :]
    hsg_ref[...] = (b_sc[...] * gl_ref[...]).astype(hsg_ref.dtype)


def _conv_lru(lx, gl, conv0, h0, w_conv, b_conv, wa, ba, wx, bx, lru_lambda, nb_, tlen, tt, first_pos_is_zero):
    nt = tlen // tt
    rows = pl.BlockSpec((tt, LRU_WIDTH), lambda b, t: (b * nt + t, 0))
    vecw = _const_spec((1, LRU_WIDTH))
    wblk = _const_spec((LRU_BLOCKS, LRU_BLOCK_W, LRU_BLOCK_W))
    return pl.pallas_call(
        functools.partial(_conv_lru_kernel, tt=tt, first_pos_is_zero=first_pos_is_zero),
        grid=(nb_, nt),
        in_specs=[rows, rows,
                  pl.BlockSpec((None, SUBLANES, LRU_WIDTH), lambda b, t: (b, 0, 0)),
                  pl.BlockSpec((None, 1, LRU_WIDTH), lambda b, t: (b, 0, 0)),
                  _const_spec((CONV_W, LRU_WIDTH)), vecw, wblk, vecw, wblk, vecw, vecw],
        out_specs=[rows,
                   pl.BlockSpec((None, SUBLANES, LRU_WIDTH), lambda b, t: (b, 0, 0)),
                   pl.BlockSpec((None, 1, LRU_WIDTH), lambda b, t: (b, 0, 0))],
        out_shape=[jax.ShapeDtypeStruct((nb_ * tlen, LRU_WIDTH), bf16),
                   jax.ShapeDtypeStruct((nb_, SUBLANES, LRU_WIDTH), f32),
                   jax.ShapeDtypeStruct((nb_, 1, LRU_WIDTH), f32)],
        scratch_shapes=[pltpu.VMEM((tt + SUBLANES, LRU_WIDTH), f32), pltpu.VMEM((tt, LRU_WIDTH), f32),
                        pltpu.VMEM((tt, LRU_WIDTH), f32), pltpu.VMEM((SUBLANES, LRU_WIDTH), f32)],
        compiler_params=_cparams(("arbitrary", "arbitrary")),
        name="conv_lru",
    )(lx, gl, conv0, h0, w_conv, b_conv, wa, ba, wx, bx, lru_lambda)


def _merge_kernel(x_ref, att_ref, hsg_ref, gt_ref, wa_ref, wb_ref, wo_ref, gf_ref, wr_ref, br_ref,
                  x1_ref, h2_ref, comb_ref):
    bra = jnp.dot(att_ref[...], wa_ref[...], preferred_element_type=f32)
    brb = jnp.dot(hsg_ref[...], wb_ref[...], preferred_element_type=f32)
    gt = gt_ref[...]
    m = (gt[:, :D_MODEL] * bra + gt[:, D_MODEL:] * brb).astype(bf16)
    x1 = x_ref[...] + jnp.dot(m, wo_ref[...], preferred_element_type=f32)
    x1_ref[...] = x1
    h2 = _rms(x1, gf_ref[...])
    h2_ref[...] = h2.astype(bf16)

    logits = jnp.dot(h2, wr_ref[...], preferred_element_type=f32, precision=lax.Precision.HIGHEST) + br_ref[...]
    lane = lax.broadcasted_iota(jnp.int32, logits.shape, 1)
    lanef = lane.astype(f32)
    low = jnp.float32(-3.0e38)
    is_grp = (lane >= N_EXPERTS) & (lane < N_EXPERTS + N_GROUPS)
    gl = jnp.where(is_grp, logits, low)
    gmax = jnp.max(gl, axis=1, keepdims=True)
    gidx = jnp.min(jnp.where(is_grp & (gl == gmax), lanef, 1.0e3), axis=1, keepdims=True) - N_EXPERTS
    gden = jnp.sum(jnp.where(is_grp, jnp.exp(gl - gmax), 0.0), axis=1, keepdims=True)
    g_w = 1.0 / gden
    lo = gidx * EXPERTS_PER_GROUP
    in_sel = (lanef >= lo) & (lanef < lo + EXPERTS_PER_GROUP)
    sel = jnp.where(in_sel, logits, low)
    v1 = jnp.max(sel, axis=1, keepdims=True)
    i1 = jnp.min(jnp.where(in_sel & (sel == v1), lanef, 1.0e3), axis=1, keepdims=True)
    in_sel2 = in_sel & (lanef != i1)
    sel2 = jnp.where(in_sel2, logits, low)
    v2 = jnp.max(sel2, axis=1, keepdims=True)
    i2 = jnp.min(jnp.where(in_sel2 & (sel2 == v2), lanef, 1.0e3), axis=1, keepdims=True)
    e2 = jnp.exp(v2 - v1)
    w1 = g_w / (1.0 + e2)
    w2 = g_w * e2 / (1.0 + e2)
    comb_ref[...] = jnp.where(lanef == i1, w1, 0.0) + jnp.where(lanef == i2, w2, 0.0)


def _merge(x, att, hsg, gt, wa, wb, wo, g_ffn, w_router, b_router, tm):
    n = x.shape[0]
    row = lambda w: pl.BlockSpec((tm, w), lambda i: (i, 0))
    sq = _const_spec((D_MODEL, D_MODEL))
    return pl.pallas_call(
        _merge_kernel,
        grid=(n // tm,),
        in_specs=[row(D_MODEL), row(V_COLS), row(LRU_WIDTH), row(2 * D_MODEL), sq, sq, sq,
                  _const_spec((1, D_MODEL)), _const_spec((D_MODEL, LANES)), _const_spec((1, LANES))],
        out_specs=[row(D_MODEL), row(D_MODEL), row(LANES)],
        out_shape=[jax.ShapeDtypeStruct((n, D_MODEL), f32), jax.ShapeDtypeStruct((n, D_MODEL), bf16),
                   jax.ShapeDtypeStruct((n, LANES), f32)],
        compiler_params=_cparams(("parallel",)),
        name="merge_router",
    )(x, att, hsg, gt, wa, wb, wo, g_ffn, w_router, b_router)


def _moe_kernel(h_ref, comb_ref, wg_ref, wu_ref, wd_ref, y_ref):
    e = pl.program_id(1)

    @pl.when(e == 0)
    def _():
        y_ref[...] = jnp.zeros(y_ref.shape, f32)

    h = h_ref[...]
    gate = jnp.dot(h, wg_ref[...], preferred_element_type=f32)
    up = jnp.dot(h, wu_ref[...], preferred_element_type=f32)
    comb = comb_ref[...]
    lane = lax.broadcasted_iota(jnp.int32, comb.shape, 1)
    c = jnp.sum(jnp.where(lane == e, comb, 0.0), axis=1, keepdims=True)
    he = (gate * _sigmoid(gate) * up * c).astype(bf16)
    y_ref[...] += jnp.dot(he, wd_ref[...], preferred_element_type=f32)


def _moe(h2, comb, wg, wu, wd, tm):
    n = h2.shape[0]
    return pl.pallas_call(
        _moe_kernel,
        grid=(n // tm, N_EXPERTS),
        in_specs=[pl.BlockSpec((tm, D_MODEL), lambda i, e: (i, 0)),
                  pl.BlockSpec((tm, LANES), lambda i, e: (i, 0)),
                  pl.BlockSpec((None, D_MODEL, D_EXPERT), lambda i, e: (e, 0, 0)),
                  pl.BlockSpec((None, D_MODEL, D_EXPERT), lambda i, e: (e, 0, 0)),
                  pl.BlockSpec((None, D_EXPERT, D_MODEL), lambda i, e: (e, 0, 0))],
        out_specs=pl.BlockSpec((tm, D_MODEL), lambda i, e: (i, 0)),
        out_shape=jax.ShapeDtypeStruct((n, D_MODEL), f32),
        compiler_params=_cparams(("parallel", "arbitrary")),
        name="moe",
    )(h2, comb, wg, wu, wd)


def _ple_kernel(x1_ref, y_ref, p_ref, gp_ref, wpg_ref, bpg_ref, wpi_ref, gfin_ref, o_ref):
    x2 = x1_ref[...] + y_ref[...]
    hp = _rms(x2, gp_ref[...]).astype(bf16)
    pg = _sigmoid(jnp.dot(hp, wpg_ref[...], preferred_element_type=f32) + bpg_ref[...])
    pin = jnp.dot(p_ref[...].astype(bf16), wpi_ref[...], preferred_element_type=f32)
    x3 = x2 + pg * pin
    o_ref[...] = _rms(x3, gfin_ref[...])


def _ple(x1, y, p, g_ple, wpg, bpg, wpi, g_final, tm):
    n = x1.shape[0]
    row = lambda w: pl.BlockSpec((tm, w), lambda i: (i, 0))
    vec = _const_spec((1, D_MODEL))
    return pl.pallas_call(
        _ple_kernel,
        grid=(n // tm,),
        in_specs=[row(D_MODEL), row(D_MODEL), row(PLE_DIM), vec, _const_spec((D_MODEL, D_MODEL)), vec,
                  _const_spec((PLE_DIM, D_MODEL)), vec],
        out_specs=row(D_MODEL),
        out_shape=jax.ShapeDtypeStruct((n, D_MODEL), f32),
        compiler_params=_cparams(("parallel",)),
        name="ple_final",
    )(x1, y, p, g_ple, wpg, bpg, wpi, g_final)


def _tile(n, pref):
    return pref if n % pref == 0 else n


def _stream(x, p, conv0, h0, cache, li, rel_bias, w):
    nb_, tlen, _ = x.shape
    n = nb_ * tlen
    xf = x.reshape(n, D_MODEL)
    lam_init = 0.8 - 0.6 * math.exp(-0.3 * li)
    q, k, kb, v, vb, lx, gl, gt = _in_proj(xf, w["g_mix"], w["w_in"], w["b_merge"], _tile(n, 256))
    if cache is None:
        att = _attn_prompt(q, kb, vb, rel_bias, w["lam_vecs"], w["g_subln"], lam_init, _tile(tlen, 256))
    else:
        att = _attn_sample(q, kb, vb, cache[0], cache[1], rel_bias, w["lam_vecs"], w["g_subln"], lam_init,
                           nb_, tlen)
    conv_pad = jnp.pad(conv0, ((0, 0), (SUBLANES - (CONV_W - 1), 0), (0, 0)))
    hsg, cout, hout = _conv_lru(lx, gl, conv_pad, h0[:, None, :], w["w_conv"], w["b_conv"], w["w_rg_a"],
                                w["b_rg_a"], w["w_rg_x"], w["b_rg_x"], w["lru_lambda"], nb_, tlen,
                                _tile(tlen, 256), cache is None)
    tm = _tile(n, 256)
    x1, h2, comb = _merge(xf, att, hsg, gt, w["w_attn_br"], w["w_lru_br"], w["w_out"], w["g_ffn"],
                          w["w_router"], w["b_router"], tm)
    y = _moe(h2, comb, w["w_e_gate"], w["w_e_up"], w["w_e_down"], _tile(n, 1024))
    out = _ple(x1, y, p.reshape(n, PLE_DIM), w["g_ple"], w["w_ple_gate"], w["b_ple_gate"], w["w_ple_in"],
               w["g_final"], tm)
    return out, k, v, cout[:, SUBLANES - (CONV_W - 1):, :], hout[:, 0, :]


def kernel(x_prompt, x_sample, cache_k, cache_v, state_conv, state_lru, p_prompt, p_sample, rel_bias, g_mix, w_in, b_merge, lam_q1, lam_k1, lam_q2, lam_k2, g_subln, w_attn_br, w_conv, b_conv, w_rg_a, b_rg_a, w_rg_x, b_rg_x, lru_lambda, w_lru_br, w_out, g_ffn, w_grp, b_grp, w_rt, b_rt, w_e_gate, w_e_up, w_e_down, g_ple, w_ple_gate, b_ple_gate, w_ple_in, g_final):
    depth = w_in.shape[0]
    assert depth == 1, "the final norm is fused into the single layer's last stage"
    bp, tp, _ = x_prompt.shape
    bs, ts, _ = x_sample.shape
    past = cache_k.shape[2]
    li = 0
    row = lambda a: a[li].reshape(1, -1).astype(f32)
    w_router = jnp.concatenate(
        [jnp.transpose(w_rt[li], (1, 0, 2)).reshape(D_MODEL, N_EXPERTS), w_grp[li]], axis=1).astype(f32)
    w_router = jnp.pad(w_router, ((0, 0), (0, LANES - N_EXPERTS - N_GROUPS)))
    b_router = jnp.pad(jnp.concatenate([b_rt[li].reshape(-1), b_grp[li]]).astype(f32),
                       (0, LANES - N_EXPERTS - N_GROUPS)).reshape(1, LANES)
    w = dict(
        g_mix=row(g_mix), w_in=w_in[li].astype(bf16), b_merge=row(b_merge),
        lam_vecs=(row(lam_q1), row(lam_k1), row(lam_q2), row(lam_k2)), g_subln=row(g_subln),
        w_attn_br=w_attn_br[li].astype(bf16), w_conv=w_conv[li].astype(f32), b_conv=row(b_conv),
        w_rg_a=w_rg_a[li].astype(bf16), b_rg_a=row(b_rg_a), w_rg_x=w_rg_x[li].astype(bf16), b_rg_x=row(b_rg_x),
        lru_lambda=row(lru_lambda), w_lru_br=w_lru_br[li].astype(bf16), w_out=w_out[li].astype(bf16),
        g_ffn=row(g_ffn), w_router=w_router, b_router=b_router,
        w_e_gate=w_e_gate[li].astype(bf16), w_e_up=w_e_up[li].astype(bf16), w_e_down=w_e_down[li].astype(bf16),
        g_ple=row(g_ple), w_ple_gate=w_ple_gate[li].astype(bf16), b_ple_gate=row(b_ple_gate),
        w_ple_in=w_ple_in[li].astype(bf16), g_final=g_final.reshape(1, -1).astype(f32),
    )
    conv0 = jnp.zeros((bp, CONV_W - 1, LRU_WIDTH), f32)
    h0 = jnp.zeros((bp, LRU_WIDTH), f32)
    yp, kp, vp, cp, hp = _stream(x_prompt, p_prompt[li], conv0, h0, None, li, rel_bias, w)
    cache = (cache_k[li].reshape(bs, past, QK_COLS), cache_v[li].reshape(bs, past, V_COLS))
    ys, ks, vs, cs, hs = _stream(x_sample, p_sample[li], state_conv[li], state_lru[li], cache, li, rel_bias, w)
    return (yp.reshape(bp, tp, D_MODEL), ys.reshape(bs, ts, D_MODEL),
            kp.reshape(1, bp, tp, N_HEADS, 2, HEAD_DIM), vp.reshape(1, bp, tp, N_HEADS, V_DIM),
            cp[None], hp[None],
            ks.reshape(1, bs, ts, N_HEADS, 2, HEAD_DIM), vs.reshape(1, bs, ts, N_HEADS, V_DIM),
            cs[None], hs[None])
```

```python
import functools
import math

import jax
import jax.numpy as jnp
from jax import lax
from jax.experimental import pallas as pl
from jax.experimental.pallas import tpu as pltpu

f32 = jnp.float32
bf16 = jnp.bfloat16

D_MODEL = 1024
N_HEADS = 8
HEAD_DIM = 64
V_DIM = 128
QK_COLS = N_HEADS * 2 * HEAD_DIM
V_COLS = N_HEADS * V_DIM
LRU_WIDTH = 1024
LRU_BLOCKS = 8
LRU_BLOCK_W = LRU_WIDTH // LRU_BLOCKS
CONV_W = 4
LRU_C = 8.0
CHUNK = 64
N_BUCKETS = 32
MAX_DISTANCE = 128
N_GROUPS = 4
EXPERTS_PER_GROUP = 8
N_EXPERTS = N_GROUPS * EXPERTS_PER_GROUP
D_EXPERT = 256
PLE_DIM = 256
EPS = 1e-6
NEG_INF = -1e30
LANES = 128
SUBLANES = 8
VMEM_LIMIT = 56 * 1024 * 1024

C_Q, C_K, C_V = 0, QK_COLS, 2 * QK_COLS
C_LX = 2 * QK_COLS + V_COLS
C_LY = C_LX + LRU_WIDTH
C_G = C_LY + LRU_WIDTH
IN_COLS = C_G + 2 * D_MODEL


def _cparams(sem):
    return pltpu.CompilerParams(dimension_semantics=sem, vmem_limit_bytes=VMEM_LIMIT)


def _const_spec(shape):
    nd = len(shape)
    return pl.BlockSpec(shape, lambda *_: (0,) * nd, pipeline_mode=pl.Buffered(1))


def _rms(x, g):
    return x * lax.rsqrt(jnp.mean(x * x, axis=-1, keepdims=True) + EPS) * g


def _sigmoid(x):
    return 1.0 / (1.0 + jnp.exp(-x))


def _gelu_tanh(x):
    c = math.sqrt(2.0 / math.pi)
    return 0.5 * x * (1.0 + jnp.tanh(c * (x + 0.044715 * (x * x * x))))


def _in_proj_kernel(x_ref, g_ref, w_ref, bm_ref, q_ref, k_ref, kb_ref, v_ref, vb_ref, lx_ref, gl_ref, gt_ref):
    h = _rms(x_ref[...], g_ref[...]).astype(bf16)

    def proj(lo, width):
        return jnp.dot(h, w_ref[:, lo:lo + width], preferred_element_type=f32)

    q_ref[...] = (proj(C_Q, QK_COLS) * (HEAD_DIM ** -0.5)).astype(bf16)
    k = proj(C_K, QK_COLS)
    k_ref[...] = k
    kb_ref[...] = k.astype(bf16)
    v = proj(C_V, V_COLS)
    v_ref[...] = v
    vb_ref[...] = v.astype(bf16)
    lx_ref[...] = proj(C_LX, LRU_WIDTH)
    gl_ref[...] = _gelu_tanh(proj(C_LY, LRU_WIDTH))
    gt_ref[...] = _sigmoid(proj(C_G, 2 * D_MODEL) + bm_ref[...])


def _in_proj(x, g_mix, w_in_b, b_merge, tm):
    n = x.shape[0]
    row = lambda w: pl.BlockSpec((tm, w), lambda i: (i, 0))
    outs = [
        jax.ShapeDtypeStruct((n, QK_COLS), bf16),
        jax.ShapeDtypeStruct((n, QK_COLS), f32),
        jax.ShapeDtypeStruct((n, QK_COLS), bf16),
        jax.ShapeDtypeStruct((n, V_COLS), f32),
        jax.ShapeDtypeStruct((n, V_COLS), bf16),
        jax.ShapeDtypeStruct((n, LRU_WIDTH), f32),
        jax.ShapeDtypeStruct((n, LRU_WIDTH), f32),
        jax.ShapeDtypeStruct((n, 2 * D_MODEL), f32),
    ]
    return pl.pallas_call(
        _in_proj_kernel,
        grid=(n // tm,),
        in_specs=[row(D_MODEL), _const_spec((1, D_MODEL)), _const_spec((D_MODEL, IN_COLS)),
                  _const_spec((1, 2 * D_MODEL))],
        out_specs=[row(QK_COLS), row(QK_COLS), row(QK_COLS), row(V_COLS), row(V_COLS),
                   row(LRU_WIDTH), row(LRU_WIDTH), row(2 * D_MODEL)],
        out_shape=outs,
        compiler_params=_cparams(("parallel",)),
        name="in_proj",
    )(x, g_mix, w_in_b, b_merge)


def _rel_bucket(rel):
    half = N_BUCKETS // 2
    max_exact = half // 2
    ret = jnp.where(rel > 0, half, 0)
    n = jnp.abs(rel)
    nf = jnp.maximum(n, 1).astype(f32)
    large = max_exact + (jnp.log(nf / max_exact) / math.log(MAX_DISTANCE / max_exact)
                         * (half - max_exact)).astype(jnp.int32)
    large = jnp.minimum(large, half - 1)
    return ret + jnp.where(n < max_exact, n, large)


def _toeplitz(t, nq, nk):
    nh, length = t.shape
    a = jnp.broadcast_to(t[:, None, :], (nh, nq, length))
    a = jnp.pad(a, ((0, 0), (0, 0), (0, 1))).reshape(nh, nq * (length + 1))
    a = a[:, :nq * length].reshape(nh, nq, length)
    return a[:, :, nq - 1:nq - 1 + nk]


def _bias_table(rel_bias, qpos0, nq, kpos0, nk, shift):
    rel = jnp.arange(-(nq - 1), nk) + (kpos0 - qpos0)
    t = rel_bias.astype(f32)[_rel_bucket(rel)].T - shift[:, None]
    qpos = qpos0 + jnp.arange(nq)
    kpos = kpos0 + jnp.arange(nk)
    mask = (kpos[None, :] // CHUNK) <= (qpos[:, None] // CHUNK)
    return jnp.where(mask[None], _toeplitz(t, nq, nk), NEG_INF)


def _lam_value(lq1, lk1, lq2, lk2, lam_init):
    s1 = jnp.sum(lq1[...] * lk1[...], axis=-1, keepdims=True)
    s2 = jnp.sum(lq2[...] * lk2[...], axis=-1, keepdims=True)
    return jnp.exp(s1) - jnp.exp(s2) + lam_init


def _split_maps(q):
    lane = lax.broadcasted_iota(jnp.int32, q.shape, 1)
    zero = jnp.zeros_like(q)
    return jnp.where(lane < HEAD_DIM, q, zero), jnp.where(lane >= HEAD_DIM, q, zero)


def _qk(qz, kt):
    return lax.dot_general(qz, kt, (((1,), (1,)), ((), ())), preferred_element_type=f32)


def _finish_heads(o1, o2, lam, gs, lam_init):
    att = o1 - lam * o2
    return _rms(att, gs) * (1.0 - lam_init)


ONES_ROWS = 16


def _attn_prompt_kernel(q_ref, k_ref, vt_ref, nb_ref, lq1, lk1, lq2, lk2, gs_ref, o_ref,
                        s_sc, p_sc, al_sc, m_sc, acc_sc, *, tq, lam_init):
    i = pl.program_id(1)
    qt = q_ref[...].astype(f32).T
    row = lax.broadcasted_iota(jnp.int32, qt.shape, 0)
    qz = (jnp.where(row < HEAD_DIM, qt, 0.0).astype(bf16), jnp.where(row >= HEAD_DIM, qt, 0.0).astype(bf16))
    ones = jnp.ones((ONES_ROWS, tq), bf16)

    m_sc[...] = jnp.full(m_sc.shape, NEG_INF, f32)
    acc_sc[...] = jnp.zeros(acc_sc.shape, f32)

    def qk(mi, tile, bias):
        kt = k_ref[pl.ds(pl.multiple_of(tile * tq, tq), tq), :]
        s = jnp.dot(kt, qz[mi], preferred_element_type=f32)
        s_sc[mi] = s if bias is None else s + bias

    def softmax(mi):
        s = s_sc[mi]
        m_prev = m_sc[mi]
        m_new = jnp.maximum(m_prev, jnp.max(s, axis=0, keepdims=True))
        al_sc[mi] = jnp.exp(m_prev - m_new)
        p_sc[mi] = jnp.exp(s - m_new).astype(bf16)
        m_sc[mi] = m_new

    def pv(mi, tile):
        va = jnp.concatenate([vt_ref[tile], ones], axis=0)
        acc_sc[mi] = acc_sc[mi] * al_sc[mi] + jnp.dot(va, p_sc[mi], preferred_element_type=f32)

    def step(tile, bias, nxt, nxt_bias, prev):
        if prev is not None:
            pv(1, prev)
        qk(1, tile, bias)
        softmax(0)
        pv(0, tile)
        qk(0, nxt, nxt_bias)
        softmax(1)

    near0 = jnp.maximum(i - 1, 0)
    n_far = near0
    qk(0, near0, nb_ref[0])
    step(near0, nb_ref[0], near0 + 1, nb_ref[1], None)
    step(near0 + 1, nb_ref[1], 0, None, near0)

    def far_step(f, carry):
        step(f, None, jnp.minimum(f + 1, n_far - 1), None, jnp.where(f == 0, near0 + 1, f - 1))
        return carry

    lax.fori_loop(0, n_far, far_step, 0)
    pv(1, jnp.where(n_far > 0, n_far - 1, near0 + 1))

    a1 = acc_sc[0]
    a2 = acc_sc[1]
    o1 = a1[:V_DIM] / a1[V_DIM:V_DIM + 1]
    o2 = a2[:V_DIM] / a2[V_DIM:V_DIM + 1]
    lam = _lam_value(lq1, lk1, lq2, lk2, lam_init)
    att = (o1 - lam * o2).T
    o_ref[...] = (_rms(att, gs_ref[...]) * (1.0 - lam_init)).astype(o_ref.dtype)


def _attn_prompt(q, kb, vb, rel_bias, lam_vecs, g_subln, lam_init, tq):
    t = q.shape[0]
    nq = t // tq
    far = rel_bias.astype(f32)[_rel_bucket(jnp.asarray(-4 * MAX_DISTANCE))]
    nb = jnp.stack([_bias_table(rel_bias, 0, tq, 0, 2 * tq, far), _bias_table(rel_bias, tq, tq, 0, 2 * tq, far)])
    nbt = nb.reshape(2, N_HEADS, tq, 2, tq).transpose(0, 1, 3, 4, 2)
    vt = vb.reshape(nq, tq, N_HEADS, V_DIM).transpose(2, 0, 3, 1)
    vec = _const_spec((1, HEAD_DIM))
    return pl.pallas_call(
        functools.partial(_attn_prompt_kernel, tq=tq, lam_init=lam_init),
        grid=(N_HEADS, nq),
        in_specs=[
            pl.BlockSpec((tq, 2 * HEAD_DIM), lambda h, i: (i, h)),
            pl.BlockSpec((t, 2 * HEAD_DIM), lambda h, i: (0, h)),
            pl.BlockSpec((None, nq, V_DIM, tq), lambda h, i: (h, 0, 0, 0)),
            pl.BlockSpec((None, None, 2, tq, tq), lambda h, i: (jnp.minimum(i, 1), h, 0, 0, 0)),
            vec, vec, vec, vec, _const_spec((1, V_DIM)),
        ],
        out_specs=pl.BlockSpec((tq, V_DIM), lambda h, i: (i, h)),
        out_shape=jax.ShapeDtypeStruct((t, V_COLS), bf16),
        scratch_shapes=[pltpu.VMEM((2, tq, tq), f32), pltpu.VMEM((2, tq, tq), bf16),
                        pltpu.VMEM((2, 1, tq), f32), pltpu.VMEM((2, 1, tq), f32),
                        pltpu.VMEM((2, V_DIM + ONES_ROWS, tq), f32)],
        compiler_params=_cparams(("parallel", "arbitrary")),
        name="attn_prompt",
    )(q, kb, vt, nbt, *lam_vecs, g_subln)


def _attn_sample_kernel(q_ref, kp_ref, vp_ref, kn_ref, vn_ref, nbp_ref, nbn_ref, lq1, lk1, lq2, lk2, gs_ref,
                        o_ref, *, lam_init):
    qz = _split_maps(q_ref[...])
    kp = kp_ref[...].astype(bf16)
    vp = vp_ref[...].astype(bf16)
    kn = kn_ref[...]
    vn = vn_ref[...]
    outs = []
    for mi in range(2):
        sp = _qk(qz[mi], kp) + nbp_ref[...]
        sn = _qk(qz[mi], kn) + nbn_ref[...]
        m = jnp.maximum(jnp.max(sp, axis=1, keepdims=True), jnp.max(sn, axis=1, keepdims=True))
        pp = jnp.exp(sp - m).astype(bf16)
        pn = jnp.exp(sn - m).astype(bf16)
        den = (jnp.sum(pp.astype(f32), axis=1, keepdims=True) + jnp.sum(pn.astype(f32), axis=1, keepdims=True))
        num = jnp.dot(pp, vp, preferred_element_type=f32) + jnp.dot(pn, vn, preferred_element_type=f32)
        outs.append(num / den)
    lam = _lam_value(lq1, lk1, lq2, lk2, lam_init)
    o_ref[...] = _finish_heads(outs[0], outs[1], lam, gs_ref[...], lam_init).astype(o_ref.dtype)


def _attn_sample(q, kb, vb, cache_k, cache_v, rel_bias, lam_vecs, g_subln, lam_init, nb_, ts):
    past = cache_k.shape[1]
    zero = jnp.zeros((N_HEADS,), f32)
    nbp = _bias_table(rel_bias, past, ts, 0, past, zero)
    nbn = _bias_table(rel_bias, past, ts, past, ts, zero)
    vec = _const_spec((1, HEAD_DIM))
    return pl.pallas_call(
        functools.partial(_attn_sample_kernel, lam_init=lam_init),
        grid=(nb_, N_HEADS),
        in_specs=[
            pl.BlockSpec((ts, 2 * HEAD_DIM), lambda b, h: (b, h)),
            pl.BlockSpec((None, past, 2 * HEAD_DIM), lambda b, h: (b, 0, h)),
            pl.BlockSpec((None, past, V_DIM), lambda b, h: (b, 0, h)),
            pl.BlockSpec((ts, 2 * HEAD_DIM), lambda b, h: (b, h)),
            pl.BlockSpec((ts, V_DIM), lambda b, h: (b, h)),
            pl.BlockSpec((None, ts, past), lambda b, h: (h, 0, 0)),
            pl.BlockSpec((None, ts, ts), lambda b, h: (h, 0, 0)),
            vec, vec, vec, vec, _const_spec((1, V_DIM)),
        ],
        out_specs=pl.BlockSpec((ts, V_DIM), lambda b, h: (b, h)),
        out_shape=jax.ShapeDtypeStruct((nb_ * ts, V_COLS), bf16),
        compiler_params=_cparams(("parallel", "parallel")),
        name="attn_sample",
    )(q, cache_k, cache_v, kb, vb, nbp, nbn, *lam_vecs, g_subln)


def _conv_lru_kernel(lx_ref, gl_ref, c0_ref, h0_ref, wc_ref, bc_ref, wa_ref, ba_ref, wx_ref, bx_ref, lam_ref,
                     hsg_ref, cout_ref, hout_ref, xbuf, a_sc, b_sc, hcar, *, tt, first_pos_is_zero):
    t = pl.program_id(1)

    @pl.when(t == 0)
    def _():
        xbuf[0:SUBLANES] = c0_ref[...]
        hcar[...] = jnp.broadcast_to(h0_ref[...], hcar.shape)

    xbuf[SUBLANES:SUBLANES + tt] = lx_ref[...]
    full = xbuf[...]
    xc = jnp.broadcast_to(bc_ref[...], (tt, LRU_WIDTH))
    for j in range(CONV_W):
        shift = CONV_W - 1 - j
        sh = full if shift == 0 else pltpu.roll(full, shift, axis=0)
        xc = xc + sh[SUBLANES:SUBLANES + tt] * wc_ref[j:j + 1, :]
    xbuf[0:SUBLANES] = full[tt:tt + SUBLANES]
    cout_ref[...] = full[tt:tt + SUBLANES]

    xcb = xc.astype(bf16)

    def block_diag(w_ref, b_ref):
        cols = [jnp.dot(xcb[:, n * LRU_BLOCK_W:(n + 1) * LRU_BLOCK_W], w_ref[n], preferred_element_type=f32)
                for n in range(LRU_BLOCKS)]
        return jnp.concatenate(cols, axis=1) + b_ref[...]

    r = _sigmoid(block_diag(wa_ref, ba_ref))
    ig = _sigmoid(block_diag(wx_ref, bx_ref))
    z = -lam_ref[...]
    softplus = jnp.maximum(z, 0.0) + jnp.log1p(jnp.exp(-jnp.abs(z)))
    log_a = -LRU_C * r * softplus
    a = jnp.exp(log_a)
    th = jnp.tanh(log_a)
    mult = jnp.sqrt(-2.0 * th / (1.0 - th))
    row = lax.broadcasted_iota(jnp.int32, (tt, LRU_WIDTH), 0)
    if first_pos_is_zero:
        mult = jnp.where((row == 0) & (t == 0), 1.0, mult)
    bv = mult * ig * xc

    rowmod = row % SUBLANES
    d = 1
    while d < SUBLANES:
        valid = rowmod >= d
        a_s = pltpu.roll(a, d, axis=0)
        b_s = pltpu.roll(bv, d, axis=0)
        bv = jnp.where(valid, a * b_s + bv, bv)
        a = jnp.where(valid, a * a_s, a)
        d *= 2
    a_sc[...] = a
    b_sc[...] = bv

    def group(g, h):
        sl = pl.ds(pl.multiple_of(g * SUBLANES, SUBLANES), SUBLANES)
        hg = a_sc[sl, :] * h + b_sc[sl, :]
        b_sc[sl, :] = hg
        return jnp.broadcast_to(hg[SUBLANES - 1:SUBLANES, :], hg.shape)

    h = lax.fori_loop(0, tt // SUBLANES, group, hcar[...], unroll=4)
    hcar[...] = h
    hout_ref[...] = h[0:1, :]
    hsg_ref[...] = (b_sc[...] * gl_ref[...]).astype(hsg_ref.dtype)


def _conv_lru(lx, gl, conv0, h0, w_conv, b_conv, wa, ba, wx, bx, lru_lambda, nb_, tlen, tt, first_pos_is_zero):
    nt = tlen // tt
    rows = pl.BlockSpec((tt, LRU_WIDTH), lambda b, t: (b * nt + t, 0))
    vecw = _const_spec((1, LRU_WIDTH))
    wblk = _const_spec((LRU_BLOCKS, LRU_BLOCK_W, LRU_BLOCK_W))
    return pl.pallas_call(
        functools.partial(_conv_lru_kernel, tt=tt, first_pos_is_zero=first_pos_is_zero),
        grid=(nb_, nt),
        in_specs=[rows, rows,
                  pl.BlockSpec((None, SUBLANES, LRU_WIDTH), lambda b, t: (b, 0, 0)),
                  pl.BlockSpec((None, 1, LRU_WIDTH), lambda b, t: (b, 0, 0)),
                  _const_spec((CONV_W, LRU_WIDTH)), vecw, wblk, vecw, wblk, vecw, vecw],
        out_specs=[rows,
                   pl.BlockSpec((None, SUBLANES, LRU_WIDTH), lambda b, t: (b, 0, 0)),
                   pl.BlockSpec((None, 1, LRU_WIDTH), lambda b, t: (b, 0, 0))],
        out_shape=[jax.ShapeDtypeStruct((nb_ * tlen, LRU_WIDTH), bf16),
                   jax.ShapeDtypeStruct((nb_, SUBLANES, LRU_WIDTH), f32),
                   jax.ShapeDtypeStruct((nb_, 1, LRU_WIDTH), f32)],
        scratch_shapes=[pltpu.VMEM((tt + SUBLANES, LRU_WIDTH), f32), pltpu.VMEM((tt, LRU_WIDTH), f32),
                        pltpu.VMEM((tt, LRU_WIDTH), f32), pltpu.VMEM((SUBLANES, LRU_WIDTH), f32)],
        compiler_params=_cparams(("arbitrary", "arbitrary")),
        name="conv_lru",
    )(lx, gl, conv0, h0, w_conv, b_conv, wa, ba, wx, bx, lru_lambda)


def _merge_kernel(x_ref, att_ref, hsg_ref, gt_ref, wa_ref, wb_ref, wo_ref, gf_ref, wr_ref, br_ref,
                  x1_ref, h2_ref, comb_ref):
    bra = jnp.dot(att_ref[...], wa_ref[...], preferred_element_type=f32)
    brb = jnp.dot(hsg_ref[...], wb_ref[...], preferred_element_type=f32)
    gt = gt_ref[...]
    m = (gt[:, :D_MODEL] * bra + gt[:, D_MODEL:] * brb).astype(bf16)
    x1 = x_ref[...] + jnp.dot(m, wo_ref[...], preferred_element_type=f32)
    x1_ref[...] = x1
    h2 = _rms(x1, gf_ref[...])
    h2_ref[...] = h2.astype(bf16)

    logits = jnp.dot(h2, wr_ref[...], preferred_element_type=f32, precision=lax.Precision.HIGHEST) + br_ref[...]
    lane = lax.broadcasted_iota(jnp.int32, logits.shape, 1)
    lanef = lane.astype(f32)
    low = jnp.float32(-3.0e38)
    is_grp = (lane >= N_EXPERTS) & (lane < N_EXPERTS + N_GROUPS)
    gl = jnp.where(is_grp, logits, low)
    gmax = jnp.max(gl, axis=1, keepdims=True)
    gidx = jnp.min(jnp.where(is_grp & (gl == gmax), lanef, 1.0e3), axis=1, keepdims=True) - N_EXPERTS
    gden = jnp.sum(jnp.where(is_grp, jnp.exp(gl - gmax), 0.0), axis=1, keepdims=True)
    g_w = 1.0 / gden
    lo = gidx * EXPERTS_PER_GROUP
    in_sel = (lanef >= lo) & (lanef < lo + EXPERTS_PER_GROUP)
    sel = jnp.where(in_sel, logits, low)
    v1 = jnp.max(sel, axis=1, keepdims=True)
    i1 = jnp.min(jnp.where(in_sel & (sel == v1), lanef, 1.0e3), axis=1, keepdims=True)
    in_sel2 = in_sel & (lanef != i1)
    sel2 = jnp.where(in_sel2, logits, low)
    v2 = jnp.max(sel2, axis=1, keepdims=True)
    i2 = jnp.min(jnp.where(in_sel2 & (sel2 == v2), lanef, 1.0e3), axis=1, keepdims=True)
    e2 = jnp.exp(v2 - v1)
    w1 = g_w / (1.0 + e2)
    w2 = g_w * e2 / (1.0 + e2)
    comb_ref[...] = jnp.where(lanef == i1, w1, 0.0) + jnp.where(lanef == i2, w2, 0.0)


def _merge(x, att, hsg, gt, wa, wb, wo, g_ffn, w_router, b_router, tm):
    n = x.shape[0]
    row = lambda w: pl.BlockSpec((tm, w), lambda i: (i, 0))
    sq = _const_spec((D_MODEL, D_MODEL))
    return pl.pallas_call(
        _merge_kernel,
        grid=(n // tm,),
        in_specs=[row(D_MODEL), row(V_COLS), row(LRU_WIDTH), row(2 * D_MODEL), sq, sq, sq,
                  _const_spec((1, D_MODEL)), _const_spec((D_MODEL, LANES)), _const_spec((1, LANES))],
        out_specs=[row(D_MODEL), row(D_MODEL), row(LANES)],
        out_shape=[jax.ShapeDtypeStruct((n, D_MODEL), f32), jax.ShapeDtypeStruct((n, D_MODEL), bf16),
                   jax.ShapeDtypeStruct((n, LANES), f32)],
        compiler_params=_cparams(("parallel",)),
        name="merge_router",
    )(x, att, hsg, gt, wa, wb, wo, g_ffn, w_router, b_router)


def _moe_kernel(h_ref, comb_ref, wg_ref, wu_ref, wd_ref, y_ref):
    e = pl.program_id(1)

    @pl.when(e == 0)
    def _():
        y_ref[...] = jnp.zeros(y_ref.shape, f32)

    h = h_ref[...]
    gate = jnp.dot(h, wg_ref[...], preferred_element_type=f32)
    up = jnp.dot(h, wu_ref[...], preferred_element_type=f32)
    comb = comb_ref[...]
    lane = lax.broadcasted_iota(jnp.int32, comb.shape, 1)
    c = jnp.sum(jnp.where(lane == e, comb, 0.0), axis=1, keepdims=True)
    he = (gate * _sigmoid(gate) * up * c).astype(bf16)
    y_ref[...] += jnp.dot(he, wd_ref[...], preferred_element_type=f32)


def _moe(h2, comb, wg, wu, wd, tm):
    n = h2.shape[0]
    return pl.pallas_call(
        _moe_kernel,
        grid=(n // tm, N_EXPERTS),
        in_specs=[pl.BlockSpec((tm, D_MODEL), lambda i, e: (i, 0)),
                  pl.BlockSpec((tm, LANES), lambda i, e: (i, 0)),
                  pl.BlockSpec((None, D_MODEL, D_EXPERT), lambda i, e: (e, 0, 0)),
                  pl.BlockSpec((None, D_MODEL, D_EXPERT), lambda i, e: (e, 0, 0)),
                  pl.BlockSpec((None, D_EXPERT, D_MODEL), lambda i, e: (e, 0, 0))],
        out_specs=pl.BlockSpec((tm, D_MODEL), lambda i, e: (i, 0)),
        out_shape=jax.ShapeDtypeStruct((n, D_MODEL), f32),
        compiler_params=_cparams(("parallel", "arbitrary")),
        name="moe",
    )(h2, comb, wg, wu, wd)


def _ple_kernel(x1_ref, y_ref, p_ref, gp_ref, wpg_ref, bpg_ref, wpi_ref, gfin_ref, o_ref):
    x2 = x1_ref[...] + y_ref[...]
    hp = _rms(x2, gp_ref[...]).astype(bf16)
    pg = _sigmoid(jnp.dot(hp, wpg_ref[...], preferred_element_type=f32) + bpg_ref[...])
    pin = jnp.dot(p_ref[...].astype(bf16), wpi_ref[...], preferred_element_type=f32)
    x3 = x2 + pg * pin
    o_ref[...] = _rms(x3, gfin_ref[...])


def _ple(x1, y, p, g_ple, wpg, bpg, wpi, g_final, tm):
    n = x1.shape[0]
    row = lambda w: pl.BlockSpec((tm, w), lambda i: (i, 0))
    vec = _const_spec((1, D_MODEL))
    return pl.pallas_call(
        _ple_kernel,
        grid=(n // tm,),
        in_specs=[row(D_MODEL), row(D_MODEL), row(PLE_DIM), vec, _const_spec((D_MODEL, D_MODEL)), vec,
                  _const_spec((PLE_DIM, D_MODEL)), vec],
        out_specs=row(D_MODEL),
        out_shape=jax.ShapeDtypeStruct((n, D_MODEL), f32),
        compiler_params=_cparams(("parallel",)),
        name="ple_final",
    )(x1, y, p, g_ple, wpg, bpg, wpi, g_final)


def _tile(n, pref):
    return pref if n % pref == 0 else n


def _stream(x, p, conv0, h0, cache, li, rel_bias, w):
    nb_, tlen, _ = x.shape
    n = nb_ * tlen
    xf = x.reshape(n, D_MODEL)
    lam_init = 0.8 - 0.6 * math.exp(-0.3 * li)
    q, k, kb, v, vb, lx, gl, gt = _in_proj(xf, w["g_mix"], w["w_in"], w["b_merge"], _tile(n, 256))
    if cache is None:
        att = _attn_prompt(q, kb, vb, rel_bias, w["lam_vecs"], w["g_subln"], lam_init, _tile(tlen, 512))
    else:
        att = _attn_sample(q, kb, vb, cache[0], cache[1], rel_bias, w["lam_vecs"], w["g_subln"], lam_init,
                           nb_, tlen)
    conv_pad = jnp.pad(conv0, ((0, 0), (SUBLANES - (CONV_W - 1), 0), (0, 0)))
    hsg, cout, hout = _conv_lru(lx, gl, conv_pad, h0[:, None, :], w["w_conv"], w["b_conv"], w["w_rg_a"],
                                w["b_rg_a"], w["w_rg_x"], w["b_rg_x"], w["lru_lambda"], nb_, tlen,
                                _tile(tlen, 256), cache is None)
    tm = _tile(n, 256)
    x1, h2, comb = _merge(xf, att, hsg, gt, w["w_attn_br"], w["w_lru_br"], w["w_out"], w["g_ffn"],
                          w["w_router"], w["b_router"], tm)
    y = _moe(h2, comb, w["w_e_gate"], w["w_e_up"], w["w_e_down"], _tile(n, 1024))
    out = _ple(x1, y, p.reshape(n, PLE_DIM), w["g_ple"], w["w_ple_gate"], w["b_ple_gate"], w["w_ple_in"],
               w["g_final"], tm)
    return out, k, v, cout[:, SUBLANES - (CONV_W - 1):, :], hout[:, 0, :]


def kernel(x_prompt, x_sample, cache_k, cache_v, state_conv, state_lru, p_prompt, p_sample, rel_bias, g_mix, w_in, b_merge, lam_q1, lam_k1, lam_q2, lam_k2, g_subln, w_attn_br, w_conv, b_conv, w_rg_a, b_rg_a, w_rg_x, b_rg_x, lru_lambda, w_lru_br, w_out, g_ffn, w_grp, b_grp, w_rt, b_rt, w_e_gate, w_e_up, w_e_down, g_ple, w_ple_gate, b_ple_gate, w_ple_in, g_final):
    depth = w_in.shape[0]
    assert depth == 1, "the final norm is fused into the single layer's last stage"
    bp, tp, _ = x_prompt.shape
    bs, ts, _ = x_sample.shape
    past = cache_k.shape[2]
    li = 0
    row = lambda a: a[li].reshape(1, -1).astype(f32)
    w_router = jnp.concatenate(
        [jnp.transpose(w_rt[li], (1, 0, 2)).reshape(D_MODEL, N_EXPERTS), w_grp[li]], axis=1).astype(f32)
    w_router = jnp.pad(w_router, ((0, 0), (0, LANES - N_EXPERTS - N_GROUPS)))
    b_router = jnp.pad(jnp.concatenate([b_rt[li].reshape(-1), b_grp[li]]).astype(f32),
                       (0, LANES - N_EXPERTS - N_GROUPS)).reshape(1, LANES)
    w = dict(
        g_mix=row(g_mix), w_in=w_in[li].astype(bf16), b_merge=row(b_merge),
        lam_vecs=(row(lam_q1), row(lam_k1), row(lam_q2), row(lam_k2)), g_subln=row(g_subln),
        w_attn_br=w_attn_br[li].astype(bf16), w_conv=w_conv[li].astype(f32), b_conv=row(b_conv),
        w_rg_a=w_rg_a[li].astype(bf16), b_rg_a=row(b_rg_a), w_rg_x=w_rg_x[li].astype(bf16), b_rg_x=row(b_rg_x),
        lru_lambda=row(lru_lambda), w_lru_br=w_lru_br[li].astype(bf16), w_out=w_out[li].astype(bf16),
        g_ffn=row(g_ffn), w_router=w_router, b_router=b_router,
        w_e_gate=w_e_gate[li].astype(bf16), w_e_up=w_e_up[li].astype(bf16), w_e_down=w_e_down[li].astype(bf16),
        g_ple=row(g_ple), w_ple_gate=w_ple_gate[li].astype(bf16), b_ple_gate=row(b_ple_gate),
        w_ple_in=w_ple_in[li].astype(bf16), g_final=g_final.reshape(1, -1).astype(f32),
    )
    conv0 = jnp.zeros((bp, CONV_W - 1, LRU_WIDTH), f32)
    h0 = jnp.zeros((bp, LRU_WIDTH), f32)
    yp, kp, vp, cp, hp = _stream(x_prompt, p_prompt[li], conv0, h0, None, li, rel_bias, w)
    cache = (cache_k[li].reshape(bs, past, QK_COLS), cache_v[li].reshape(bs, past, V_COLS))
    ys, ks, vs, cs, hs = _stream(x_sample, p_sample[li], state_conv[li], state_lru[li], cache, li, rel_bias, w)
    return (yp.reshape(bp, tp, D_MODEL), ys.reshape(bs, ts, D_MODEL),
            kp.reshape(1, bp, tp, N_HEADS, 2, HEAD_DIM), vp.reshape(1, bp, tp, N_HEADS, V_DIM),
            cp[None], hp[None],
            ks.reshape(1, bs, ts, N_HEADS, 2, HEAD_DIM), vs.reshape(1, bs, ts, N_HEADS, V_DIM),
            cs[None], hs[None])
```

```python
import functools
import math

import jax
import jax.numpy as jnp
from jax import lax
from jax.experimental import pallas as pl
from jax.experimental.pallas import tpu as pltpu

f32 = jnp.float32
bf16 = jnp.bfloat16

D_MODEL = 1024
N_HEADS = 8
HEAD_DIM = 64
V_DIM = 128
QK_COLS = N_HEADS * 2 * HEAD_DIM
V_COLS = N_HEADS * V_DIM
LRU_WIDTH = 1024
LRU_BLOCKS = 8
LRU_BLOCK_W = LRU_WIDTH // LRU_BLOCKS
CONV_W = 4
LRU_C = 8.0
CHUNK = 64
N_BUCKETS = 32
MAX_DISTANCE = 128
N_GROUPS = 4
EXPERTS_PER_GROUP = 8
N_EXPERTS = N_GROUPS * EXPERTS_PER_GROUP
D_EXPERT = 256
PLE_DIM = 256
EPS = 1e-6
NEG_INF = -1e30
LOG2E = math.log2(math.e)
LANES = 128
SUBLANES = 8
VMEM_LIMIT = 56 * 1024 * 1024

C_Q, C_K, C_V = 0, QK_COLS, 2 * QK_COLS
C_LX = 2 * QK_COLS + V_COLS
C_LY = C_LX + LRU_WIDTH
C_G = C_LY + LRU_WIDTH
IN_COLS = C_G + 2 * D_MODEL


def _cparams(sem):
    return pltpu.CompilerParams(dimension_semantics=sem, vmem_limit_bytes=VMEM_LIMIT)


def _const_spec(shape):
    nd = len(shape)
    return pl.BlockSpec(shape, lambda *_: (0,) * nd, pipeline_mode=pl.Buffered(1))


def _rms(x, g):
    return x * lax.rsqrt(jnp.mean(x * x, axis=-1, keepdims=True) + EPS) * g


def _sigmoid(x):
    return 1.0 / (1.0 + jnp.exp(-x))


def _gelu_tanh(x):
    c = math.sqrt(2.0 / math.pi)
    return 0.5 * x * (1.0 + jnp.tanh(c * (x + 0.044715 * (x * x * x))))


def _in_proj_kernel(x_ref, g_ref, w_ref, bm_ref, q_ref, k_ref, kb_ref, v_ref, vb_ref, lx_ref, gl_ref, gt_ref):
    h = _rms(x_ref[...], g_ref[...]).astype(bf16)

    def proj(lo, width):
        return jnp.dot(h, w_ref[:, lo:lo + width], preferred_element_type=f32)

    q_ref[...] = (proj(C_Q, QK_COLS) * (HEAD_DIM ** -0.5 * LOG2E)).astype(bf16)
    k = proj(C_K, QK_COLS)
    k_ref[...] = k
    kb_ref[...] = k.astype(bf16)
    v = proj(C_V, V_COLS)
    v_ref[...] = v
    vb_ref[...] = v.astype(bf16)
    lx_ref[...] = proj(C_LX, LRU_WIDTH)
    gl_ref[...] = _gelu_tanh(proj(C_LY, LRU_WIDTH))
    gt_ref[...] = _sigmoid(proj(C_G, 2 * D_MODEL) + bm_ref[...])


def _in_proj(x, g_mix, w_in_b, b_merge, tm):
    n = x.shape[0]
    row = lambda w: pl.BlockSpec((tm, w), lambda i: (i, 0))
    outs = [
        jax.ShapeDtypeStruct((n, QK_COLS), bf16),
        jax.ShapeDtypeStruct((n, QK_COLS), f32),
        jax.ShapeDtypeStruct((n, QK_COLS), bf16),
        jax.ShapeDtypeStruct((n, V_COLS), f32),
        jax.ShapeDtypeStruct((n, V_COLS), bf16),
        jax.ShapeDtypeStruct((n, LRU_WIDTH), f32),
        jax.ShapeDtypeStruct((n, LRU_WIDTH), f32),
        jax.ShapeDtypeStruct((n, 2 * D_MODEL), f32),
    ]
    return pl.pallas_call(
        _in_proj_kernel,
        grid=(n // tm,),
        in_specs=[row(D_MODEL), _const_spec((1, D_MODEL)), _const_spec((D_MODEL, IN_COLS)),
                  _const_spec((1, 2 * D_MODEL))],
        out_specs=[row(QK_COLS), row(QK_COLS), row(QK_COLS), row(V_COLS), row(V_COLS),
                   row(LRU_WIDTH), row(LRU_WIDTH), row(2 * D_MODEL)],
        out_shape=outs,
        compiler_params=_cparams(("parallel",)),
        name="in_proj",
    )(x, g_mix, w_in_b, b_merge)


def _rel_bucket(rel):
    half = N_BUCKETS // 2
    max_exact = half // 2
    ret = jnp.where(rel > 0, half, 0)
    n = jnp.abs(rel)
    nf = jnp.maximum(n, 1).astype(f32)
    large = max_exact + (jnp.log(nf / max_exact) / math.log(MAX_DISTANCE / max_exact)
                         * (half - max_exact)).astype(jnp.int32)
    large = jnp.minimum(large, half - 1)
    return ret + jnp.where(n < max_exact, n, large)


def _toeplitz(t, nq, nk):
    nh, length = t.shape
    a = jnp.broadcast_to(t[:, None, :], (nh, nq, length))
    a = jnp.pad(a, ((0, 0), (0, 0), (0, 1))).reshape(nh, nq * (length + 1))
    a = a[:, :nq * length].reshape(nh, nq, length)
    return a[:, :, nq - 1:nq - 1 + nk]


def _bias_table(rel_bias, qpos0, nq, kpos0, nk, shift):
    rel = jnp.arange(-(nq - 1), nk) + (kpos0 - qpos0)
    t = (rel_bias.astype(f32)[_rel_bucket(rel)].T - shift[:, None]) * LOG2E
    qpos = qpos0 + jnp.arange(nq)
    kpos = kpos0 + jnp.arange(nk)
    mask = (kpos[None, :] // CHUNK) <= (qpos[:, None] // CHUNK)
    return jnp.where(mask[None], _toeplitz(t, nq, nk), NEG_INF)


def _bias_ring(rel_bias, offset, tq, shift):
    r = jnp.arange(2 * tq)
    x = jnp.where(r < tq, r, r - 2 * tq)
    t = (rel_bias.astype(f32)[_rel_bucket(offset - x)].T - shift[:, None]) * LOG2E
    return t[:, None, :]


def _lam_value(lq1, lk1, lq2, lk2, lam_init):
    s1 = jnp.sum(lq1[...] * lk1[...], axis=-1, keepdims=True)
    s2 = jnp.sum(lq2[...] * lk2[...], axis=-1, keepdims=True)
    return jnp.exp(s1) - jnp.exp(s2) + lam_init


def _split_maps(q):
    lane = lax.broadcasted_iota(jnp.int32, q.shape, 1)
    zero = jnp.zeros_like(q)
    return jnp.where(lane < HEAD_DIM, q, zero), jnp.where(lane >= HEAD_DIM, q, zero)


def _qk(qz, kt):
    return lax.dot_general(qz, kt, (((1,), (1,)), ((), ())), preferred_element_type=f32)


def _finish_heads(o1, o2, lam, gs, lam_init):
    att = o1 - lam * o2
    return _rms(att, gs) * (1.0 - lam_init)


ONES_ROWS = 16


def _attn_prompt_kernel(q_ref, k_ref, vt_ref, ring_ref, lq1, lk1, lq2, lk2, gs_ref, o_ref,
                        nb_sc, s_sc, p_sc, al_sc, m_sc, acc_sc, *, tq, nq, lam_init):
    i = pl.program_id(1)

    def bias_tile(ring):
        return pltpu.roll(jnp.broadcast_to(ring, (tq, 2 * tq)), 0, 1, stride=1, stride_axis=0)[:, :tq]

    @pl.when(i == 0)
    def _():
        kk = lax.broadcasted_iota(jnp.int32, (tq, tq), 0)
        qq = lax.broadcasted_iota(jnp.int32, (tq, tq), 1)
        nb_sc[0] = jnp.where((kk // CHUNK) <= (qq // CHUNK), bias_tile(ring_ref[1]), NEG_INF)
        nb_sc[1] = jnp.full((tq, tq), NEG_INF, f32)

    @pl.when(i == 1)
    def _():
        nb_sc[1] = nb_sc[0]
        nb_sc[0] = bias_tile(ring_ref[0])

    qt = q_ref[...].astype(f32).T
    row = lax.broadcasted_iota(jnp.int32, qt.shape, 0)
    qz = (jnp.where(row < HEAD_DIM, qt, 0.0).astype(bf16), jnp.where(row >= HEAD_DIM, qt, 0.0).astype(bf16))

    m_sc[...] = jnp.full(m_sc.shape, NEG_INF, f32)
    acc_sc[...] = jnp.zeros(acc_sc.shape, f32)

    def qk(slot, tile, bias):
        kt = k_ref[pl.ds(pl.multiple_of(tile * tq, tq), tq), :]
        for mi in range(2):
            s = jnp.dot(kt, qz[mi], preferred_element_type=f32)
            s_sc[mi, slot] = s if bias is None else s + bias

    def softmax(slot):
        for mi in range(2):
            s = s_sc[mi, slot]
            m_prev = m_sc[mi]
            m_new = jnp.maximum(m_prev, jnp.max(s, axis=0, keepdims=True))
            al_sc[mi, slot] = jnp.exp2(m_prev - m_new)
            p_sc[mi, slot] = jnp.exp2(s - m_new).astype(bf16)
            m_sc[mi] = m_new

    def pv(slot, vtile):
        va = vt_ref[vtile]
        for mi in range(2):
            acc_sc[mi] = acc_sc[mi] * al_sc[mi, slot] + jnp.dot(va, p_sc[mi, slot], preferred_element_type=f32)

    near0 = jnp.maximum(i - 1, 0)
    n_far = near0
    n_pairs = (n_far + 1) // 2

    def far_k(f):
        return jnp.clip(f, 0, jnp.maximum(n_far - 1, 0))

    def far_v(f):
        return jnp.where(f < n_far, f, nq)

    qk(0, near0, nb_sc[0])
    qk(1, near0 + 1, nb_sc[1])
    softmax(0)
    qk(0, far_k(0), None)
    softmax(1)
    pv(0, near0)

    def far_pair(p, carry):
        f0 = 2 * p
        qk(1, far_k(f0 + 1), None)
        softmax(0)
        pv(1, jnp.where(p == 0, near0 + 1, far_v(f0 - 1)))
        qk(0, far_k(f0 + 2), None)
        softmax(1)
        pv(0, far_v(f0))
        return carry

    lax.fori_loop(0, n_pairs, far_pair, 0)
    pv(1, jnp.where(n_pairs > 0, far_v(2 * n_pairs - 1), near0 + 1))

    a1 = acc_sc[0]
    a2 = acc_sc[1]
    o1 = a1[:V_DIM] / a1[V_DIM:V_DIM + 1]
    o2 = a2[:V_DIM] / a2[V_DIM:V_DIM + 1]
    lam = _lam_value(lq1, lk1, lq2, lk2, lam_init)
    att = (o1 - lam * o2).T
    o_ref[...] = (_rms(att, gs_ref[...]) * (1.0 - lam_init)).astype(o_ref.dtype)


def _attn_prompt(q, kb, vb, rel_bias, lam_vecs, g_subln, lam_init, tq):
    t = q.shape[0]
    nq = t // tq
    far = rel_bias.astype(f32)[_rel_bucket(jnp.asarray(-4 * MAX_DISTANCE))]
    rings = jnp.stack([_bias_ring(rel_bias, -tq, tq, far), _bias_ring(rel_bias, 0, tq, far)], axis=1)
    vt = vb.reshape(nq, tq, N_HEADS, V_DIM).transpose(2, 0, 3, 1)
    vt = jnp.concatenate([vt, jnp.ones((N_HEADS, nq, ONES_ROWS, tq), bf16)], axis=2)
    vt = jnp.pad(vt, ((0, 0), (0, 1), (0, 0), (0, 0)))
    vec = _const_spec((1, HEAD_DIM))
    return pl.pallas_call(
        functools.partial(_attn_prompt_kernel, tq=tq, nq=nq, lam_init=lam_init),
        grid=(N_HEADS, nq),
        in_specs=[
            pl.BlockSpec((tq, 2 * HEAD_DIM), lambda h, i: (i, h)),
            pl.BlockSpec((t, 2 * HEAD_DIM), lambda h, i: (0, h)),
            pl.BlockSpec((None, nq + 1, V_DIM + ONES_ROWS, tq), lambda h, i: (h, 0, 0, 0)),
            pl.BlockSpec((None, 2, 1, 2 * tq), lambda h, i: (h, 0, 0, 0)),
            vec, vec, vec, vec, _const_spec((1, V_DIM)),
        ],
        out_specs=pl.BlockSpec((tq, V_DIM), lambda h, i: (i, h)),
        out_shape=jax.ShapeDtypeStruct((t, V_COLS), bf16),
        scratch_shapes=[pltpu.VMEM((2, tq, tq), f32), pltpu.VMEM((2, 2, tq, tq), f32),
                        pltpu.VMEM((2, 2, tq, tq), bf16), pltpu.VMEM((2, 2, 1, tq), f32),
                        pltpu.VMEM((2, 1, tq), f32), pltpu.VMEM((2, V_DIM + ONES_ROWS, tq), f32)],
        compiler_params=_cparams(("arbitrary", "arbitrary")),
        name="attn_prompt",
    )(q, kb, vt, rings, *lam_vecs, g_subln)


def _attn_sample_kernel(q_ref, kp_ref, vp_ref, kn_ref, vn_ref, nbp_ref, nbn_ref, lq1, lk1, lq2, lk2, gs_ref,
                        o_ref, *, lam_init):
    qz = _split_maps(q_ref[...])
    kp = kp_ref[...].astype(bf16)
    vp = vp_ref[...].astype(bf16)
    kn = kn_ref[...]
    vn = vn_ref[...]
    outs = []
    for mi in range(2):
        sp = _qk(qz[mi], kp) + nbp_ref[...]
        sn = _qk(qz[mi], kn) + nbn_ref[...]
        m = jnp.maximum(jnp.max(sp, axis=1, keepdims=True), jnp.max(sn, axis=1, keepdims=True))
        pp = jnp.exp2(sp - m).astype(bf16)
        pn = jnp.exp2(sn - m).astype(bf16)
        den = (jnp.sum(pp.astype(f32), axis=1, keepdims=True) + jnp.sum(pn.astype(f32), axis=1, keepdims=True))
        num = jnp.dot(pp, vp, preferred_element_type=f32) + jnp.dot(pn, vn, preferred_element_type=f32)
        outs.append(num / den)
    lam = _lam_value(lq1, lk1, lq2, lk2, lam_init)
    o_ref[...] = _finish_heads(outs[0], outs[1], lam, gs_ref[...], lam_init).astype(o_ref.dtype)


def _attn_sample(q, kb, vb, cache_k, cache_v, rel_bias, lam_vecs, g_subln, lam_init, nb_, ts):
    past = cache_k.shape[1]
    zero = jnp.zeros((N_HEADS,), f32)
    nbp = _bias_table(rel_bias, past, ts, 0, past, zero)
    nbn = _bias_table(rel_bias, past, ts, past, ts, zero)
    vec = _const_spec((1, HEAD_DIM))
    return pl.pallas_call(
        functools.partial(_attn_sample_kernel, lam_init=lam_init),
        grid=(nb_, N_HEADS),
        in_specs=[
            pl.BlockSpec((ts, 2 * HEAD_DIM), lambda b, h: (b, h)),
            pl.BlockSpec((None, past, 2 * HEAD_DIM), lambda b, h: (b, 0, h)),
            pl.BlockSpec((None, past, V_DIM), lambda b, h: (b, 0, h)),
            pl.BlockSpec((ts, 2 * HEAD_DIM), lambda b, h: (b, h)),
            pl.BlockSpec((ts, V_DIM), lambda b, h: (b, h)),
            pl.BlockSpec((None, ts, past), lambda b, h: (h, 0, 0)),
            pl.BlockSpec((None, ts, ts), lambda b, h: (h, 0, 0)),
            vec, vec, vec, vec, _const_spec((1, V_DIM)),
        ],
        out_specs=pl.BlockSpec((ts, V_DIM), lambda b, h: (b, h)),
        out_shape=jax.ShapeDtypeStruct((nb_ * ts, V_COLS), bf16),
        compiler_params=_cparams(("parallel", "parallel")),
        name="attn_sample",
    )(q, cache_k, cache_v, kb, vb, nbp, nbn, *lam_vecs, g_subln)


def _conv_lru_kernel(lx_ref, gl_ref, c0_ref, h0_ref, wc_ref, bc_ref, wa_ref, ba_ref, wx_ref, bx_ref, lam_ref,
                     hsg_ref, cout_ref, hout_ref, xbuf, a_sc, b_sc, hcar, *, tt, first_pos_is_zero):
    t = pl.program_id(1)

    @pl.when(t == 0)
    def _():
        xbuf[0:SUBLANES] = c0_ref[...]
        hcar[...] = jnp.broadcast_to(h0_ref[...], hcar.shape)

    xbuf[SUBLANES:SUBLANES + tt] = lx_ref[...]
    full = xbuf[...]
    xc = jnp.broadcast_to(bc_ref[...], (tt, LRU_WIDTH))
    for j in range(CONV_W):
        shift = CONV_W - 1 - j
        sh = full if shift == 0 else pltpu.roll(full, shift, axis=0)
        xc = xc + sh[SUBLANES:SUBLANES + tt] * wc_ref[j:j + 1, :]
    xbuf[0:SUBLANES] = full[tt:tt + SUBLANES]
    cout_ref[...] = full[tt:tt + SUBLANES]

    xcb = xc.astype(bf16)

    def block_diag(w_ref, b_ref):
        cols = [jnp.dot(xcb[:, n * LRU_BLOCK_W:(n + 1) * LRU_BLOCK_W], w_ref[n], preferred_element_type=f32)
                for n in range(LRU_BLOCKS)]
        return jnp.concatenate(cols, axis=1) + b_ref[...]

    r = _sigmoid(block_diag(wa_ref, ba_ref))
    ig = _sigmoid(block_diag(wx_ref, bx_ref))
    z = -lam_ref[...]
    softplus = jnp.maximum(z, 0.0) + jnp.log1p(jnp.exp(-jnp.abs(z)))
    log_a = -LRU_C * r * softplus
    a = jnp.exp(log_a)
    th = jnp.tanh(log_a)
    mult = jnp.sqrt(-2.0 * th / (1.0 - th))
    row = lax.broadcasted_iota(jnp.int32, (tt, LRU_WIDTH), 0)
    if first_pos_is_zero:
        mult = jnp.where((row == 0) & (t == 0), 1.0, mult)
    bv = mult * ig * xc

    rowmod = row % SUBLANES
    d = 1
    while d < SUBLANES:
        valid = rowmod >= d
        a_s = pltpu.roll(a, d, axis=0)
        b_s = pltpu.roll(bv, d, axis=0)
        bv = jnp.where(valid, a * b_s + bv, bv)
        a = jnp.where(valid, a * a_s, a)
        d *= 2
    a_sc[...] = a
    b_sc[...] = bv

    def group(g, h):
        sl = pl.ds(pl.multiple_of(g * SUBLANES, SUBLANES), SUBLANES)
        hg = a_sc[sl, :] * h + b_sc[sl, :]
        b_sc[sl, :] = hg
        return jnp.broadcast_to(hg[SUBLANES - 1:SUBLANES, :], hg.shape)

    h = lax.fori_loop(0, tt // SUBLANES, group, hcar[...], unroll=4)
    hcar[...] = h
    hout_ref[...] = h[0:1, :]
    hsg_ref[...] = (b_sc[...] * gl_ref[...]).astype(hsg_ref.dtype)


def _conv_lru(lx, gl, conv0, h0, w_conv, b_conv, wa, ba, wx, bx, lru_lambda, nb_, tlen, tt, first_pos_is_zero):
    nt = tlen // tt
    rows = pl.BlockSpec((tt, LRU_WIDTH), lambda b, t: (b * nt + t, 0))
    vecw = _const_spec((1, LRU_WIDTH))
    wblk = _const_spec((LRU_BLOCKS, LRU_BLOCK_W, LRU_BLOCK_W))
    return pl.pallas_call(
        functools.partial(_conv_lru_kernel, tt=tt, first_pos_is_zero=first_pos_is_zero),
        grid=(nb_, nt),
        in_specs=[rows, rows,
                  pl.BlockSpec((None, SUBLANES, LRU_WIDTH), lambda b, t: (b, 0, 0)),
                  pl.BlockSpec((None, 1, LRU_WIDTH), lambda b, t: (b, 0, 0)),
                  _const_spec((CONV_W, LRU_WIDTH)), vecw, wblk, vecw, wblk, vecw, vecw],
        out_specs=[rows,
                   pl.BlockSpec((None, SUBLANES, LRU_WIDTH), lambda b, t: (b, 0, 0)),
                   pl.BlockSpec((None, 1, LRU_WIDTH), lambda b, t: (b, 0, 0))],
        out_shape=[jax.ShapeDtypeStruct((nb_ * tlen, LRU_WIDTH), bf16),
                   jax.ShapeDtypeStruct((nb_, SUBLANES, LRU_WIDTH), f32),
                   jax.ShapeDtypeStruct((nb_, 1, LRU_WIDTH), f32)],
        scratch_shapes=[pltpu.VMEM((tt + SUBLANES, LRU_WIDTH), f32), pltpu.VMEM((tt, LRU_WIDTH), f32),
                        pltpu.VMEM((tt, LRU_WIDTH), f32), pltpu.VMEM((SUBLANES, LRU_WIDTH), f32)],
        compiler_params=_cparams(("arbitrary", "arbitrary")),
        name="conv_lru",
    )(lx, gl, conv0, h0, w_conv, b_conv, wa, ba, wx, bx, lru_lambda)


def _merge_kernel(x_ref, att_ref, hsg_ref, gt_ref, wa_ref, wb_ref, wo_ref, gf_ref, wr_ref, br_ref,
                  x1_ref, h2_ref, comb_ref):
    bra = jnp.dot(att_ref[...], wa_ref[...], preferred_element_type=f32)
    brb = jnp.dot(hsg_ref[...], wb_ref[...], preferred_element_type=f32)
    gt = gt_ref[...]
    m = (gt[:, :D_MODEL] * bra + gt[:, D_MODEL:] * brb).astype(bf16)
    x1 = x_ref[...] + jnp.dot(m, wo_ref[...], preferred_element_type=f32)
    x1_ref[...] = x1
    h2 = _rms(x1, gf_ref[...])
    h2_ref[...] = h2.astype(bf16)

    logits = jnp.dot(h2, wr_ref[...], preferred_element_type=f32, precision=lax.Precision.HIGHEST) + br_ref[...]
    lane = lax.broadcasted_iota(jnp.int32, logits.shape, 1)
    lanef = lane.astype(f32)
    low = jnp.float32(-3.0e38)
    is_grp = (lane >= N_EXPERTS) & (lane < N_EXPERTS + N_GROUPS)
    gl = jnp.where(is_grp, logits, low)
    gmax = jnp.max(gl, axis=1, keepdims=True)
    gidx = jnp.min(jnp.where(is_grp & (gl == gmax), lanef, 1.0e3), axis=1, keepdims=True) - N_EXPERTS
    gden = jnp.sum(jnp.where(is_grp, jnp.exp(gl - gmax), 0.0), axis=1, keepdims=True)
    g_w = 1.0 / gden
    lo = gidx * EXPERTS_PER_GROUP
    in_sel = (lanef >= lo) & (lanef < lo + EXPERTS_PER_GROUP)
    sel = jnp.where(in_sel, logits, low)
    v1 = jnp.max(sel, axis=1, keepdims=True)
    i1 = jnp.min(jnp.where(in_sel & (sel == v1), lanef, 1.0e3), axis=1, keepdims=True)
    in_sel2 = in_sel & (lanef != i1)
    sel2 = jnp.where(in_sel2, logits, low)
    v2 = jnp.max(sel2, axis=1, keepdims=True)
    i2 = jnp.min(jnp.where(in_sel2 & (sel2 == v2), lanef, 1.0e3), axis=1, keepdims=True)
    e2 = jnp.exp(v2 - v1)
    w1 = g_w / (1.0 + e2)
    w2 = g_w * e2 / (1.0 + e2)
    comb_ref[...] = jnp.where(lanef == i1, w1, 0.0) + jnp.where(lanef == i2, w2, 0.0)


def _merge(x, att, hsg, gt, wa, wb, wo, g_ffn, w_router, b_router, tm):
    n = x.shape[0]
    row = lambda w: pl.BlockSpec((tm, w), lambda i: (i, 0))
    sq = _const_spec((D_MODEL, D_MODEL))
    return pl.pallas_call(
        _merge_kernel,
        grid=(n // tm,),
        in_specs=[row(D_MODEL), row(V_COLS), row(LRU_WIDTH), row(2 * D_MODEL), sq, sq, sq,
                  _const_spec((1, D_MODEL)), _const_spec((D_MODEL, LANES)), _const_spec((1, LANES))],
        out_specs=[row(D_MODEL), row(D_MODEL), row(LANES)],
        out_shape=[jax.ShapeDtypeStruct((n, D_MODEL), f32), jax.ShapeDtypeStruct((n, D_MODEL), bf16),
                   jax.ShapeDtypeStruct((n, LANES), f32)],
        compiler_params=_cparams(("parallel",)),
        name="merge_router",
    )(x, att, hsg, gt, wa, wb, wo, g_ffn, w_router, b_router)


def _moe_kernel(h_ref, comb_ref, wg_ref, wu_ref, wd_ref, y_ref):
    e = pl.program_id(1)

    @pl.when(e == 0)
    def _():
        y_ref[...] = jnp.zeros(y_ref.shape, f32)

    h = h_ref[...]
    gate = jnp.dot(h, wg_ref[...], preferred_element_type=f32)
    up = jnp.dot(h, wu_ref[...], preferred_element_type=f32)
    comb = comb_ref[...]
    lane = lax.broadcasted_iota(jnp.int32, comb.shape, 1)
    c = jnp.sum(jnp.where(lane == e, comb, 0.0), axis=1, keepdims=True)
    he = (gate * _sigmoid(gate) * up * c).astype(bf16)
    y_ref[...] += jnp.dot(he, wd_ref[...], preferred_element_type=f32)


def _moe(h2, comb, wg, wu, wd, tm):
    n = h2.shape[0]
    return pl.pallas_call(
        _moe_kernel,
        grid=(n // tm, N_EXPERTS),
        in_specs=[pl.BlockSpec((tm, D_MODEL), lambda i, e: (i, 0)),
                  pl.BlockSpec((tm, LANES), lambda i, e: (i, 0)),
                  pl.BlockSpec((None, D_MODEL, D_EXPERT), lambda i, e: (e, 0, 0)),
                  pl.BlockSpec((None, D_MODEL, D_EXPERT), lambda i, e: (e, 0, 0)),
                  pl.BlockSpec((None, D_EXPERT, D_MODEL), lambda i, e: (e, 0, 0))],
        out_specs=pl.BlockSpec((tm, D_MODEL), lambda i, e: (i, 0)),
        out_shape=jax.ShapeDtypeStruct((n, D_MODEL), f32),
        compiler_params=_cparams(("parallel", "arbitrary")),
        name="moe",
    )(h2, comb, wg, wu, wd)


def _ple_kernel(x1_ref, y_ref, p_ref, gp_ref, wpg_ref, bpg_ref, wpi_ref, gfin_ref, o_ref):
    x2 = x1_ref[...] + y_ref[...]
    hp = _rms(x2, gp_ref[...]).astype(bf16)
    pg = _sigmoid(jnp.dot(hp, wpg_ref[...], preferred_element_type=f32) + bpg_ref[...])
    pin = jnp.dot(p_ref[...].astype(bf16), wpi_ref[...], preferred_element_type=f32)
    x3 = x2 + pg * pin
    o_ref[...] = _rms(x3, gfin_ref[...])


def _ple(x1, y, p, g_ple, wpg, bpg, wpi, g_final, tm):
    n = x1.shape[0]
    row = lambda w: pl.BlockSpec((tm, w), lambda i: (i, 0))
    vec = _const_spec((1, D_MODEL))
    return pl.pallas_call(
        _ple_kernel,
        grid=(n // tm,),
        in_specs=[row(D_MODEL), row(D_MODEL), row(PLE_DIM), vec, _const_spec((D_MODEL, D_MODEL)), vec,
                  _const_spec((PLE_DIM, D_MODEL)), vec],
        out_specs=row(D_MODEL),
        out_shape=jax.ShapeDtypeStruct((n, D_MODEL), f32),
        compiler_params=_cparams(("parallel",)),
        name="ple_final",
    )(x1, y, p, g_ple, wpg, bpg, wpi, g_final)


def _tile(n, pref):
    return pref if n % pref == 0 else n


def _stream(x, p, conv0, h0, cache, li, rel_bias, w):
    nb_, tlen, _ = x.shape
    n = nb_ * tlen
    xf = x.reshape(n, D_MODEL)
    lam_init = 0.8 - 0.6 * math.exp(-0.3 * li)
    q, k, kb, v, vb, lx, gl, gt = _in_proj(xf, w["g_mix"], w["w_in"], w["b_merge"], _tile(n, 256))
    if cache is None:
        att = _attn_prompt(q, kb, vb, rel_bias, w["lam_vecs"], w["g_subln"], lam_init, _tile(tlen, 512))
    else:
        att = _attn_sample(q, kb, vb, cache[0], cache[1], rel_bias, w["lam_vecs"], w["g_subln"], lam_init,
                           nb_, tlen)
    conv_pad = jnp.pad(conv0, ((0, 0), (SUBLANES - (CONV_W - 1), 0), (0, 0)))
    hsg, cout, hout = _conv_lru(lx, gl, conv_pad, h0[:, None, :], w["w_conv"], w["b_conv"], w["w_rg_a"],
                                w["b_rg_a"], w["w_rg_x"], w["b_rg_x"], w["lru_lambda"], nb_, tlen,
                                _tile(tlen, 256), cache is None)
    tm = _tile(n, 256)
    x1, h2, comb = _merge(xf, att, hsg, gt, w["w_attn_br"], w["w_lru_br"], w["w_out"], w["g_ffn"],
                          w["w_router"], w["b_router"], tm)
    y = _moe(h2, comb, w["w_e_gate"], w["w_e_up"], w["w_e_down"], _tile(n, 1024))
    out = _ple(x1, y, p.reshape(n, PLE_DIM), w["g_ple"], w["w_ple_gate"], w["b_ple_gate"], w["w_ple_in"],
               w["g_final"], tm)
    return out, k, v, cout[:, SUBLANES - (CONV_W - 1):, :], hout[:, 0, :]


def kernel(x_prompt, x_sample, cache_k, cache_v, state_conv, state_lru, p_prompt, p_sample, rel_bias, g_mix, w_in, b_merge, lam_q1, lam_k1, lam_q2, lam_k2, g_subln, w_attn_br, w_conv, b_conv, w_rg_a, b_rg_a, w_rg_x, b_rg_x, lru_lambda, w_lru_br, w_out, g_ffn, w_grp, b_grp, w_rt, b_rt, w_e_gate, w_e_up, w_e_down, g_ple, w_ple_gate, b_ple_gate, w_ple_in, g_final):
    depth = w_in.shape[0]
    assert depth == 1, "the final norm is fused into the single layer's last stage"
    bp, tp, _ = x_prompt.shape
    bs, ts, _ = x_sample.shape
    past = cache_k.shape[2]
    li = 0
    row = lambda a: a[li].reshape(1, -1).astype(f32)
    w_router = jnp.concatenate(
        [jnp.transpose(w_rt[li], (1, 0, 2)).reshape(D_MODEL, N_EXPERTS), w_grp[li]], axis=1).astype(f32)
    w_router = jnp.pad(w_router, ((0, 0), (0, LANES - N_EXPERTS - N_GROUPS)))
    b_router = jnp.pad(jnp.concatenate([b_rt[li].reshape(-1), b_grp[li]]).astype(f32),
                       (0, LANES - N_EXPERTS - N_GROUPS)).reshape(1, LANES)
    w = dict(
        g_mix=row(g_mix), w_in=w_in[li].astype(bf16), b_merge=row(b_merge),
        lam_vecs=(row(lam_q1), row(lam_k1), row(lam_q2), row(lam_k2)), g_subln=row(g_subln),
        w_attn_br=w_attn_br[li].astype(bf16), w_conv=w_conv[li].astype(f32), b_conv=row(b_conv),
        w_rg_a=w_rg_a[li].astype(bf16), b_rg_a=row(b_rg_a), w_rg_x=w_rg_x[li].astype(bf16), b_rg_x=row(b_rg_x),
        lru_lambda=row(lru_lambda), w_lru_br=w_lru_br[li].astype(bf16), w_out=w_out[li].astype(bf16),
        g_ffn=row(g_ffn), w_router=w_router, b_router=b_router,
        w_e_gate=w_e_gate[li].astype(bf16), w_e_up=w_e_up[li].astype(bf16), w_e_down=w_e_down[li].astype(bf16),
        g_ple=row(g_ple), w_ple_gate=w_ple_gate[li].astype(bf16), b_ple_gate=row(b_ple_gate),
        w_ple_in=w_ple_in[li].astype(bf16), g_final=g_final.reshape(1, -1).astype(f32),
    )
    conv0 = jnp.zeros((bp, CONV_W - 1, LRU_WIDTH), f32)
    h0 = jnp.zeros((bp, LRU_WIDTH), f32)
    yp, kp, vp, cp, hp = _stream(x_prompt, p_prompt[li], conv0, h0, None, li, rel_bias, w)
    cache = (cache_k[li].reshape(bs, past, QK_COLS), cache_v[li].reshape(bs, past, V_COLS))
    ys, ks, vs, cs, hs = _stream(x_sample, p_sample[li], state_conv[li], state_lru[li], cache, li, rel_bias, w)
    return (yp.reshape(bp, tp, D_MODEL), ys.reshape(bs, ts, D_MODEL),
            kp.reshape(1, bp, tp, N_HEADS, 2, HEAD_DIM), vp.reshape(1, bp, tp, N_HEADS, V_DIM),
            cp[None], hp[None],
            ks.reshape(1, bs, ts, N_HEADS, 2, HEAD_DIM), vs.reshape(1, bs, ts, N_HEADS, V_DIM),
            cs[None], hs[None])
```

```python
import functools
import math

import jax
import jax.numpy as jnp
from jax import lax
from jax.experimental import pallas as pl
from jax.experimental.pallas import tpu as pltpu

f32 = jnp.float32
bf16 = jnp.bfloat16

D_MODEL = 1024
N_HEADS = 8
HEAD_DIM = 64
V_DIM = 128
QK_COLS = N_HEADS * 2 * HEAD_DIM
V_COLS = N_HEADS * V_DIM
LRU_WIDTH = 1024
LRU_BLOCKS = 8
LRU_BLOCK_W = LRU_WIDTH // LRU_BLOCKS
CONV_W = 4
LRU_C = 8.0
CHUNK = 64
N_BUCKETS = 32
MAX_DISTANCE = 128
N_GROUPS = 4
EXPERTS_PER_GROUP = 8
N_EXPERTS = N_GROUPS * EXPERTS_PER_GROUP
D_EXPERT = 256
PLE_DIM = 256
EPS = 1e-6
NEG_INF = -1e30
LOG2E = math.log2(math.e)
ONES_ROWS = 16
LANES = 128
SUBLANES = 8
VMEM_LIMIT = 56 * 1024 * 1024

C_Q, C_K, C_V = 0, QK_COLS, 2 * QK_COLS
C_LX = 2 * QK_COLS + V_COLS
C_LY = C_LX + LRU_WIDTH
C_G = C_LY + LRU_WIDTH
IN_COLS = C_G + 2 * D_MODEL


def _cparams(sem):
    return pltpu.CompilerParams(dimension_semantics=sem, vmem_limit_bytes=VMEM_LIMIT)


def _const_spec(shape):
    nd = len(shape)
    return pl.BlockSpec(shape, lambda *_: (0,) * nd, pipeline_mode=pl.Buffered(1))


def _rms(x, g):
    return x * lax.rsqrt(jnp.mean(x * x, axis=-1, keepdims=True) + EPS) * g


def _sigmoid(x):
    return 1.0 / (1.0 + jnp.exp(-x))


def _gelu_tanh(x):
    c = math.sqrt(2.0 / math.pi)
    return 0.5 * x * (1.0 + jnp.tanh(c * (x + 0.044715 * (x * x * x))))


def _in_proj_kernel(x_ref, g_ref, w_ref, bm_ref, q_ref, k_ref, kb_ref, v_ref, vb_ref, lx_ref, gl_ref, gt_ref,
                    *, transposed_v):
    h = _rms(x_ref[...], g_ref[...]).astype(bf16)

    def proj(lo, width):
        return jnp.dot(h, w_ref[:, lo:lo + width], preferred_element_type=f32)

    q_ref[...] = (proj(C_Q, QK_COLS) * (HEAD_DIM ** -0.5 * LOG2E)).astype(bf16)
    k = proj(C_K, QK_COLS)
    k_ref[...] = k
    kb_ref[...] = k.astype(bf16)
    v = proj(C_V, V_COLS)
    v_ref[...] = v
    if transposed_v:
        for hd in range(N_HEADS):
            vb_ref[hd, :V_DIM, :] = v[:, hd * V_DIM:(hd + 1) * V_DIM].T.astype(bf16)
            vb_ref[hd, V_DIM:, :] = jnp.ones((ONES_ROWS, v.shape[0]), bf16)
    else:
        vb_ref[...] = v.astype(bf16)
    lx_ref[...] = proj(C_LX, LRU_WIDTH)
    gl_ref[...] = _gelu_tanh(proj(C_LY, LRU_WIDTH))
    gt_ref[...] = _sigmoid(proj(C_G, 2 * D_MODEL) + bm_ref[...])


def _in_proj(x, g_mix, w_in_b, b_merge, tm, tq=None):
    n = x.shape[0]
    row = lambda w: pl.BlockSpec((tm, w), lambda i: (i, 0))
    if tq is None:
        vb_shape, vb_spec = jax.ShapeDtypeStruct((n, V_COLS), bf16), row(V_COLS)
    else:
        per = tq // tm
        vb_shape = jax.ShapeDtypeStruct((N_HEADS, n // tq, V_DIM + ONES_ROWS, tq), bf16)
        vb_spec = pl.BlockSpec((N_HEADS, None, V_DIM + ONES_ROWS, tm), lambda i: (0, i // per, 0, i % per))
    outs = [
        jax.ShapeDtypeStruct((n, QK_COLS), bf16),
        jax.ShapeDtypeStruct((n, QK_COLS), f32),
        jax.ShapeDtypeStruct((n, QK_COLS), bf16),
        jax.ShapeDtypeStruct((n, V_COLS), f32),
        vb_shape,
        jax.ShapeDtypeStruct((n, LRU_WIDTH), f32),
        jax.ShapeDtypeStruct((n, LRU_WIDTH), f32),
        jax.ShapeDtypeStruct((n, 2 * D_MODEL), f32),
    ]
    return pl.pallas_call(
        functools.partial(_in_proj_kernel, transposed_v=tq is not None),
        grid=(n // tm,),
        in_specs=[row(D_MODEL), _const_spec((1, D_MODEL)), _const_spec((D_MODEL, IN_COLS)),
                  _const_spec((1, 2 * D_MODEL))],
        out_specs=[row(QK_COLS), row(QK_COLS), row(QK_COLS), row(V_COLS), vb_spec,
                   row(LRU_WIDTH), row(LRU_WIDTH), row(2 * D_MODEL)],
        out_shape=outs,
        compiler_params=_cparams(("parallel",)),
        name="in_proj",
    )(x, g_mix, w_in_b, b_merge)


def _rel_bucket(rel):
    half = N_BUCKETS // 2
    max_exact = half // 2
    ret = jnp.where(rel > 0, half, 0)
    n = jnp.abs(rel)
    nf = jnp.maximum(n, 1).astype(f32)
    large = max_exact + (jnp.log(nf / max_exact) / math.log(MAX_DISTANCE / max_exact)
                         * (half - max_exact)).astype(jnp.int32)
    large = jnp.minimum(large, half - 1)
    return ret + jnp.where(n < max_exact, n, large)


def _toeplitz(t, nq, nk):
    nh, length = t.shape
    a = jnp.broadcast_to(t[:, None, :], (nh, nq, length))
    a = jnp.pad(a, ((0, 0), (0, 0), (0, 1))).reshape(nh, nq * (length + 1))
    a = a[:, :nq * length].reshape(nh, nq, length)
    return a[:, :, nq - 1:nq - 1 + nk]


def _bias_table(rel_bias, qpos0, nq, kpos0, nk, shift):
    rel = jnp.arange(-(nq - 1), nk) + (kpos0 - qpos0)
    t = (rel_bias.astype(f32)[_rel_bucket(rel)].T - shift[:, None]) * LOG2E
    qpos = qpos0 + jnp.arange(nq)
    kpos = kpos0 + jnp.arange(nk)
    mask = (kpos[None, :] // CHUNK) <= (qpos[:, None] // CHUNK)
    return jnp.where(mask[None], _toeplitz(t, nq, nk), NEG_INF)


def _bias_ring(rel_bias, offset, tq, shift):
    r = jnp.arange(2 * tq)
    x = jnp.where(r < tq, r, r - 2 * tq)
    t = (rel_bias.astype(f32)[_rel_bucket(offset - x)].T - shift[:, None]) * LOG2E
    return t[:, None, :]


def _lam_value(lq1, lk1, lq2, lk2, lam_init):
    s1 = jnp.sum(lq1[...] * lk1[...], axis=-1, keepdims=True)
    s2 = jnp.sum(lq2[...] * lk2[...], axis=-1, keepdims=True)
    return jnp.exp(s1) - jnp.exp(s2) + lam_init


def _split_maps(q):
    lane = lax.broadcasted_iota(jnp.int32, q.shape, 1)
    zero = jnp.zeros_like(q)
    return jnp.where(lane < HEAD_DIM, q, zero), jnp.where(lane >= HEAD_DIM, q, zero)


def _qk(qz, kt):
    return lax.dot_general(qz, kt, (((1,), (1,)), ((), ())), preferred_element_type=f32)


def _finish_heads(o1, o2, lam, gs, lam_init):
    att = o1 - lam * o2
    return _rms(att, gs) * (1.0 - lam_init)


def _attn_prompt_kernel(q_ref, k_ref, vt_ref, ring_ref, lq1, lk1, lq2, lk2, gs_ref, o_ref,
                        nb_sc, s_sc, p_sc, al_sc, tm_sc, m_sc, acc_sc, *, tq, lam_init):
    i = pl.program_id(1)

    def bias_tile(ring):
        return pltpu.roll(jnp.broadcast_to(ring, (tq, 2 * tq)), 0, 1, stride=1, stride_axis=0)[:, :tq]

    @pl.when(i == 0)
    def _():
        kk = lax.broadcasted_iota(jnp.int32, (tq, tq), 0)
        qq = lax.broadcasted_iota(jnp.int32, (tq, tq), 1)
        nb_sc[0] = jnp.where((kk // CHUNK) <= (qq // CHUNK), bias_tile(ring_ref[1]), NEG_INF)
        nb_sc[1] = jnp.full((tq, tq), NEG_INF, f32)

    @pl.when(i == 1)
    def _():
        nb_sc[1] = nb_sc[0]
        nb_sc[0] = bias_tile(ring_ref[0])

    qt = q_ref[...].astype(f32).T
    row = lax.broadcasted_iota(jnp.int32, qt.shape, 0)
    qz = (jnp.where(row < HEAD_DIM, qt, 0.0).astype(bf16), jnp.where(row >= HEAD_DIM, qt, 0.0).astype(bf16))

    m_sc[...] = jnp.full(m_sc.shape, NEG_INF, f32)
    acc_sc[...] = jnp.zeros(acc_sc.shape, f32)

    def qk(slot, tile, bias):
        kt = k_ref[pl.ds(pl.multiple_of(tile * tq, tq), tq), :]
        for mi in range(2):
            s = jnp.dot(kt, qz[mi], preferred_element_type=f32)
            s = s if bias is None else s + bias
            s_sc[mi, slot] = s
            tm_sc[mi, slot] = jnp.max(s, axis=0, keepdims=True)

    def softmax(slot):
        for mi in range(2):
            s = s_sc[mi, slot]
            m_prev = m_sc[mi]
            m_new = jnp.maximum(m_prev, tm_sc[mi, slot])
            al_sc[mi, slot] = jnp.exp2(m_prev - m_new)
            p_sc[mi, slot] = jnp.exp2(s - m_new).astype(bf16)
            m_sc[mi] = m_new

    def pv(slot, vtile):
        va = vt_ref[vtile]
        for mi in range(2):
            acc_sc[mi] = acc_sc[mi] * al_sc[mi, slot] + jnp.dot(va, p_sc[mi, slot], preferred_element_type=f32)

    near0 = jnp.maximum(i - 1, 0)
    n_far = near0
    n_pairs = (n_far + 1) // 2

    def far_k(f):
        return jnp.clip(f, 0, jnp.maximum(n_far - 1, 0))

    qk(0, near0, nb_sc[0])
    qk(1, near0 + 1, nb_sc[1])
    softmax(0)
    qk(0, far_k(0), None)
    softmax(1)
    pv(0, near0)

    def far_pair(p, carry):
        f0 = 2 * p
        qk(1, far_k(f0 + 1), None)
        softmax(0)
        pv(1, jnp.where(p == 0, near0 + 1, f0 - 1))
        qk(0, far_k(f0 + 2), None)
        softmax(1)
        pv(0, f0)
        return carry

    lax.fori_loop(0, n_pairs, far_pair, 0)

    @pl.when(n_far % 2 == 0)
    def _():
        pv(1, jnp.where(n_pairs > 0, 2 * n_pairs - 1, near0 + 1))

    a1 = acc_sc[0]
    a2 = acc_sc[1]
    o1 = a1[:V_DIM] / a1[V_DIM:V_DIM + 1]
    o2 = a2[:V_DIM] / a2[V_DIM:V_DIM + 1]
    lam = _lam_value(lq1, lk1, lq2, lk2, lam_init)
    att = (o1 - lam * o2).T
    o_ref[...] = (_rms(att, gs_ref[...]) * (1.0 - lam_init)).astype(o_ref.dtype)


def _attn_prompt(q, kb, vt, rel_bias, lam_vecs, g_subln, lam_init, tq):
    t = q.shape[0]
    nq = t // tq
    far = rel_bias.astype(f32)[_rel_bucket(jnp.asarray(-4 * MAX_DISTANCE))]
    rings = jnp.stack([_bias_ring(rel_bias, -tq, tq, far), _bias_ring(rel_bias, 0, tq, far)], axis=1)
    vec = _const_spec((1, HEAD_DIM))
    return pl.pallas_call(
        functools.partial(_attn_prompt_kernel, tq=tq, lam_init=lam_init),
        grid=(N_HEADS, nq),
        in_specs=[
            pl.BlockSpec((tq, 2 * HEAD_DIM), lambda h, i: (i, h)),
            pl.BlockSpec((t, 2 * HEAD_DIM), lambda h, i: (0, h)),
            pl.BlockSpec((None, nq, V_DIM + ONES_ROWS, tq), lambda h, i: (h, 0, 0, 0)),
            pl.BlockSpec((None, 2, 1, 2 * tq), lambda h, i: (h, 0, 0, 0)),
            vec, vec, vec, vec, _const_spec((1, V_DIM)),
        ],
        out_specs=pl.BlockSpec((tq, V_DIM), lambda h, i: (i, h)),
        out_shape=jax.ShapeDtypeStruct((t, V_COLS), bf16),
        scratch_shapes=[pltpu.VMEM((2, tq, tq), f32), pltpu.VMEM((2, 2, tq, tq), f32),
                        pltpu.VMEM((2, 2, tq, tq), bf16), pltpu.VMEM((2, 2, 1, tq), f32),
                        pltpu.VMEM((2, 2, 1, tq), f32), pltpu.VMEM((2, 1, tq), f32), pltpu.VMEM((2, V_DIM + ONES_ROWS, tq), f32)],
        compiler_params=_cparams(("arbitrary", "arbitrary")),
        name="attn_prompt",
    )(q, kb, vt, rings, *lam_vecs, g_subln)


def _attn_sample_kernel(q_ref, kp_ref, vp_ref, kn_ref, vn_ref, nbp_ref, nbn_ref, lq1, lk1, lq2, lk2, gs_ref,
                        o_ref, *, lam_init):
    qz = _split_maps(q_ref[...])
    kp = kp_ref[...].astype(bf16)
    vp = vp_ref[...].astype(bf16)
    kn = kn_ref[...]
    vn = vn_ref[...]
    outs = []
    for mi in range(2):
        sp = _qk(qz[mi], kp) + nbp_ref[...]
        sn = _qk(qz[mi], kn) + nbn_ref[...]
        m = jnp.maximum(jnp.max(sp, axis=1, keepdims=True), jnp.max(sn, axis=1, keepdims=True))
        pp = jnp.exp2(sp - m).astype(bf16)
        pn = jnp.exp2(sn - m).astype(bf16)
        den = (jnp.sum(pp.astype(f32), axis=1, keepdims=True) + jnp.sum(pn.astype(f32), axis=1, keepdims=True))
        num = jnp.dot(pp, vp, preferred_element_type=f32) + jnp.dot(pn, vn, preferred_element_type=f32)
        outs.append(num / den)
    lam = _lam_value(lq1, lk1, lq2, lk2, lam_init)
    o_ref[...] = _finish_heads(outs[0], outs[1], lam, gs_ref[...], lam_init).astype(o_ref.dtype)


def _attn_sample(q, kb, vb, cache_k, cache_v, rel_bias, lam_vecs, g_subln, lam_init, nb_, ts):
    past = cache_k.shape[1]
    zero = jnp.zeros((N_HEADS,), f32)
    nbp = _bias_table(rel_bias, past, ts, 0, past, zero)
    nbn = _bias_table(rel_bias, past, ts, past, ts, zero)
    vec = _const_spec((1, HEAD_DIM))
    return pl.pallas_call(
        functools.partial(_attn_sample_kernel, lam_init=lam_init),
        grid=(nb_, N_HEADS),
        in_specs=[
            pl.BlockSpec((ts, 2 * HEAD_DIM), lambda b, h: (b, h)),
            pl.BlockSpec((None, past, 2 * HEAD_DIM), lambda b, h: (b, 0, h)),
            pl.BlockSpec((None, past, V_DIM), lambda b, h: (b, 0, h)),
            pl.BlockSpec((ts, 2 * HEAD_DIM), lambda b, h: (b, h)),
            pl.BlockSpec((ts, V_DIM), lambda b, h: (b, h)),
            pl.BlockSpec((None, ts, past), lambda b, h: (h, 0, 0)),
            pl.BlockSpec((None, ts, ts), lambda b, h: (h, 0, 0)),
            vec, vec, vec, vec, _const_spec((1, V_DIM)),
        ],
        out_specs=pl.BlockSpec((ts, V_DIM), lambda b, h: (b, h)),
        out_shape=jax.ShapeDtypeStruct((nb_ * ts, V_COLS), bf16),
        compiler_params=_cparams(("parallel", "parallel")),
        name="attn_sample",
    )(q, cache_k, cache_v, kb, vb, nbp, nbn, *lam_vecs, g_subln)


def _conv_lru_kernel(lx_ref, gl_ref, c0_ref, h0_ref, wc_ref, bc_ref, wa_ref, ba_ref, wx_ref, bx_ref, lam_ref,
                     hsg_ref, cout_ref, hout_ref, xbuf, a_sc, b_sc, hcar, *, tt, first_pos_is_zero):
    t = pl.program_id(1)

    @pl.when(t == 0)
    def _():
        xbuf[0:SUBLANES] = c0_ref[...]
        hcar[...] = jnp.broadcast_to(h0_ref[...], hcar.shape)

    xbuf[SUBLANES:SUBLANES + tt] = lx_ref[...]
    xc = jnp.broadcast_to(bc_ref[...], (tt, LRU_WIDTH))
    for j in range(CONV_W):
        lo = SUBLANES - (CONV_W - 1) + j
        xc = xc + xbuf[lo:lo + tt, :] * wc_ref[j:j + 1, :]
    tail = xbuf[tt:tt + SUBLANES, :]
    xbuf[0:SUBLANES] = tail
    cout_ref[...] = tail

    xcb = xc.astype(bf16)

    def block_diag(w_ref, b_ref):
        cols = [jnp.dot(xcb[:, n * LRU_BLOCK_W:(n + 1) * LRU_BLOCK_W], w_ref[n], preferred_element_type=f32)
                for n in range(LRU_BLOCKS)]
        return jnp.concatenate(cols, axis=1) + b_ref[...]

    r = _sigmoid(block_diag(wa_ref, ba_ref))
    ig = _sigmoid(block_diag(wx_ref, bx_ref))
    z = -lam_ref[...]
    softplus = jnp.maximum(z, 0.0) + jnp.log1p(jnp.exp(-jnp.abs(z)))
    log_a = -LRU_C * r * softplus
    a = jnp.exp(log_a)
    th = jnp.tanh(log_a)
    m2 = -2.0 * th / (1.0 - th)
    mult = jnp.where(m2 > 0.0, m2 * lax.rsqrt(m2), 0.0)
    if first_pos_is_zero:
        row = lax.broadcasted_iota(jnp.int32, (tt, LRU_WIDTH), 0)
        mult = jnp.where((row == 0) & (t == 0), 1.0, mult)
    bv = mult * ig * xc

    groups = (tt // SUBLANES, SUBLANES, LRU_WIDTH)
    a = a.reshape(groups)
    bv = bv.reshape(groups)
    sub = lax.broadcasted_iota(jnp.int32, groups, 1)
    d = 1
    while d < SUBLANES:
        valid = sub >= d
        a_s = pltpu.roll(a, d, axis=1)
        b_s = pltpu.roll(bv, d, axis=1)
        bv = jnp.where(valid, a * b_s + bv, bv)
        a = jnp.where(valid, a * a_s, a)
        d *= 2
    a_sc[...] = a.reshape(tt, LRU_WIDTH)
    b_sc[...] = bv.reshape(tt, LRU_WIDTH)

    def group(g, h):
        sl = pl.ds(pl.multiple_of(g * SUBLANES, SUBLANES), SUBLANES)
        hg = a_sc[sl, :] * h + b_sc[sl, :]
        b_sc[sl, :] = hg
        return jnp.broadcast_to(hg[SUBLANES - 1:SUBLANES, :], hg.shape)

    h = lax.fori_loop(0, tt // SUBLANES, group, hcar[...], unroll=4)
    hcar[...] = h
    hout_ref[...] = h[0:1, :]
    hsg_ref[...] = (b_sc[...] * gl_ref[...]).astype(hsg_ref.dtype)


def _conv_lru(lx, gl, conv0, h0, w_conv, b_conv, wa, ba, wx, bx, lru_lambda, nb_, tlen, tt, first_pos_is_zero):
    nt = tlen // tt
    rows = pl.BlockSpec((tt, LRU_WIDTH), lambda b, t: (b * nt + t, 0))
    vecw = _const_spec((1, LRU_WIDTH))
    wblk = _const_spec((LRU_BLOCKS, LRU_BLOCK_W, LRU_BLOCK_W))
    return pl.pallas_call(
        functools.partial(_conv_lru_kernel, tt=tt, first_pos_is_zero=first_pos_is_zero),
        grid=(nb_, nt),
        in_specs=[rows, rows,
                  pl.BlockSpec((None, SUBLANES, LRU_WIDTH), lambda b, t: (b, 0, 0)),
                  pl.BlockSpec((None, 1, LRU_WIDTH), lambda b, t: (b, 0, 0)),
                  _const_spec((CONV_W, LRU_WIDTH)), vecw, wblk, vecw, wblk, vecw, vecw],
        out_specs=[rows,
                   pl.BlockSpec((None, SUBLANES, LRU_WIDTH), lambda b, t: (b, 0, 0)),
                   pl.BlockSpec((None, 1, LRU_WIDTH), lambda b, t: (b, 0, 0))],
        out_shape=[jax.ShapeDtypeStruct((nb_ * tlen, LRU_WIDTH), bf16),
                   jax.ShapeDtypeStruct((nb_, SUBLANES, LRU_WIDTH), f32),
                   jax.ShapeDtypeStruct((nb_, 1, LRU_WIDTH), f32)],
        scratch_shapes=[pltpu.VMEM((tt + SUBLANES, LRU_WIDTH), f32), pltpu.VMEM((tt, LRU_WIDTH), f32),
                        pltpu.VMEM((tt, LRU_WIDTH), f32), pltpu.VMEM((SUBLANES, LRU_WIDTH), f32)],
        compiler_params=_cparams(("arbitrary", "arbitrary")),
        name="conv_lru",
    )(lx, gl, conv0, h0, w_conv, b_conv, wa, ba, wx, bx, lru_lambda)


def _merge_kernel(x_ref, att_ref, hsg_ref, gt_ref, wa_ref, wb_ref, wo_ref, gf_ref, wr_ref, br_ref,
                  x1_ref, h2_ref, comb_ref):
    bra = jnp.dot(att_ref[...], wa_ref[...], preferred_element_type=f32)
    brb = jnp.dot(hsg_ref[...], wb_ref[...], preferred_element_type=f32)
    gt = gt_ref[...]
    m = (gt[:, :D_MODEL] * bra + gt[:, D_MODEL:] * brb).astype(bf16)
    x1 = x_ref[...] + jnp.dot(m, wo_ref[...], preferred_element_type=f32)
    x1_ref[...] = x1
    h2 = _rms(x1, gf_ref[...])
    h2_hi = h2.astype(bf16)
    h2_ref[...] = h2_hi

    h2_lo = (h2 - h2_hi.astype(f32)).astype(bf16)
    parts = (jnp.dot(h2_hi, wr_ref[...], preferred_element_type=f32)
             + jnp.dot(h2_lo, wr_ref[...], preferred_element_type=f32))
    logits = parts[:, :LANES] + parts[:, LANES:] + br_ref[...]
    lane = lax.broadcasted_iota(jnp.int32, logits.shape, 1)
    lanef = lane.astype(f32)
    low = jnp.float32(-3.0e38)
    is_grp = (lane >= N_EXPERTS) & (lane < N_EXPERTS + N_GROUPS)
    gl = jnp.where(is_grp, logits, low)
    gmax = jnp.max(gl, axis=1, keepdims=True)
    gidx = jnp.min(jnp.where(is_grp & (gl == gmax), lanef, 1.0e3), axis=1, keepdims=True) - N_EXPERTS
    gden = jnp.sum(jnp.where(is_grp, jnp.exp(gl - gmax), 0.0), axis=1, keepdims=True)
    g_w = 1.0 / gden
    lo = gidx * EXPERTS_PER_GROUP
    in_sel = (lanef >= lo) & (lanef < lo + EXPERTS_PER_GROUP)
    sel = jnp.where(in_sel, logits, low)
    v1 = jnp.max(sel, axis=1, keepdims=True)
    i1 = jnp.min(jnp.where(in_sel & (sel == v1), lanef, 1.0e3), axis=1, keepdims=True)
    in_sel2 = in_sel & (lanef != i1)
    sel2 = jnp.where(in_sel2, logits, low)
    v2 = jnp.max(sel2, axis=1, keepdims=True)
    i2 = jnp.min(jnp.where(in_sel2 & (sel2 == v2), lanef, 1.0e3), axis=1, keepdims=True)
    e2 = jnp.exp(v2 - v1)
    w1 = g_w / (1.0 + e2)
    w2 = g_w * e2 / (1.0 + e2)
    comb_ref[...] = jnp.where(lanef == i1, w1, 0.0) + jnp.where(lanef == i2, w2, 0.0)


def _merge(x, att, hsg, gt, wa, wb, wo, g_ffn, w_router, b_router, tm):
    n = x.shape[0]
    row = lambda w: pl.BlockSpec((tm, w), lambda i: (i, 0))
    sq = _const_spec((D_MODEL, D_MODEL))
    return pl.pallas_call(
        _merge_kernel,
        grid=(n // tm,),
        in_specs=[row(D_MODEL), row(V_COLS), row(LRU_WIDTH), row(2 * D_MODEL), sq, sq, sq,
                  _const_spec((1, D_MODEL)), _const_spec((D_MODEL, 2 * LANES)), _const_spec((1, LANES))],
        out_specs=[row(D_MODEL), row(D_MODEL), row(LANES)],
        out_shape=[jax.ShapeDtypeStruct((n, D_MODEL), f32), jax.ShapeDtypeStruct((n, D_MODEL), bf16),
                   jax.ShapeDtypeStruct((n, LANES), f32)],
        compiler_params=_cparams(("parallel",)),
        name="merge_router",
    )(x, att, hsg, gt, wa, wb, wo, g_ffn, w_router, b_router)


EXPERTS_PER_STEP = 4


def _moe_kernel(h_ref, comb_ref, wg_ref, wu_ref, wd_ref, y_ref):
    e0 = pl.program_id(1) * EXPERTS_PER_STEP

    @pl.when(e0 == 0)
    def _():
        y_ref[...] = jnp.zeros(y_ref.shape, f32)

    h = h_ref[...]
    comb = comb_ref[...]
    lane = lax.broadcasted_iota(jnp.int32, comb.shape, 1)
    hidden = []
    for j in range(EXPERTS_PER_STEP):
        gate = jnp.dot(h, wg_ref[j], preferred_element_type=f32)
        up = jnp.dot(h, wu_ref[j], preferred_element_type=f32)
        c = jnp.sum(jnp.where(lane == e0 + j, comb, 0.0), axis=1, keepdims=True)
        hidden.append((gate * _sigmoid(gate) * up * c).astype(bf16))
    y_ref[...] += jnp.dot(jnp.concatenate(hidden, axis=1), wd_ref[...], preferred_element_type=f32)


def _moe(h2, comb, wg, wu, wd, tm):
    n = h2.shape[0]
    wd_rows = wd.reshape(N_EXPERTS * D_EXPERT, D_MODEL)
    per = EXPERTS_PER_STEP
    return pl.pallas_call(
        _moe_kernel,
        grid=(n // tm, N_EXPERTS // per),
        in_specs=[pl.BlockSpec((tm, D_MODEL), lambda i, e: (i, 0)),
                  pl.BlockSpec((tm, LANES), lambda i, e: (i, 0)),
                  pl.BlockSpec((per, D_MODEL, D_EXPERT), lambda i, e: (e, 0, 0)),
                  pl.BlockSpec((per, D_MODEL, D_EXPERT), lambda i, e: (e, 0, 0)),
                  pl.BlockSpec((per * D_EXPERT, D_MODEL), lambda i, e: (e, 0))],
        out_specs=pl.BlockSpec((tm, D_MODEL), lambda i, e: (i, 0)),
        out_shape=jax.ShapeDtypeStruct((n, D_MODEL), f32),
        compiler_params=_cparams(("parallel", "arbitrary")),
        name="moe",
    )(h2, comb, wg, wu, wd_rows)


def _ple_kernel(x1_ref, y_ref, p_ref, gp_ref, wpg_ref, bpg_ref, wpi_ref, gfin_ref, o_ref):
    x2 = x1_ref[...] + y_ref[...]
    hp = _rms(x2, gp_ref[...]).astype(bf16)
    pg = _sigmoid(jnp.dot(hp, wpg_ref[...], preferred_element_type=f32) + bpg_ref[...])
    pin = jnp.dot(p_ref[...].astype(bf16), wpi_ref[...], preferred_element_type=f32)
    x3 = x2 + pg * pin
    o_ref[...] = _rms(x3, gfin_ref[...])


def _ple(x1, y, p, g_ple, wpg, bpg, wpi, g_final, tm):
    n = x1.shape[0]
    row = lambda w: pl.BlockSpec((tm, w), lambda i: (i, 0))
    vec = _const_spec((1, D_MODEL))
    return pl.pallas_call(
        _ple_kernel,
        grid=(n // tm,),
        in_specs=[row(D_MODEL), row(D_MODEL), row(PLE_DIM), vec, _const_spec((D_MODEL, D_MODEL)), vec,
                  _const_spec((PLE_DIM, D_MODEL)), vec],
        out_specs=row(D_MODEL),
        out_shape=jax.ShapeDtypeStruct((n, D_MODEL), f32),
        compiler_params=_cparams(("parallel",)),
        name="ple_final",
    )(x1, y, p, g_ple, wpg, bpg, wpi, g_final)


def _tile(n, pref):
    return pref if n % pref == 0 else n


def _stream(x, p, conv0, h0, cache, li, rel_bias, w):
    nb_, tlen, _ = x.shape
    n = nb_ * tlen
    xf = x.reshape(n, D_MODEL)
    lam_init = 0.8 - 0.6 * math.exp(-0.3 * li)
    tq = None if cache is not None else _tile(tlen, 512)
    q, k, kb, v, vb, lx, gl, gt = _in_proj(xf, w["g_mix"], w["w_in"], w["b_merge"], _tile(n, 256), tq)
    if cache is None:
        att = _attn_prompt(q, kb, vb, rel_bias, w["lam_vecs"], w["g_subln"], lam_init, tq)
    else:
        att = _attn_sample(q, kb, vb, cache[0], cache[1], rel_bias, w["lam_vecs"], w["g_subln"], lam_init,
                           nb_, tlen)
    conv_pad = jnp.pad(conv0, ((0, 0), (SUBLANES - (CONV_W - 1), 0), (0, 0)))
    hsg, cout, hout = _conv_lru(lx, gl, conv_pad, h0[:, None, :], w["w_conv"], w["b_conv"], w["w_rg_a"],
                                w["b_rg_a"], w["w_rg_x"], w["b_rg_x"], w["lru_lambda"], nb_, tlen,
                                _tile(tlen, 256), cache is None)
    tm = _tile(n, 512)
    x1, h2, comb = _merge(xf, att, hsg, gt, w["w_attn_br"], w["w_lru_br"], w["w_out"], w["g_ffn"],
                          w["w_router"], w["b_router"], tm)
    y = _moe(h2, comb, w["w_e_gate"], w["w_e_up"], w["w_e_down"], _tile(n, 1024))
    out = _ple(x1, y, p.reshape(n, PLE_DIM), w["g_ple"], w["w_ple_gate"], w["b_ple_gate"], w["w_ple_in"],
               w["g_final"], tm)
    return out, k, v, cout[:, SUBLANES - (CONV_W - 1):, :], hout[:, 0, :]


def kernel(x_prompt, x_sample, cache_k, cache_v, state_conv, state_lru, p_prompt, p_sample, rel_bias, g_mix, w_in, b_merge, lam_q1, lam_k1, lam_q2, lam_k2, g_subln, w_attn_br, w_conv, b_conv, w_rg_a, b_rg_a, w_rg_x, b_rg_x, lru_lambda, w_lru_br, w_out, g_ffn, w_grp, b_grp, w_rt, b_rt, w_e_gate, w_e_up, w_e_down, g_ple, w_ple_gate, b_ple_gate, w_ple_in, g_final):
    depth = w_in.shape[0]
    assert depth == 1, "the final norm is fused into the single layer's last stage"
    bp, tp, _ = x_prompt.shape
    bs, ts, _ = x_sample.shape
    past = cache_k.shape[2]
    li = 0
    row = lambda a: a[li].reshape(1, -1).astype(f32)
    w_router = jnp.concatenate(
        [jnp.transpose(w_rt[li], (1, 0, 2)).reshape(D_MODEL, N_EXPERTS), w_grp[li]], axis=1).astype(f32)
    w_router = jnp.pad(w_router, ((0, 0), (0, LANES - N_EXPERTS - N_GROUPS)))
    w_router_hi = w_router.astype(bf16)
    w_router = jnp.concatenate([w_router_hi, (w_router - w_router_hi.astype(f32)).astype(bf16)], axis=1)
    b_router = jnp.pad(jnp.concatenate([b_rt[li].reshape(-1), b_grp[li]]).astype(f32),
                       (0, LANES - N_EXPERTS - N_GROUPS)).reshape(1, LANES)
    w = dict(
        g_mix=row(g_mix), w_in=w_in[li].astype(bf16), b_merge=row(b_merge),
        lam_vecs=(row(lam_q1), row(lam_k1), row(lam_q2), row(lam_k2)), g_subln=row(g_subln),
        w_attn_br=w_attn_br[li].astype(bf16), w_conv=w_conv[li].astype(f32), b_conv=row(b_conv),
        w_rg_a=w_rg_a[li].astype(bf16), b_rg_a=row(b_rg_a), w_rg_x=w_rg_x[li].astype(bf16), b_rg_x=row(b_rg_x),
        lru_lambda=row(lru_lambda), w_lru_br=w_lru_br[li].astype(bf16), w_out=w_out[li].astype(bf16),
        g_ffn=row(g_ffn), w_router=w_router, b_router=b_router,
        w_e_gate=w_e_gate[li].astype(bf16), w_e_up=w_e_up[li].astype(bf16), w_e_down=w_e_down[li].astype(bf16),
        g_ple=row(g_ple), w_ple_gate=w_ple_gate[li].astype(bf16), b_ple_gate=row(b_ple_gate),
        w_ple_in=w_ple_in[li].astype(bf16), g_final=g_final.reshape(1, -1).astype(f32),
    )
    conv0 = jnp.zeros((bp, CONV_W - 1, LRU_WIDTH), f32)
    h0 = jnp.zeros((bp, LRU_WIDTH), f32)
    yp, kp, vp, cp, hp = _stream(x_prompt, p_prompt[li], conv0, h0, None, li, rel_bias, w)
    cache = (cache_k[li].reshape(bs, past, QK_COLS), cache_v[li].reshape(bs, past, V_COLS))
    ys, ks, vs, cs, hs = _stream(x_sample, p_sample[li], state_conv[li], state_lru[li], cache, li, rel_bias, w)
    return (yp.reshape(bp, tp, D_MODEL), ys.reshape(bs, ts, D_MODEL),
            kp.reshape(1, bp, tp, N_HEADS, 2, HEAD_DIM), vp.reshape(1, bp, tp, N_HEADS, V_DIM),
            cp[None], hp[None],
            ks.reshape(1, bs, ts, N_HEADS, 2, HEAD_DIM), vs.reshape(1, bs, ts, N_HEADS, V_DIM),
            cs[None], hs[None])
```

```python
import functools
import math

import jax
import jax.numpy as jnp
from jax import lax
from jax.experimental import pallas as pl
from jax.experimental.pallas import tpu as pltpu

f32 = jnp.float32
bf16 = jnp.bfloat16

D_MODEL = 1024
N_HEADS = 8
HEAD_DIM = 64
V_DIM = 128
QK_COLS = N_HEADS * 2 * HEAD_DIM
V_COLS = N_HEADS * V_DIM
LRU_WIDTH = 1024
LRU_BLOCKS = 8
LRU_BLOCK_W = LRU_WIDTH // LRU_BLOCKS
CONV_W = 4
LRU_C = 8.0
CHUNK = 64
N_BUCKETS = 32
MAX_DISTANCE = 128
N_GROUPS = 4
EXPERTS_PER_GROUP = 8
N_EXPERTS = N_GROUPS * EXPERTS_PER_GROUP
D_EXPERT = 256
PLE_DIM = 256
EPS = 1e-6
NEG_INF = -1e30
LOG2E = math.log2(math.e)
ONES_ROWS = 16
LANES = 128
SUBLANES = 8
VMEM_LIMIT = 56 * 1024 * 1024

C_Q, C_K, C_V = 0, QK_COLS, 2 * QK_COLS
C_LX = 2 * QK_COLS + V_COLS
C_LY = C_LX + LRU_WIDTH
C_G = C_LY + LRU_WIDTH
IN_COLS = C_G + 2 * D_MODEL


def _cparams(sem):
    return pltpu.CompilerParams(dimension_semantics=sem, vmem_limit_bytes=VMEM_LIMIT)


def _const_spec(shape):
    nd = len(shape)
    return pl.BlockSpec(shape, lambda *_: (0,) * nd, pipeline_mode=pl.Buffered(1))


def _rms(x, g):
    return x * lax.rsqrt(jnp.mean(x * x, axis=-1, keepdims=True) + EPS) * g


def _sigmoid(x):
    return 1.0 / (1.0 + jnp.exp(-x))


def _gelu_tanh(x):
    c = math.sqrt(2.0 / math.pi)
    return 0.5 * x * (1.0 + jnp.tanh(c * (x + 0.044715 * (x * x * x))))


def _in_proj_kernel(x_ref, g_ref, w_ref, bm_ref, q_ref, k_ref, kb_ref, v_ref, vb_ref, lx_ref, gl_ref, gt_ref,
                    *, transposed_v):
    h = _rms(x_ref[...], g_ref[...]).astype(bf16)

    def proj(lo, width):
        return jnp.dot(h, w_ref[:, lo:lo + width], preferred_element_type=f32)

    q_ref[...] = (proj(C_Q, QK_COLS) * (HEAD_DIM ** -0.5 * LOG2E)).astype(bf16)
    k = proj(C_K, QK_COLS)
    k_ref[...] = k
    kb_ref[...] = k.astype(bf16)
    v = proj(C_V, V_COLS)
    v_ref[...] = v
    if transposed_v:
        for hd in range(N_HEADS):
            vb_ref[hd, :V_DIM, :] = v[:, hd * V_DIM:(hd + 1) * V_DIM].T.astype(bf16)
            vb_ref[hd, V_DIM:, :] = jnp.ones((ONES_ROWS, v.shape[0]), bf16)
    else:
        vb_ref[...] = v.astype(bf16)
    lx_ref[...] = proj(C_LX, LRU_WIDTH)
    gl_ref[...] = _gelu_tanh(proj(C_LY, LRU_WIDTH))
    gt_ref[...] = _sigmoid(proj(C_G, 2 * D_MODEL) + bm_ref[...])


def _in_proj(x, g_mix, w_in_b, b_merge, tm, tq=None):
    n = x.shape[0]
    row = lambda w: pl.BlockSpec((tm, w), lambda i: (i, 0))
    if tq is None:
        vb_shape, vb_spec = jax.ShapeDtypeStruct((n, V_COLS), bf16), row(V_COLS)
    else:
        per = tq // tm
        vb_shape = jax.ShapeDtypeStruct((N_HEADS, n // tq, V_DIM + ONES_ROWS, tq), bf16)
        vb_spec = pl.BlockSpec((N_HEADS, None, V_DIM + ONES_ROWS, tm), lambda i: (0, i // per, 0, i % per))
    outs = [
        jax.ShapeDtypeStruct((n, QK_COLS), bf16),
        jax.ShapeDtypeStruct((n, QK_COLS), f32),
        jax.ShapeDtypeStruct((n, QK_COLS), bf16),
        jax.ShapeDtypeStruct((n, V_COLS), f32),
        vb_shape,
        jax.ShapeDtypeStruct((n, LRU_WIDTH), f32),
        jax.ShapeDtypeStruct((n, LRU_WIDTH), f32),
        jax.ShapeDtypeStruct((n, 2 * D_MODEL), f32),
    ]
    return pl.pallas_call(
        functools.partial(_in_proj_kernel, transposed_v=tq is not None),
        grid=(n // tm,),
        in_specs=[row(D_MODEL), _const_spec((1, D_MODEL)), _const_spec((D_MODEL, IN_COLS)),
                  _const_spec((1, 2 * D_MODEL))],
        out_specs=[row(QK_COLS), row(QK_COLS), row(QK_COLS), row(V_COLS), vb_spec,
                   row(LRU_WIDTH), row(LRU_WIDTH), row(2 * D_MODEL)],
        out_shape=outs,
        compiler_params=_cparams(("parallel",)),
        name="in_proj",
    )(x, g_mix, w_in_b, b_merge)


def _rel_bucket(rel):
    half = N_BUCKETS // 2
    max_exact = half // 2
    ret = jnp.where(rel > 0, half, 0)
    n = jnp.abs(rel)
    nf = jnp.maximum(n, 1).astype(f32)
    large = max_exact + (jnp.log(nf / max_exact) / math.log(MAX_DISTANCE / max_exact)
                         * (half - max_exact)).astype(jnp.int32)
    large = jnp.minimum(large, half - 1)
    return ret + jnp.where(n < max_exact, n, large)


def _toeplitz(t, nq, nk):
    nh, length = t.shape
    a = jnp.broadcast_to(t[:, None, :], (nh, nq, length))
    a = jnp.pad(a, ((0, 0), (0, 0), (0, 1))).reshape(nh, nq * (length + 1))
    a = a[:, :nq * length].reshape(nh, nq, length)
    return a[:, :, nq - 1:nq - 1 + nk]


def _bias_table(rel_bias, qpos0, nq, kpos0, nk, shift):
    rel = jnp.arange(-(nq - 1), nk) + (kpos0 - qpos0)
    t = (rel_bias.astype(f32)[_rel_bucket(rel)].T - shift[:, None]) * LOG2E
    qpos = qpos0 + jnp.arange(nq)
    kpos = kpos0 + jnp.arange(nk)
    mask = (kpos[None, :] // CHUNK) <= (qpos[:, None] // CHUNK)
    return jnp.where(mask[None], _toeplitz(t, nq, nk), NEG_INF)


def _bias_ring(rel_bias, offset, tq, shift):
    r = jnp.arange(2 * tq)
    x = jnp.where(r < tq, r, r - 2 * tq)
    t = (rel_bias.astype(f32)[_rel_bucket(offset - x)].T - shift[:, None]) * LOG2E
    return t[:, None, :]


def _lam_value(lq1, lk1, lq2, lk2, lam_init):
    s1 = jnp.sum(lq1[...] * lk1[...], axis=-1, keepdims=True)
    s2 = jnp.sum(lq2[...] * lk2[...], axis=-1, keepdims=True)
    return jnp.exp(s1) - jnp.exp(s2) + lam_init


def _split_maps(q):
    lane = lax.broadcasted_iota(jnp.int32, q.shape, 1)
    zero = jnp.zeros_like(q)
    return jnp.where(lane < HEAD_DIM, q, zero), jnp.where(lane >= HEAD_DIM, q, zero)


def _qk(qz, kt):
    return lax.dot_general(qz, kt, (((1,), (1,)), ((), ())), preferred_element_type=f32)


def _finish_heads(o1, o2, lam, gs, lam_init):
    att = o1 - lam * o2
    return _rms(att, gs) * (1.0 - lam_init)


def _attn_prompt_kernel(q_ref, k_ref, vt_ref, ring_ref, lq1, lk1, lq2, lk2, gs_ref, o_ref,
                        nb_sc, s_sc, p_sc, al_sc, tm_sc, m_sc, acc_sc, *, tq, lam_init):
    i = pl.program_id(1)

    def bias_tile(ring):
        return pltpu.roll(jnp.broadcast_to(ring, (tq, 2 * tq)), 0, 1, stride=1, stride_axis=0)[:, :tq]

    @pl.when(i == 0)
    def _():
        kk = lax.broadcasted_iota(jnp.int32, (tq, tq), 0)
        qq = lax.broadcasted_iota(jnp.int32, (tq, tq), 1)
        nb_sc[0] = jnp.where((kk // CHUNK) <= (qq // CHUNK), bias_tile(ring_ref[1]), NEG_INF)
        nb_sc[1] = jnp.full((tq, tq), NEG_INF, f32)

    @pl.when(i == 1)
    def _():
        nb_sc[1] = nb_sc[0]
        nb_sc[0] = bias_tile(ring_ref[0])

    qt = q_ref[...].astype(f32).T
    row = lax.broadcasted_iota(jnp.int32, qt.shape, 0)
    qz = (jnp.where(row < HEAD_DIM, qt, 0.0).astype(bf16), jnp.where(row >= HEAD_DIM, qt, 0.0).astype(bf16))

    m_sc[...] = jnp.full(m_sc.shape, NEG_INF, f32)
    acc_sc[...] = jnp.zeros(acc_sc.shape, f32)

    def qk(slot, tile, bias):
        kt = k_ref[pl.ds(pl.multiple_of(tile * tq, tq), tq), :]
        for mi in range(2):
            s = jnp.dot(kt, qz[mi], preferred_element_type=f32)
            s = s if bias is None else s + bias
            s_sc[mi, slot] = s
            tm_sc[mi, slot] = jnp.max(s, axis=0, keepdims=True)

    def softmax(slot):
        for mi in range(2):
            s = s_sc[mi, slot]
            m_prev = m_sc[mi]
            m_new = jnp.maximum(m_prev, tm_sc[mi, slot])
            al_sc[mi, slot] = jnp.exp2(m_prev - m_new)
            p_sc[mi, slot] = jnp.exp2(s - m_new).astype(bf16)
            m_sc[mi] = m_new

    def pv(slot, vtile):
        va = vt_ref[vtile]
        for mi in range(2):
            acc_sc[mi] = acc_sc[mi] * al_sc[mi, slot] + jnp.dot(va, p_sc[mi, slot], preferred_element_type=f32)

    near0 = jnp.maximum(i - 1, 0)
    n_far = near0
    n_pairs = (n_far + 1) // 2

    def far_k(f):
        return jnp.clip(f, 0, jnp.maximum(n_far - 1, 0))

    qk(0, near0, nb_sc[0])
    qk(1, near0 + 1, nb_sc[1])
    softmax(0)
    qk(0, far_k(0), None)
    softmax(1)
    pv(0, near0)

    def far_pair(p, carry):
        f0 = 2 * p
        qk(1, far_k(f0 + 1), None)
        softmax(0)
        pv(1, jnp.where(p == 0, near0 + 1, f0 - 1))
        qk(0, far_k(f0 + 2), None)
        softmax(1)
        pv(0, f0)
        return carry

    lax.fori_loop(0, n_pairs, far_pair, 0)

    @pl.when(n_far % 2 == 0)
    def _():
        pv(1, jnp.where(n_pairs > 0, 2 * n_pairs - 1, near0 + 1))

    a1 = acc_sc[0]
    a2 = acc_sc[1]
    o1 = a1[:V_DIM] / a1[V_DIM:V_DIM + 1]
    o2 = a2[:V_DIM] / a2[V_DIM:V_DIM + 1]
    lam = _lam_value(lq1, lk1, lq2, lk2, lam_init)
    att = (o1 - lam * o2).T
    o_ref[...] = (_rms(att, gs_ref[...]) * (1.0 - lam_init)).astype(o_ref.dtype)


def _attn_prompt(q, kb, vt, rel_bias, lam_vecs, g_subln, lam_init, tq):
    t = q.shape[0]
    nq = t // tq
    far = rel_bias.astype(f32)[_rel_bucket(jnp.asarray(-4 * MAX_DISTANCE))]
    rings = jnp.stack([_bias_ring(rel_bias, -tq, tq, far), _bias_ring(rel_bias, 0, tq, far)], axis=1)
    vec = _const_spec((1, HEAD_DIM))
    return pl.pallas_call(
        functools.partial(_attn_prompt_kernel, tq=tq, lam_init=lam_init),
        grid=(N_HEADS, nq),
        in_specs=[
            pl.BlockSpec((tq, 2 * HEAD_DIM), lambda h, i: (i, h)),
            pl.BlockSpec((t, 2 * HEAD_DIM), lambda h, i: (0, h)),
            pl.BlockSpec((None, nq, V_DIM + ONES_ROWS, tq), lambda h, i: (h, 0, 0, 0)),
            pl.BlockSpec((None, 2, 1, 2 * tq), lambda h, i: (h, 0, 0, 0)),
            vec, vec, vec, vec, _const_spec((1, V_DIM)),
        ],
        out_specs=pl.BlockSpec((tq, V_DIM), lambda h, i: (i, h)),
        out_shape=jax.ShapeDtypeStruct((t, V_COLS), bf16),
        scratch_shapes=[pltpu.VMEM((2, tq, tq), f32), pltpu.VMEM((2, 2, tq, tq), f32),
                        pltpu.VMEM((2, 2, tq, tq), bf16), pltpu.VMEM((2, 2, 1, tq), f32),
                        pltpu.VMEM((2, 2, 1, tq), f32), pltpu.VMEM((2, 1, tq), f32), pltpu.VMEM((2, V_DIM + ONES_ROWS, tq), f32)],
        compiler_params=_cparams(("arbitrary", "arbitrary")),
        name="attn_prompt",
    )(q, kb, vt, rings, *lam_vecs, g_subln)


def _attn_sample_kernel(q_ref, kp_ref, vp_ref, kn_ref, vn_ref, nbp_ref, nbn_ref, lq1, lk1, lq2, lk2, gs_ref,
                        o_ref, *, lam_init):
    lam = _lam_value(lq1, lk1, lq2, lk2, lam_init)
    for hd in range(N_HEADS):
        qk_cols = slice(hd * 2 * HEAD_DIM, (hd + 1) * 2 * HEAD_DIM)
        v_cols = slice(hd * V_DIM, (hd + 1) * V_DIM)
        qz = _split_maps(q_ref[:, qk_cols])
        kp = kp_ref[:, qk_cols].astype(bf16)
        vp = vp_ref[:, v_cols].astype(bf16)
        kn = kn_ref[:, qk_cols]
        vn = vn_ref[:, v_cols]
        outs = []
        for mi in range(2):
            sp = _qk(qz[mi], kp) + nbp_ref[hd]
            sn = _qk(qz[mi], kn) + nbn_ref[hd]
            m = jnp.maximum(jnp.max(sp, axis=1, keepdims=True), jnp.max(sn, axis=1, keepdims=True))
            pp = jnp.exp2(sp - m).astype(bf16)
            pn = jnp.exp2(sn - m).astype(bf16)
            den = (jnp.sum(pp.astype(f32), axis=1, keepdims=True)
                   + jnp.sum(pn.astype(f32), axis=1, keepdims=True))
            num = jnp.dot(pp, vp, preferred_element_type=f32) + jnp.dot(pn, vn, preferred_element_type=f32)
            outs.append(num / den)
        o_ref[:, v_cols] = _finish_heads(outs[0], outs[1], lam, gs_ref[...], lam_init).astype(o_ref.dtype)


def _attn_sample(q, kb, vb, cache_k, cache_v, rel_bias, lam_vecs, g_subln, lam_init, nb_, ts):
    past = cache_k.shape[1]
    zero = jnp.zeros((N_HEADS,), f32)
    nbp = _bias_table(rel_bias, past, ts, 0, past, zero)
    nbn = _bias_table(rel_bias, past, ts, past, ts, zero)
    vec = _const_spec((1, HEAD_DIM))
    return pl.pallas_call(
        functools.partial(_attn_sample_kernel, lam_init=lam_init),
        grid=(nb_,),
        in_specs=[
            pl.BlockSpec((ts, QK_COLS), lambda b: (b, 0)),
            pl.BlockSpec((None, past, QK_COLS), lambda b: (b, 0, 0)),
            pl.BlockSpec((None, past, V_COLS), lambda b: (b, 0, 0)),
            pl.BlockSpec((ts, QK_COLS), lambda b: (b, 0)),
            pl.BlockSpec((ts, V_COLS), lambda b: (b, 0)),
            _const_spec((N_HEADS, ts, past)), _const_spec((N_HEADS, ts, ts)),
            vec, vec, vec, vec, _const_spec((1, V_DIM)),
        ],
        out_specs=pl.BlockSpec((ts, V_COLS), lambda b: (b, 0)),
        out_shape=jax.ShapeDtypeStruct((nb_ * ts, V_COLS), bf16),
        compiler_params=_cparams(("parallel",)),
        name="attn_sample",
    )(q, cache_k, cache_v, kb, vb, nbp, nbn, *lam_vecs, g_subln)


def _conv_lru_kernel(lx_ref, gl_ref, c0_ref, h0_ref, wc_ref, bc_ref, wa_ref, ba_ref, wx_ref, bx_ref, lam_ref,
                     hsg_ref, cout_ref, hout_ref, xbuf, a_sc, b_sc, hcar, *, tt, first_pos_is_zero):
    t = pl.program_id(1)

    @pl.when(t == 0)
    def _():
        xbuf[0:SUBLANES] = c0_ref[...]
        hcar[...] = jnp.broadcast_to(h0_ref[...], hcar.shape)

    xbuf[SUBLANES:SUBLANES + tt] = lx_ref[...]
    xc = jnp.broadcast_to(bc_ref[...], (tt, LRU_WIDTH))
    for j in range(CONV_W):
        lo = SUBLANES - (CONV_W - 1) + j
        xc = xc + xbuf[lo:lo + tt, :] * wc_ref[j:j + 1, :]
    tail = xbuf[tt:tt + SUBLANES, :]
    xbuf[0:SUBLANES] = tail
    cout_ref[...] = tail

    xcb = xc.astype(bf16)

    def block_diag(w_ref, b_ref):
        cols = [jnp.dot(xcb[:, n * LRU_BLOCK_W:(n + 1) * LRU_BLOCK_W], w_ref[n], preferred_element_type=f32)
                for n in range(LRU_BLOCKS)]
        return jnp.concatenate(cols, axis=1) + b_ref[...]

    r = _sigmoid(block_diag(wa_ref, ba_ref))
    ig = _sigmoid(block_diag(wx_ref, bx_ref))
    z = -lam_ref[...]
    softplus = jnp.maximum(z, 0.0) + jnp.log1p(jnp.exp(-jnp.abs(z)))
    log_a = -LRU_C * r * softplus
    a = jnp.exp(log_a)
    th = jnp.tanh(log_a)
    m2 = -2.0 * th / (1.0 - th)
    mult = jnp.where(m2 > 0.0, m2 * lax.rsqrt(m2), 0.0)
    if first_pos_is_zero:
        row = lax.broadcasted_iota(jnp.int32, (tt, LRU_WIDTH), 0)
        mult = jnp.where((row == 0) & (t == 0), 1.0, mult)
    bv = mult * ig * xc

    groups = (tt // SUBLANES, SUBLANES, LRU_WIDTH)
    a = a.reshape(groups)
    bv = bv.reshape(groups)
    sub = lax.broadcasted_iota(jnp.int32, groups, 1)
    d = 1
    while d < SUBLANES:
        valid = sub >= d
        a_s = pltpu.roll(a, d, axis=1)
        b_s = pltpu.roll(bv, d, axis=1)
        bv = jnp.where(valid, a * b_s + bv, bv)
        a = jnp.where(valid, a * a_s, a)
        d *= 2
    a_sc[...] = a.reshape(tt, LRU_WIDTH)
    b_sc[...] = bv.reshape(tt, LRU_WIDTH)

    def group(g, h):
        sl = pl.ds(pl.multiple_of(g * SUBLANES, SUBLANES), SUBLANES)
        hg = a_sc[sl, :] * h + b_sc[sl, :]
        b_sc[sl, :] = hg
        return jnp.broadcast_to(hg[SUBLANES - 1:SUBLANES, :], hg.shape)

    h = lax.fori_loop(0, tt // SUBLANES, group, hcar[...], unroll=4)
    hcar[...] = h
    hout_ref[...] = h[0:1, :]
    hsg_ref[...] = (b_sc[...] * gl_ref[...]).astype(hsg_ref.dtype)


def _conv_lru(lx, gl, conv0, h0, w_conv, b_conv, wa, ba, wx, bx, lru_lambda, nb_, tlen, tt, first_pos_is_zero):
    nt = tlen // tt
    rows = pl.BlockSpec((tt, LRU_WIDTH), lambda b, t: (b * nt + t, 0))
    vecw = _const_spec((1, LRU_WIDTH))
    wblk = _const_spec((LRU_BLOCKS, LRU_BLOCK_W, LRU_BLOCK_W))
    return pl.pallas_call(
        functools.partial(_conv_lru_kernel, tt=tt, first_pos_is_zero=first_pos_is_zero),
        grid=(nb_, nt),
        in_specs=[rows, rows,
                  pl.BlockSpec((None, SUBLANES, LRU_WIDTH), lambda b, t: (b, 0, 0)),
                  pl.BlockSpec((None, 1, LRU_WIDTH), lambda b, t: (b, 0, 0)),
                  _const_spec((CONV_W, LRU_WIDTH)), vecw, wblk, vecw, wblk, vecw, vecw],
        out_specs=[rows,
                   pl.BlockSpec((None, SUBLANES, LRU_WIDTH), lambda b, t: (b, 0, 0)),
                   pl.BlockSpec((None, 1, LRU_WIDTH), lambda b, t: (b, 0, 0))],
        out_shape=[jax.ShapeDtypeStruct((nb_ * tlen, LRU_WIDTH), bf16),
                   jax.ShapeDtypeStruct((nb_, SUBLANES, LRU_WIDTH), f32),
                   jax.ShapeDtypeStruct((nb_, 1, LRU_WIDTH), f32)],
        scratch_shapes=[pltpu.VMEM((tt + SUBLANES, LRU_WIDTH), f32), pltpu.VMEM((tt, LRU_WIDTH), f32),
                        pltpu.VMEM((tt, LRU_WIDTH), f32), pltpu.VMEM((SUBLANES, LRU_WIDTH), f32)],
        compiler_params=_cparams(("arbitrary", "arbitrary")),
        name="conv_lru",
    )(lx, gl, conv0, h0, w_conv, b_conv, wa, ba, wx, bx, lru_lambda)


def _merge_kernel(x_ref, att_ref, hsg_ref, gt_ref, wa_ref, wb_ref, wo_ref, gf_ref, wr_ref, br_ref,
                  x1_ref, h2_ref, comb_ref):
    bra = jnp.dot(att_ref[...], wa_ref[...], preferred_element_type=f32)
    brb = jnp.dot(hsg_ref[...], wb_ref[...], preferred_element_type=f32)
    gt = gt_ref[...]
    m = (gt[:, :D_MODEL] * bra + gt[:, D_MODEL:] * brb).astype(bf16)
    x1 = x_ref[...] + jnp.dot(m, wo_ref[...], preferred_element_type=f32)
    x1_ref[...] = x1
    h2 = _rms(x1, gf_ref[...])
    h2_hi = h2.astype(bf16)
    h2_ref[...] = h2_hi

    h2_lo = (h2 - h2_hi.astype(f32)).astype(bf16)
    parts = (jnp.dot(h2_hi, wr_ref[...], preferred_element_type=f32)
             + jnp.dot(h2_lo, wr_ref[...], preferred_element_type=f32))
    logits = parts[:, :LANES] + parts[:, LANES:] + br_ref[...]
    lane = lax.broadcasted_iota(jnp.int32, logits.shape, 1)
    lanef = lane.astype(f32)
    low = jnp.float32(-3.0e38)
    is_grp = (lane >= N_EXPERTS) & (lane < N_EXPERTS + N_GROUPS)
    gl = jnp.where(is_grp, logits, low)
    gmax = jnp.max(gl, axis=1, keepdims=True)
    gidx = jnp.min(jnp.where(is_grp & (gl == gmax), lanef, 1.0e3), axis=1, keepdims=True) - N_EXPERTS
    gden = jnp.sum(jnp.where(is_grp, jnp.exp(gl - gmax), 0.0), axis=1, keepdims=True)
    g_w = 1.0 / gden
    lo = gidx * EXPERTS_PER_GROUP
    in_sel = (lanef >= lo) & (lanef < lo + EXPERTS_PER_GROUP)
    sel = jnp.where(in_sel, logits, low)
    v1 = jnp.max(sel, axis=1, keepdims=True)
    i1 = jnp.min(jnp.where(in_sel & (sel == v1), lanef, 1.0e3), axis=1, keepdims=True)
    in_sel2 = in_sel & (lanef != i1)
    sel2 = jnp.where(in_sel2, logits, low)
    v2 = jnp.max(sel2, axis=1, keepdims=True)
    i2 = jnp.min(jnp.where(in_sel2 & (sel2 == v2), lanef, 1.0e3), axis=1, keepdims=True)
    e2 = jnp.exp(v2 - v1)
    w1 = g_w / (1.0 + e2)
    w2 = g_w * e2 / (1.0 + e2)
    comb_ref[...] = jnp.where(lanef == i1, w1, 0.0) + jnp.where(lanef == i2, w2, 0.0)


def _merge(x, att, hsg, gt, wa, wb, wo, g_ffn, w_router, b_router, tm):
    n = x.shape[0]
    row = lambda w: pl.BlockSpec((tm, w), lambda i: (i, 0))
    sq = _const_spec((D_MODEL, D_MODEL))
    return pl.pallas_call(
        _merge_kernel,
        grid=(n // tm,),
        in_specs=[row(D_MODEL), row(V_COLS), row(LRU_WIDTH), row(2 * D_MODEL), sq, sq, sq,
                  _const_spec((1, D_MODEL)), _const_spec((D_MODEL, 2 * LANES)), _const_spec((1, LANES))],
        out_specs=[row(D_MODEL), row(D_MODEL), row(LANES)],
        out_shape=[jax.ShapeDtypeStruct((n, D_MODEL), f32), jax.ShapeDtypeStruct((n, D_MODEL), bf16),
                   jax.ShapeDtypeStruct((n, LANES), f32)],
        compiler_params=_cparams(("parallel",)),
        name="merge_router",
    )(x, att, hsg, gt, wa, wb, wo, g_ffn, w_router, b_router)


EXPERTS_PER_STEP = 4


MOE_CHUNK = 128


def _moe_kernel(h_ref, comb_ref, wg_ref, wu_ref, wd_ref, y_ref, pt_sc, x_sc, c_sc, ys_sc, seg_sc):
    e = pl.program_id(1)
    tm = h_ref.shape[0]
    lane1 = lax.broadcasted_iota(jnp.int32, (1, LANES), 1)

    @pl.when(e == 0)
    def _():
        comb = comb_ref[...]
        ex = lax.broadcasted_iota(jnp.int32, (LANES, LANES), 0)
        gr = lax.broadcasted_iota(jnp.int32, (LANES, LANES), 1)
        in_group = jnp.where((ex // EXPERTS_PER_GROUP == gr) & (ex < N_EXPERTS), 1.0, 0.0).astype(bf16)
        gsum = jnp.dot(comb.astype(bf16), in_group, preferred_element_type=f32)
        member = jnp.where(gsum > 0.0, 1.0, 0.0)
        r_i = lax.broadcasted_iota(jnp.int32, (tm, tm), 0)
        c_i = lax.broadcasted_iota(jnp.int32, (tm, tm), 1)
        tri = jnp.where(c_i <= r_i, 1.0, 0.0).astype(bf16)
        member_b = member.astype(bf16)
        half = tm // 2
        rank = (jnp.dot(tri[:, :half], member_b[:half], preferred_element_type=f32)
                + jnp.dot(tri[:, half:], member_b[half:], preferred_element_type=f32))
        totals = rank[tm - 1:tm, :]
        offs = jnp.zeros((1, LANES), f32)
        for g in range(1, N_GROUPS):
            prev = jnp.sum(jnp.where(lane1 == g - 1, totals, 0.0), axis=1, keepdims=True)
            offs = offs + jnp.where(lane1 >= g, prev, 0.0)
        seg_sc[0:1, :] = offs.astype(jnp.int32)
        seg_sc[1:2, :] = totals.astype(jnp.int32)
        pos = jnp.sum(member * (offs + rank - 1.0), axis=1, keepdims=True)
        pos_row = jnp.broadcast_to(pos, (tm, LANES)).T[0:1, :].astype(jnp.int32)
        perm = jnp.where(r_i == pos_row, 1.0, 0.0).astype(bf16)
        pt_sc[...] = jnp.where(c_i == pos.astype(jnp.int32), 1.0, 0.0).astype(bf16)
        comb_hi = comb.astype(bf16)
        comb_lo = (comb - comb_hi.astype(f32)).astype(bf16)
        srt = jnp.dot(perm, jnp.concatenate([h_ref[...], comb_hi, comb_lo], axis=1), preferred_element_type=f32)
        x_sc[...] = srt[:, :D_MODEL].astype(bf16)
        c_sc[...] = srt[:, D_MODEL:D_MODEL + LANES] + srt[:, D_MODEL + LANES:]
        ys_sc[...] = jnp.zeros(ys_sc.shape, f32)

    g = (e * EXPERTS_PER_STEP) // EXPERTS_PER_GROUP
    off = jnp.sum(jnp.where(lane1 == g, seg_sc[0:1, :], 0))
    cnt = jnp.sum(jnp.where(lane1 == g, seg_sc[1:2, :], 0))
    lane = lax.broadcasted_iota(jnp.int32, (MOE_CHUNK, LANES), 1)

    def chunk(k, carry):
        rows = pl.ds(pl.multiple_of(k * MOE_CHUNK, MOE_CHUNK), MOE_CHUNK)
        x = x_sc[rows, :]
        cw = c_sc[rows, :]
        hidden = []
        for j in range(EXPERTS_PER_STEP):
            gate = jnp.dot(x, wg_ref[j], preferred_element_type=f32)
            up = jnp.dot(x, wu_ref[j], preferred_element_type=f32)
            c = jnp.sum(jnp.where(lane == e * EXPERTS_PER_STEP + j, cw, 0.0), axis=1, keepdims=True)
            hidden.append((gate * _sigmoid(gate) * up * c).astype(bf16))
        ys_sc[rows, :] += jnp.dot(jnp.concatenate(hidden, axis=1), wd_ref[...], preferred_element_type=f32)
        return carry

    lax.fori_loop(off // MOE_CHUNK, (off + cnt + MOE_CHUNK - 1) // MOE_CHUNK, chunk, 0)

    @pl.when(e == pl.num_programs(1) - 1)
    def _():
        y_ref[...] = jnp.dot(pt_sc[...], ys_sc[...].astype(bf16), preferred_element_type=f32)


def _moe(h2, comb, wg, wu, wd, tm):
    n = h2.shape[0]
    wd_rows = wd.reshape(N_EXPERTS * D_EXPERT, D_MODEL)
    per = EXPERTS_PER_STEP
    assert EXPERTS_PER_GROUP % per == 0 and tm % MOE_CHUNK == 0
    return pl.pallas_call(
        _moe_kernel,
        grid=(n // tm, N_EXPERTS // per),
        in_specs=[pl.BlockSpec((tm, D_MODEL), lambda i, e: (i, 0)),
                  pl.BlockSpec((tm, LANES), lambda i, e: (i, 0)),
                  pl.BlockSpec((per, D_MODEL, D_EXPERT), lambda i, e: (e, 0, 0)),
                  pl.BlockSpec((per, D_MODEL, D_EXPERT), lambda i, e: (e, 0, 0)),
                  pl.BlockSpec((per * D_EXPERT, D_MODEL), lambda i, e: (e, 0))],
        out_specs=pl.BlockSpec((tm, D_MODEL), lambda i, e: (i, 0)),
        out_shape=jax.ShapeDtypeStruct((n, D_MODEL), f32),
        scratch_shapes=[pltpu.VMEM((tm, tm), bf16), pltpu.VMEM((tm, D_MODEL), bf16),
                        pltpu.VMEM((tm, LANES), f32), pltpu.VMEM((tm, D_MODEL), f32),
                        pltpu.VMEM((SUBLANES, LANES), jnp.int32)],
        compiler_params=_cparams(("arbitrary", "arbitrary")),
        name="moe",
    )(h2, comb, wg, wu, wd_rows)


def _ple_kernel(x1_ref, y_ref, p_ref, gp_ref, wpg_ref, bpg_ref, wpi_ref, gfin_ref, o_ref):
    x2 = x1_ref[...] + y_ref[...]
    hp = _rms(x2, gp_ref[...]).astype(bf16)
    pg = _sigmoid(jnp.dot(hp, wpg_ref[...], preferred_element_type=f32) + bpg_ref[...])
    pin = jnp.dot(p_ref[...].astype(bf16), wpi_ref[...], preferred_element_type=f32)
    x3 = x2 + pg * pin
    o_ref[...] = _rms(x3, gfin_ref[...])


def _ple(x1, y, p, g_ple, wpg, bpg, wpi, g_final, tm):
    n = x1.shape[0]
    row = lambda w: pl.BlockSpec((tm, w), lambda i: (i, 0))
    vec = _const_spec((1, D_MODEL))
    return pl.pallas_call(
        _ple_kernel,
        grid=(n // tm,),
        in_specs=[row(D_MODEL), row(D_MODEL), row(PLE_DIM), vec, _const_spec((D_MODEL, D_MODEL)), vec,
                  _const_spec((PLE_DIM, D_MODEL)), vec],
        out_specs=row(D_MODEL),
        out_shape=jax.ShapeDtypeStruct((n, D_MODEL), f32),
        compiler_params=_cparams(("parallel",)),
        name="ple_final",
    )(x1, y, p, g_ple, wpg, bpg, wpi, g_final)


def _tile(n, pref):
    return pref if n % pref == 0 else n


def _stream(x, p, conv0, h0, cache, li, rel_bias, w):
    nb_, tlen, _ = x.shape
    n = nb_ * tlen
    xf = x.reshape(n, D_MODEL)
    lam_init = 0.8 - 0.6 * math.exp(-0.3 * li)
    tq = None if cache is not None else _tile(tlen, 512)
    q, k, kb, v, vb, lx, gl, gt = _in_proj(xf, w["g_mix"], w["w_in"], w["b_merge"], _tile(n, 256), tq)
    if cache is None:
        att = _attn_prompt(q, kb, vb, rel_bias, w["lam_vecs"], w["g_subln"], lam_init, tq)
    else:
        att = _attn_sample(q, kb, vb, cache[0], cache[1], rel_bias, w["lam_vecs"], w["g_subln"], lam_init,
                           nb_, tlen)
    conv_pad = jnp.pad(conv0, ((0, 0), (SUBLANES - (CONV_W - 1), 0), (0, 0)))
    hsg, cout, hout = _conv_lru(lx, gl, conv_pad, h0[:, None, :], w["w_conv"], w["b_conv"], w["w_rg_a"],
                                w["b_rg_a"], w["w_rg_x"], w["b_rg_x"], w["lru_lambda"], nb_, tlen,
                                _tile(tlen, 256), cache is None)
    tm = _tile(n, 512)
    x1, h2, comb = _merge(xf, att, hsg, gt, w["w_attn_br"], w["w_lru_br"], w["w_out"], w["g_ffn"],
                          w["w_router"], w["b_router"], tm)
    y = _moe(h2, comb, w["w_e_gate"], w["w_e_up"], w["w_e_down"], _tile(n, 1024))
    out = _ple(x1, y, p.reshape(n, PLE_DIM), w["g_ple"], w["w_ple_gate"], w["b_ple_gate"], w["w_ple_in"],
               w["g_final"], tm)
    return out, k, v, cout[:, SUBLANES - (CONV_W - 1):, :], hout[:, 0, :]


def kernel(x_prompt, x_sample, cache_k, cache_v, state_conv, state_lru, p_prompt, p_sample, rel_bias, g_mix, w_in, b_merge, lam_q1, lam_k1, lam_q2, lam_k2, g_subln, w_attn_br, w_conv, b_conv, w_rg_a, b_rg_a, w_rg_x, b_rg_x, lru_lambda, w_lru_br, w_out, g_ffn, w_grp, b_grp, w_rt, b_rt, w_e_gate, w_e_up, w_e_down, g_ple, w_ple_gate, b_ple_gate, w_ple_in, g_final):
    depth = w_in.shape[0]
    assert depth == 1, "the final norm is fused into the single layer's last stage"
    bp, tp, _ = x_prompt.shape
    bs, ts, _ = x_sample.shape
    past = cache_k.shape[2]
    li = 0
    row = lambda a: a[li].reshape(1, -1).astype(f32)
    w_router = jnp.concatenate(
        [jnp.transpose(w_rt[li], (1, 0, 2)).reshape(D_MODEL, N_EXPERTS), w_grp[li]], axis=1).astype(f32)
    w_router = jnp.pad(w_router, ((0, 0), (0, LANES - N_EXPERTS - N_GROUPS)))
    w_router_hi = w_router.astype(bf16)
    w_router = jnp.concatenate([w_router_hi, (w_router - w_router_hi.astype(f32)).astype(bf16)], axis=1)
    b_router = jnp.pad(jnp.concatenate([b_rt[li].reshape(-1), b_grp[li]]).astype(f32),
                       (0, LANES - N_EXPERTS - N_GROUPS)).reshape(1, LANES)
    w = dict(
        g_mix=row(g_mix), w_in=w_in[li].astype(bf16), b_merge=row(b_merge),
        lam_vecs=(row(lam_q1), row(lam_k1), row(lam_q2), row(lam_k2)), g_subln=row(g_subln),
        w_attn_br=w_attn_br[li].astype(bf16), w_conv=w_conv[li].astype(f32), b_conv=row(b_conv),
        w_rg_a=w_rg_a[li].astype(bf16), b_rg_a=row(b_rg_a), w_rg_x=w_rg_x[li].astype(bf16), b_rg_x=row(b_rg_x),
        lru_lambda=row(lru_lambda), w_lru_br=w_lru_br[li].astype(bf16), w_out=w_out[li].astype(bf16),
        g_ffn=row(g_ffn), w_router=w_router, b_router=b_router,
        w_e_gate=w_e_gate[li].astype(bf16), w_e_up=w_e_up[li].astype(bf16), w_e_down=w_e_down[li].astype(bf16),
        g_ple=row(g_ple), w_ple_gate=w_ple_gate[li].astype(bf16), b_ple_gate=row(b_ple_gate),
        w_ple_in=w_ple_in[li].astype(bf16), g_final=g_final.reshape(1, -1).astype(f32),
    )
    conv0 = jnp.zeros((bp, CONV_W - 1, LRU_WIDTH), f32)
    h0 = jnp.zeros((bp, LRU_WIDTH), f32)
    yp, kp, vp, cp, hp = _stream(x_prompt, p_prompt[li], conv0, h0, None, li, rel_bias, w)
    cache = (cache_k[li].reshape(bs, past, QK_COLS), cache_v[li].reshape(bs, past, V_COLS))
    ys, ks, vs, cs, hs = _stream(x_sample, p_sample[li], state_conv[li], state_lru[li], cache, li, rel_bias, w)
    return (yp.reshape(bp, tp, D_MODEL), ys.reshape(bs, ts, D_MODEL),
            kp.reshape(1, bp, tp, N_HEADS, 2, HEAD_DIM), vp.reshape(1, bp, tp, N_HEADS, V_DIM),
            cp[None], hp[None],
            ks.reshape(1, bs, ts, N_HEADS, 2, HEAD_DIM), vs.reshape(1, bs, ts, N_HEADS, V_DIM),
            cs[None], hs[None])
```

```python
import functools
import math

import jax
import jax.numpy as jnp
from jax import lax
from jax.experimental import pallas as pl
from jax.experimental.pallas import tpu as pltpu

f32 = jnp.float32
bf16 = jnp.bfloat16

D_MODEL = 1024
N_HEADS = 8
HEAD_DIM = 64
V_DIM = 128
QK_COLS = N_HEADS * 2 * HEAD_DIM
V_COLS = N_HEADS * V_DIM
LRU_WIDTH = 1024
LRU_BLOCKS = 8
LRU_BLOCK_W = LRU_WIDTH // LRU_BLOCKS
CONV_W = 4
LRU_C = 8.0
CHUNK = 64
N_BUCKETS = 32
MAX_DISTANCE = 128
N_GROUPS = 4
EXPERTS_PER_GROUP = 8
N_EXPERTS = N_GROUPS * EXPERTS_PER_GROUP
D_EXPERT = 256
PLE_DIM = 256
EPS = 1e-6
NEG_INF = -1e30
LOG2E = math.log2(math.e)
ONES_ROWS = 16
LANES = 128
SUBLANES = 8
VMEM_LIMIT = 56 * 1024 * 1024

C_Q, C_K, C_V = 0, QK_COLS, 2 * QK_COLS
C_LX = 2 * QK_COLS + V_COLS
C_LY = C_LX + LRU_WIDTH
C_G = C_LY + LRU_WIDTH
IN_COLS = C_G + 2 * D_MODEL


def _cparams(sem):
    return pltpu.CompilerParams(dimension_semantics=sem, vmem_limit_bytes=VMEM_LIMIT)


def _const_spec(shape):
    nd = len(shape)
    return pl.BlockSpec(shape, lambda *_: (0,) * nd, pipeline_mode=pl.Buffered(1))


def _rms(x, g):
    return x * lax.rsqrt(jnp.mean(x * x, axis=-1, keepdims=True) + EPS) * g


def _sigmoid(x):
    return 1.0 / (1.0 + jnp.exp(-x))


def _gelu_tanh(x):
    c = math.sqrt(2.0 / math.pi)
    return 0.5 * x * (1.0 + jnp.tanh(c * (x + 0.044715 * (x * x * x))))


def _in_proj_kernel(x_ref, g_ref, w_ref, bm_ref, q_ref, k_ref, kb_ref, v_ref, vb_ref, lx_ref, gl_ref, gt_ref,
                    *, transposed_v):
    h = _rms(x_ref[...], g_ref[...]).astype(bf16)

    def proj(lo, width):
        return jnp.dot(h, w_ref[:, lo:lo + width], preferred_element_type=f32)

    q_ref[...] = (proj(C_Q, QK_COLS) * (HEAD_DIM ** -0.5 * LOG2E)).astype(bf16)
    k = proj(C_K, QK_COLS)
    k_ref[...] = k
    kb_ref[...] = k.astype(bf16)
    v = proj(C_V, V_COLS)
    v_ref[...] = v
    if transposed_v:
        for hd in range(N_HEADS):
            vb_ref[hd, :V_DIM, :] = v[:, hd * V_DIM:(hd + 1) * V_DIM].T.astype(bf16)
            vb_ref[hd, V_DIM:, :] = jnp.ones((ONES_ROWS, v.shape[0]), bf16)
    else:
        vb_ref[...] = v.astype(bf16)
    lx_ref[...] = proj(C_LX, LRU_WIDTH)
    gl_ref[...] = _gelu_tanh(proj(C_LY, LRU_WIDTH))
    gt_ref[...] = _sigmoid(proj(C_G, 2 * D_MODEL) + bm_ref[...])


def _in_proj(x, g_mix, w_in_b, b_merge, tm, tq=None):
    n = x.shape[0]
    row = lambda w: pl.BlockSpec((tm, w), lambda i: (i, 0))
    if tq is None:
        vb_shape, vb_spec = jax.ShapeDtypeStruct((n, V_COLS), bf16), row(V_COLS)
    else:
        per = tq // tm
        vb_shape = jax.ShapeDtypeStruct((N_HEADS, n // tq, V_DIM + ONES_ROWS, tq), bf16)
        vb_spec = pl.BlockSpec((N_HEADS, None, V_DIM + ONES_ROWS, tm), lambda i: (0, i // per, 0, i % per))
    outs = [
        jax.ShapeDtypeStruct((n, QK_COLS), bf16),
        jax.ShapeDtypeStruct((n, QK_COLS), f32),
        jax.ShapeDtypeStruct((n, QK_COLS), bf16),
        jax.ShapeDtypeStruct((n, V_COLS), f32),
        vb_shape,
        jax.ShapeDtypeStruct((n, LRU_WIDTH), f32),
        jax.ShapeDtypeStruct((n, LRU_WIDTH), f32),
        jax.ShapeDtypeStruct((n, 2 * D_MODEL), f32),
    ]
    return pl.pallas_call(
        functools.partial(_in_proj_kernel, transposed_v=tq is not None),
        grid=(n // tm,),
        in_specs=[row(D_MODEL), _const_spec((1, D_MODEL)), _const_spec((D_MODEL, IN_COLS)),
                  _const_spec((1, 2 * D_MODEL))],
        out_specs=[row(QK_COLS), row(QK_COLS), row(QK_COLS), row(V_COLS), vb_spec,
                   row(LRU_WIDTH), row(LRU_WIDTH), row(2 * D_MODEL)],
        out_shape=outs,
        compiler_params=_cparams(("parallel",)),
        name="in_proj",
    )(x, g_mix, w_in_b, b_merge)


def _rel_bucket(rel):
    half = N_BUCKETS // 2
    max_exact = half // 2
    ret = jnp.where(rel > 0, half, 0)
    n = jnp.abs(rel)
    nf = jnp.maximum(n, 1).astype(f32)
    large = max_exact + (jnp.log(nf / max_exact) / math.log(MAX_DISTANCE / max_exact)
                         * (half - max_exact)).astype(jnp.int32)
    large = jnp.minimum(large, half - 1)
    return ret + jnp.where(n < max_exact, n, large)


def _toeplitz(t, nq, nk):
    nh, length = t.shape
    a = jnp.broadcast_to(t[:, None, :], (nh, nq, length))
    a = jnp.pad(a, ((0, 0), (0, 0), (0, 1))).reshape(nh, nq * (length + 1))
    a = a[:, :nq * length].reshape(nh, nq, length)
    return a[:, :, nq - 1:nq - 1 + nk]


def _bias_table(rel_bias, qpos0, nq, kpos0, nk, shift):
    rel = jnp.arange(-(nq - 1), nk) + (kpos0 - qpos0)
    t = (rel_bias.astype(f32)[_rel_bucket(rel)].T - shift[:, None]) * LOG2E
    qpos = qpos0 + jnp.arange(nq)
    kpos = kpos0 + jnp.arange(nk)
    mask = (kpos[None, :] // CHUNK) <= (qpos[:, None] // CHUNK)
    return jnp.where(mask[None], _toeplitz(t, nq, nk), NEG_INF)


def _bias_ring(rel_bias, offset, tq, shift):
    r = jnp.arange(2 * tq)
    x = jnp.where(r < tq, r, r - 2 * tq)
    t = (rel_bias.astype(f32)[_rel_bucket(offset - x)].T - shift[:, None]) * LOG2E
    return t[:, None, :]


def _lam_value(lq1, lk1, lq2, lk2, lam_init):
    s1 = jnp.sum(lq1[...] * lk1[...], axis=-1, keepdims=True)
    s2 = jnp.sum(lq2[...] * lk2[...], axis=-1, keepdims=True)
    return jnp.exp(s1) - jnp.exp(s2) + lam_init


def _split_maps(q):
    lane = lax.broadcasted_iota(jnp.int32, q.shape, 1)
    zero = jnp.zeros_like(q)
    return jnp.where(lane < HEAD_DIM, q, zero), jnp.where(lane >= HEAD_DIM, q, zero)


def _qk(qz, kt):
    return lax.dot_general(qz, kt, (((1,), (1,)), ((), ())), preferred_element_type=f32)


def _finish_heads(o1, o2, lam, gs, lam_init):
    att = o1 - lam * o2
    return _rms(att, gs) * (1.0 - lam_init)


def _attn_prompt_kernel(q_ref, k_ref, vt_ref, ring_ref, lq1, lk1, lq2, lk2, gs_ref, o_ref,
                        nb_sc, s_sc, p_sc, al_sc, tm_sc, m_sc, acc_sc, *, tq, tk, lam_init):
    i = pl.program_id(1)
    ring_len = ring_ref.shape[-1]

    def bias_tile(ring):
        return pltpu.roll(jnp.broadcast_to(ring, (tk, ring_len)), 0, 1, stride=1, stride_axis=0)[:, :tq]

    @pl.when(i == 0)
    def _():
        kk = lax.broadcasted_iota(jnp.int32, (tk, tq), 0)
        qq = lax.broadcasted_iota(jnp.int32, (tk, tq), 1)
        nb_sc[0] = jnp.where((kk // CHUNK) <= (qq // CHUNK), bias_tile(ring_ref[1]), NEG_INF)
        nb_sc[1] = jnp.where(((kk + tk) // CHUNK) <= (qq // CHUNK), bias_tile(ring_ref[2]), NEG_INF)
        nb_sc[2] = jnp.full((tk, tq), NEG_INF, f32)

    @pl.when(i == 1)
    def _():
        nb_sc[2] = nb_sc[1]
        nb_sc[1] = nb_sc[0]
        nb_sc[0] = bias_tile(ring_ref[0])

    qt = q_ref[...].astype(f32).T
    row = lax.broadcasted_iota(jnp.int32, qt.shape, 0)
    qz = (jnp.where(row < HEAD_DIM, qt, 0.0).astype(bf16), jnp.where(row >= HEAD_DIM, qt, 0.0).astype(bf16))

    m_sc[...] = jnp.full(m_sc.shape, NEG_INF, f32)
    acc_sc[...] = jnp.zeros(acc_sc.shape, f32)

    def qk(slot, tile, bias):
        kt = k_ref[pl.ds(pl.multiple_of(tile * tk, tk), tk), :]
        for mi in range(2):
            s = jnp.dot(kt, qz[mi], preferred_element_type=f32)
            s = s if bias is None else s + bias
            s_sc[mi, slot] = s
            tm_sc[mi, slot] = jnp.max(s, axis=0, keepdims=True)

    def softmax(slot):
        for mi in range(2):
            s = s_sc[mi, slot]
            m_prev = m_sc[mi]
            m_new = jnp.maximum(m_prev, tm_sc[mi, slot])
            al_sc[mi, slot] = jnp.exp2(m_prev - m_new)
            p_sc[mi, slot] = jnp.exp2(s - m_new).astype(bf16)
            m_sc[mi] = m_new

    def pv(slot, vtile):
        va = vt_ref[vtile]
        for mi in range(2):
            acc_sc[mi] = acc_sc[mi] * al_sc[mi, slot] + jnp.dot(va, p_sc[mi, slot], preferred_element_type=f32)

    per = tq // tk
    a0 = jnp.maximum(per * i - 1, 0)
    n_far = jnp.maximum(per * i - 1, 0)
    last_tile = vt_ref.shape[0] - 1

    def far_k(f):
        return jnp.clip(f, 0, jnp.maximum(n_far - 1, 0))

    qk(0, a0, nb_sc[0])
    qk(1, a0 + 1, nb_sc[1])
    softmax(0)
    qk(0, jnp.minimum(a0 + 2, last_tile), nb_sc[2])
    softmax(1)
    pv(0, a0)

    @pl.when(i == 0)
    def _():
        pv(1, a0 + 1)

    @pl.when(i > 0)
    def _():
        qk(1, 0, None)
        softmax(0)
        pv(1, a0 + 1)
        qk(0, far_k(1), None)
        softmax(1)
        pv(0, a0 + 2)

        def far_pair(p, carry):
            qk(1, 2 * p + 2, None)
            softmax(0)
            pv(1, 2 * p)
            qk(0, far_k(2 * p + 3), None)
            softmax(1)
            pv(0, 2 * p + 1)
            return carry

        lax.fori_loop(0, (n_far - 1) // 2, far_pair, 0)
        pv(1, n_far - 1)

    a1 = acc_sc[0]
    a2 = acc_sc[1]
    o1 = a1[:V_DIM] / a1[V_DIM:V_DIM + 1]
    o2 = a2[:V_DIM] / a2[V_DIM:V_DIM + 1]
    lam = _lam_value(lq1, lk1, lq2, lk2, lam_init)
    att = (o1 - lam * o2).T
    o_ref[...] = (_rms(att, gs_ref[...]) * (1.0 - lam_init)).astype(o_ref.dtype)


def _attn_prompt(q, kb, vt, rel_bias, lam_vecs, g_subln, lam_init, tq, tk):
    t = q.shape[0]
    nq = t // tq
    assert tq == 2 * tk and t % tq == 0
    far = rel_bias.astype(f32)[_rel_bucket(jnp.asarray(-4 * MAX_DISTANCE))]
    rings = jnp.stack([_bias_ring(rel_bias, d, tq, far) for d in (-tk, 0, tk)], axis=1)
    vec = _const_spec((1, HEAD_DIM))
    return pl.pallas_call(
        functools.partial(_attn_prompt_kernel, tq=tq, tk=tk, lam_init=lam_init),
        grid=(N_HEADS, nq),
        in_specs=[
            pl.BlockSpec((tq, 2 * HEAD_DIM), lambda h, i: (i, h)),
            pl.BlockSpec((t, 2 * HEAD_DIM), lambda h, i: (0, h)),
            pl.BlockSpec((None, t // tk, V_DIM + ONES_ROWS, tk), lambda h, i: (h, 0, 0, 0)),
            pl.BlockSpec((None, 3, 1, 2 * tq), lambda h, i: (h, 0, 0, 0)),
            vec, vec, vec, vec, _const_spec((1, V_DIM)),
        ],
        out_specs=pl.BlockSpec((tq, V_DIM), lambda h, i: (i, h)),
        out_shape=jax.ShapeDtypeStruct((t, V_COLS), bf16),
        scratch_shapes=[pltpu.VMEM((3, tk, tq), f32), pltpu.VMEM((2, 2, tk, tq), f32),
                        pltpu.VMEM((2, 2, tk, tq), bf16), pltpu.VMEM((2, 2, 1, tq), f32),
                        pltpu.VMEM((2, 2, 1, tq), f32), pltpu.VMEM((2, 1, tq), f32),
                        pltpu.VMEM((2, V_DIM + ONES_ROWS, tq), f32)],
        compiler_params=_cparams(("arbitrary", "arbitrary")),
        name="attn_prompt",
    )(q, kb, vt, rings, *lam_vecs, g_subln)


def _attn_sample_kernel(q_ref, kp_ref, vp_ref, kn_ref, vn_ref, nbp_ref, nbn_ref, lq1, lk1, lq2, lk2, gs_ref,
                        o_ref, *, lam_init):
    lam = _lam_value(lq1, lk1, lq2, lk2, lam_init)
    for hd in range(N_HEADS):
        qk_cols = slice(hd * 2 * HEAD_DIM, (hd + 1) * 2 * HEAD_DIM)
        v_cols = slice(hd * V_DIM, (hd + 1) * V_DIM)
        qz = _split_maps(q_ref[:, qk_cols])
        kp = kp_ref[:, qk_cols].astype(bf16)
        vp = vp_ref[:, v_cols].astype(bf16)
        kn = kn_ref[:, qk_cols]
        vn = vn_ref[:, v_cols]
        outs = []
        for mi in range(2):
            sp = _qk(qz[mi], kp) + nbp_ref[hd]
            sn = _qk(qz[mi], kn) + nbn_ref[hd]
            m = jnp.maximum(jnp.max(sp, axis=1, keepdims=True), jnp.max(sn, axis=1, keepdims=True))
            pp = jnp.exp2(sp - m).astype(bf16)
            pn = jnp.exp2(sn - m).astype(bf16)
            den = (jnp.sum(pp.astype(f32), axis=1, keepdims=True)
                   + jnp.sum(pn.astype(f32), axis=1, keepdims=True))
            num = jnp.dot(pp, vp, preferred_element_type=f32) + jnp.dot(pn, vn, preferred_element_type=f32)
            outs.append(num / den)
        o_ref[:, v_cols] = _finish_heads(outs[0], outs[1], lam, gs_ref[...], lam_init).astype(o_ref.dtype)


def _attn_sample(q, kb, vb, cache_k, cache_v, rel_bias, lam_vecs, g_subln, lam_init, nb_, ts):
    past = cache_k.shape[1]
    zero = jnp.zeros((N_HEADS,), f32)
    nbp = _bias_table(rel_bias, past, ts, 0, past, zero)
    nbn = _bias_table(rel_bias, past, ts, past, ts, zero)
    vec = _const_spec((1, HEAD_DIM))
    return pl.pallas_call(
        functools.partial(_attn_sample_kernel, lam_init=lam_init),
        grid=(nb_,),
        in_specs=[
            pl.BlockSpec((ts, QK_COLS), lambda b: (b, 0)),
            pl.BlockSpec((None, past, QK_COLS), lambda b: (b, 0, 0)),
            pl.BlockSpec((None, past, V_COLS), lambda b: (b, 0, 0)),
            pl.BlockSpec((ts, QK_COLS), lambda b: (b, 0)),
            pl.BlockSpec((ts, V_COLS), lambda b: (b, 0)),
            _const_spec((N_HEADS, ts, past)), _const_spec((N_HEADS, ts, ts)),
            vec, vec, vec, vec, _const_spec((1, V_DIM)),
        ],
        out_specs=pl.BlockSpec((ts, V_COLS), lambda b: (b, 0)),
        out_shape=jax.ShapeDtypeStruct((nb_ * ts, V_COLS), bf16),
        compiler_params=_cparams(("parallel",)),
        name="attn_sample",
    )(q, cache_k, cache_v, kb, vb, nbp, nbn, *lam_vecs, g_subln)


def _conv_lru_kernel(lx_ref, gl_ref, c0_ref, h0_ref, wc_ref, bc_ref, wa_ref, ba_ref, wx_ref, bx_ref, lam_ref,
                     hsg_ref, cout_ref, hout_ref, xbuf, a_sc, b_sc, hcar, *, tt, first_pos_is_zero):
    t = pl.program_id(1)

    @pl.when(t == 0)
    def _():
        xbuf[0:SUBLANES] = c0_ref[...]
        hcar[...] = jnp.broadcast_to(h0_ref[...], hcar.shape)

    xbuf[SUBLANES:SUBLANES + tt] = lx_ref[...]
    xc = jnp.broadcast_to(bc_ref[...], (tt, LRU_WIDTH))
    for j in range(CONV_W):
        lo = SUBLANES - (CONV_W - 1) + j
        xc = xc + xbuf[lo:lo + tt, :] * wc_ref[j:j + 1, :]
    tail = xbuf[tt:tt + SUBLANES, :]
    xbuf[0:SUBLANES] = tail
    cout_ref[...] = tail

    xcb = xc.astype(bf16)

    def block_diag(w_ref, b_ref):
        cols = [jnp.dot(xcb[:, n * LRU_BLOCK_W:(n + 1) * LRU_BLOCK_W], w_ref[n], preferred_element_type=f32)
                for n in range(LRU_BLOCKS)]
        return jnp.concatenate(cols, axis=1) + b_ref[...]

    r = _sigmoid(block_diag(wa_ref, ba_ref))
    ig = _sigmoid(block_diag(wx_ref, bx_ref))
    z = -lam_ref[...]
    softplus = jnp.maximum(z, 0.0) + jnp.log1p(jnp.exp(-jnp.abs(z)))
    log_a = -LRU_C * r * softplus
    a = jnp.exp(log_a)
    th = jnp.tanh(log_a)
    m2 = -2.0 * th / (1.0 - th)
    mult = jnp.where(m2 > 0.0, m2 * lax.rsqrt(m2), 0.0)
    if first_pos_is_zero:
        row = lax.broadcasted_iota(jnp.int32, (tt, LRU_WIDTH), 0)
        mult = jnp.where((row == 0) & (t == 0), 1.0, mult)
    bv = mult * ig * xc

    groups = (tt // SUBLANES, SUBLANES, LRU_WIDTH)
    a = a.reshape(groups)
    bv = bv.reshape(groups)
    sub = lax.broadcasted_iota(jnp.int32, groups, 1)
    d = 1
    while d < SUBLANES:
        valid = sub >= d
        a_s = pltpu.roll(a, d, axis=1)
        b_s = pltpu.roll(bv, d, axis=1)
        bv = jnp.where(valid, a * b_s + bv, bv)
        a = jnp.where(valid, a * a_s, a)
        d *= 2
    a_sc[...] = a.reshape(tt, LRU_WIDTH)
    b_sc[...] = bv.reshape(tt, LRU_WIDTH)

    def group(g, h):
        sl = pl.ds(pl.multiple_of(g * SUBLANES, SUBLANES), SUBLANES)
        hg = a_sc[sl, :] * h + b_sc[sl, :]
        b_sc[sl, :] = hg
        return jnp.broadcast_to(hg[SUBLANES - 1:SUBLANES, :], hg.shape)

    h = lax.fori_loop(0, tt // SUBLANES, group, hcar[...], unroll=4)
    hcar[...] = h
    hout_ref[...] = h[0:1, :]
    hsg_ref[...] = (b_sc[...] * gl_ref[...]).astype(hsg_ref.dtype)


def _conv_lru(lx, gl, conv0, h0, w_conv, b_conv, wa, ba, wx, bx, lru_lambda, nb_, tlen, tt, first_pos_is_zero):
    nt = tlen // tt
    rows = pl.BlockSpec((tt, LRU_WIDTH), lambda b, t: (b * nt + t, 0))
    vecw = _const_spec((1, LRU_WIDTH))
    wblk = _const_spec((LRU_BLOCKS, LRU_BLOCK_W, LRU_BLOCK_W))
    return pl.pallas_call(
        functools.partial(_conv_lru_kernel, tt=tt, first_pos_is_zero=first_pos_is_zero),
        grid=(nb_, nt),
        in_specs=[rows, rows,
                  pl.BlockSpec((None, SUBLANES, LRU_WIDTH), lambda b, t: (b, 0, 0)),
                  pl.BlockSpec((None, 1, LRU_WIDTH), lambda b, t: (b, 0, 0)),
                  _const_spec((CONV_W, LRU_WIDTH)), vecw, wblk, vecw, wblk, vecw, vecw],
        out_specs=[rows,
                   pl.BlockSpec((None, SUBLANES, LRU_WIDTH), lambda b, t: (b, 0, 0)),
                   pl.BlockSpec((None, 1, LRU_WIDTH), lambda b, t: (b, 0, 0))],
        out_shape=[jax.ShapeDtypeStruct((nb_ * tlen, LRU_WIDTH), bf16),
                   jax.ShapeDtypeStruct((nb_, SUBLANES, LRU_WIDTH), f32),
                   jax.ShapeDtypeStruct((nb_, 1, LRU_WIDTH), f32)],
        scratch_shapes=[pltpu.VMEM((tt + SUBLANES, LRU_WIDTH), f32), pltpu.VMEM((tt, LRU_WIDTH), f32),
                        pltpu.VMEM((tt, LRU_WIDTH), f32), pltpu.VMEM((SUBLANES, LRU_WIDTH), f32)],
        compiler_params=_cparams(("arbitrary", "arbitrary")),
        name="conv_lru",
    )(lx, gl, conv0, h0, w_conv, b_conv, wa, ba, wx, bx, lru_lambda)


def _merge_kernel(x_ref, att_ref, hsg_ref, gt_ref, wa_ref, wb_ref, wo_ref, gf_ref, wr_ref, br_ref,
                  x1_ref, h2_ref, comb_ref):
    bra = jnp.dot(att_ref[...], wa_ref[...], preferred_element_type=f32)
    brb = jnp.dot(hsg_ref[...], wb_ref[...], preferred_element_type=f32)
    gt = gt_ref[...]
    m = (gt[:, :D_MODEL] * bra + gt[:, D_MODEL:] * brb).astype(bf16)
    x1 = x_ref[...] + jnp.dot(m, wo_ref[...], preferred_element_type=f32)
    x1_ref[...] = x1
    h2 = _rms(x1, gf_ref[...])
    h2_hi = h2.astype(bf16)
    h2_ref[...] = h2_hi

    h2_lo = (h2 - h2_hi.astype(f32)).astype(bf16)
    parts = (jnp.dot(h2_hi, wr_ref[...], preferred_element_type=f32)
             + jnp.dot(h2_lo, wr_ref[...], preferred_element_type=f32))
    logits = parts[:, :LANES] + parts[:, LANES:] + br_ref[...]
    lane = lax.broadcasted_iota(jnp.int32, logits.shape, 1)
    lanef = lane.astype(f32)
    low = jnp.float32(-3.0e38)
    is_grp = (lane >= N_EXPERTS) & (lane < N_EXPERTS + N_GROUPS)
    gl = jnp.where(is_grp, logits, low)
    gmax = jnp.max(gl, axis=1, keepdims=True)
    gidx = jnp.min(jnp.where(is_grp & (gl == gmax), lanef, 1.0e3), axis=1, keepdims=True) - N_EXPERTS
    gden = jnp.sum(jnp.where(is_grp, jnp.exp(gl - gmax), 0.0), axis=1, keepdims=True)
    g_w = 1.0 / gden
    lo = gidx * EXPERTS_PER_GROUP
    in_sel = (lanef >= lo) & (lanef < lo + EXPERTS_PER_GROUP)
    sel = jnp.where(in_sel, logits, low)
    v1 = jnp.max(sel, axis=1, keepdims=True)
    i1 = jnp.min(jnp.where(in_sel & (sel == v1), lanef, 1.0e3), axis=1, keepdims=True)
    in_sel2 = in_sel & (lanef != i1)
    sel2 = jnp.where(in_sel2, logits, low)
    v2 = jnp.max(sel2, axis=1, keepdims=True)
    i2 = jnp.min(jnp.where(in_sel2 & (sel2 == v2), lanef, 1.0e3), axis=1, keepdims=True)
    e2 = jnp.exp(v2 - v1)
    w1 = g_w / (1.0 + e2)
    w2 = g_w * e2 / (1.0 + e2)
    comb_ref[...] = jnp.where(lanef == i1, w1, 0.0) + jnp.where(lanef == i2, w2, 0.0)


def _merge(x, att, hsg, gt, wa, wb, wo, g_ffn, w_router, b_router, tm):
    n = x.shape[0]
    row = lambda w: pl.BlockSpec((tm, w), lambda i: (i, 0))
    sq = _const_spec((D_MODEL, D_MODEL))
    return pl.pallas_call(
        _merge_kernel,
        grid=(n // tm,),
        in_specs=[row(D_MODEL), row(V_COLS), row(LRU_WIDTH), row(2 * D_MODEL), sq, sq, sq,
                  _const_spec((1, D_MODEL)), _const_spec((D_MODEL, 2 * LANES)), _const_spec((1, LANES))],
        out_specs=[row(D_MODEL), row(D_MODEL), row(LANES)],
        out_shape=[jax.ShapeDtypeStruct((n, D_MODEL), f32), jax.ShapeDtypeStruct((n, D_MODEL), bf16),
                   jax.ShapeDtypeStruct((n, LANES), f32)],
        compiler_params=_cparams(("parallel",)),
        name="merge_router",
    )(x, att, hsg, gt, wa, wb, wo, g_ffn, w_router, b_router)


EXPERTS_PER_STEP = 4


MOE_CHUNK = 128


def _moe_kernel(h_ref, comb_ref, wg_ref, wu_ref, wd_ref, y_ref, pt_sc, x_sc, c_sc, ys_sc, seg_sc):
    e = pl.program_id(1)
    tm = h_ref.shape[0]
    lane1 = lax.broadcasted_iota(jnp.int32, (1, LANES), 1)

    @pl.when(e == 0)
    def _():
        comb = comb_ref[...]
        ex = lax.broadcasted_iota(jnp.int32, (LANES, LANES), 0)
        gr = lax.broadcasted_iota(jnp.int32, (LANES, LANES), 1)
        in_group = jnp.where((ex // EXPERTS_PER_GROUP == gr) & (ex < N_EXPERTS), 1.0, 0.0).astype(bf16)
        gsum = jnp.dot(comb.astype(bf16), in_group, preferred_element_type=f32)
        member = jnp.where(gsum > 0.0, 1.0, 0.0)
        r_i = lax.broadcasted_iota(jnp.int32, (tm, tm), 0)
        c_i = lax.broadcasted_iota(jnp.int32, (tm, tm), 1)
        tri = jnp.where(c_i <= r_i, 1.0, 0.0).astype(bf16)
        member_b = member.astype(bf16)
        half = tm // 2
        rank = (jnp.dot(tri[:, :half], member_b[:half], preferred_element_type=f32)
                + jnp.dot(tri[:, half:], member_b[half:], preferred_element_type=f32))
        totals = rank[tm - 1:tm, :]
        offs = jnp.zeros((1, LANES), f32)
        for g in range(1, N_GROUPS):
            prev = jnp.sum(jnp.where(lane1 == g - 1, totals, 0.0), axis=1, keepdims=True)
            offs = offs + jnp.where(lane1 >= g, prev, 0.0)
        seg_sc[0:1, :] = offs.astype(jnp.int32)
        seg_sc[1:2, :] = totals.astype(jnp.int32)
        pos = jnp.sum(member * (offs + rank - 1.0), axis=1, keepdims=True)
        pos_row = jnp.broadcast_to(pos, (tm, LANES)).T[0:1, :].astype(jnp.int32)
        perm = jnp.where(r_i == pos_row, 1.0, 0.0).astype(bf16)
        pt_sc[...] = jnp.where(c_i == pos.astype(jnp.int32), 1.0, 0.0).astype(bf16)
        comb_hi = comb.astype(bf16)
        comb_lo = (comb - comb_hi.astype(f32)).astype(bf16)
        srt = jnp.dot(perm, jnp.concatenate([h_ref[...], comb_hi, comb_lo], axis=1), preferred_element_type=f32)
        x_sc[...] = srt[:, :D_MODEL].astype(bf16)
        c_sc[...] = srt[:, D_MODEL:D_MODEL + LANES] + srt[:, D_MODEL + LANES:]
        ys_sc[...] = jnp.zeros(ys_sc.shape, f32)

    g = (e * EXPERTS_PER_STEP) // EXPERTS_PER_GROUP
    off = jnp.sum(jnp.where(lane1 == g, seg_sc[0:1, :], 0))
    cnt = jnp.sum(jnp.where(lane1 == g, seg_sc[1:2, :], 0))
    lane = lax.broadcasted_iota(jnp.int32, (MOE_CHUNK, LANES), 1)

    def chunk(k, carry):
        rows = pl.ds(pl.multiple_of(k * MOE_CHUNK, MOE_CHUNK), MOE_CHUNK)
        x = x_sc[rows, :]
        cw = c_sc[rows, :]
        hidden = []
        for j in range(EXPERTS_PER_STEP):
            gate = jnp.dot(x, wg_ref[j], preferred_element_type=f32)
            up = jnp.dot(x, wu_ref[j], preferred_element_type=f32)
            c = jnp.sum(jnp.where(lane == e * EXPERTS_PER_STEP + j, cw, 0.0), axis=1, keepdims=True)
            hidden.append((gate * _sigmoid(gate) * up * c).astype(bf16))
        ys_sc[rows, :] += jnp.dot(jnp.concatenate(hidden, axis=1), wd_ref[...], preferred_element_type=f32)
        return carry

    lax.fori_loop(off // MOE_CHUNK, (off + cnt + MOE_CHUNK - 1) // MOE_CHUNK, chunk, 0)

    @pl.when(e == pl.num_programs(1) - 1)
    def _():
        y_ref[...] = jnp.dot(pt_sc[...], ys_sc[...].astype(bf16), preferred_element_type=f32)


def _moe(h2, comb, wg, wu, wd, tm):
    n = h2.shape[0]
    wd_rows = wd.reshape(N_EXPERTS * D_EXPERT, D_MODEL)
    per = EXPERTS_PER_STEP
    assert EXPERTS_PER_GROUP % per == 0 and tm % MOE_CHUNK == 0
    return pl.pallas_call(
        _moe_kernel,
        grid=(n // tm, N_EXPERTS // per),
        in_specs=[pl.BlockSpec((tm, D_MODEL), lambda i, e: (i, 0)),
                  pl.BlockSpec((tm, LANES), lambda i, e: (i, 0)),
                  pl.BlockSpec((per, D_MODEL, D_EXPERT), lambda i, e: (e, 0, 0)),
                  pl.BlockSpec((per, D_MODEL, D_EXPERT), lambda i, e: (e, 0, 0)),
                  pl.BlockSpec((per * D_EXPERT, D_MODEL), lambda i, e: (e, 0))],
        out_specs=pl.BlockSpec((tm, D_MODEL), lambda i, e: (i, 0)),
        out_shape=jax.ShapeDtypeStruct((n, D_MODEL), f32),
        scratch_shapes=[pltpu.VMEM((tm, tm), bf16), pltpu.VMEM((tm, D_MODEL), bf16),
                        pltpu.VMEM((tm, LANES), f32), pltpu.VMEM((tm, D_MODEL), f32),
                        pltpu.VMEM((SUBLANES, LANES), jnp.int32)],
        compiler_params=_cparams(("arbitrary", "arbitrary")),
        name="moe",
    )(h2, comb, wg, wu, wd_rows)


def _ple_kernel(x1_ref, y_ref, p_ref, gp_ref, wpg_ref, bpg_ref, wpi_ref, gfin_ref, o_ref):
    x2 = x1_ref[...] + y_ref[...]
    hp = _rms(x2, gp_ref[...]).astype(bf16)
    pg = _sigmoid(jnp.dot(hp, wpg_ref[...], preferred_element_type=f32) + bpg_ref[...])
    pin = jnp.dot(p_ref[...].astype(bf16), wpi_ref[...], preferred_element_type=f32)
    x3 = x2 + pg * pin
    o_ref[...] = _rms(x3, gfin_ref[...])


def _ple(x1, y, p, g_ple, wpg, bpg, wpi, g_final, tm):
    n = x1.shape[0]
    row = lambda w: pl.BlockSpec((tm, w), lambda i: (i, 0))
    vec = _const_spec((1, D_MODEL))
    return pl.pallas_call(
        _ple_kernel,
        grid=(n // tm,),
        in_specs=[row(D_MODEL), row(D_MODEL), row(PLE_DIM), vec, _const_spec((D_MODEL, D_MODEL)), vec,
                  _const_spec((PLE_DIM, D_MODEL)), vec],
        out_specs=row(D_MODEL),
        out_shape=jax.ShapeDtypeStruct((n, D_MODEL), f32),
        compiler_params=_cparams(("parallel",)),
        name="ple_final",
    )(x1, y, p, g_ple, wpg, bpg, wpi, g_final)


def _tile(n, pref):
    return pref if n % pref == 0 else n


def _stream(x, p, conv0, h0, cache, li, rel_bias, w):
    nb_, tlen, _ = x.shape
    n = nb_ * tlen
    xf = x.reshape(n, D_MODEL)
    lam_init = 0.8 - 0.6 * math.exp(-0.3 * li)
    tk = None if cache is not None else 512
    q, k, kb, v, vb, lx, gl, gt = _in_proj(xf, w["g_mix"], w["w_in"], w["b_merge"], _tile(n, 256), tk)
    if cache is None:
        att = _attn_prompt(q, kb, vb, rel_bias, w["lam_vecs"], w["g_subln"], lam_init, 2 * tk, tk)
    else:
        att = _attn_sample(q, kb, vb, cache[0], cache[1], rel_bias, w["lam_vecs"], w["g_subln"], lam_init,
                           nb_, tlen)
    conv_pad = jnp.pad(conv0, ((0, 0), (SUBLANES - (CONV_W - 1), 0), (0, 0)))
    hsg, cout, hout = _conv_lru(lx, gl, conv_pad, h0[:, None, :], w["w_conv"], w["b_conv"], w["w_rg_a"],
                                w["b_rg_a"], w["w_rg_x"], w["b_rg_x"], w["lru_lambda"], nb_, tlen,
                                _tile(tlen, 256), cache is None)
    tm = _tile(n, 512)
    x1, h2, comb = _merge(xf, att, hsg, gt, w["w_attn_br"], w["w_lru_br"], w["w_out"], w["g_ffn"],
                          w["w_router"], w["b_router"], tm)
    y = _moe(h2, comb, w["w_e_gate"], w["w_e_up"], w["w_e_down"], _tile(n, 1024))
    out = _ple(x1, y, p.reshape(n, PLE_DIM), w["g_ple"], w["w_ple_gate"], w["b_ple_gate"], w["w_ple_in"],
               w["g_final"], tm)
    return out, k, v, cout[:, SUBLANES - (CONV_W - 1):, :], hout[:, 0, :]


def kernel(x_prompt, x_sample, cache_k, cache_v, state_conv, state_lru, p_prompt, p_sample, rel_bias, g_mix, w_in, b_merge, lam_q1, lam_k1, lam_q2, lam_k2, g_subln, w_attn_br, w_conv, b_conv, w_rg_a, b_rg_a, w_rg_x, b_rg_x, lru_lambda, w_lru_br, w_out, g_ffn, w_grp, b_grp, w_rt, b_rt, w_e_gate, w_e_up, w_e_down, g_ple, w_ple_gate, b_ple_gate, w_ple_in, g_final):
    depth = w_in.shape[0]
    assert depth == 1, "the final norm is fused into the single layer's last stage"
    bp, tp, _ = x_prompt.shape
    bs, ts, _ = x_sample.shape
    past = cache_k.shape[2]
    li = 0
    row = lambda a: a[li].reshape(1, -1).astype(f32)
    w_router = jnp.concatenate(
        [jnp.transpose(w_rt[li], (1, 0, 2)).reshape(D_MODEL, N_EXPERTS), w_grp[li]], axis=1).astype(f32)
    w_router = jnp.pad(w_router, ((0, 0), (0, LANES - N_EXPERTS - N_GROUPS)))
    w_router_hi = w_router.astype(bf16)
    w_router = jnp.concatenate([w_router_hi, (w_router - w_router_hi.astype(f32)).astype(bf16)], axis=1)
    b_router = jnp.pad(jnp.concatenate([b_rt[li].reshape(-1), b_grp[li]]).astype(f32),
                       (0, LANES - N_EXPERTS - N_GROUPS)).reshape(1, LANES)
    w = dict(
        g_mix=row(g_mix), w_in=w_in[li].astype(bf16), b_merge=row(b_merge),
        lam_vecs=(row(lam_q1), row(lam_k1), row(lam_q2), row(lam_k2)), g_subln=row(g_subln),
        w_attn_br=w_attn_br[li].astype(bf16), w_conv=w_conv[li].astype(f32), b_conv=row(b_conv),
        w_rg_a=w_rg_a[li].astype(bf16), b_rg_a=row(b_rg_a), w_rg_x=w_rg_x[li].astype(bf16), b_rg_x=row(b_rg_x),
        lru_lambda=row(lru_lambda), w_lru_br=w_lru_br[li].astype(bf16), w_out=w_out[li].astype(bf16),
        g_ffn=row(g_ffn), w_router=w_router, b_router=b_router,
        w_e_gate=w_e_gate[li].astype(bf16), w_e_up=w_e_up[li].astype(bf16), w_e_down=w_e_down[li].astype(bf16),
        g_ple=row(g_ple), w_ple_gate=w_ple_gate[li].astype(bf16), b_ple_gate=row(b_ple_gate),
        w_ple_in=w_ple_in[li].astype(bf16), g_final=g_final.reshape(1, -1).astype(f32),
    )
    conv0 = jnp.zeros((bp, CONV_W - 1, LRU_WIDTH), f32)
    h0 = jnp.zeros((bp, LRU_WIDTH), f32)
    yp, kp, vp, cp, hp = _stream(x_prompt, p_prompt[li], conv0, h0, None, li, rel_bias, w)
    cache = (cache_k[li].reshape(bs, past, QK_COLS), cache_v[li].reshape(bs, past, V_COLS))
    ys, ks, vs, cs, hs = _stream(x_sample, p_sample[li], state_conv[li], state_lru[li], cache, li, rel_bias, w)
    return (yp.reshape(bp, tp, D_MODEL), ys.reshape(bs, ts, D_MODEL),
            kp.reshape(1, bp, tp, N_HEADS, 2, HEAD_DIM), vp.reshape(1, bp, tp, N_HEADS, V_DIM),
            cp[None], hp[None],
            ks.reshape(1, bs, ts, N_HEADS, 2, HEAD_DIM), vs.reshape(1, bs, ts, N_HEADS, V_DIM),
            cs[None], hs[None])
```

```python
import functools
import math

import jax
import jax.numpy as jnp
from jax import lax
from jax.experimental import pallas as pl
from jax.experimental.pallas import tpu as pltpu

f32 = jnp.float32
bf16 = jnp.bfloat16

D_MODEL = 1024
N_HEADS = 8
HEAD_DIM = 64
V_DIM = 128
QK_COLS = N_HEADS * 2 * HEAD_DIM
V_COLS = N_HEADS * V_DIM
LRU_WIDTH = 1024
LRU_BLOCKS = 8
LRU_BLOCK_W = LRU_WIDTH // LRU_BLOCKS
CONV_W = 4
LRU_C = 8.0
CHUNK = 64
N_BUCKETS = 32
MAX_DISTANCE = 128
N_GROUPS = 4
EXPERTS_PER_GROUP = 8
N_EXPERTS = N_GROUPS * EXPERTS_PER_GROUP
D_EXPERT = 256
PLE_DIM = 256
EPS = 1e-6
NEG_INF = -1e30
LOG2E = math.log2(math.e)
ONES_ROWS = 16
LANES = 128
SUBLANES = 8
VMEM_LIMIT = 56 * 1024 * 1024

C_Q, C_K, C_V = 0, QK_COLS, 2 * QK_COLS
C_LX = 2 * QK_COLS + V_COLS
C_LY = C_LX + LRU_WIDTH
C_G = C_LY + LRU_WIDTH
IN_COLS = C_G + 2 * D_MODEL


def _cparams(sem):
    return pltpu.CompilerParams(dimension_semantics=sem, vmem_limit_bytes=VMEM_LIMIT)


def _const_spec(shape):
    nd = len(shape)
    return pl.BlockSpec(shape, lambda *_: (0,) * nd, pipeline_mode=pl.Buffered(1))


def _rms(x, g):
    return x * lax.rsqrt(jnp.mean(x * x, axis=-1, keepdims=True) + EPS) * g


def _sigmoid(x):
    return 1.0 / (1.0 + jnp.exp(-x))


def _gelu_tanh(x):
    c = math.sqrt(2.0 / math.pi)
    return 0.5 * x * (1.0 + jnp.tanh(c * (x + 0.044715 * (x * x * x))))


def _in_proj_kernel(x_ref, g_ref, w_ref, bm_ref, q_ref, k_ref, kb_ref, v_ref, vb_ref, lx_ref, gl_ref, gt_ref,
                    *, transposed_v):
    h = _rms(x_ref[...], g_ref[...]).astype(bf16)

    def proj(lo, width):
        return jnp.dot(h, w_ref[:, lo:lo + width], preferred_element_type=f32)

    q_ref[...] = (proj(C_Q, QK_COLS) * (HEAD_DIM ** -0.5 * LOG2E)).astype(bf16)
    k = proj(C_K, QK_COLS)
    k_ref[...] = k
    kb_ref[...] = k.astype(bf16)
    v = proj(C_V, V_COLS)
    v_ref[...] = v
    if transposed_v:
        for hd in range(N_HEADS):
            vb_ref[hd, :V_DIM, :] = v[:, hd * V_DIM:(hd + 1) * V_DIM].T.astype(bf16)
            vb_ref[hd, V_DIM:, :] = jnp.ones((ONES_ROWS, v.shape[0]), bf16)
    else:
        vb_ref[...] = v.astype(bf16)
    lx_ref[...] = proj(C_LX, LRU_WIDTH)
    gl_ref[...] = _gelu_tanh(proj(C_LY, LRU_WIDTH))
    gt_ref[...] = _sigmoid(proj(C_G, 2 * D_MODEL) + bm_ref[...])


def _in_proj(x, g_mix, w_in_b, b_merge, tm, tq=None):
    n = x.shape[0]
    row = lambda w: pl.BlockSpec((tm, w), lambda i: (i, 0))
    if tq is None:
        vb_shape, vb_spec = jax.ShapeDtypeStruct((n, V_COLS), bf16), row(V_COLS)
    else:
        per = tq // tm
        vb_shape = jax.ShapeDtypeStruct((N_HEADS, n // tq, V_DIM + ONES_ROWS, tq), bf16)
        vb_spec = pl.BlockSpec((N_HEADS, None, V_DIM + ONES_ROWS, tm), lambda i: (0, i // per, 0, i % per))
    outs = [
        jax.ShapeDtypeStruct((n, QK_COLS), bf16),
        jax.ShapeDtypeStruct((n, QK_COLS), f32),
        jax.ShapeDtypeStruct((n, QK_COLS), bf16),
        jax.ShapeDtypeStruct((n, V_COLS), f32),
        vb_shape,
        jax.ShapeDtypeStruct((n, LRU_WIDTH), f32),
        jax.ShapeDtypeStruct((n, LRU_WIDTH), f32),
        jax.ShapeDtypeStruct((n, 2 * D_MODEL), f32),
    ]
    return pl.pallas_call(
        functools.partial(_in_proj_kernel, transposed_v=tq is not None),
        grid=(n // tm,),
        in_specs=[row(D_MODEL), _const_spec((1, D_MODEL)), _const_spec((D_MODEL, IN_COLS)),
                  _const_spec((1, 2 * D_MODEL))],
        out_specs=[row(QK_COLS), row(QK_COLS), row(QK_COLS), row(V_COLS), vb_spec,
                   row(LRU_WIDTH), row(LRU_WIDTH), row(2 * D_MODEL)],
        out_shape=outs,
        compiler_params=_cparams(("parallel",)),
        name="in_proj",
    )(x, g_mix, w_in_b, b_merge)


def _rel_bucket(rel):
    half = N_BUCKETS // 2
    max_exact = half // 2
    ret = jnp.where(rel > 0, half, 0)
    n = jnp.abs(rel)
    nf = jnp.maximum(n, 1).astype(f32)
    large = max_exact + (jnp.log(nf / max_exact) / math.log(MAX_DISTANCE / max_exact)
                         * (half - max_exact)).astype(jnp.int32)
    large = jnp.minimum(large, half - 1)
    return ret + jnp.where(n < max_exact, n, large)


def _toeplitz(t, nq, nk):
    nh, length = t.shape
    a = jnp.broadcast_to(t[:, None, :], (nh, nq, length))
    a = jnp.pad(a, ((0, 0), (0, 0), (0, 1))).reshape(nh, nq * (length + 1))
    a = a[:, :nq * length].reshape(nh, nq, length)
    return a[:, :, nq - 1:nq - 1 + nk]


def _bias_table(rel_bias, qpos0, nq, kpos0, nk, shift):
    rel = jnp.arange(-(nq - 1), nk) + (kpos0 - qpos0)
    t = (rel_bias.astype(f32)[_rel_bucket(rel)].T - shift[:, None]) * LOG2E
    qpos = qpos0 + jnp.arange(nq)
    kpos = kpos0 + jnp.arange(nk)
    mask = (kpos[None, :] // CHUNK) <= (qpos[:, None] // CHUNK)
    return jnp.where(mask[None], _toeplitz(t, nq, nk), NEG_INF)


def _bias_ring(rel_bias, offset, tq, shift):
    r = jnp.arange(2 * tq)
    x = jnp.where(r < tq, r, r - 2 * tq)
    t = (rel_bias.astype(f32)[_rel_bucket(offset - x)].T - shift[:, None]) * LOG2E
    return t[:, None, :]


def _lam_value(lq1, lk1, lq2, lk2, lam_init):
    s1 = jnp.sum(lq1[...] * lk1[...], axis=-1, keepdims=True)
    s2 = jnp.sum(lq2[...] * lk2[...], axis=-1, keepdims=True)
    return jnp.exp(s1) - jnp.exp(s2) + lam_init


def _split_maps(q):
    lane = lax.broadcasted_iota(jnp.int32, q.shape, 1)
    zero = jnp.zeros_like(q)
    return jnp.where(lane < HEAD_DIM, q, zero), jnp.where(lane >= HEAD_DIM, q, zero)


def _qk(qz, kt):
    return lax.dot_general(qz, kt, (((1,), (1,)), ((), ())), preferred_element_type=f32)


def _finish_heads(o1, o2, lam, gs, lam_init):
    att = o1 - lam * o2
    return _rms(att, gs) * (1.0 - lam_init)


def _attn_prompt_kernel(q_ref, k_ref, vt_ref, ring_ref, lq1, lk1, lq2, lk2, gs_ref, o_ref,
                        nb_sc, s_sc, p_sc, al_sc, tm_sc, m_sc, acc_sc, *, tq, tk, lam_init):
    i = pl.program_id(1)
    ring_len = ring_ref.shape[-1]

    def bias_tile(ring):
        return pltpu.roll(jnp.broadcast_to(ring, (tk, ring_len)), 0, 1, stride=1, stride_axis=0)[:, :tq]

    @pl.when(i == 0)
    def _():
        kk = lax.broadcasted_iota(jnp.int32, (tk, tq), 0)
        qq = lax.broadcasted_iota(jnp.int32, (tk, tq), 1)
        nb_sc[0] = jnp.where((kk // CHUNK) <= (qq // CHUNK), bias_tile(ring_ref[1]), NEG_INF)
        nb_sc[1] = jnp.where(((kk + tk) // CHUNK) <= (qq // CHUNK), bias_tile(ring_ref[2]), NEG_INF)
        nb_sc[2] = jnp.full((tk, tq), NEG_INF, f32)

    @pl.when(i == 1)
    def _():
        nb_sc[2] = nb_sc[1]
        nb_sc[1] = nb_sc[0]
        nb_sc[0] = bias_tile(ring_ref[0])

    qt = q_ref[...].astype(f32).T
    row = lax.broadcasted_iota(jnp.int32, qt.shape, 0)
    qz = (jnp.where(row < HEAD_DIM, qt, 0.0).astype(bf16), jnp.where(row >= HEAD_DIM, qt, 0.0).astype(bf16))

    m_sc[...] = jnp.full(m_sc.shape, NEG_INF, f32)
    acc_sc[...] = jnp.zeros(acc_sc.shape, f32)

    def qk(slot, tile, bias):
        kt = k_ref[pl.ds(pl.multiple_of(tile * tk, tk), tk), :]
        for mi in range(2):
            s = jnp.dot(kt, qz[mi], preferred_element_type=f32)
            s = s if bias is None else s + bias
            s_sc[mi, slot] = s
            tm_sc[mi, slot] = jnp.max(s, axis=0, keepdims=True)

    def softmax(slot):
        for mi in range(2):
            s = s_sc[mi, slot]
            m_prev = m_sc[mi]
            m_new = jnp.maximum(m_prev, tm_sc[mi, slot])
            al_sc[mi, slot] = jnp.exp2(m_prev - m_new)
            p_sc[mi, slot] = jnp.exp2(s - m_new).astype(bf16)
            m_sc[mi] = m_new

    def pv(slot, vtile):
        va = vt_ref[vtile]
        for mi in range(2):
            acc_sc[mi] = acc_sc[mi] * al_sc[mi, slot] + jnp.dot(va, p_sc[mi, slot], preferred_element_type=f32)

    per = tq // tk
    a0 = jnp.maximum(per * i - 1, 0)
    n_far = jnp.maximum(per * i - 1, 0)
    last_tile = vt_ref.shape[0] - 1

    def far_k(f):
        return jnp.clip(f, 0, jnp.maximum(n_far - 1, 0))

    qk(0, a0, nb_sc[0])
    qk(1, a0 + 1, nb_sc[1])
    softmax(0)
    qk(0, jnp.minimum(a0 + 2, last_tile), nb_sc[2])
    softmax(1)
    pv(0, a0)

    @pl.when(i == 0)
    def _():
        pv(1, a0 + 1)

    @pl.when(i > 0)
    def _():
        qk(1, 0, None)
        softmax(0)
        pv(1, a0 + 1)
        qk(0, far_k(1), None)
        softmax(1)
        pv(0, a0 + 2)

        def far_pair(p, carry):
            qk(1, 2 * p + 2, None)
            softmax(0)
            pv(1, 2 * p)
            qk(0, far_k(2 * p + 3), None)
            softmax(1)
            pv(0, 2 * p + 1)
            return carry

        lax.fori_loop(0, (n_far - 1) // 2, far_pair, 0)
        pv(1, n_far - 1)

    a1 = acc_sc[0]
    a2 = acc_sc[1]
    o1 = a1[:V_DIM] / a1[V_DIM:V_DIM + 1]
    o2 = a2[:V_DIM] / a2[V_DIM:V_DIM + 1]
    lam = _lam_value(lq1, lk1, lq2, lk2, lam_init)
    att = (o1 - lam * o2).T
    o_ref[...] = (_rms(att, gs_ref[...]) * (1.0 - lam_init)).astype(o_ref.dtype)


def _attn_prompt(q, kb, vt, rel_bias, lam_vecs, g_subln, lam_init, tq, tk):
    t = q.shape[0]
    nq = t // tq
    assert tq == 2 * tk and t % tq == 0
    far = rel_bias.astype(f32)[_rel_bucket(jnp.asarray(-4 * MAX_DISTANCE))]
    rings = jnp.stack([_bias_ring(rel_bias, d, tq, far) for d in (-tk, 0, tk)], axis=1)
    vec = _const_spec((1, HEAD_DIM))
    return pl.pallas_call(
        functools.partial(_attn_prompt_kernel, tq=tq, tk=tk, lam_init=lam_init),
        grid=(N_HEADS, nq),
        in_specs=[
            pl.BlockSpec((tq, 2 * HEAD_DIM), lambda h, i: (i, h)),
            pl.BlockSpec((t, 2 * HEAD_DIM), lambda h, i: (0, h)),
            pl.BlockSpec((None, t // tk, V_DIM + ONES_ROWS, tk), lambda h, i: (h, 0, 0, 0)),
            pl.BlockSpec((None, 3, 1, 2 * tq), lambda h, i: (h, 0, 0, 0)),
            vec, vec, vec, vec, _const_spec((1, V_DIM)),
        ],
        out_specs=pl.BlockSpec((tq, V_DIM), lambda h, i: (i, h)),
        out_shape=jax.ShapeDtypeStruct((t, V_COLS), bf16),
        scratch_shapes=[pltpu.VMEM((3, tk, tq), f32), pltpu.VMEM((2, 2, tk, tq), f32),
                        pltpu.VMEM((2, 2, tk, tq), bf16), pltpu.VMEM((2, 2, 1, tq), f32),
                        pltpu.VMEM((2, 2, 1, tq), f32), pltpu.VMEM((2, 1, tq), f32),
                        pltpu.VMEM((2, V_DIM + ONES_ROWS, tq), f32)],
        compiler_params=_cparams(("arbitrary", "arbitrary")),
        name="attn_prompt",
    )(q, kb, vt, rings, *lam_vecs, g_subln)


def _attn_sample_kernel(q_ref, kp_ref, vp_ref, kn_ref, vn_ref, nbp_ref, nbn_ref, lq1, lk1, lq2, lk2, gs_ref,
                        o_ref, *, lam_init):
    lam = _lam_value(lq1, lk1, lq2, lk2, lam_init)
    for hd in range(N_HEADS):
        qk_cols = slice(hd * 2 * HEAD_DIM, (hd + 1) * 2 * HEAD_DIM)
        v_cols = slice(hd * V_DIM, (hd + 1) * V_DIM)
        qz = _split_maps(q_ref[:, qk_cols])
        kp = kp_ref[:, qk_cols].astype(bf16)
        vp = vp_ref[:, v_cols].astype(bf16)
        kn = kn_ref[:, qk_cols]
        vn = vn_ref[:, v_cols]
        outs = []
        for mi in range(2):
            sp = _qk(qz[mi], kp) + nbp_ref[hd]
            sn = _qk(qz[mi], kn) + nbn_ref[hd]
            m = jnp.maximum(jnp.max(sp, axis=1, keepdims=True), jnp.max(sn, axis=1, keepdims=True))
            pp = jnp.exp2(sp - m).astype(bf16)
            pn = jnp.exp2(sn - m).astype(bf16)
            den = (jnp.sum(pp.astype(f32), axis=1, keepdims=True)
                   + jnp.sum(pn.astype(f32), axis=1, keepdims=True))
            num = jnp.dot(pp, vp, preferred_element_type=f32) + jnp.dot(pn, vn, preferred_element_type=f32)
            outs.append(num / den)
        o_ref[:, v_cols] = _finish_heads(outs[0], outs[1], lam, gs_ref[...], lam_init).astype(o_ref.dtype)


def _attn_sample(q, kb, vb, cache_k, cache_v, rel_bias, lam_vecs, g_subln, lam_init, nb_, ts):
    past = cache_k.shape[1]
    zero = jnp.zeros((N_HEADS,), f32)
    nbp = _bias_table(rel_bias, past, ts, 0, past, zero)
    nbn = _bias_table(rel_bias, past, ts, past, ts, zero)
    vec = _const_spec((1, HEAD_DIM))
    return pl.pallas_call(
        functools.partial(_attn_sample_kernel, lam_init=lam_init),
        grid=(nb_,),
        in_specs=[
            pl.BlockSpec((ts, QK_COLS), lambda b: (b, 0)),
            pl.BlockSpec((None, past, QK_COLS), lambda b: (b, 0, 0)),
            pl.BlockSpec((None, past, V_COLS), lambda b: (b, 0, 0)),
            pl.BlockSpec((ts, QK_COLS), lambda b: (b, 0)),
            pl.BlockSpec((ts, V_COLS), lambda b: (b, 0)),
            _const_spec((N_HEADS, ts, past)), _const_spec((N_HEADS, ts, ts)),
            vec, vec, vec, vec, _const_spec((1, V_DIM)),
        ],
        out_specs=pl.BlockSpec((ts, V_COLS), lambda b: (b, 0)),
        out_shape=jax.ShapeDtypeStruct((nb_ * ts, V_COLS), bf16),
        compiler_params=_cparams(("parallel",)),
        name="attn_sample",
    )(q, cache_k, cache_v, kb, vb, nbp, nbn, *lam_vecs, g_subln)


def _conv_lru_kernel(lx_ref, gl_ref, c0_ref, h0_ref, wc_ref, bc_ref, wa_ref, ba_ref, wx_ref, bx_ref, lam_ref,
                     hsg_ref, cout_ref, hout_ref, xbuf, a_sc, b_sc, hcar, *, tt, first_pos_is_zero):
    t = pl.program_id(1)

    @pl.when(t == 0)
    def _():
        xbuf[0:SUBLANES] = c0_ref[...]
        hcar[...] = jnp.broadcast_to(h0_ref[...], hcar.shape)

    xbuf[SUBLANES:SUBLANES + tt] = lx_ref[...]
    xc = jnp.broadcast_to(bc_ref[...], (tt, LRU_WIDTH))
    for j in range(CONV_W):
        lo = SUBLANES - (CONV_W - 1) + j
        xc = xc + xbuf[lo:lo + tt, :] * wc_ref[j:j + 1, :]
    tail = xbuf[tt:tt + SUBLANES, :]
    xbuf[0:SUBLANES] = tail
    cout_ref[...] = tail

    xcb = xc.astype(bf16)

    def block_diag(w_ref, b_ref):
        cols = [jnp.dot(xcb[:, n * LRU_BLOCK_W:(n + 1) * LRU_BLOCK_W], w_ref[n], preferred_element_type=f32)
                for n in range(LRU_BLOCKS)]
        return jnp.concatenate(cols, axis=1) + b_ref[...]

    r = _sigmoid(block_diag(wa_ref, ba_ref))
    ig = _sigmoid(block_diag(wx_ref, bx_ref))
    z = -lam_ref[...]
    softplus = jnp.maximum(z, 0.0) + jnp.log1p(jnp.exp(-jnp.abs(z)))
    log_a = -LRU_C * r * softplus
    a = jnp.exp(log_a)
    th = jnp.tanh(log_a)
    m2 = -2.0 * th / (1.0 - th)
    mult = jnp.where(m2 > 0.0, m2 * lax.rsqrt(m2), 0.0)
    if first_pos_is_zero:
        row = lax.broadcasted_iota(jnp.int32, (tt, LRU_WIDTH), 0)
        mult = jnp.where((row == 0) & (t == 0), 1.0, mult)
    bv = mult * ig * xc

    groups = (tt // SUBLANES, SUBLANES, LRU_WIDTH)
    a = a.reshape(groups)
    bv = bv.reshape(groups)
    sub = lax.broadcasted_iota(jnp.int32, groups, 1)
    d = 1
    while d < SUBLANES:
        valid = sub >= d
        a_s = pltpu.roll(a, d, axis=1)
        b_s = pltpu.roll(bv, d, axis=1)
        bv = jnp.where(valid, a * b_s + bv, bv)
        a = jnp.where(valid, a * a_s, a)
        d *= 2
    a_sc[...] = a.reshape(tt, LRU_WIDTH)
    b_sc[...] = bv.reshape(tt, LRU_WIDTH)

    def group(g, h):
        sl = pl.ds(pl.multiple_of(g * SUBLANES, SUBLANES), SUBLANES)
        hg = a_sc[sl, :] * h + b_sc[sl, :]
        b_sc[sl, :] = hg
        return jnp.broadcast_to(hg[SUBLANES - 1:SUBLANES, :], hg.shape)

    h = lax.fori_loop(0, tt // SUBLANES, group, hcar[...], unroll=4)
    hcar[...] = h
    hout_ref[...] = h[0:1, :]
    hsg_ref[...] = (b_sc[...] * gl_ref[...]).astype(hsg_ref.dtype)


def _conv_lru(lx, gl, conv0, h0, w_conv, b_conv, wa, ba, wx, bx, lru_lambda, nb_, tlen, tt, first_pos_is_zero):
    nt = tlen // tt
    rows = pl.BlockSpec((tt, LRU_WIDTH), lambda b, t: (b * nt + t, 0))
    vecw = _const_spec((1, LRU_WIDTH))
    wblk = _const_spec((LRU_BLOCKS, LRU_BLOCK_W, LRU_BLOCK_W))
    return pl.pallas_call(
        functools.partial(_conv_lru_kernel, tt=tt, first_pos_is_zero=first_pos_is_zero),
        grid=(nb_, nt),
        in_specs=[rows, rows,
                  pl.BlockSpec((None, SUBLANES, LRU_WIDTH), lambda b, t: (b, 0, 0)),
                  pl.BlockSpec((None, 1, LRU_WIDTH), lambda b, t: (b, 0, 0)),
                  _const_spec((CONV_W, LRU_WIDTH)), vecw, wblk, vecw, wblk, vecw, vecw],
        out_specs=[rows,
                   pl.BlockSpec((None, SUBLANES, LRU_WIDTH), lambda b, t: (b, 0, 0)),
                   pl.BlockSpec((None, 1, LRU_WIDTH), lambda b, t: (b, 0, 0))],
        out_shape=[jax.ShapeDtypeStruct((nb_ * tlen, LRU_WIDTH), bf16),
                   jax.ShapeDtypeStruct((nb_, SUBLANES, LRU_WIDTH), f32),
                   jax.ShapeDtypeStruct((nb_, 1, LRU_WIDTH), f32)],
        scratch_shapes=[pltpu.VMEM((tt + SUBLANES, LRU_WIDTH), f32), pltpu.VMEM((tt, LRU_WIDTH), f32),
                        pltpu.VMEM((tt, LRU_WIDTH), f32), pltpu.VMEM((SUBLANES, LRU_WIDTH), f32)],
        compiler_params=_cparams(("arbitrary", "arbitrary")),
        name="conv_lru",
    )(lx, gl, conv0, h0, w_conv, b_conv, wa, ba, wx, bx, lru_lambda)


def _merge_kernel(x_ref, att_ref, hsg_ref, gt_ref, wa_ref, wb_ref, wo_ref, gf_ref, wr_ref, br_ref,
                  x1_ref, h2_ref, comb_ref):
    bra = jnp.dot(att_ref[...], wa_ref[...], preferred_element_type=f32)
    brb = jnp.dot(hsg_ref[...], wb_ref[...], preferred_element_type=f32)
    gt = gt_ref[...]
    m = (gt[:, :D_MODEL] * bra + gt[:, D_MODEL:] * brb).astype(bf16)
    x1 = x_ref[...] + jnp.dot(m, wo_ref[...], preferred_element_type=f32)
    x1_ref[...] = x1
    h2 = _rms(x1, gf_ref[...])
    h2_hi = h2.astype(bf16)
    h2_ref[...] = h2_hi

    h2_lo = (h2 - h2_hi.astype(f32)).astype(bf16)
    parts = (jnp.dot(h2_hi, wr_ref[...], preferred_element_type=f32)
             + jnp.dot(h2_lo, wr_ref[...], preferred_element_type=f32))
    logits = parts[:, :LANES] + parts[:, LANES:] + br_ref[...]
    lane = lax.broadcasted_iota(jnp.int32, logits.shape, 1)
    lanef = lane.astype(f32)
    low = jnp.float32(-3.0e38)
    is_grp = (lane >= N_EXPERTS) & (lane < N_EXPERTS + N_GROUPS)
    gl = jnp.where(is_grp, logits, low)
    gmax = jnp.max(gl, axis=1, keepdims=True)
    gidx = jnp.min(jnp.where(is_grp & (gl == gmax), lanef, 1.0e3), axis=1, keepdims=True) - N_EXPERTS
    gden = jnp.sum(jnp.where(is_grp, jnp.exp(gl - gmax), 0.0), axis=1, keepdims=True)
    g_w = 1.0 / gden
    lo = gidx * EXPERTS_PER_GROUP
    in_sel = (lanef >= lo) & (lanef < lo + EXPERTS_PER_GROUP)
    sel = jnp.where(in_sel, logits, low)
    v1 = jnp.max(sel, axis=1, keepdims=True)
    i1 = jnp.min(jnp.where(in_sel & (sel == v1), lanef, 1.0e3), axis=1, keepdims=True)
    in_sel2 = in_sel & (lanef != i1)
    sel2 = jnp.where(in_sel2, logits, low)
    v2 = jnp.max(sel2, axis=1, keepdims=True)
    i2 = jnp.min(jnp.where(in_sel2 & (sel2 == v2), lanef, 1.0e3), axis=1, keepdims=True)
    e2 = jnp.exp(v2 - v1)
    w1 = g_w / (1.0 + e2)
    w2 = g_w * e2 / (1.0 + e2)
    comb_ref[...] = jnp.where(lanef == i1, w1, 0.0) + jnp.where(lanef == i2, w2, 0.0)


def _merge(x, att, hsg, gt, wa, wb, wo, g_ffn, w_router, b_router, tm):
    n = x.shape[0]
    row = lambda w: pl.BlockSpec((tm, w), lambda i: (i, 0))
    sq = _const_spec((D_MODEL, D_MODEL))
    return pl.pallas_call(
        _merge_kernel,
        grid=(n // tm,),
        in_specs=[row(D_MODEL), row(V_COLS), row(LRU_WIDTH), row(2 * D_MODEL), sq, sq, sq,
                  _const_spec((1, D_MODEL)), _const_spec((D_MODEL, 2 * LANES)), _const_spec((1, LANES))],
        out_specs=[row(D_MODEL), row(D_MODEL), row(LANES)],
        out_shape=[jax.ShapeDtypeStruct((n, D_MODEL), f32), jax.ShapeDtypeStruct((n, D_MODEL), bf16),
                   jax.ShapeDtypeStruct((n, LANES), f32)],
        compiler_params=_cparams(("parallel",)),
        name="merge_router",
    )(x, att, hsg, gt, wa, wb, wo, g_ffn, w_router, b_router)


EXPERTS_PER_STEP = 8


MOE_CHUNK = 128


def _moe_kernel(h_ref, comb_ref, wg_ref, wu_ref, wd_ref, y_ref, pt_sc, x_sc, c_sc, ys_sc, seg_sc):
    e = pl.program_id(1)
    tm = h_ref.shape[0]
    lane1 = lax.broadcasted_iota(jnp.int32, (1, LANES), 1)

    @pl.when(e == 0)
    def _():
        comb = comb_ref[...]
        ex = lax.broadcasted_iota(jnp.int32, (LANES, LANES), 0)
        gr = lax.broadcasted_iota(jnp.int32, (LANES, LANES), 1)
        in_group = jnp.where((ex // EXPERTS_PER_GROUP == gr) & (ex < N_EXPERTS), 1.0, 0.0).astype(bf16)
        gsum = jnp.dot(comb.astype(bf16), in_group, preferred_element_type=f32)
        member = jnp.where(gsum > 0.0, 1.0, 0.0)
        r_i = lax.broadcasted_iota(jnp.int32, (tm, tm), 0)
        c_i = lax.broadcasted_iota(jnp.int32, (tm, tm), 1)
        tri = jnp.where(c_i <= r_i, 1.0, 0.0).astype(bf16)
        member_b = member.astype(bf16)
        half = tm // 2
        rank = (jnp.dot(tri[:, :half], member_b[:half], preferred_element_type=f32)
                + jnp.dot(tri[:, half:], member_b[half:], preferred_element_type=f32))
        totals = rank[tm - 1:tm, :]
        offs = jnp.zeros((1, LANES), f32)
        for g in range(1, N_GROUPS):
            prev = jnp.sum(jnp.where(lane1 == g - 1, totals, 0.0), axis=1, keepdims=True)
            offs = offs + jnp.where(lane1 >= g, prev, 0.0)
        seg_sc[0:1, :] = offs.astype(jnp.int32)
        seg_sc[1:2, :] = totals.astype(jnp.int32)
        pos = jnp.sum(member * (offs + rank - 1.0), axis=1, keepdims=True)
        pos_row = jnp.broadcast_to(pos, (tm, LANES)).T[0:1, :].astype(jnp.int32)
        perm = jnp.where(r_i == pos_row, 1.0, 0.0).astype(bf16)
        pt_sc[...] = jnp.where(c_i == pos.astype(jnp.int32), 1.0, 0.0).astype(bf16)
        comb_hi = comb.astype(bf16)
        comb_lo = (comb - comb_hi.astype(f32)).astype(bf16)
        srt = jnp.dot(perm, jnp.concatenate([h_ref[...], comb_hi, comb_lo], axis=1), preferred_element_type=f32)
        x_sc[...] = srt[:, :D_MODEL].astype(bf16)
        c_sc[...] = srt[:, D_MODEL:D_MODEL + LANES] + srt[:, D_MODEL + LANES:]
        ys_sc[...] = jnp.zeros(ys_sc.shape, f32)

    g = (e * EXPERTS_PER_STEP) // EXPERTS_PER_GROUP
    off = jnp.sum(jnp.where(lane1 == g, seg_sc[0:1, :], 0))
    cnt = jnp.sum(jnp.where(lane1 == g, seg_sc[1:2, :], 0))
    lane = lax.broadcasted_iota(jnp.int32, (MOE_CHUNK, LANES), 1)

    def chunk(k, carry):
        rows = pl.ds(pl.multiple_of(k * MOE_CHUNK, MOE_CHUNK), MOE_CHUNK)
        x = x_sc[rows, :]
        cw = c_sc[rows, :]
        hidden = []
        for j in range(EXPERTS_PER_STEP):
            gate = jnp.dot(x, wg_ref[j], preferred_element_type=f32)
            up = jnp.dot(x, wu_ref[j], preferred_element_type=f32)
            c = jnp.sum(jnp.where(lane == e * EXPERTS_PER_STEP + j, cw, 0.0), axis=1, keepdims=True)
            hidden.append((gate * _sigmoid(gate) * up * c).astype(bf16))
        ys_sc[rows, :] += jnp.dot(jnp.concatenate(hidden, axis=1), wd_ref[...], preferred_element_type=f32)
        return carry

    lax.fori_loop(off // MOE_CHUNK, (off + cnt + MOE_CHUNK - 1) // MOE_CHUNK, chunk, 0)

    @pl.when(e == pl.num_programs(1) - 1)
    def _():
        y_ref[...] = jnp.dot(pt_sc[...], ys_sc[...].astype(bf16), preferred_element_type=f32)


def _moe(h2, comb, wg, wu, wd, tm):
    n = h2.shape[0]
    wd_rows = wd.reshape(N_EXPERTS * D_EXPERT, D_MODEL)
    per = EXPERTS_PER_STEP
    assert EXPERTS_PER_GROUP % per == 0 and tm % MOE_CHUNK == 0
    return pl.pallas_call(
        _moe_kernel,
        grid=(n // tm, N_EXPERTS // per),
        in_specs=[pl.BlockSpec((tm, D_MODEL), lambda i, e: (i, 0)),
                  pl.BlockSpec((tm, LANES), lambda i, e: (i, 0)),
                  pl.BlockSpec((per, D_MODEL, D_EXPERT), lambda i, e: (e, 0, 0)),
                  pl.BlockSpec((per, D_MODEL, D_EXPERT), lambda i, e: (e, 0, 0)),
                  pl.BlockSpec((per * D_EXPERT, D_MODEL), lambda i, e: (e, 0))],
        out_specs=pl.BlockSpec((tm, D_MODEL), lambda i, e: (i, 0)),
        out_shape=jax.ShapeDtypeStruct((n, D_MODEL), f32),
        scratch_shapes=[pltpu.VMEM((tm, tm), bf16), pltpu.VMEM((tm, D_MODEL), bf16),
                        pltpu.VMEM((tm, LANES), f32), pltpu.VMEM((tm, D_MODEL), f32),
                        pltpu.VMEM((SUBLANES, LANES), jnp.int32)],
        compiler_params=_cparams(("arbitrary", "arbitrary")),
        name="moe",
    )(h2, comb, wg, wu, wd_rows)


def _ple_kernel(x1_ref, y_ref, p_ref, gp_ref, wpg_ref, bpg_ref, wpi_ref, gfin_ref, o_ref):
    x2 = x1_ref[...] + y_ref[...]
    hp = _rms(x2, gp_ref[...]).astype(bf16)
    pg = _sigmoid(jnp.dot(hp, wpg_ref[...], preferred_element_type=f32) + bpg_ref[...])
    pin = jnp.dot(p_ref[...].astype(bf16), wpi_ref[...], preferred_element_type=f32)
    x3 = x2 + pg * pin
    o_ref[...] = _rms(x3, gfin_ref[...])


def _ple(x1, y, p, g_ple, wpg, bpg, wpi, g_final, tm):
    n = x1.shape[0]
    row = lambda w: pl.BlockSpec((tm, w), lambda i: (i, 0))
    vec = _const_spec((1, D_MODEL))
    return pl.pallas_call(
        _ple_kernel,
        grid=(n // tm,),
        in_specs=[row(D_MODEL), row(D_MODEL), row(PLE_DIM), vec, _const_spec((D_MODEL, D_MODEL)), vec,
                  _const_spec((PLE_DIM, D_MODEL)), vec],
        out_specs=row(D_MODEL),
        out_shape=jax.ShapeDtypeStruct((n, D_MODEL), f32),
        compiler_params=_cparams(("parallel",)),
        name="ple_final",
    )(x1, y, p, g_ple, wpg, bpg, wpi, g_final)


def _tile(n, pref):
    return pref if n % pref == 0 else n


def _stream(x, p, conv0, h0, cache, li, rel_bias, w):
    nb_, tlen, _ = x.shape
    n = nb_ * tlen
    xf = x.reshape(n, D_MODEL)
    lam_init = 0.8 - 0.6 * math.exp(-0.3 * li)
    tk = None if cache is not None else 512
    q, k, kb, v, vb, lx, gl, gt = _in_proj(xf, w["g_mix"], w["w_in"], w["b_merge"], _tile(n, 256), tk)
    if cache is None:
        att = _attn_prompt(q, kb, vb, rel_bias, w["lam_vecs"], w["g_subln"], lam_init, 2 * tk, tk)
    else:
        att = _attn_sample(q, kb, vb, cache[0], cache[1], rel_bias, w["lam_vecs"], w["g_subln"], lam_init,
                           nb_, tlen)
    conv_pad = jnp.pad(conv0, ((0, 0), (SUBLANES - (CONV_W - 1), 0), (0, 0)))
    hsg, cout, hout = _conv_lru(lx, gl, conv_pad, h0[:, None, :], w["w_conv"], w["b_conv"], w["w_rg_a"],
                                w["b_rg_a"], w["w_rg_x"], w["b_rg_x"], w["lru_lambda"], nb_, tlen,
                                _tile(tlen, 256), cache is None)
    tm = _tile(n, 512)
    x1, h2, comb = _merge(xf, att, hsg, gt, w["w_attn_br"], w["w_lru_br"], w["w_out"], w["g_ffn"],
                          w["w_router"], w["b_router"], tm)
    y = _moe(h2, comb, w["w_e_gate"], w["w_e_up"], w["w_e_down"], _tile(n, 1024))
    out = _ple(x1, y, p.reshape(n, PLE_DIM), w["g_ple"], w["w_ple_gate"], w["b_ple_gate"], w["w_ple_in"],
               w["g_final"], tm)
    return out, k, v, cout[:, SUBLANES - (CONV_W - 1):, :], hout[:, 0, :]


def kernel(x_prompt, x_sample, cache_k, cache_v, state_conv, state_lru, p_prompt, p_sample, rel_bias, g_mix, w_in, b_merge, lam_q1, lam_k1, lam_q2, lam_k2, g_subln, w_attn_br, w_conv, b_conv, w_rg_a, b_rg_a, w_rg_x, b_rg_x, lru_lambda, w_lru_br, w_out, g_ffn, w_grp, b_grp, w_rt, b_rt, w_e_gate, w_e_up, w_e_down, g_ple, w_ple_gate, b_ple_gate, w_ple_in, g_final):
    depth = w_in.shape[0]
    assert depth == 1, "the final norm is fused into the single layer's last stage"
    bp, tp, _ = x_prompt.shape
    bs, ts, _ = x_sample.shape
    past = cache_k.shape[2]
    li = 0
    row = lambda a: a[li].reshape(1, -1).astype(f32)
    w_router = jnp.concatenate(
        [jnp.transpose(w_rt[li], (1, 0, 2)).reshape(D_MODEL, N_EXPERTS), w_grp[li]], axis=1).astype(f32)
    w_router = jnp.pad(w_router, ((0, 0), (0, LANES - N_EXPERTS - N_GROUPS)))
    w_router_hi = w_router.astype(bf16)
    w_router = jnp.concatenate([w_router_hi, (w_router - w_router_hi.astype(f32)).astype(bf16)], axis=1)
    b_router = jnp.pad(jnp.concatenate([b_rt[li].reshape(-1), b_grp[li]]).astype(f32),
                       (0, LANES - N_EXPERTS - N_GROUPS)).reshape(1, LANES)
    w = dict(
        g_mix=row(g_mix), w_in=w_in[li].astype(bf16), b_merge=row(b_merge),
        lam_vecs=(row(lam_q1), row(lam_k1), row(lam_q2), row(lam_k2)), g_subln=row(g_subln),
        w_attn_br=w_attn_br[li].astype(bf16), w_conv=w_conv[li].astype(f32), b_conv=row(b_conv),
        w_rg_a=w_rg_a[li].astype(bf16), b_rg_a=row(b_rg_a), w_rg_x=w_rg_x[li].astype(bf16), b_rg_x=row(b_rg_x),
        lru_lambda=row(lru_lambda), w_lru_br=w_lru_br[li].astype(bf16), w_out=w_out[li].astype(bf16),
        g_ffn=row(g_ffn), w_router=w_router, b_router=b_router,
        w_e_gate=w_e_gate[li].astype(bf16), w_e_up=w_e_up[li].astype(bf16), w_e_down=w_e_down[li].astype(bf16),
        g_ple=row(g_ple), w_ple_gate=w_ple_gate[li].astype(bf16), b_ple_gate=row(b_ple_gate),
        w_ple_in=w_ple_in[li].astype(bf16), g_final=g_final.reshape(1, -1).astype(f32),
    )
    conv0 = jnp.zeros((bp, CONV_W - 1, LRU_WIDTH), f32)
    h0 = jnp.zeros((bp, LRU_WIDTH), f32)
    yp, kp, vp, cp, hp = _stream(x_prompt, p_prompt[li], conv0, h0, None, li, rel_bias, w)
    cache = (cache_k[li].reshape(bs, past, QK_COLS), cache_v[li].reshape(bs, past, V_COLS))
    ys, ks, vs, cs, hs = _stream(x_sample, p_sample[li], state_conv[li], state_lru[li], cache, li, rel_bias, w)
    return (yp.reshape(bp, tp, D_MODEL), ys.reshape(bs, ts, D_MODEL),
            kp.reshape(1, bp, tp, N_HEADS, 2, HEAD_DIM), vp.reshape(1, bp, tp, N_HEADS, V_DIM),
            cp[None], hp[None],
            ks.reshape(1, bs, ts, N_HEADS, 2, HEAD_DIM), vs.reshape(1, bs, ts, N_HEADS, V_DIM),
            cs[None], hs[None])
```

```python
import functools
import math

import jax
import jax.numpy as jnp
from jax import lax
from jax.experimental import pallas as pl
from jax.experimental.pallas import tpu as pltpu

f32 = jnp.float32
bf16 = jnp.bfloat16

D_MODEL = 1024
N_HEADS = 8
HEAD_DIM = 64
V_DIM = 128
QK_COLS = N_HEADS * 2 * HEAD_DIM
V_COLS = N_HEADS * V_DIM
LRU_WIDTH = 1024
LRU_BLOCKS = 8
LRU_BLOCK_W = LRU_WIDTH // LRU_BLOCKS
CONV_W = 4
LRU_C = 8.0
CHUNK = 64
N_BUCKETS = 32
MAX_DISTANCE = 128
N_GROUPS = 4
EXPERTS_PER_GROUP = 8
N_EXPERTS = N_GROUPS * EXPERTS_PER_GROUP
D_EXPERT = 256
PLE_DIM = 256
EPS = 1e-6
NEG_INF = -1e30
LOG2E = math.log2(math.e)
ONES_ROWS = 16
LANES = 128
SUBLANES = 8
VMEM_LIMIT = 56 * 1024 * 1024

C_Q, C_K, C_V = 0, QK_COLS, 2 * QK_COLS
C_LX = 2 * QK_COLS + V_COLS
C_LY = C_LX + LRU_WIDTH
C_G = C_LY + LRU_WIDTH
IN_COLS = C_G + 2 * D_MODEL


def _cparams(sem):
    return pltpu.CompilerParams(dimension_semantics=sem, vmem_limit_bytes=VMEM_LIMIT)


def _const_spec(shape):
    nd = len(shape)
    return pl.BlockSpec(shape, lambda *_: (0,) * nd, pipeline_mode=pl.Buffered(1))


def _rms(x, g):
    return x * lax.rsqrt(jnp.mean(x * x, axis=-1, keepdims=True) + EPS) * g


def _sigmoid(x):
    return 1.0 / (1.0 + jnp.exp(-x))


def _gelu_tanh(x):
    c = math.sqrt(2.0 / math.pi)
    return 0.5 * x * (1.0 + jnp.tanh(c * (x + 0.044715 * (x * x * x))))


def _in_proj_kernel(x_ref, g_ref, w_ref, bm_ref, q_ref, k_ref, kb_ref, v_ref, vb_ref, lx_ref, gl_ref, gt_ref,
                    *, transposed_v):
    h = _rms(x_ref[...], g_ref[...]).astype(bf16)

    def proj(lo, width):
        return jnp.dot(h, w_ref[:, lo:lo + width], preferred_element_type=f32)

    q_ref[...] = (proj(C_Q, QK_COLS) * (HEAD_DIM ** -0.5 * LOG2E)).astype(bf16)
    k = proj(C_K, QK_COLS)
    k_ref[...] = k
    kb_ref[...] = k.astype(bf16)
    v = proj(C_V, V_COLS)
    v_ref[...] = v
    if transposed_v:
        for hd in range(N_HEADS):
            vb_ref[hd, :V_DIM, :] = v[:, hd * V_DIM:(hd + 1) * V_DIM].T.astype(bf16)
            vb_ref[hd, V_DIM:, :] = jnp.ones((ONES_ROWS, v.shape[0]), bf16)
    else:
        vb_ref[...] = v.astype(bf16)
    lx_ref[...] = proj(C_LX, LRU_WIDTH)
    gl_ref[...] = _gelu_tanh(proj(C_LY, LRU_WIDTH))
    gt_ref[...] = _sigmoid(proj(C_G, 2 * D_MODEL) + bm_ref[...])


def _in_proj(x, g_mix, w_in_b, b_merge, tm, tq=None):
    n = x.shape[0]
    row = lambda w: pl.BlockSpec((tm, w), lambda i: (i, 0))
    if tq is None:
        vb_shape, vb_spec = jax.ShapeDtypeStruct((n, V_COLS), bf16), row(V_COLS)
    else:
        per = tq // tm
        vb_shape = jax.ShapeDtypeStruct((N_HEADS, n // tq, V_DIM + ONES_ROWS, tq), bf16)
        vb_spec = pl.BlockSpec((N_HEADS, None, V_DIM + ONES_ROWS, tm), lambda i: (0, i // per, 0, i % per))
    outs = [
        jax.ShapeDtypeStruct((n, QK_COLS), bf16),
        jax.ShapeDtypeStruct((n, QK_COLS), f32),
        jax.ShapeDtypeStruct((n, QK_COLS), bf16),
        jax.ShapeDtypeStruct((n, V_COLS), f32),
        vb_shape,
        jax.ShapeDtypeStruct((n, LRU_WIDTH), f32),
        jax.ShapeDtypeStruct((n, LRU_WIDTH), f32),
        jax.ShapeDtypeStruct((n, 2 * D_MODEL), f32),
    ]
    return pl.pallas_call(
        functools.partial(_in_proj_kernel, transposed_v=tq is not None),
        grid=(n // tm,),
        in_specs=[row(D_MODEL), _const_spec((1, D_MODEL)), _const_spec((D_MODEL, IN_COLS)),
                  _const_spec((1, 2 * D_MODEL))],
        out_specs=[row(QK_COLS), row(QK_COLS), row(QK_COLS), row(V_COLS), vb_spec,
                   row(LRU_WIDTH), row(LRU_WIDTH), row(2 * D_MODEL)],
        out_shape=outs,
        compiler_params=_cparams(("parallel",)),
        name="in_proj",
    )(x, g_mix, w_in_b, b_merge)


def _rel_bucket(rel):
    half = N_BUCKETS // 2
    max_exact = half // 2
    ret = jnp.where(rel > 0, half, 0)
    n = jnp.abs(rel)
    nf = jnp.maximum(n, 1).astype(f32)
    large = max_exact + (jnp.log(nf / max_exact) / math.log(MAX_DISTANCE / max_exact)
                         * (half - max_exact)).astype(jnp.int32)
    large = jnp.minimum(large, half - 1)
    return ret + jnp.where(n < max_exact, n, large)


def _toeplitz(t, nq, nk):
    nh, length = t.shape
    a = jnp.broadcast_to(t[:, None, :], (nh, nq, length))
    a = jnp.pad(a, ((0, 0), (0, 0), (0, 1))).reshape(nh, nq * (length + 1))
    a = a[:, :nq * length].reshape(nh, nq, length)
    return a[:, :, nq - 1:nq - 1 + nk]


def _bias_table(rel_bias, qpos0, nq, kpos0, nk, shift):
    rel = jnp.arange(-(nq - 1), nk) + (kpos0 - qpos0)
    t = (rel_bias.astype(f32)[_rel_bucket(rel)].T - shift[:, None]) * LOG2E
    qpos = qpos0 + jnp.arange(nq)
    kpos = kpos0 + jnp.arange(nk)
    mask = (kpos[None, :] // CHUNK) <= (qpos[:, None] // CHUNK)
    return jnp.where(mask[None], _toeplitz(t, nq, nk), NEG_INF)


def _bias_ring(rel_bias, offset, tq, shift):
    r = jnp.arange(2 * tq)
    x = jnp.where(r < tq, r, r - 2 * tq)
    t = (rel_bias.astype(f32)[_rel_bucket(offset - x)].T - shift[:, None]) * LOG2E
    return t[:, None, :]


def _lam_value(lq1, lk1, lq2, lk2, lam_init):
    s1 = jnp.sum(lq1[...] * lk1[...], axis=-1, keepdims=True)
    s2 = jnp.sum(lq2[...] * lk2[...], axis=-1, keepdims=True)
    return jnp.exp(s1) - jnp.exp(s2) + lam_init


def _split_maps(q):
    lane = lax.broadcasted_iota(jnp.int32, q.shape, 1)
    zero = jnp.zeros_like(q)
    return jnp.where(lane < HEAD_DIM, q, zero), jnp.where(lane >= HEAD_DIM, q, zero)


def _qk(qz, kt):
    return lax.dot_general(qz, kt, (((1,), (1,)), ((), ())), preferred_element_type=f32)


def _finish_heads(o1, o2, lam, gs, lam_init):
    att = o1 - lam * o2
    return _rms(att, gs) * (1.0 - lam_init)


def _attn_prompt_kernel(q_ref, k_ref, vt_ref, ring_ref, lq1, lk1, lq2, lk2, gs_ref, o_ref,
                        nb_sc, s_sc, p_sc, al_sc, tm_sc, m_sc, acc_sc, *, tq, tk, lam_init):
    i = pl.program_id(1)
    ring_len = ring_ref.shape[-1]

    def bias_tile(ring):
        return pltpu.roll(jnp.broadcast_to(ring, (tk, ring_len)), 0, 1, stride=1, stride_axis=0)[:, :tq]

    @pl.when(i == 0)
    def _():
        kk = lax.broadcasted_iota(jnp.int32, (tk, tq), 0)
        qq = lax.broadcasted_iota(jnp.int32, (tk, tq), 1)
        nb_sc[0] = jnp.where((kk // CHUNK) <= (qq // CHUNK), bias_tile(ring_ref[1]), NEG_INF)
        nb_sc[1] = jnp.where(((kk + tk) // CHUNK) <= (qq // CHUNK), bias_tile(ring_ref[2]), NEG_INF)
        nb_sc[2] = jnp.full((tk, tq), NEG_INF, f32)

    @pl.when(i == 1)
    def _():
        nb_sc[2] = nb_sc[1]
        nb_sc[1] = nb_sc[0]
        nb_sc[0] = bias_tile(ring_ref[0])

    qt = q_ref[...].astype(f32).T
    row = lax.broadcasted_iota(jnp.int32, qt.shape, 0)
    qz = (jnp.where(row < HEAD_DIM, qt, 0.0).astype(bf16), jnp.where(row >= HEAD_DIM, qt, 0.0).astype(bf16))

    m_sc[...] = jnp.full(m_sc.shape, NEG_INF, f32)
    acc_sc[...] = jnp.zeros(acc_sc.shape, f32)

    def qk(slot, tile, bias):
        kt = k_ref[pl.ds(pl.multiple_of(tile * tk, tk), tk), :]
        for mi in range(2):
            s = jnp.dot(kt, qz[mi], preferred_element_type=f32)
            s = s if bias is None else s + bias
            s_sc[mi, slot] = s
            tm_sc[mi, slot] = jnp.max(s, axis=0, keepdims=True)

    def softmax(slot):
        for mi in range(2):
            s = s_sc[mi, slot]
            m_prev = m_sc[mi]
            m_new = jnp.maximum(m_prev, tm_sc[mi, slot])
            al_sc[mi, slot] = jnp.exp2(m_prev - m_new)
            p_sc[mi, slot] = jnp.exp2((s - m_new).astype(bf16))
            m_sc[mi] = m_new

    def pv(slot, vtile):
        va = vt_ref[vtile]
        for mi in range(2):
            acc_sc[mi] = acc_sc[mi] * al_sc[mi, slot] + jnp.dot(va, p_sc[mi, slot], preferred_element_type=f32)

    per = tq // tk
    a0 = jnp.maximum(per * i - 1, 0)
    n_far = jnp.maximum(per * i - 1, 0)
    last_tile = vt_ref.shape[0] - 1

    def far_k(f):
        return jnp.clip(f, 0, jnp.maximum(n_far - 1, 0))

    qk(0, a0, nb_sc[0])
    qk(1, a0 + 1, nb_sc[1])
    softmax(0)
    qk(0, jnp.minimum(a0 + 2, last_tile), nb_sc[2])
    softmax(1)
    pv(0, a0)

    @pl.when(i == 0)
    def _():
        pv(1, a0 + 1)

    @pl.when(i > 0)
    def _():
        qk(1, 0, None)
        softmax(0)
        pv(1, a0 + 1)
        qk(0, far_k(1), None)
        softmax(1)
        pv(0, a0 + 2)

        def far_pair(p, carry):
            qk(1, 2 * p + 2, None)
            softmax(0)
            pv(1, 2 * p)
            qk(0, far_k(2 * p + 3), None)
            softmax(1)
            pv(0, 2 * p + 1)
            return carry

        lax.fori_loop(0, (n_far - 1) // 2, far_pair, 0)
        pv(1, n_far - 1)

    a1 = acc_sc[0]
    a2 = acc_sc[1]
    o1 = a1[:V_DIM] / a1[V_DIM:V_DIM + 1]
    o2 = a2[:V_DIM] / a2[V_DIM:V_DIM + 1]
    lam = _lam_value(lq1, lk1, lq2, lk2, lam_init)
    att = (o1 - lam * o2).T
    o_ref[...] = (_rms(att, gs_ref[...]) * (1.0 - lam_init)).astype(o_ref.dtype)


def _attn_prompt(q, kb, vt, rel_bias, lam_vecs, g_subln, lam_init, tq, tk):
    t = q.shape[0]
    nq = t // tq
    assert tq == 2 * tk and t % tq == 0
    far = rel_bias.astype(f32)[_rel_bucket(jnp.asarray(-4 * MAX_DISTANCE))]
    rings = jnp.stack([_bias_ring(rel_bias, d, tq, far) for d in (-tk, 0, tk)], axis=1)
    vec = _const_spec((1, HEAD_DIM))
    return pl.pallas_call(
        functools.partial(_attn_prompt_kernel, tq=tq, tk=tk, lam_init=lam_init),
        grid=(N_HEADS, nq),
        in_specs=[
            pl.BlockSpec((tq, 2 * HEAD_DIM), lambda h, i: (i, h)),
            pl.BlockSpec((t, 2 * HEAD_DIM), lambda h, i: (0, h)),
            pl.BlockSpec((None, t // tk, V_DIM + ONES_ROWS, tk), lambda h, i: (h, 0, 0, 0)),
            pl.BlockSpec((None, 3, 1, 2 * tq), lambda h, i: (h, 0, 0, 0)),
            vec, vec, vec, vec, _const_spec((1, V_DIM)),
        ],
        out_specs=pl.BlockSpec((tq, V_DIM), lambda h, i: (i, h)),
        out_shape=jax.ShapeDtypeStruct((t, V_COLS), bf16),
        scratch_shapes=[pltpu.VMEM((3, tk, tq), f32), pltpu.VMEM((2, 2, tk, tq), f32),
                        pltpu.VMEM((2, 2, tk, tq), bf16), pltpu.VMEM((2, 2, 1, tq), f32),
                        pltpu.VMEM((2, 2, 1, tq), f32), pltpu.VMEM((2, 1, tq), f32),
                        pltpu.VMEM((2, V_DIM + ONES_ROWS, tq), f32)],
        compiler_params=_cparams(("arbitrary", "arbitrary")),
        name="attn_prompt",
    )(q, kb, vt, rings, *lam_vecs, g_subln)


def _attn_sample_kernel(q_ref, kp_ref, vp_ref, kn_ref, vn_ref, nbp_ref, nbn_ref, lq1, lk1, lq2, lk2, gs_ref,
                        o_ref, *, lam_init):
    lam = _lam_value(lq1, lk1, lq2, lk2, lam_init)
    for hd in range(N_HEADS):
        qk_cols = slice(hd * 2 * HEAD_DIM, (hd + 1) * 2 * HEAD_DIM)
        v_cols = slice(hd * V_DIM, (hd + 1) * V_DIM)
        qz = _split_maps(q_ref[:, qk_cols])
        kp = kp_ref[:, qk_cols].astype(bf16)
        vp = vp_ref[:, v_cols].astype(bf16)
        kn = kn_ref[:, qk_cols]
        vn = vn_ref[:, v_cols]
        outs = []
        for mi in range(2):
            sp = _qk(qz[mi], kp) + nbp_ref[hd]
            sn = _qk(qz[mi], kn) + nbn_ref[hd]
            m = jnp.maximum(jnp.max(sp, axis=1, keepdims=True), jnp.max(sn, axis=1, keepdims=True))
            pp = jnp.exp2(sp - m).astype(bf16)
            pn = jnp.exp2(sn - m).astype(bf16)
            den = (jnp.sum(pp.astype(f32), axis=1, keepdims=True)
                   + jnp.sum(pn.astype(f32), axis=1, keepdims=True))
            num = jnp.dot(pp, vp, preferred_element_type=f32) + jnp.dot(pn, vn, preferred_element_type=f32)
            outs.append(num / den)
        o_ref[:, v_cols] = _finish_heads(outs[0], outs[1], lam, gs_ref[...], lam_init).astype(o_ref.dtype)


def _attn_sample(q, kb, vb, cache_k, cache_v, rel_bias, lam_vecs, g_subln, lam_init, nb_, ts):
    past = cache_k.shape[1]
    zero = jnp.zeros((N_HEADS,), f32)
    nbp = _bias_table(rel_bias, past, ts, 0, past, zero)
    nbn = _bias_table(rel_bias, past, ts, past, ts, zero)
    vec = _const_spec((1, HEAD_DIM))
    return pl.pallas_call(
        functools.partial(_attn_sample_kernel, lam_init=lam_init),
        grid=(nb_,),
        in_specs=[
            pl.BlockSpec((ts, QK_COLS), lambda b: (b, 0)),
            pl.BlockSpec((None, past, QK_COLS), lambda b: (b, 0, 0)),
            pl.BlockSpec((None, past, V_COLS), lambda b: (b, 0, 0)),
            pl.BlockSpec((ts, QK_COLS), lambda b: (b, 0)),
            pl.BlockSpec((ts, V_COLS), lambda b: (b, 0)),
            _const_spec((N_HEADS, ts, past)), _const_spec((N_HEADS, ts, ts)),
            vec, vec, vec, vec, _const_spec((1, V_DIM)),
        ],
        out_specs=pl.BlockSpec((ts, V_COLS), lambda b: (b, 0)),
        out_shape=jax.ShapeDtypeStruct((nb_ * ts, V_COLS), bf16),
        compiler_params=_cparams(("parallel",)),
        name="attn_sample",
    )(q, cache_k, cache_v, kb, vb, nbp, nbn, *lam_vecs, g_subln)


def _conv_lru_kernel(lx_ref, gl_ref, c0_ref, h0_ref, wc_ref, bc_ref, wa_ref, ba_ref, wx_ref, bx_ref, lam_ref,
                     hsg_ref, cout_ref, hout_ref, xbuf, a_sc, b_sc, hcar, *, tt, first_pos_is_zero):
    t = pl.program_id(1)

    @pl.when(t == 0)
    def _():
        xbuf[0:SUBLANES] = c0_ref[...]
        hcar[...] = jnp.broadcast_to(h0_ref[...], hcar.shape)

    xbuf[SUBLANES:SUBLANES + tt] = lx_ref[...]
    xc = jnp.broadcast_to(bc_ref[...], (tt, LRU_WIDTH))
    for j in range(CONV_W):
        lo = SUBLANES - (CONV_W - 1) + j
        xc = xc + xbuf[lo:lo + tt, :] * wc_ref[j:j + 1, :]
    tail = xbuf[tt:tt + SUBLANES, :]
    xbuf[0:SUBLANES] = tail
    cout_ref[...] = tail

    xcb = xc.astype(bf16)

    def block_diag(w_ref, b_ref):
        cols = [jnp.dot(xcb[:, n * LRU_BLOCK_W:(n + 1) * LRU_BLOCK_W], w_ref[n], preferred_element_type=f32)
                for n in range(LRU_BLOCKS)]
        return jnp.concatenate(cols, axis=1) + b_ref[...]

    r = _sigmoid(block_diag(wa_ref, ba_ref))
    ig = _sigmoid(block_diag(wx_ref, bx_ref))
    z = -lam_ref[...]
    softplus = jnp.maximum(z, 0.0) + jnp.log1p(jnp.exp(-jnp.abs(z)))
    log_a = -LRU_C * r * softplus
    a = jnp.exp(log_a)
    th = jnp.tanh(log_a)
    m2 = -2.0 * th / (1.0 - th)
    mult = jnp.where(m2 > 0.0, m2 * lax.rsqrt(m2), 0.0)
    if first_pos_is_zero:
        row = lax.broadcasted_iota(jnp.int32, (tt, LRU_WIDTH), 0)
        mult = jnp.where((row == 0) & (t == 0), 1.0, mult)
    bv = mult * ig * xc

    groups = (tt // SUBLANES, SUBLANES, LRU_WIDTH)
    a = a.reshape(groups)
    bv = bv.reshape(groups)
    sub = lax.broadcasted_iota(jnp.int32, groups, 1)
    d = 1
    while d < SUBLANES:
        valid = sub >= d
        a_s = pltpu.roll(a, d, axis=1)
        b_s = pltpu.roll(bv, d, axis=1)
        bv = jnp.where(valid, a * b_s + bv, bv)
        a = jnp.where(valid, a * a_s, a)
        d *= 2
    a_sc[...] = a.reshape(tt, LRU_WIDTH)
    b_sc[...] = bv.reshape(tt, LRU_WIDTH)

    def group(g, h):
        sl = pl.ds(pl.multiple_of(g * SUBLANES, SUBLANES), SUBLANES)
        hg = a_sc[sl, :] * h + b_sc[sl, :]
        b_sc[sl, :] = hg
        return jnp.broadcast_to(hg[SUBLANES - 1:SUBLANES, :], hg.shape)

    h = lax.fori_loop(0, tt // SUBLANES, group, hcar[...], unroll=4)
    hcar[...] = h
    hout_ref[...] = h[0:1, :]
    hsg_ref[...] = (b_sc[...] * gl_ref[...]).astype(hsg_ref.dtype)


def _conv_lru(lx, gl, conv0, h0, w_conv, b_conv, wa, ba, wx, bx, lru_lambda, nb_, tlen, tt, first_pos_is_zero):
    nt = tlen // tt
    rows = pl.BlockSpec((tt, LRU_WIDTH), lambda b, t: (b * nt + t, 0))
    vecw = _const_spec((1, LRU_WIDTH))
    wblk = _const_spec((LRU_BLOCKS, LRU_BLOCK_W, LRU_BLOCK_W))
    return pl.pallas_call(
        functools.partial(_conv_lru_kernel, tt=tt, first_pos_is_zero=first_pos_is_zero),
        grid=(nb_, nt),
        in_specs=[rows, rows,
                  pl.BlockSpec((None, SUBLANES, LRU_WIDTH), lambda b, t: (b, 0, 0)),
                  pl.BlockSpec((None, 1, LRU_WIDTH), lambda b, t: (b, 0, 0)),
                  _const_spec((CONV_W, LRU_WIDTH)), vecw, wblk, vecw, wblk, vecw, vecw],
        out_specs=[rows,
                   pl.BlockSpec((None, SUBLANES, LRU_WIDTH), lambda b, t: (b, 0, 0)),
                   pl.BlockSpec((None, 1, LRU_WIDTH), lambda b, t: (b, 0, 0))],
        out_shape=[jax.ShapeDtypeStruct((nb_ * tlen, LRU_WIDTH), bf16),
                   jax.ShapeDtypeStruct((nb_, SUBLANES, LRU_WIDTH), f32),
                   jax.ShapeDtypeStruct((nb_, 1, LRU_WIDTH), f32)],
        scratch_shapes=[pltpu.VMEM((tt + SUBLANES, LRU_WIDTH), f32), pltpu.VMEM((tt, LRU_WIDTH), f32),
                        pltpu.VMEM((tt, LRU_WIDTH), f32), pltpu.VMEM((SUBLANES, LRU_WIDTH), f32)],
        compiler_params=_cparams(("arbitrary", "arbitrary")),
        name="conv_lru",
    )(lx, gl, conv0, h0, w_conv, b_conv, wa, ba, wx, bx, lru_lambda)


def _merge_kernel(x_ref, att_ref, hsg_ref, gt_ref, wa_ref, wb_ref, wo_ref, gf_ref, wr_ref, br_ref,
                  x1_ref, h2_ref, comb_ref):
    bra = jnp.dot(att_ref[...], wa_ref[...], preferred_element_type=f32)
    brb = jnp.dot(hsg_ref[...], wb_ref[...], preferred_element_type=f32)
    gt = gt_ref[...]
    m = (gt[:, :D_MODEL] * bra + gt[:, D_MODEL:] * brb).astype(bf16)
    x1 = x_ref[...] + jnp.dot(m, wo_ref[...], preferred_element_type=f32)
    x1_ref[...] = x1
    h2 = _rms(x1, gf_ref[...])
    h2_hi = h2.astype(bf16)
    h2_ref[...] = h2_hi

    h2_lo = (h2 - h2_hi.astype(f32)).astype(bf16)
    parts = (jnp.dot(h2_hi, wr_ref[...], preferred_element_type=f32)
             + jnp.dot(h2_lo, wr_ref[...], preferred_element_type=f32))
    logits = parts[:, :LANES] + parts[:, LANES:] + br_ref[...]
    lane = lax.broadcasted_iota(jnp.int32, logits.shape, 1)
    lanef = lane.astype(f32)
    low = jnp.float32(-3.0e38)
    is_grp = (lane >= N_EXPERTS) & (lane < N_EXPERTS + N_GROUPS)
    gl = jnp.where(is_grp, logits, low)
    gmax = jnp.max(gl, axis=1, keepdims=True)
    gidx = jnp.min(jnp.where(is_grp & (gl == gmax), lanef, 1.0e3), axis=1, keepdims=True) - N_EXPERTS
    gden = jnp.sum(jnp.where(is_grp, jnp.exp(gl - gmax), 0.0), axis=1, keepdims=True)
    g_w = 1.0 / gden
    lo = gidx * EXPERTS_PER_GROUP
    in_sel = (lanef >= lo) & (lanef < lo + EXPERTS_PER_GROUP)
    sel = jnp.where(in_sel, logits, low)
    v1 = jnp.max(sel, axis=1, keepdims=True)
    i1 = jnp.min(jnp.where(in_sel & (sel == v1), lanef, 1.0e3), axis=1, keepdims=True)
    in_sel2 = in_sel & (lanef != i1)
    sel2 = jnp.where(in_sel2, logits, low)
    v2 = jnp.max(sel2, axis=1, keepdims=True)
    i2 = jnp.min(jnp.where(in_sel2 & (sel2 == v2), lanef, 1.0e3), axis=1, keepdims=True)
    e2 = jnp.exp(v2 - v1)
    w1 = g_w / (1.0 + e2)
    w2 = g_w * e2 / (1.0 + e2)
    comb_ref[...] = jnp.where(lanef == i1, w1, 0.0) + jnp.where(lanef == i2, w2, 0.0)


def _merge(x, att, hsg, gt, wa, wb, wo, g_ffn, w_router, b_router, tm):
    n = x.shape[0]
    row = lambda w: pl.BlockSpec((tm, w), lambda i: (i, 0))
    sq = _const_spec((D_MODEL, D_MODEL))
    return pl.pallas_call(
        _merge_kernel,
        grid=(n // tm,),
        in_specs=[row(D_MODEL), row(V_COLS), row(LRU_WIDTH), row(2 * D_MODEL), sq, sq, sq,
                  _const_spec((1, D_MODEL)), _const_spec((D_MODEL, 2 * LANES)), _const_spec((1, LANES))],
        out_specs=[row(D_MODEL), row(D_MODEL), row(LANES)],
        out_shape=[jax.ShapeDtypeStruct((n, D_MODEL), f32), jax.ShapeDtypeStruct((n, D_MODEL), bf16),
                   jax.ShapeDtypeStruct((n, LANES), f32)],
        compiler_params=_cparams(("parallel",)),
        name="merge_router",
    )(x, att, hsg, gt, wa, wb, wo, g_ffn, w_router, b_router)


EXPERTS_PER_STEP = 8


MOE_CHUNK = 128


def _moe_kernel(h_ref, comb_ref, wg_ref, wu_ref, wd_ref, y_ref, pt_sc, x_sc, c_sc, ys_sc, seg_sc):
    e = pl.program_id(1)
    tm = h_ref.shape[0]
    lane1 = lax.broadcasted_iota(jnp.int32, (1, LANES), 1)

    @pl.when(e == 0)
    def _():
        comb = comb_ref[...]
        ex = lax.broadcasted_iota(jnp.int32, (LANES, LANES), 0)
        gr = lax.broadcasted_iota(jnp.int32, (LANES, LANES), 1)
        in_group = jnp.where((ex // EXPERTS_PER_GROUP == gr) & (ex < N_EXPERTS), 1.0, 0.0).astype(bf16)
        gsum = jnp.dot(comb.astype(bf16), in_group, preferred_element_type=f32)
        member = jnp.where(gsum > 0.0, 1.0, 0.0)
        r_i = lax.broadcasted_iota(jnp.int32, (tm, tm), 0)
        c_i = lax.broadcasted_iota(jnp.int32, (tm, tm), 1)
        tri = jnp.where(c_i <= r_i, 1.0, 0.0).astype(bf16)
        member_b = member.astype(bf16)
        half = tm // 2
        rank = (jnp.dot(tri[:, :half], member_b[:half], preferred_element_type=f32)
                + jnp.dot(tri[:, half:], member_b[half:], preferred_element_type=f32))
        totals = rank[tm - 1:tm, :]
        offs = jnp.zeros((1, LANES), f32)
        for g in range(1, N_GROUPS):
            prev = jnp.sum(jnp.where(lane1 == g - 1, totals, 0.0), axis=1, keepdims=True)
            offs = offs + jnp.where(lane1 >= g, prev, 0.0)
        seg_sc[0:1, :] = offs.astype(jnp.int32)
        seg_sc[1:2, :] = totals.astype(jnp.int32)
        pos = jnp.sum(member * (offs + rank - 1.0), axis=1, keepdims=True)
        pos_row = jnp.broadcast_to(pos, (tm, LANES)).T[0:1, :].astype(jnp.int32)
        perm = jnp.where(r_i == pos_row, 1.0, 0.0).astype(bf16)
        pt_sc[...] = jnp.where(c_i == pos.astype(jnp.int32), 1.0, 0.0).astype(bf16)
        comb_hi = comb.astype(bf16)
        comb_lo = (comb - comb_hi.astype(f32)).astype(bf16)
        srt = jnp.dot(perm, jnp.concatenate([h_ref[...], comb_hi, comb_lo], axis=1), preferred_element_type=f32)
        x_sc[...] = srt[:, :D_MODEL].astype(bf16)
        c_sc[...] = srt[:, D_MODEL:D_MODEL + LANES] + srt[:, D_MODEL + LANES:]
        ys_sc[...] = jnp.zeros(ys_sc.shape, f32)

    g = (e * EXPERTS_PER_STEP) // EXPERTS_PER_GROUP
    off = jnp.sum(jnp.where(lane1 == g, seg_sc[0:1, :], 0))
    cnt = jnp.sum(jnp.where(lane1 == g, seg_sc[1:2, :], 0))
    lane = lax.broadcasted_iota(jnp.int32, (MOE_CHUNK, LANES), 1)

    def chunk(k, carry):
        rows = pl.ds(pl.multiple_of(k * MOE_CHUNK, MOE_CHUNK), MOE_CHUNK)
        x = x_sc[rows, :]
        cw = c_sc[rows, :]
        hidden = []
        for j in range(EXPERTS_PER_STEP):
            gate = jnp.dot(x, wg_ref[j], preferred_element_type=f32)
            up = jnp.dot(x, wu_ref[j], preferred_element_type=f32)
            c = jnp.sum(jnp.where(lane == e * EXPERTS_PER_STEP + j, cw, 0.0), axis=1, keepdims=True)
            hidden.append((gate * _sigmoid(gate) * up * c).astype(bf16))
        ys_sc[rows, :] += jnp.dot(jnp.concatenate(hidden, axis=1), wd_ref[...], preferred_element_type=f32)
        return carry

    lax.fori_loop(off // MOE_CHUNK, (off + cnt + MOE_CHUNK - 1) // MOE_CHUNK, chunk, 0)

    @pl.when(e == pl.num_programs(1) - 1)
    def _():
        y_ref[...] = jnp.dot(pt_sc[...], ys_sc[...].astype(bf16), preferred_element_type=f32)


def _moe(h2, comb, wg, wu, wd, tm):
    n = h2.shape[0]
    wd_rows = wd.reshape(N_EXPERTS * D_EXPERT, D_MODEL)
    per = EXPERTS_PER_STEP
    assert EXPERTS_PER_GROUP % per == 0 and tm % MOE_CHUNK == 0
    return pl.pallas_call(
        _moe_kernel,
        grid=(n // tm, N_EXPERTS // per),
        in_specs=[pl.BlockSpec((tm, D_MODEL), lambda i, e: (i, 0)),
                  pl.BlockSpec((tm, LANES), lambda i, e: (i, 0)),
                  pl.BlockSpec((per, D_MODEL, D_EXPERT), lambda i, e: (e, 0, 0)),
                  pl.BlockSpec((per, D_MODEL, D_EXPERT), lambda i, e: (e, 0, 0)),
                  pl.BlockSpec((per * D_EXPERT, D_MODEL), lambda i, e: (e, 0))],
        out_specs=pl.BlockSpec((tm, D_MODEL), lambda i, e: (i, 0)),
        out_shape=jax.ShapeDtypeStruct((n, D_MODEL), f32),
        scratch_shapes=[pltpu.VMEM((tm, tm), bf16), pltpu.VMEM((tm, D_MODEL), bf16),
                        pltpu.VMEM((tm, LANES), f32), pltpu.VMEM((tm, D_MODEL), f32),
                        pltpu.VMEM((SUBLANES, LANES), jnp.int32)],
        compiler_params=_cparams(("arbitrary", "arbitrary")),
        name="moe",
    )(h2, comb, wg, wu, wd_rows)


def _ple_kernel(x1_ref, y_ref, p_ref, gp_ref, wpg_ref, bpg_ref, wpi_ref, gfin_ref, o_ref):
    x2 = x1_ref[...] + y_ref[...]
    hp = _rms(x2, gp_ref[...]).astype(bf16)
    pg = _sigmoid(jnp.dot(hp, wpg_ref[...], preferred_element_type=f32) + bpg_ref[...])
    pin = jnp.dot(p_ref[...].astype(bf16), wpi_ref[...], preferred_element_type=f32)
    x3 = x2 + pg * pin
    o_ref[...] = _rms(x3, gfin_ref[...])


def _ple(x1, y, p, g_ple, wpg, bpg, wpi, g_final, tm):
    n = x1.shape[0]
    row = lambda w: pl.BlockSpec((tm, w), lambda i: (i, 0))
    vec = _const_spec((1, D_MODEL))
    return pl.pallas_call(
        _ple_kernel,
        grid=(n // tm,),
        in_specs=[row(D_MODEL), row(D_MODEL), row(PLE_DIM), vec, _const_spec((D_MODEL, D_MODEL)), vec,
                  _const_spec((PLE_DIM, D_MODEL)), vec],
        out_specs=row(D_MODEL),
        out_shape=jax.ShapeDtypeStruct((n, D_MODEL), f32),
        compiler_params=_cparams(("parallel",)),
        name="ple_final",
    )(x1, y, p, g_ple, wpg, bpg, wpi, g_final)


def _tile(n, pref):
    return pref if n % pref == 0 else n


def _stream(x, p, conv0, h0, cache, li, rel_bias, w):
    nb_, tlen, _ = x.shape
    n = nb_ * tlen
    xf = x.reshape(n, D_MODEL)
    lam_init = 0.8 - 0.6 * math.exp(-0.3 * li)
    tk = None if cache is not None else 512
    q, k, kb, v, vb, lx, gl, gt = _in_proj(xf, w["g_mix"], w["w_in"], w["b_merge"], _tile(n, 256), tk)
    if cache is None:
        att = _attn_prompt(q, kb, vb, rel_bias, w["lam_vecs"], w["g_subln"], lam_init, 2 * tk, tk)
    else:
        att = _attn_sample(q, kb, vb, cache[0], cache[1], rel_bias, w["lam_vecs"], w["g_subln"], lam_init,
                           nb_, tlen)
    conv_pad = jnp.pad(conv0, ((0, 0), (SUBLANES - (CONV_W - 1), 0), (0, 0)))
    hsg, cout, hout = _conv_lru(lx, gl, conv_pad, h0[:, None, :], w["w_conv"], w["b_conv"], w["w_rg_a"],
                                w["b_rg_a"], w["w_rg_x"], w["b_rg_x"], w["lru_lambda"], nb_, tlen,
                                _tile(tlen, 256), cache is None)
    tm = _tile(n, 512)
    x1, h2, comb = _merge(xf, att, hsg, gt, w["w_attn_br"], w["w_lru_br"], w["w_out"], w["g_ffn"],
                          w["w_router"], w["b_router"], tm)
    y = _moe(h2, comb, w["w_e_gate"], w["w_e_up"], w["w_e_down"], _tile(n, 1024))
    out = _ple(x1, y, p.reshape(n, PLE_DIM), w["g_ple"], w["w_ple_gate"], w["b_ple_gate"], w["w_ple_in"],
               w["g_final"], tm)
    return out, k, v, cout[:, SUBLANES - (CONV_W - 1):, :], hout[:, 0, :]


def kernel(x_prompt, x_sample, cache_k, cache_v, state_conv, state_lru, p_prompt, p_sample, rel_bias, g_mix, w_in, b_merge, lam_q1, lam_k1, lam_q2, lam_k2, g_subln, w_attn_br, w_conv, b_conv, w_rg_a, b_rg_a, w_rg_x, b_rg_x, lru_lambda, w_lru_br, w_out, g_ffn, w_grp, b_grp, w_rt, b_rt, w_e_gate, w_e_up, w_e_down, g_ple, w_ple_gate, b_ple_gate, w_ple_in, g_final):
    depth = w_in.shape[0]
    assert depth == 1, "the final norm is fused into the single layer's last stage"
    bp, tp, _ = x_prompt.shape
    bs, ts, _ = x_sample.shape
    past = cache_k.shape[2]
    li = 0
    row = lambda a: a[li].reshape(1, -1).astype(f32)
    w_router = jnp.concatenate(
        [jnp.transpose(w_rt[li], (1, 0, 2)).reshape(D_MODEL, N_EXPERTS), w_grp[li]], axis=1).astype(f32)
    w_router = jnp.pad(w_router, ((0, 0), (0, LANES - N_EXPERTS - N_GROUPS)))
    w_router_hi = w_router.astype(bf16)
    w_router = jnp.concatenate([w_router_hi, (w_router - w_router_hi.astype(f32)).astype(bf16)], axis=1)
    b_router = jnp.pad(jnp.concatenate([b_rt[li].reshape(-1), b_grp[li]]).astype(f32),
                       (0, LANES - N_EXPERTS - N_GROUPS)).reshape(1, LANES)
    w = dict(
        g_mix=row(g_mix), w_in=w_in[li].astype(bf16), b_merge=row(b_merge),
        lam_vecs=(row(lam_q1), row(lam_k1), row(lam_q2), row(lam_k2)), g_subln=row(g_subln),
        w_attn_br=w_attn_br[li].astype(bf16), w_conv=w_conv[li].astype(f32), b_conv=row(b_conv),
        w_rg_a=w_rg_a[li].astype(bf16), b_rg_a=row(b_rg_a), w_rg_x=w_rg_x[li].astype(bf16), b_rg_x=row(b_rg_x),
        lru_lambda=row(lru_lambda), w_lru_br=w_lru_br[li].astype(bf16), w_out=w_out[li].astype(bf16),
        g_ffn=row(g_ffn), w_router=w_router, b_router=b_router,
        w_e_gate=w_e_gate[li].astype(bf16), w_e_up=w_e_up[li].astype(bf16), w_e_down=w_e_down[li].astype(bf16),
        g_ple=row(g_ple), w_ple_gate=w_ple_gate[li].astype(bf16), b_ple_gate=row(b_ple_gate),
        w_ple_in=w_ple_in[li].astype(bf16), g_final=g_final.reshape(1, -1).astype(f32),
    )
    conv0 = jnp.zeros((bp, CONV_W - 1, LRU_WIDTH), f32)
    h0 = jnp.zeros((bp, LRU_WIDTH), f32)
    yp, kp, vp, cp, hp = _stream(x_prompt, p_prompt[li], conv0, h0, None, li, rel_bias, w)
    cache = (cache_k[li].reshape(bs, past, QK_COLS), cache_v[li].reshape(bs, past, V_COLS))
    ys, ks, vs, cs, hs = _stream(x_sample, p_sample[li], state_conv[li], state_lru[li], cache, li, rel_bias, w)
    return (yp.reshape(bp, tp, D_MODEL), ys.reshape(bs, ts, D_MODEL),
            kp.reshape(1, bp, tp, N_HEADS, 2, HEAD_DIM), vp.reshape(1, bp, tp, N_HEADS, V_DIM),
            cp[None], hp[None],
            ks.reshape(1, bs, ts, N_HEADS, 2, HEAD_DIM), vs.reshape(1, bs, ts, N_HEADS, V_DIM),
            cs[None], hs[None])
```

```python
import functools
import math

import jax
import jax.numpy as jnp
from jax import lax
from jax.experimental import pallas as pl
from jax.experimental.pallas import tpu as pltpu

f32 = jnp.float32
bf16 = jnp.bfloat16

D_MODEL = 1024
N_HEADS = 8
HEAD_DIM = 64
V_DIM = 128
QK_COLS = N_HEADS * 2 * HEAD_DIM
V_COLS = N_HEADS * V_DIM
LRU_WIDTH = 1024
LRU_BLOCKS = 8
LRU_BLOCK_W = LRU_WIDTH // LRU_BLOCKS
CONV_W = 4
LRU_C = 8.0
CHUNK = 64
N_BUCKETS = 32
MAX_DISTANCE = 128
N_GROUPS = 4
EXPERTS_PER_GROUP = 8
N_EXPERTS = N_GROUPS * EXPERTS_PER_GROUP
D_EXPERT = 256
PLE_DIM = 256
EPS = 1e-6
NEG_INF = -1e30
LOG2E = math.log2(math.e)
ONES_ROWS = 16
LANES = 128
SUBLANES = 8
VMEM_LIMIT = 56 * 1024 * 1024

C_Q, C_K, C_V = 0, QK_COLS, 2 * QK_COLS
C_LX = 2 * QK_COLS + V_COLS
C_LY = C_LX + LRU_WIDTH
C_G = C_LY + LRU_WIDTH
IN_COLS = C_G + 2 * D_MODEL


def _cparams(sem):
    return pltpu.CompilerParams(dimension_semantics=sem, vmem_limit_bytes=VMEM_LIMIT)


def _const_spec(shape):
    nd = len(shape)
    return pl.BlockSpec(shape, lambda *_: (0,) * nd, pipeline_mode=pl.Buffered(1))


def _rms(x, g):
    return x * lax.rsqrt(jnp.mean(x * x, axis=-1, keepdims=True) + EPS) * g


def _sigmoid(x):
    return 1.0 / (1.0 + jnp.exp(-x))


def _gelu_tanh(x):
    c = math.sqrt(2.0 / math.pi)
    return 0.5 * x * (1.0 + jnp.tanh(c * (x + 0.044715 * (x * x * x))))


def _in_proj_kernel(x_ref, g_ref, w_ref, bm_ref, q_ref, k_ref, kb_ref, v_ref, vb_ref, lx_ref, gl_ref, gt_ref,
                    *, transposed_v):
    h = _rms(x_ref[...], g_ref[...]).astype(bf16)

    def proj(lo, width):
        return jnp.dot(h, w_ref[:, lo:lo + width], preferred_element_type=f32)

    q_ref[...] = (proj(C_Q, QK_COLS) * (HEAD_DIM ** -0.5 * LOG2E)).astype(bf16)
    k = proj(C_K, QK_COLS)
    k_ref[...] = k
    kb_ref[...] = k.astype(bf16)
    v = proj(C_V, V_COLS)
    v_ref[...] = v
    if transposed_v:
        for hd in range(N_HEADS):
            vb_ref[hd, :V_DIM, :] = v[:, hd * V_DIM:(hd + 1) * V_DIM].T.astype(bf16)
            vb_ref[hd, V_DIM:, :] = jnp.ones((ONES_ROWS, v.shape[0]), bf16)
    else:
        vb_ref[...] = v.astype(bf16)
    lx_ref[...] = proj(C_LX, LRU_WIDTH)
    gl_ref[...] = _gelu_tanh(proj(C_LY, LRU_WIDTH))
    gt_ref[...] = _sigmoid(proj(C_G, 2 * D_MODEL) + bm_ref[...])


def _in_proj(x, g_mix, w_in_b, b_merge, tm, tq=None):
    n = x.shape[0]
    row = lambda w: pl.BlockSpec((tm, w), lambda i: (i, 0))
    if tq is None:
        vb_shape, vb_spec = jax.ShapeDtypeStruct((n, V_COLS), bf16), row(V_COLS)
    else:
        per = tq // tm
        vb_shape = jax.ShapeDtypeStruct((N_HEADS, n // tq, V_DIM + ONES_ROWS, tq), bf16)
        vb_spec = pl.BlockSpec((N_HEADS, None, V_DIM + ONES_ROWS, tm), lambda i: (0, i // per, 0, i % per))
    outs = [
        jax.ShapeDtypeStruct((n, QK_COLS), bf16),
        jax.ShapeDtypeStruct((n, QK_COLS), f32),
        jax.ShapeDtypeStruct((n, QK_COLS), bf16),
        jax.ShapeDtypeStruct((n, V_COLS), f32),
        vb_shape,
        jax.ShapeDtypeStruct((n, LRU_WIDTH), f32),
        jax.ShapeDtypeStruct((n, LRU_WIDTH), f32),
        jax.ShapeDtypeStruct((n, 2 * D_MODEL), f32),
    ]
    return pl.pallas_call(
        functools.partial(_in_proj_kernel, transposed_v=tq is not None),
        grid=(n // tm,),
        in_specs=[row(D_MODEL), _const_spec((1, D_MODEL)), _const_spec((D_MODEL, IN_COLS)),
                  _const_spec((1, 2 * D_MODEL))],
        out_specs=[row(QK_COLS), row(QK_COLS), row(QK_COLS), row(V_COLS), vb_spec,
                   row(LRU_WIDTH), row(LRU_WIDTH), row(2 * D_MODEL)],
        out_shape=outs,
        compiler_params=_cparams(("parallel",)),
        name="in_proj",
    )(x, g_mix, w_in_b, b_merge)


def _rel_bucket(rel):
    half = N_BUCKETS // 2
    max_exact = half // 2
    ret = jnp.where(rel > 0, half, 0)
    n = jnp.abs(rel)
    nf = jnp.maximum(n, 1).astype(f32)
    large = max_exact + (jnp.log(nf / max_exact) / math.log(MAX_DISTANCE / max_exact)
                         * (half - max_exact)).astype(jnp.int32)
    large = jnp.minimum(large, half - 1)
    return ret + jnp.where(n < max_exact, n, large)


def _toeplitz(t, nq, nk):
    nh, length = t.shape
    a = jnp.broadcast_to(t[:, None, :], (nh, nq, length))
    a = jnp.pad(a, ((0, 0), (0, 0), (0, 1))).reshape(nh, nq * (length + 1))
    a = a[:, :nq * length].reshape(nh, nq, length)
    return a[:, :, nq - 1:nq - 1 + nk]


def _bias_table(rel_bias, qpos0, nq, kpos0, nk, shift):
    rel = jnp.arange(-(nq - 1), nk) + (kpos0 - qpos0)
    t = (rel_bias.astype(f32)[_rel_bucket(rel)].T - shift[:, None]) * LOG2E
    qpos = qpos0 + jnp.arange(nq)
    kpos = kpos0 + jnp.arange(nk)
    mask = (kpos[None, :] // CHUNK) <= (qpos[:, None] // CHUNK)
    return jnp.where(mask[None], _toeplitz(t, nq, nk), NEG_INF)


def _bias_ring(rel_bias, offset, tq, shift):
    r = jnp.arange(2 * tq)
    x = jnp.where(r < tq, r, r - 2 * tq)
    t = (rel_bias.astype(f32)[_rel_bucket(offset - x)].T - shift[:, None]) * LOG2E
    return t[:, None, :]


def _lam_value(lq1, lk1, lq2, lk2, lam_init):
    s1 = jnp.sum(lq1[...] * lk1[...], axis=-1, keepdims=True)
    s2 = jnp.sum(lq2[...] * lk2[...], axis=-1, keepdims=True)
    return jnp.exp(s1) - jnp.exp(s2) + lam_init


def _split_maps(q):
    lane = lax.broadcasted_iota(jnp.int32, q.shape, 1)
    zero = jnp.zeros_like(q)
    return jnp.where(lane < HEAD_DIM, q, zero), jnp.where(lane >= HEAD_DIM, q, zero)


def _qk(qz, kt):
    return lax.dot_general(qz, kt, (((1,), (1,)), ((), ())), preferred_element_type=f32)


def _finish_heads(o1, o2, lam, gs, lam_init):
    att = o1 - lam * o2
    return _rms(att, gs) * (1.0 - lam_init)


def _attn_prompt_kernel(q_ref, k_ref, vt_ref, ring_ref, lq1, lk1, lq2, lk2, gs_ref, o_ref,
                        nb_sc, s_sc, p_sc, al_sc, tm_sc, m_sc, acc_sc, *, tq, tk, lam_init):
    i = pl.program_id(1)
    ring_len = ring_ref.shape[-1]

    def bias_tile(ring):
        return pltpu.roll(jnp.broadcast_to(ring, (tk, ring_len)), 0, 1, stride=1, stride_axis=0)[:, :tq]

    @pl.when(i == 0)
    def _():
        kk = lax.broadcasted_iota(jnp.int32, (tk, tq), 0)
        qq = lax.broadcasted_iota(jnp.int32, (tk, tq), 1)
        nb_sc[0] = jnp.where((kk // CHUNK) <= (qq // CHUNK), bias_tile(ring_ref[1]), NEG_INF)
        nb_sc[1] = jnp.where(((kk + tk) // CHUNK) <= (qq // CHUNK), bias_tile(ring_ref[2]), NEG_INF)
        nb_sc[2] = jnp.full((tk, tq), NEG_INF, f32)

    @pl.when(i == 1)
    def _():
        nb_sc[2] = nb_sc[1]
        nb_sc[1] = nb_sc[0]
        nb_sc[0] = bias_tile(ring_ref[0])

    qt = q_ref[...].astype(f32).T
    row = lax.broadcasted_iota(jnp.int32, qt.shape, 0)
    qz = (jnp.where(row < HEAD_DIM, qt, 0.0).astype(bf16), jnp.where(row >= HEAD_DIM, qt, 0.0).astype(bf16))

    m_sc[...] = jnp.full(m_sc.shape, NEG_INF, f32)
    acc_sc[...] = jnp.zeros(acc_sc.shape, f32)

    def qk(slot, tile, bias):
        kt = k_ref[pl.ds(pl.multiple_of(tile * tk, tk), tk), :]
        for mi in range(2):
            s = jnp.dot(kt, qz[mi], preferred_element_type=f32)
            s = s if bias is None else s + bias
            s_sc[mi, slot] = s
            tm_sc[mi, slot] = jnp.max(s, axis=0, keepdims=True)

    def softmax(slot):
        for mi in range(2):
            s = s_sc[mi, slot]
            m_prev = m_sc[mi]
            m_new = jnp.maximum(m_prev, tm_sc[mi, slot])
            al_sc[mi, slot] = jnp.exp2(m_prev - m_new)
            p_sc[mi, slot] = jnp.exp2(s - m_new).astype(bf16)
            m_sc[mi] = m_new

    def pv(slot, vtile):
        va = vt_ref[vtile]
        for mi in range(2):
            acc_sc[mi] = acc_sc[mi] * al_sc[mi, slot] + jnp.dot(va, p_sc[mi, slot], preferred_element_type=f32)

    per = tq // tk
    a0 = jnp.maximum(per * i - 1, 0)
    n_far = jnp.maximum(per * i - 1, 0)
    last_tile = vt_ref.shape[0] - 1

    def far_k(f):
        return jnp.clip(f, 0, jnp.maximum(n_far - 1, 0))

    qk(0, a0, nb_sc[0])
    qk(1, a0 + 1, nb_sc[1])
    softmax(0)
    qk(0, jnp.minimum(a0 + 2, last_tile), nb_sc[2])
    softmax(1)
    pv(0, a0)

    @pl.when(i == 0)
    def _():
        pv(1, a0 + 1)

    @pl.when(i > 0)
    def _():
        qk(1, 0, None)
        softmax(0)
        pv(1, a0 + 1)
        qk(0, far_k(1), None)
        softmax(1)
        pv(0, a0 + 2)

        def far_pair(p, carry):
            qk(1, 2 * p + 2, None)
            softmax(0)
            pv(1, 2 * p)
            qk(0, far_k(2 * p + 3), None)
            softmax(1)
            pv(0, 2 * p + 1)
            return carry

        lax.fori_loop(0, (n_far - 1) // 2, far_pair, 0)
        pv(1, n_far - 1)

    a1 = acc_sc[0]
    a2 = acc_sc[1]
    o1 = a1[:V_DIM] / a1[V_DIM:V_DIM + 1]
    o2 = a2[:V_DIM] / a2[V_DIM:V_DIM + 1]
    lam = _lam_value(lq1, lk1, lq2, lk2, lam_init)
    att = (o1 - lam * o2).T
    o_ref[...] = (_rms(att, gs_ref[...]) * (1.0 - lam_init)).astype(o_ref.dtype)


def _attn_prompt(q, kb, vt, rel_bias, lam_vecs, g_subln, lam_init, tq, tk):
    t = q.shape[0]
    nq = t // tq
    assert tq == 2 * tk and t % tq == 0
    far = rel_bias.astype(f32)[_rel_bucket(jnp.asarray(-4 * MAX_DISTANCE))]
    rings = jnp.stack([_bias_ring(rel_bias, d, tq, far) for d in (-tk, 0, tk)], axis=1)
    vec = _const_spec((1, HEAD_DIM))
    return pl.pallas_call(
        functools.partial(_attn_prompt_kernel, tq=tq, tk=tk, lam_init=lam_init),
        grid=(N_HEADS, nq),
        in_specs=[
            pl.BlockSpec((tq, 2 * HEAD_DIM), lambda h, i: (i, h)),
            pl.BlockSpec((t, 2 * HEAD_DIM), lambda h, i: (0, h)),
            pl.BlockSpec((None, t // tk, V_DIM + ONES_ROWS, tk), lambda h, i: (h, 0, 0, 0)),
            pl.BlockSpec((None, 3, 1, 2 * tq), lambda h, i: (h, 0, 0, 0)),
            vec, vec, vec, vec, _const_spec((1, V_DIM)),
        ],
        out_specs=pl.BlockSpec((tq, V_DIM), lambda h, i: (i, h)),
        out_shape=jax.ShapeDtypeStruct((t, V_COLS), bf16),
        scratch_shapes=[pltpu.VMEM((3, tk, tq), f32), pltpu.VMEM((2, 2, tk, tq), f32),
                        pltpu.VMEM((2, 2, tk, tq), bf16), pltpu.VMEM((2, 2, 1, tq), f32),
                        pltpu.VMEM((2, 2, 1, tq), f32), pltpu.VMEM((2, 1, tq), f32),
                        pltpu.VMEM((2, V_DIM + ONES_ROWS, tq), f32)],
        compiler_params=_cparams(("arbitrary", "arbitrary")),
        name="attn_prompt",
    )(q, kb, vt, rings, *lam_vecs, g_subln)


def _attn_sample_kernel(q_ref, kp_ref, vp_ref, kn_ref, vn_ref, nbp_ref, nbn_ref, lq1, lk1, lq2, lk2, gs_ref,
                        o_ref, *, lam_init):
    lam = _lam_value(lq1, lk1, lq2, lk2, lam_init)
    for hd in range(N_HEADS):
        qk_cols = slice(hd * 2 * HEAD_DIM, (hd + 1) * 2 * HEAD_DIM)
        v_cols = slice(hd * V_DIM, (hd + 1) * V_DIM)
        qz = _split_maps(q_ref[:, qk_cols])
        kp = kp_ref[:, qk_cols].astype(bf16)
        vp = vp_ref[:, v_cols].astype(bf16)
        kn = kn_ref[:, qk_cols]
        vn = vn_ref[:, v_cols]
        outs = []
        for mi in range(2):
            sp = _qk(qz[mi], kp) + nbp_ref[hd]
            sn = _qk(qz[mi], kn) + nbn_ref[hd]
            m = jnp.maximum(jnp.max(sp, axis=1, keepdims=True), jnp.max(sn, axis=1, keepdims=True))
            pp = jnp.exp2(sp - m).astype(bf16)
            pn = jnp.exp2(sn - m).astype(bf16)
            den = (jnp.sum(pp.astype(f32), axis=1, keepdims=True)
                   + jnp.sum(pn.astype(f32), axis=1, keepdims=True))
            num = jnp.dot(pp, vp, preferred_element_type=f32) + jnp.dot(pn, vn, preferred_element_type=f32)
            outs.append(num / den)
        o_ref[:, v_cols] = _finish_heads(outs[0], outs[1], lam, gs_ref[...], lam_init).astype(o_ref.dtype)


def _attn_sample(q, kb, vb, cache_k, cache_v, rel_bias, lam_vecs, g_subln, lam_init, nb_, ts):
    past = cache_k.shape[1]
    zero = jnp.zeros((N_HEADS,), f32)
    nbp = _bias_table(rel_bias, past, ts, 0, past, zero)
    nbn = _bias_table(rel_bias, past, ts, past, ts, zero)
    vec = _const_spec((1, HEAD_DIM))
    return pl.pallas_call(
        functools.partial(_attn_sample_kernel, lam_init=lam_init),
        grid=(nb_,),
        in_specs=[
            pl.BlockSpec((ts, QK_COLS), lambda b: (b, 0)),
            pl.BlockSpec((None, past, QK_COLS), lambda b: (b, 0, 0)),
            pl.BlockSpec((None, past, V_COLS), lambda b: (b, 0, 0)),
            pl.BlockSpec((ts, QK_COLS), lambda b: (b, 0)),
            pl.BlockSpec((ts, V_COLS), lambda b: (b, 0)),
            _const_spec((N_HEADS, ts, past)), _const_spec((N_HEADS, ts, ts)),
            vec, vec, vec, vec, _const_spec((1, V_DIM)),
        ],
        out_specs=pl.BlockSpec((ts, V_COLS), lambda b: (b, 0)),
        out_shape=jax.ShapeDtypeStruct((nb_ * ts, V_COLS), bf16),
        compiler_params=_cparams(("parallel",)),
        name="attn_sample",
    )(q, cache_k, cache_v, kb, vb, nbp, nbn, *lam_vecs, g_subln)


def _conv_lru_kernel(lx_ref, gl_ref, c0_ref, h0_ref, wc_ref, bc_ref, wa_ref, ba_ref, wx_ref, bx_ref, lam_ref,
                     hsg_ref, cout_ref, hout_ref, xbuf, a_sc, b_sc, hcar, *, tt, first_pos_is_zero):
    t = pl.program_id(1)

    @pl.when(t == 0)
    def _():
        xbuf[0:SUBLANES] = c0_ref[...]
        hcar[...] = jnp.broadcast_to(h0_ref[...], hcar.shape)

    xbuf[SUBLANES:SUBLANES + tt] = lx_ref[...]
    xc = jnp.broadcast_to(bc_ref[...], (tt, LRU_WIDTH))
    for j in range(CONV_W):
        lo = SUBLANES - (CONV_W - 1) + j
        xc = xc + xbuf[lo:lo + tt, :] * wc_ref[j:j + 1, :]
    tail = xbuf[tt:tt + SUBLANES, :]
    xbuf[0:SUBLANES] = tail
    cout_ref[...] = tail

    xcb = xc.astype(bf16)

    def block_diag(w_ref, b_ref):
        cols = [jnp.dot(xcb[:, n * LRU_BLOCK_W:(n + 1) * LRU_BLOCK_W], w_ref[n], preferred_element_type=f32)
                for n in range(LRU_BLOCKS)]
        return jnp.concatenate(cols, axis=1) + b_ref[...]

    r = _sigmoid(block_diag(wa_ref, ba_ref))
    ig = _sigmoid(block_diag(wx_ref, bx_ref))
    z = -lam_ref[...]
    softplus = jnp.maximum(z, 0.0) + jnp.log1p(jnp.exp(-jnp.abs(z)))
    log_a = -LRU_C * r * softplus
    a = jnp.exp(log_a)
    th = jnp.tanh(log_a)
    m2 = -2.0 * th / (1.0 - th)
    mult = jnp.where(m2 > 0.0, m2 * lax.rsqrt(m2), 0.0)
    if first_pos_is_zero:
        row = lax.broadcasted_iota(jnp.int32, (tt, LRU_WIDTH), 0)
        mult = jnp.where((row == 0) & (t == 0), 1.0, mult)
    bv = mult * ig * xc

    groups = (tt // SUBLANES, SUBLANES, LRU_WIDTH)
    a = a.reshape(groups)
    bv = bv.reshape(groups)
    sub = lax.broadcasted_iota(jnp.int32, groups, 1)
    d = 1
    while d < SUBLANES:
        valid = sub >= d
        a_s = pltpu.roll(a, d, axis=1)
        b_s = pltpu.roll(bv, d, axis=1)
        bv = jnp.where(valid, a * b_s + bv, bv)
        a = jnp.where(valid, a * a_s, a)
        d *= 2
    a_sc[...] = a.reshape(tt, LRU_WIDTH)
    b_sc[...] = bv.reshape(tt, LRU_WIDTH)

    def group(g, h):
        sl = pl.ds(pl.multiple_of(g * SUBLANES, SUBLANES), SUBLANES)
        hg = a_sc[sl, :] * h + b_sc[sl, :]
        b_sc[sl, :] = hg
        return jnp.broadcast_to(hg[SUBLANES - 1:SUBLANES, :], hg.shape)

    h = lax.fori_loop(0, tt // SUBLANES, group, hcar[...], unroll=4)
    hcar[...] = h
    hout_ref[...] = h[0:1, :]
    hsg_ref[...] = (b_sc[...] * gl_ref[...]).astype(hsg_ref.dtype)


def _conv_lru(lx, gl, conv0, h0, w_conv, b_conv, wa, ba, wx, bx, lru_lambda, nb_, tlen, tt, first_pos_is_zero):
    nt = tlen // tt
    rows = pl.BlockSpec((tt, LRU_WIDTH), lambda b, t: (b * nt + t, 0))
    vecw = _const_spec((1, LRU_WIDTH))
    wblk = _const_spec((LRU_BLOCKS, LRU_BLOCK_W, LRU_BLOCK_W))
    return pl.pallas_call(
        functools.partial(_conv_lru_kernel, tt=tt, first_pos_is_zero=first_pos_is_zero),
        grid=(nb_, nt),
        in_specs=[rows, rows,
                  pl.BlockSpec((None, SUBLANES, LRU_WIDTH), lambda b, t: (b, 0, 0)),
                  pl.BlockSpec((None, 1, LRU_WIDTH), lambda b, t: (b, 0, 0)),
                  _const_spec((CONV_W, LRU_WIDTH)), vecw, wblk, vecw, wblk, vecw, vecw],
        out_specs=[rows,
                   pl.BlockSpec((None, SUBLANES, LRU_WIDTH), lambda b, t: (b, 0, 0)),
                   pl.BlockSpec((None, 1, LRU_WIDTH), lambda b, t: (b, 0, 0))],
        out_shape=[jax.ShapeDtypeStruct((nb_ * tlen, LRU_WIDTH), bf16),
                   jax.ShapeDtypeStruct((nb_, SUBLANES, LRU_WIDTH), f32),
                   jax.ShapeDtypeStruct((nb_, 1, LRU_WIDTH), f32)],
        scratch_shapes=[pltpu.VMEM((tt + SUBLANES, LRU_WIDTH), f32), pltpu.VMEM((tt, LRU_WIDTH), f32),
                        pltpu.VMEM((tt, LRU_WIDTH), f32), pltpu.VMEM((SUBLANES, LRU_WIDTH), f32)],
        compiler_params=_cparams(("arbitrary", "arbitrary")),
        name="conv_lru",
    )(lx, gl, conv0, h0, w_conv, b_conv, wa, ba, wx, bx, lru_lambda)


def _merge_kernel(x_ref, att_ref, hsg_ref, gt_ref, wa_ref, wb_ref, wo_ref, gf_ref, wr_ref, br_ref,
                  x1_ref, h2_ref, comb_ref, *, parts):
    rows_per = x_ref.shape[0] // parts
    for part in range(parts):
        rows = slice(part * rows_per, (part + 1) * rows_per)
        bra = jnp.dot(att_ref[rows, :], wa_ref[...], preferred_element_type=f32)
        brb = jnp.dot(hsg_ref[rows, :], wb_ref[...], preferred_element_type=f32)
        gt = gt_ref[rows, :]
        m = (gt[:, :D_MODEL] * bra + gt[:, D_MODEL:] * brb).astype(bf16)
        x1 = x_ref[rows, :] + jnp.dot(m, wo_ref[...], preferred_element_type=f32)
        x1_ref[rows, :] = x1
        h2 = _rms(x1, gf_ref[...])
        h2_hi = h2.astype(bf16)
        h2_ref[rows, :] = h2_hi

        h2_lo = (h2 - h2_hi.astype(f32)).astype(bf16)
        prods = (jnp.dot(h2_hi, wr_ref[...], preferred_element_type=f32)
                 + jnp.dot(h2_lo, wr_ref[...], preferred_element_type=f32))
        logits = prods[:, :LANES] + prods[:, LANES:] + br_ref[...]
        lane = lax.broadcasted_iota(jnp.int32, logits.shape, 1)
        lanef = lane.astype(f32)
        low = jnp.float32(-3.0e38)
        is_grp = (lane >= N_EXPERTS) & (lane < N_EXPERTS + N_GROUPS)
        gl = jnp.where(is_grp, logits, low)
        gmax = jnp.max(gl, axis=1, keepdims=True)
        gidx = jnp.min(jnp.where(is_grp & (gl == gmax), lanef, 1.0e3), axis=1, keepdims=True) - N_EXPERTS
        gden = jnp.sum(jnp.where(is_grp, jnp.exp(gl - gmax), 0.0), axis=1, keepdims=True)
        g_w = 1.0 / gden
        lo = gidx * EXPERTS_PER_GROUP
        in_sel = (lanef >= lo) & (lanef < lo + EXPERTS_PER_GROUP)
        sel = jnp.where(in_sel, logits, low)
        v1 = jnp.max(sel, axis=1, keepdims=True)
        i1 = jnp.min(jnp.where(in_sel & (sel == v1), lanef, 1.0e3), axis=1, keepdims=True)
        in_sel2 = in_sel & (lanef != i1)
        sel2 = jnp.where(in_sel2, logits, low)
        v2 = jnp.max(sel2, axis=1, keepdims=True)
        i2 = jnp.min(jnp.where(in_sel2 & (sel2 == v2), lanef, 1.0e3), axis=1, keepdims=True)
        e2 = jnp.exp(v2 - v1)
        w1 = g_w / (1.0 + e2)
        w2 = g_w * e2 / (1.0 + e2)
        comb_ref[rows, :] = jnp.where(lanef == i1, w1, 0.0) + jnp.where(lanef == i2, w2, 0.0)


def _merge(x, att, hsg, gt, wa, wb, wo, g_ffn, w_router, b_router, tm):
    n = x.shape[0]
    row = lambda w: pl.BlockSpec((tm, w), lambda i: (i, 0))
    sq = _const_spec((D_MODEL, D_MODEL))
    return pl.pallas_call(
        functools.partial(_merge_kernel, parts=2 if tm % 512 == 0 else 1),
        grid=(n // tm,),
        in_specs=[row(D_MODEL), row(V_COLS), row(LRU_WIDTH), row(2 * D_MODEL), sq, sq, sq,
                  _const_spec((1, D_MODEL)), _const_spec((D_MODEL, 2 * LANES)), _const_spec((1, LANES))],
        out_specs=[row(D_MODEL), row(D_MODEL), row(LANES)],
        out_shape=[jax.ShapeDtypeStruct((n, D_MODEL), f32), jax.ShapeDtypeStruct((n, D_MODEL), bf16),
                   jax.ShapeDtypeStruct((n, LANES), f32)],
        compiler_params=_cparams(("parallel",)),
        name="merge_router",
    )(x, att, hsg, gt, wa, wb, wo, g_ffn, w_router, b_router)


EXPERTS_PER_STEP = 8


MOE_CHUNK = 128


def _moe_kernel(h_ref, comb_ref, wg_ref, wu_ref, wd_ref, y_ref, pt_sc, x_sc, c_sc, ys_sc, seg_sc):
    e = pl.program_id(1)
    tm = h_ref.shape[0]
    lane1 = lax.broadcasted_iota(jnp.int32, (1, LANES), 1)

    @pl.when(e == 0)
    def _():
        comb = comb_ref[...]
        ex = lax.broadcasted_iota(jnp.int32, (LANES, LANES), 0)
        gr = lax.broadcasted_iota(jnp.int32, (LANES, LANES), 1)
        in_group = jnp.where((ex // EXPERTS_PER_GROUP == gr) & (ex < N_EXPERTS), 1.0, 0.0).astype(bf16)
        gsum = jnp.dot(comb.astype(bf16), in_group, preferred_element_type=f32)
        member = jnp.where(gsum > 0.0, 1.0, 0.0)
        r_i = lax.broadcasted_iota(jnp.int32, (tm, tm), 0)
        c_i = lax.broadcasted_iota(jnp.int32, (tm, tm), 1)
        tri = jnp.where(gr <= ex, 1.0, 0.0).astype(bf16)
        member_b = member.astype(bf16)
        totals = jnp.zeros((1, LANES), f32)
        blocks = []
        for blk in range(tm // LANES):
            part = jnp.dot(tri, member_b[blk * LANES:(blk + 1) * LANES], preferred_element_type=f32) + totals
            blocks.append(part)
            totals = part[LANES - 1:LANES, :]
        rank = jnp.concatenate(blocks, axis=0)
        offs = jnp.zeros((1, LANES), f32)
        for g in range(1, N_GROUPS):
            prev = jnp.sum(jnp.where(lane1 == g - 1, totals, 0.0), axis=1, keepdims=True)
            offs = offs + jnp.where(lane1 >= g, prev, 0.0)
        seg_sc[0:1, :] = offs.astype(jnp.int32)
        seg_sc[1:2, :] = totals.astype(jnp.int32)
        pos = jnp.sum(member * (offs + rank - 1.0), axis=1, keepdims=True)
        pos_row = jnp.broadcast_to(pos, (tm, LANES)).T[0:1, :].astype(jnp.int32)
        perm = jnp.where(r_i == pos_row, 1.0, 0.0).astype(bf16)
        pt_sc[...] = jnp.where(c_i == pos.astype(jnp.int32), 1.0, 0.0).astype(bf16)
        comb_hi = comb.astype(bf16)
        comb_lo = (comb - comb_hi.astype(f32)).astype(bf16)
        srt = jnp.dot(perm, jnp.concatenate([h_ref[...], comb_hi, comb_lo], axis=1), preferred_element_type=f32)
        x_sc[...] = srt[:, :D_MODEL].astype(bf16)
        c_sc[...] = srt[:, D_MODEL:D_MODEL + LANES] + srt[:, D_MODEL + LANES:]
        ys_sc[...] = jnp.zeros(ys_sc.shape, f32)

    g = (e * EXPERTS_PER_STEP) // EXPERTS_PER_GROUP
    off = jnp.sum(jnp.where(lane1 == g, seg_sc[0:1, :], 0))
    cnt = jnp.sum(jnp.where(lane1 == g, seg_sc[1:2, :], 0))
    lane = lax.broadcasted_iota(jnp.int32, (MOE_CHUNK, LANES), 1)

    def chunk(k, carry):
        rows = pl.ds(pl.multiple_of(k * MOE_CHUNK, MOE_CHUNK), MOE_CHUNK)
        x = x_sc[rows, :]
        cw = c_sc[rows, :]
        hidden = []
        for j in range(EXPERTS_PER_STEP):
            gate = jnp.dot(x, wg_ref[j], preferred_element_type=f32)
            up = jnp.dot(x, wu_ref[j], preferred_element_type=f32)
            c = jnp.sum(jnp.where(lane == e * EXPERTS_PER_STEP + j, cw, 0.0), axis=1, keepdims=True)
            hidden.append((gate * _sigmoid(gate) * up * c).astype(bf16))
        ys_sc[rows, :] += jnp.dot(jnp.concatenate(hidden, axis=1), wd_ref[...], preferred_element_type=f32)
        return carry

    lax.fori_loop(off // MOE_CHUNK, (off + cnt + MOE_CHUNK - 1) // MOE_CHUNK, chunk, 0)

    @pl.when(e == pl.num_programs(1) - 1)
    def _():
        y_ref[...] = jnp.dot(pt_sc[...], ys_sc[...].astype(bf16), preferred_element_type=f32)


def _moe(h2, comb, wg, wu, wd_rows, tm):
    n = h2.shape[0]
    per = EXPERTS_PER_STEP
    assert EXPERTS_PER_GROUP % per == 0 and tm % MOE_CHUNK == 0
    return pl.pallas_call(
        _moe_kernel,
        grid=(n // tm, N_EXPERTS // per),
        in_specs=[pl.BlockSpec((tm, D_MODEL), lambda i, e: (i, 0)),
                  pl.BlockSpec((tm, LANES), lambda i, e: (i, 0)),
                  pl.BlockSpec((per, D_MODEL, D_EXPERT), lambda i, e: (e, 0, 0)),
                  pl.BlockSpec((per, D_MODEL, D_EXPERT), lambda i, e: (e, 0, 0)),
                  pl.BlockSpec((per * D_EXPERT, D_MODEL), lambda i, e: (e, 0))],
        out_specs=pl.BlockSpec((tm, D_MODEL), lambda i, e: (i, 0)),
        out_shape=jax.ShapeDtypeStruct((n, D_MODEL), f32),
        scratch_shapes=[pltpu.VMEM((tm, tm), bf16), pltpu.VMEM((tm, D_MODEL), bf16),
                        pltpu.VMEM((tm, LANES), f32), pltpu.VMEM((tm, D_MODEL), f32),
                        pltpu.VMEM((SUBLANES, LANES), jnp.int32)],
        compiler_params=_cparams(("arbitrary", "arbitrary")),
        name="moe",
    )(h2, comb, wg, wu, wd_rows)


def _ple_kernel(x1_ref, y_ref, p_ref, gp_ref, wpg_ref, bpg_ref, wpi_ref, gfin_ref, o_ref):
    x2 = x1_ref[...] + y_ref[...]
    hp = _rms(x2, gp_ref[...]).astype(bf16)
    pg = _sigmoid(jnp.dot(hp, wpg_ref[...], preferred_element_type=f32) + bpg_ref[...])
    pin = jnp.dot(p_ref[...].astype(bf16), wpi_ref[...], preferred_element_type=f32)
    x3 = x2 + pg * pin
    o_ref[...] = _rms(x3, gfin_ref[...])


def _ple(x1, y, p, g_ple, wpg, bpg, wpi, g_final, tm):
    n = x1.shape[0]
    row = lambda w: pl.BlockSpec((tm, w), lambda i: (i, 0))
    vec = _const_spec((1, D_MODEL))
    return pl.pallas_call(
        _ple_kernel,
        grid=(n // tm,),
        in_specs=[row(D_MODEL), row(D_MODEL), row(PLE_DIM), vec, _const_spec((D_MODEL, D_MODEL)), vec,
                  _const_spec((PLE_DIM, D_MODEL)), vec],
        out_specs=row(D_MODEL),
        out_shape=jax.ShapeDtypeStruct((n, D_MODEL), f32),
        compiler_params=_cparams(("parallel",)),
        name="ple_final",
    )(x1, y, p, g_ple, wpg, bpg, wpi, g_final)


def _tile(n, pref):
    return pref if n % pref == 0 else n


def _stream(x, p, conv0, h0, cache, li, rel_bias, w):
    nb_, tlen, _ = x.shape
    n = nb_ * tlen
    xf = x.reshape(n, D_MODEL)
    lam_init = 0.8 - 0.6 * math.exp(-0.3 * li)
    tk = None if cache is not None else 512
    q, k, kb, v, vb, lx, gl, gt = _in_proj(xf, w["g_mix"], w["w_in"], w["b_merge"], _tile(n, 256), tk)
    if cache is None:
        att = _attn_prompt(q, kb, vb, rel_bias, w["lam_vecs"], w["g_subln"], lam_init, 2 * tk, tk)
    else:
        att = _attn_sample(q, kb, vb, cache[0], cache[1], rel_bias, w["lam_vecs"], w["g_subln"], lam_init,
                           nb_, tlen)
    conv_pad = jnp.pad(conv0, ((0, 0), (SUBLANES - (CONV_W - 1), 0), (0, 0)))
    hsg, cout, hout = _conv_lru(lx, gl, conv_pad, h0[:, None, :], w["w_conv"], w["b_conv"], w["w_rg_a"],
                                w["b_rg_a"], w["w_rg_x"], w["b_rg_x"], w["lru_lambda"], nb_, tlen,
                                _tile(tlen, 256), cache is None)
    tm = _tile(n, 512)
    x1, h2, comb = _merge(xf, att, hsg, gt, w["w_attn_br"], w["w_lru_br"], w["w_out"], w["g_ffn"],
                          w["w_router"], w["b_router"], tm)
    y = _moe(h2, comb, w["w_e_gate"], w["w_e_up"], w["w_e_down"], _tile(n, 1024))
    out = _ple(x1, y, p.reshape(n, PLE_DIM), w["g_ple"], w["w_ple_gate"], w["b_ple_gate"], w["w_ple_in"],
               w["g_final"], tm)
    return out, k, v, cout[:, SUBLANES - (CONV_W - 1):, :], hout[:, 0, :]


def kernel(x_prompt, x_sample, cache_k, cache_v, state_conv, state_lru, p_prompt, p_sample, rel_bias, g_mix, w_in, b_merge, lam_q1, lam_k1, lam_q2, lam_k2, g_subln, w_attn_br, w_conv, b_conv, w_rg_a, b_rg_a, w_rg_x, b_rg_x, lru_lambda, w_lru_br, w_out, g_ffn, w_grp, b_grp, w_rt, b_rt, w_e_gate, w_e_up, w_e_down, g_ple, w_ple_gate, b_ple_gate, w_ple_in, g_final):
    depth = w_in.shape[0]
    assert depth == 1, "the final norm is fused into the single layer's last stage"
    bp, tp, _ = x_prompt.shape
    bs, ts, _ = x_sample.shape
    past = cache_k.shape[2]
    li = 0
    row = lambda a: a[li].reshape(1, -1).astype(f32)
    w_router = jnp.concatenate(
        [jnp.transpose(w_rt[li], (1, 0, 2)).reshape(D_MODEL, N_EXPERTS), w_grp[li]], axis=1).astype(f32)
    w_router = jnp.pad(w_router, ((0, 0), (0, LANES - N_EXPERTS - N_GROUPS)))
    w_router_hi = w_router.astype(bf16)
    w_router = jnp.concatenate([w_router_hi, (w_router - w_router_hi.astype(f32)).astype(bf16)], axis=1)
    b_router = jnp.pad(jnp.concatenate([b_rt[li].reshape(-1), b_grp[li]]).astype(f32),
                       (0, LANES - N_EXPERTS - N_GROUPS)).reshape(1, LANES)
    w = dict(
        g_mix=row(g_mix), w_in=w_in[li].astype(bf16), b_merge=row(b_merge),
        lam_vecs=(row(lam_q1), row(lam_k1), row(lam_q2), row(lam_k2)), g_subln=row(g_subln),
        w_attn_br=w_attn_br[li].astype(bf16), w_conv=w_conv[li].astype(f32), b_conv=row(b_conv),
        w_rg_a=w_rg_a[li].astype(bf16), b_rg_a=row(b_rg_a), w_rg_x=w_rg_x[li].astype(bf16), b_rg_x=row(b_rg_x),
        lru_lambda=row(lru_lambda), w_lru_br=w_lru_br[li].astype(bf16), w_out=w_out[li].astype(bf16),
        g_ffn=row(g_ffn), w_router=w_router, b_router=b_router,
        w_e_gate=w_e_gate[li].astype(bf16), w_e_up=w_e_up[li].astype(bf16), w_e_down=w_e_down[li].reshape(N_EXPERTS * D_EXPERT, D_MODEL).astype(bf16),
        g_ple=row(g_ple), w_ple_gate=w_ple_gate[li].astype(bf16), b_ple_gate=row(b_ple_gate),
        w_ple_in=w_ple_in[li].astype(bf16), g_final=g_final.reshape(1, -1).astype(f32),
    )
    conv0 = jnp.zeros((bp, CONV_W - 1, LRU_WIDTH), f32)
    h0 = jnp.zeros((bp, LRU_WIDTH), f32)
    yp, kp, vp, cp, hp = _stream(x_prompt, p_prompt[li], conv0, h0, None, li, rel_bias, w)
    cache = (cache_k[li].reshape(bs, past, QK_COLS), cache_v[li].reshape(bs, past, V_COLS))
    ys, ks, vs, cs, hs = _stream(x_sample, p_sample[li], state_conv[li], state_lru[li], cache, li, rel_bias, w)
    return (yp.reshape(bp, tp, D_MODEL), ys.reshape(bs, ts, D_MODEL),
            kp.reshape(1, bp, tp, N_HEADS, 2, HEAD_DIM), vp.reshape(1, bp, tp, N_HEADS, V_DIM),
            cp[None], hp[None],
            ks.reshape(1, bs, ts, N_HEADS, 2, HEAD_DIM), vs.reshape(1, bs, ts, N_HEADS, V_DIM),
            cs[None], hs[None])
```

```python
import functools
import math

import jax
import jax.numpy as jnp
from jax import lax
from jax.experimental import pallas as pl
from jax.experimental.pallas import tpu as pltpu

f32 = jnp.float32
bf16 = jnp.bfloat16

D_MODEL = 1024
N_HEADS = 8
HEAD_DIM = 64
V_DIM = 128
QK_COLS = N_HEADS * 2 * HEAD_DIM
V_COLS = N_HEADS * V_DIM
LRU_WIDTH = 1024
LRU_BLOCKS = 8
LRU_BLOCK_W = LRU_WIDTH // LRU_BLOCKS
CONV_W = 4
LRU_C = 8.0
CHUNK = 64
N_BUCKETS = 32
MAX_DISTANCE = 128
N_GROUPS = 4
EXPERTS_PER_GROUP = 8
N_EXPERTS = N_GROUPS * EXPERTS_PER_GROUP
D_EXPERT = 256
PLE_DIM = 256
EPS = 1e-6
NEG_INF = -1e30
LOG2E = math.log2(math.e)
ONES_ROWS = 16
LANES = 128
SUBLANES = 8
VMEM_LIMIT = 56 * 1024 * 1024

C_Q, C_K, C_V = 0, QK_COLS, 2 * QK_COLS
C_LX = 2 * QK_COLS + V_COLS
C_LY = C_LX + LRU_WIDTH
C_G = C_LY + LRU_WIDTH
IN_COLS = C_G + 2 * D_MODEL


def _cparams(sem):
    return pltpu.CompilerParams(dimension_semantics=sem, vmem_limit_bytes=VMEM_LIMIT)


def _const_spec(shape):
    nd = len(shape)
    return pl.BlockSpec(shape, lambda *_: (0,) * nd, pipeline_mode=pl.Buffered(1))


def _rms(x, g):
    return x * lax.rsqrt(jnp.mean(x * x, axis=-1, keepdims=True) + EPS) * g


def _sigmoid(x):
    return 1.0 / (1.0 + jnp.exp(-x))


def _gelu_tanh(x):
    c = math.sqrt(2.0 / math.pi)
    return 0.5 * x * (1.0 + jnp.tanh(c * (x + 0.044715 * (x * x * x))))


def _in_proj_kernel(x_ref, g_ref, w_ref, bm_ref, q_ref, k_ref, kb_ref, v_ref, vb_ref, lx_ref, gl_ref, gt_ref,
                    *, transposed_v):
    h = _rms(x_ref[...], g_ref[...]).astype(bf16)

    def proj(lo, width):
        return jnp.dot(h, w_ref[:, lo:lo + width], preferred_element_type=f32)

    q_ref[...] = (proj(C_Q, QK_COLS) * (HEAD_DIM ** -0.5 * LOG2E)).astype(bf16)
    k = proj(C_K, QK_COLS)
    k_ref[...] = k
    kb_ref[...] = k.astype(bf16)
    v = proj(C_V, V_COLS)
    v_ref[...] = v
    if transposed_v:
        for hd in range(N_HEADS):
            vb_ref[hd, :V_DIM, :] = v[:, hd * V_DIM:(hd + 1) * V_DIM].T.astype(bf16)
            vb_ref[hd, V_DIM:, :] = jnp.ones((ONES_ROWS, v.shape[0]), bf16)
    else:
        vb_ref[...] = v.astype(bf16)
    lx_ref[...] = proj(C_LX, LRU_WIDTH)
    gl_ref[...] = _gelu_tanh(proj(C_LY, LRU_WIDTH))
    gt_ref[...] = _sigmoid(proj(C_G, 2 * D_MODEL) + bm_ref[...])


def _in_proj(x, g_mix, w_in_b, b_merge, tm, tq=None):
    n = x.shape[0]
    row = lambda w: pl.BlockSpec((tm, w), lambda i: (i, 0))
    if tq is None:
        vb_shape, vb_spec = jax.ShapeDtypeStruct((n, V_COLS), bf16), row(V_COLS)
    else:
        per = tq // tm
        vb_shape = jax.ShapeDtypeStruct((N_HEADS, n // tq, V_DIM + ONES_ROWS, tq), bf16)
        vb_spec = pl.BlockSpec((N_HEADS, None, V_DIM + ONES_ROWS, tm), lambda i: (0, i // per, 0, i % per))
    outs = [
        jax.ShapeDtypeStruct((n, QK_COLS), bf16),
        jax.ShapeDtypeStruct((n, QK_COLS), f32),
        jax.ShapeDtypeStruct((n, QK_COLS), bf16),
        jax.ShapeDtypeStruct((n, V_COLS), f32),
        vb_shape,
        jax.ShapeDtypeStruct((n, LRU_WIDTH), f32),
        jax.ShapeDtypeStruct((n, LRU_WIDTH), f32),
        jax.ShapeDtypeStruct((n, 2 * D_MODEL), f32),
    ]
    return pl.pallas_call(
        functools.partial(_in_proj_kernel, transposed_v=tq is not None),
        grid=(n // tm,),
        in_specs=[row(D_MODEL), _const_spec((1, D_MODEL)), _const_spec((D_MODEL, IN_COLS)),
                  _const_spec((1, 2 * D_MODEL))],
        out_specs=[row(QK_COLS), row(QK_COLS), row(QK_COLS), row(V_COLS), vb_spec,
                   row(LRU_WIDTH), row(LRU_WIDTH), row(2 * D_MODEL)],
        out_shape=outs,
        compiler_params=_cparams(("parallel",)),
        name="in_proj",
    )(x, g_mix, w_in_b, b_merge)


def _rel_bucket(rel):
    half = N_BUCKETS // 2
    max_exact = half // 2
    ret = jnp.where(rel > 0, half, 0)
    n = jnp.abs(rel)
    nf = jnp.maximum(n, 1).astype(f32)
    large = max_exact + (jnp.log(nf / max_exact) / math.log(MAX_DISTANCE / max_exact)
                         * (half - max_exact)).astype(jnp.int32)
    large = jnp.minimum(large, half - 1)
    return ret + jnp.where(n < max_exact, n, large)


def _toeplitz(t, nq, nk):
    nh, length = t.shape
    a = jnp.broadcast_to(t[:, None, :], (nh, nq, length))
    a = jnp.pad(a, ((0, 0), (0, 0), (0, 1))).reshape(nh, nq * (length + 1))
    a = a[:, :nq * length].reshape(nh, nq, length)
    return a[:, :, nq - 1:nq - 1 + nk]


def _bias_table(rel_bias, qpos0, nq, kpos0, nk, shift):
    rel = jnp.arange(-(nq - 1), nk) + (kpos0 - qpos0)
    t = (rel_bias.astype(f32)[_rel_bucket(rel)].T - shift[:, None]) * LOG2E
    qpos = qpos0 + jnp.arange(nq)
    kpos = kpos0 + jnp.arange(nk)
    mask = (kpos[None, :] // CHUNK) <= (qpos[:, None] // CHUNK)
    return jnp.where(mask[None], _toeplitz(t, nq, nk), NEG_INF)


def _bias_ring(rel_bias, offset, tq, shift):
    r = jnp.arange(2 * tq)
    x = jnp.where(r < tq, r, r - 2 * tq)
    t = (rel_bias.astype(f32)[_rel_bucket(offset - x)].T - shift[:, None]) * LOG2E
    return t[:, None, :]


def _lam_value(lq1, lk1, lq2, lk2, lam_init):
    s1 = jnp.sum(lq1[...] * lk1[...], axis=-1, keepdims=True)
    s2 = jnp.sum(lq2[...] * lk2[...], axis=-1, keepdims=True)
    return jnp.exp(s1) - jnp.exp(s2) + lam_init


def _split_maps(q):
    lane = lax.broadcasted_iota(jnp.int32, q.shape, 1)
    zero = jnp.zeros_like(q)
    return jnp.where(lane < HEAD_DIM, q, zero), jnp.where(lane >= HEAD_DIM, q, zero)


def _qk(qz, kt):
    return lax.dot_general(qz, kt, (((1,), (1,)), ((), ())), preferred_element_type=f32)


def _finish_heads(o1, o2, lam, gs, lam_init):
    att = o1 - lam * o2
    return _rms(att, gs) * (1.0 - lam_init)


def _attn_prompt_kernel(q_ref, k_ref, vt_ref, ring_ref, lq1, lk1, lq2, lk2, gs_ref, o_ref,
                        nb_sc, s_sc, p_sc, al_sc, tm_sc, m_sc, acc_sc, *, tq, tk, lam_init):
    i = pl.program_id(1)
    ring_len = ring_ref.shape[-1]

    def bias_tile(ring):
        return pltpu.roll(jnp.broadcast_to(ring, (tk, ring_len)), 0, 1, stride=1, stride_axis=0)[:, :tq]

    @pl.when(i == 0)
    def _():
        kk = lax.broadcasted_iota(jnp.int32, (tk, tq), 0)
        qq = lax.broadcasted_iota(jnp.int32, (tk, tq), 1)
        nb_sc[0] = jnp.where((kk // CHUNK) <= (qq // CHUNK), bias_tile(ring_ref[1]), NEG_INF)
        nb_sc[1] = jnp.where(((kk + tk) // CHUNK) <= (qq // CHUNK), bias_tile(ring_ref[2]), NEG_INF)
        nb_sc[2] = jnp.full((tk, tq), NEG_INF, f32)

    @pl.when(i == 1)
    def _():
        nb_sc[2] = nb_sc[1]
        nb_sc[1] = nb_sc[0]
        nb_sc[0] = bias_tile(ring_ref[0])

    qt = q_ref[...].astype(f32).T
    row = lax.broadcasted_iota(jnp.int32, qt.shape, 0)
    qz = (jnp.where(row < HEAD_DIM, qt, 0.0).astype(bf16), jnp.where(row >= HEAD_DIM, qt, 0.0).astype(bf16))

    m_sc[...] = jnp.full(m_sc.shape, NEG_INF, f32)
    acc_sc[...] = jnp.zeros(acc_sc.shape, f32)

    def qk(slot, tile, bias):
        kt = k_ref[pl.ds(pl.multiple_of(tile * tk, tk), tk), :]
        for mi in range(2):
            s = jnp.dot(kt, qz[mi], preferred_element_type=f32)
            s = s if bias is None else s + bias
            s_sc[mi, slot] = s
            tm_sc[mi, slot] = jnp.max(s, axis=0, keepdims=True)

    def softmax(slot):
        for mi in range(2):
            s = s_sc[mi, slot]
            m_prev = m_sc[mi]
            m_new = jnp.maximum(m_prev, tm_sc[mi, slot])
            al_sc[mi, slot] = jnp.exp2(m_prev - m_new)
            p_sc[mi, slot] = jnp.exp2(s - m_new).astype(bf16)
            m_sc[mi] = m_new

    def pv(slot, vtile):
        va = vt_ref[vtile]
        for mi in range(2):
            acc_sc[mi] = acc_sc[mi] * al_sc[mi, slot] + jnp.dot(va, p_sc[mi, slot], preferred_element_type=f32)

    per = tq // tk
    a0 = jnp.maximum(per * i - 1, 0)
    n_far = jnp.maximum(per * i - 1, 0)
    last_tile = vt_ref.shape[0] - 1

    def far_k(f):
        return jnp.clip(f, 0, jnp.maximum(n_far - 1, 0))

    qk(0, a0, nb_sc[0])
    qk(1, a0 + 1, nb_sc[1])
    softmax(0)
    qk(0, jnp.minimum(a0 + 2, last_tile), nb_sc[2])
    softmax(1)
    pv(0, a0)

    @pl.when(i == 0)
    def _():
        pv(1, a0 + 1)

    @pl.when(i > 0)
    def _():
        qk(1, 0, None)
        softmax(0)
        pv(1, a0 + 1)
        qk(0, far_k(1), None)
        softmax(1)
        pv(0, a0 + 2)

        def far_pair(p, carry):
            qk(1, 2 * p + 2, None)
            softmax(0)
            pv(1, 2 * p)
            qk(0, far_k(2 * p + 3), None)
            softmax(1)
            pv(0, 2 * p + 1)
            return carry

        lax.fori_loop(0, (n_far - 1) // 2, far_pair, 0)
        pv(1, n_far - 1)

    a1 = acc_sc[0]
    a2 = acc_sc[1]
    o1 = a1[:V_DIM] / a1[V_DIM:V_DIM + 1]
    o2 = a2[:V_DIM] / a2[V_DIM:V_DIM + 1]
    lam = _lam_value(lq1, lk1, lq2, lk2, lam_init)
    att = (o1 - lam * o2).T
    o_ref[...] = (_rms(att, gs_ref[...]) * (1.0 - lam_init)).astype(o_ref.dtype)


def _attn_prompt(q, kb, vt, rel_bias, lam_vecs, g_subln, lam_init, tq, tk):
    t = q.shape[0]
    nq = t // tq
    assert tq == 2 * tk and t % tq == 0
    far = rel_bias.astype(f32)[_rel_bucket(jnp.asarray(-4 * MAX_DISTANCE))]
    rings = jnp.stack([_bias_ring(rel_bias, d, tq, far) for d in (-tk, 0, tk)], axis=1)
    vec = _const_spec((1, HEAD_DIM))
    return pl.pallas_call(
        functools.partial(_attn_prompt_kernel, tq=tq, tk=tk, lam_init=lam_init),
        grid=(N_HEADS, nq),
        in_specs=[
            pl.BlockSpec((tq, 2 * HEAD_DIM), lambda h, i: (i, h)),
            pl.BlockSpec((t, 2 * HEAD_DIM), lambda h, i: (0, h)),
            pl.BlockSpec((None, t // tk, V_DIM + ONES_ROWS, tk), lambda h, i: (h, 0, 0, 0)),
            pl.BlockSpec((None, 3, 1, 2 * tq), lambda h, i: (h, 0, 0, 0)),
            vec, vec, vec, vec, _const_spec((1, V_DIM)),
        ],
        out_specs=pl.BlockSpec((tq, V_DIM), lambda h, i: (i, h)),
        out_shape=jax.ShapeDtypeStruct((t, V_COLS), bf16),
        scratch_shapes=[pltpu.VMEM((3, tk, tq), f32), pltpu.VMEM((2, 2, tk, tq), f32),
                        pltpu.VMEM((2, 2, tk, tq), bf16), pltpu.VMEM((2, 2, 1, tq), f32),
                        pltpu.VMEM((2, 2, 1, tq), f32), pltpu.VMEM((2, 1, tq), f32),
                        pltpu.VMEM((2, V_DIM + ONES_ROWS, tq), f32)],
        compiler_params=_cparams(("arbitrary", "arbitrary")),
        name="attn_prompt",
    )(q, kb, vt, rings, *lam_vecs, g_subln)


def _attn_sample_kernel(q_ref, kp_ref, vp_ref, kn_ref, vn_ref, nbp_ref, nbn_ref, lq1, lk1, lq2, lk2, gs_ref,
                        o_ref, *, lam_init):
    lam = _lam_value(lq1, lk1, lq2, lk2, lam_init)
    for hd in range(N_HEADS):
        qk_cols = slice(hd * 2 * HEAD_DIM, (hd + 1) * 2 * HEAD_DIM)
        v_cols = slice(hd * V_DIM, (hd + 1) * V_DIM)
        qz = _split_maps(q_ref[:, qk_cols])
        kp = kp_ref[:, qk_cols].astype(bf16)
        vp = vp_ref[:, v_cols].astype(bf16)
        kn = kn_ref[:, qk_cols]
        vn = vn_ref[:, v_cols]
        outs = []
        for mi in range(2):
            sp = _qk(qz[mi], kp) + nbp_ref[hd]
            sn = _qk(qz[mi], kn) + nbn_ref[hd]
            m = jnp.maximum(jnp.max(sp, axis=1, keepdims=True), jnp.max(sn, axis=1, keepdims=True))
            pp = jnp.exp2(sp - m).astype(bf16)
            pn = jnp.exp2(sn - m).astype(bf16)
            den = (jnp.sum(pp.astype(f32), axis=1, keepdims=True)
                   + jnp.sum(pn.astype(f32), axis=1, keepdims=True))
            num = jnp.dot(pp, vp, preferred_element_type=f32) + jnp.dot(pn, vn, preferred_element_type=f32)
            outs.append(num / den)
        o_ref[:, v_cols] = _finish_heads(outs[0], outs[1], lam, gs_ref[...], lam_init).astype(o_ref.dtype)


def _attn_sample(q, kb, vb, cache_k, cache_v, rel_bias, lam_vecs, g_subln, lam_init, nb_, ts):
    past = cache_k.shape[1]
    zero = jnp.zeros((N_HEADS,), f32)
    nbp = _bias_table(rel_bias, past, ts, 0, past, zero)
    nbn = _bias_table(rel_bias, past, ts, past, ts, zero)
    vec = _const_spec((1, HEAD_DIM))
    return pl.pallas_call(
        functools.partial(_attn_sample_kernel, lam_init=lam_init),
        grid=(nb_,),
        in_specs=[
            pl.BlockSpec((ts, QK_COLS), lambda b: (b, 0)),
            pl.BlockSpec((None, past, QK_COLS), lambda b: (b, 0, 0)),
            pl.BlockSpec((None, past, V_COLS), lambda b: (b, 0, 0)),
            pl.BlockSpec((ts, QK_COLS), lambda b: (b, 0)),
            pl.BlockSpec((ts, V_COLS), lambda b: (b, 0)),
            _const_spec((N_HEADS, ts, past)), _const_spec((N_HEADS, ts, ts)),
            vec, vec, vec, vec, _const_spec((1, V_DIM)),
        ],
        out_specs=pl.BlockSpec((ts, V_COLS), lambda b: (b, 0)),
        out_shape=jax.ShapeDtypeStruct((nb_ * ts, V_COLS), bf16),
        compiler_params=_cparams(("parallel",)),
        name="attn_sample",
    )(q, cache_k, cache_v, kb, vb, nbp, nbn, *lam_vecs, g_subln)


def _conv_lru_kernel(lx_ref, gl_ref, c0_ref, h0_ref, wc_ref, bc_ref, wa_ref, ba_ref, wx_ref, bx_ref, lam_ref,
                     hsg_ref, cout_ref, hout_ref, xbuf, a_sc, b_sc, hcar, *, tt, first_pos_is_zero):
    t = pl.program_id(1)

    @pl.when(t == 0)
    def _():
        xbuf[0:SUBLANES] = c0_ref[...]
        hcar[...] = jnp.broadcast_to(h0_ref[...], hcar.shape)

    xbuf[SUBLANES:SUBLANES + tt] = lx_ref[...]
    xc = jnp.broadcast_to(bc_ref[...], (tt, LRU_WIDTH))
    for j in range(CONV_W):
        lo = SUBLANES - (CONV_W - 1) + j
        xc = xc + xbuf[lo:lo + tt, :] * wc_ref[j:j + 1, :]
    tail = xbuf[tt:tt + SUBLANES, :]
    xbuf[0:SUBLANES] = tail
    cout_ref[...] = tail

    xcb = xc.astype(bf16)

    def block_diag(w_ref, b_ref):
        cols = [jnp.dot(xcb[:, n * LRU_BLOCK_W:(n + 1) * LRU_BLOCK_W], w_ref[n], preferred_element_type=f32)
                for n in range(LRU_BLOCKS)]
        return jnp.concatenate(cols, axis=1) + b_ref[...]

    r = _sigmoid(block_diag(wa_ref, ba_ref))
    ig = _sigmoid(block_diag(wx_ref, bx_ref))
    z = -lam_ref[...]
    softplus = jnp.maximum(z, 0.0) + jnp.log1p(jnp.exp(-jnp.abs(z)))
    log_a = -LRU_C * r * softplus
    a = jnp.exp(log_a)
    th = jnp.tanh(log_a)
    m2 = -2.0 * th / (1.0 - th)
    mult = jnp.where(m2 > 0.0, m2 * lax.rsqrt(m2), 0.0)
    if first_pos_is_zero:
        row = lax.broadcasted_iota(jnp.int32, (tt, LRU_WIDTH), 0)
        mult = jnp.where((row == 0) & (t == 0), 1.0, mult)
    bv = mult * ig * xc

    groups = (tt // SUBLANES, SUBLANES, LRU_WIDTH)
    a = a.reshape(groups)
    bv = bv.reshape(groups)
    sub = lax.broadcasted_iota(jnp.int32, groups, 1)
    d = 1
    while d < SUBLANES:
        valid = sub >= d
        a_s = pltpu.roll(a, d, axis=1)
        b_s = pltpu.roll(bv, d, axis=1)
        bv = jnp.where(valid, a * b_s + bv, bv)
        a = jnp.where(valid, a * a_s, a)
        d *= 2
    a_sc[...] = a.reshape(tt, LRU_WIDTH)
    b_sc[...] = bv.reshape(tt, LRU_WIDTH)

    def group(g, h):
        sl = pl.ds(pl.multiple_of(g * SUBLANES, SUBLANES), SUBLANES)
        hg = a_sc[sl, :] * h + b_sc[sl, :]
        b_sc[sl, :] = hg
        return jnp.broadcast_to(hg[SUBLANES - 1:SUBLANES, :], hg.shape)

    h = lax.fori_loop(0, tt // SUBLANES, group, hcar[...], unroll=4)
    hcar[...] = h
    hout_ref[...] = h[0:1, :]
    hsg_ref[...] = (b_sc[...] * gl_ref[...]).astype(hsg_ref.dtype)


def _conv_lru(lx, gl, conv0, h0, w_conv, b_conv, wa, ba, wx, bx, lru_lambda, nb_, tlen, tt, first_pos_is_zero):
    nt = tlen // tt
    rows = pl.BlockSpec((tt, LRU_WIDTH), lambda b, t: (b * nt + t, 0))
    vecw = _const_spec((1, LRU_WIDTH))
    wblk = _const_spec((LRU_BLOCKS, LRU_BLOCK_W, LRU_BLOCK_W))
    return pl.pallas_call(
        functools.partial(_conv_lru_kernel, tt=tt, first_pos_is_zero=first_pos_is_zero),
        grid=(nb_, nt),
        in_specs=[rows, rows,
                  pl.BlockSpec((None, SUBLANES, LRU_WIDTH), lambda b, t: (b, 0, 0)),
                  pl.BlockSpec((None, 1, LRU_WIDTH), lambda b, t: (b, 0, 0)),
                  _const_spec((CONV_W, LRU_WIDTH)), vecw, wblk, vecw, wblk, vecw, vecw],
        out_specs=[rows,
                   pl.BlockSpec((None, SUBLANES, LRU_WIDTH), lambda b, t: (b, 0, 0)),
                   pl.BlockSpec((None, 1, LRU_WIDTH), lambda b, t: (b, 0, 0))],
        out_shape=[jax.ShapeDtypeStruct((nb_ * tlen, LRU_WIDTH), bf16),
                   jax.ShapeDtypeStruct((nb_, SUBLANES, LRU_WIDTH), f32),
                   jax.ShapeDtypeStruct((nb_, 1, LRU_WIDTH), f32)],
        scratch_shapes=[pltpu.VMEM((tt + SUBLANES, LRU_WIDTH), f32), pltpu.VMEM((tt, LRU_WIDTH), f32),
                        pltpu.VMEM((tt, LRU_WIDTH), f32), pltpu.VMEM((SUBLANES, LRU_WIDTH), f32)],
        compiler_params=_cparams(("arbitrary", "arbitrary")),
        name="conv_lru",
    )(lx, gl, conv0, h0, w_conv, b_conv, wa, ba, wx, bx, lru_lambda)


def _merge_kernel(x_ref, att_ref, hsg_ref, gt_ref, wa_ref, wb_ref, wo_ref, gf_ref, wr_ref, br_ref,
                  x1_ref, h2_ref, comb_ref, *, parts):
    rows_per = x_ref.shape[0] // parts
    for part in range(parts):
        rows = slice(part * rows_per, (part + 1) * rows_per)
        bra = jnp.dot(att_ref[rows, :], wa_ref[...], preferred_element_type=f32)
        brb = jnp.dot(hsg_ref[rows, :], wb_ref[...], preferred_element_type=f32)
        gt = gt_ref[rows, :]
        m = (gt[:, :D_MODEL] * bra + gt[:, D_MODEL:] * brb).astype(bf16)
        x1 = x_ref[rows, :] + jnp.dot(m, wo_ref[...], preferred_element_type=f32)
        x1_ref[rows, :] = x1
        h2 = _rms(x1, gf_ref[...])
        h2_hi = h2.astype(bf16)
        h2_ref[rows, :] = h2_hi

        h2_lo = (h2 - h2_hi.astype(f32)).astype(bf16)
        prods = (jnp.dot(h2_hi, wr_ref[...], preferred_element_type=f32)
                 + jnp.dot(h2_lo, wr_ref[...], preferred_element_type=f32))
        logits = prods[:, :LANES] + prods[:, LANES:] + br_ref[...]
        lane = lax.broadcasted_iota(jnp.int32, logits.shape, 1)
        lanef = lane.astype(f32)
        low = jnp.float32(-3.0e38)
        is_grp = (lane >= N_EXPERTS) & (lane < N_EXPERTS + N_GROUPS)
        gl = jnp.where(is_grp, logits, low)
        gmax = jnp.max(gl, axis=1, keepdims=True)
        gidx = jnp.min(jnp.where(is_grp & (gl == gmax), lanef, 1.0e3), axis=1, keepdims=True) - N_EXPERTS
        gden = jnp.sum(jnp.where(is_grp, jnp.exp(gl - gmax), 0.0), axis=1, keepdims=True)
        g_w = 1.0 / gden
        lo = gidx * EXPERTS_PER_GROUP
        in_sel = (lanef >= lo) & (lanef < lo + EXPERTS_PER_GROUP)
        sel = jnp.where(in_sel, logits, low)
        v1 = jnp.max(sel, axis=1, keepdims=True)
        i1 = jnp.min(jnp.where(in_sel & (sel == v1), lanef, 1.0e3), axis=1, keepdims=True)
        in_sel2 = in_sel & (lanef != i1)
        sel2 = jnp.where(in_sel2, logits, low)
        v2 = jnp.max(sel2, axis=1, keepdims=True)
        i2 = jnp.min(jnp.where(in_sel2 & (sel2 == v2), lanef, 1.0e3), axis=1, keepdims=True)
        e2 = jnp.exp(v2 - v1)
        w1 = g_w / (1.0 + e2)
        w2 = g_w * e2 / (1.0 + e2)
        comb_ref[rows, :] = jnp.where(lanef == i1, w1, 0.0) + jnp.where(lanef == i2, w2, 0.0)


def _merge(x, att, hsg, gt, wa, wb, wo, g_ffn, w_router, b_router, tm):
    n = x.shape[0]
    row = lambda w: pl.BlockSpec((tm, w), lambda i: (i, 0))
    sq = _const_spec((D_MODEL, D_MODEL))
    return pl.pallas_call(
        functools.partial(_merge_kernel, parts=2 if tm % 512 == 0 else 1),
        grid=(n // tm,),
        in_specs=[row(D_MODEL), row(V_COLS), row(LRU_WIDTH), row(2 * D_MODEL), sq, sq, sq,
                  _const_spec((1, D_MODEL)), _const_spec((D_MODEL, 2 * LANES)), _const_spec((1, LANES))],
        out_specs=[row(D_MODEL), row(D_MODEL), row(LANES)],
        out_shape=[jax.ShapeDtypeStruct((n, D_MODEL), f32), jax.ShapeDtypeStruct((n, D_MODEL), bf16),
                   jax.ShapeDtypeStruct((n, LANES), f32)],
        compiler_params=_cparams(("parallel",)),
        name="merge_router",
    )(x, att, hsg, gt, wa, wb, wo, g_ffn, w_router, b_router)


EXPERTS_PER_STEP = 8


MOE_CHUNK = 128


def _moe_kernel(h_ref, comb_ref, wg_ref, wu_ref, wd_ref, y_ref, pt_sc, x_sc, c_sc, ys_sc, seg_sc):
    e = pl.program_id(1)
    tm = h_ref.shape[0]
    lane1 = lax.broadcasted_iota(jnp.int32, (1, LANES), 1)

    @pl.when(e == 0)
    def _():
        comb = comb_ref[...]
        ex = lax.broadcasted_iota(jnp.int32, (LANES, LANES), 0)
        gr = lax.broadcasted_iota(jnp.int32, (LANES, LANES), 1)
        in_group = jnp.where((ex // EXPERTS_PER_GROUP == gr) & (ex < N_EXPERTS), 1.0, 0.0).astype(bf16)
        gsum = jnp.dot(comb.astype(bf16), in_group, preferred_element_type=f32)
        member = jnp.where(gsum > 0.0, 1.0, 0.0)
        r_i = lax.broadcasted_iota(jnp.int32, (tm, tm), 0)
        c_i = lax.broadcasted_iota(jnp.int32, (tm, tm), 1)
        tri = jnp.where(gr <= ex, 1.0, 0.0).astype(bf16)
        member_b = member.astype(bf16)
        totals = jnp.zeros((1, LANES), f32)
        blocks = []
        for blk in range(tm // LANES):
            part = jnp.dot(tri, member_b[blk * LANES:(blk + 1) * LANES], preferred_element_type=f32) + totals
            blocks.append(part)
            totals = part[LANES - 1:LANES, :]
        rank = jnp.concatenate(blocks, axis=0)
        offs = jnp.zeros((1, LANES), f32)
        for g in range(1, N_GROUPS):
            prev = jnp.sum(jnp.where(lane1 == g - 1, totals, 0.0), axis=1, keepdims=True)
            offs = offs + jnp.where(lane1 >= g, prev, 0.0)
        seg_sc[0:1, :] = offs.astype(jnp.int32)
        seg_sc[1:2, :] = totals.astype(jnp.int32)
        pos = jnp.sum(member * (offs + rank - 1.0), axis=1, keepdims=True)
        pos_row = jnp.broadcast_to(pos, (tm, LANES)).T[0:1, :].astype(jnp.int32)
        perm = jnp.where(r_i == pos_row, 1.0, 0.0).astype(bf16)
        pt_sc[...] = jnp.where(c_i == pos.astype(jnp.int32), 1.0, 0.0).astype(bf16)
        comb_hi = comb.astype(bf16)
        comb_lo = (comb - comb_hi.astype(f32)).astype(bf16)
        srt = jnp.dot(perm, jnp.concatenate([h_ref[...], comb_hi, comb_lo], axis=1), preferred_element_type=f32)
        x_sc[...] = srt[:, :D_MODEL].astype(bf16)
        c_sc[...] = srt[:, D_MODEL:D_MODEL + LANES] + srt[:, D_MODEL + LANES:]
        ys_sc[...] = jnp.zeros(ys_sc.shape, f32)

    g = (e * EXPERTS_PER_STEP) // EXPERTS_PER_GROUP
    off = jnp.sum(jnp.where(lane1 == g, seg_sc[0:1, :], 0))
    cnt = jnp.sum(jnp.where(lane1 == g, seg_sc[1:2, :], 0))
    lane = lax.broadcasted_iota(jnp.int32, (MOE_CHUNK, LANES), 1)

    def chunk(k, carry):
        rows = pl.ds(pl.multiple_of(k * MOE_CHUNK, MOE_CHUNK), MOE_CHUNK)
        x = x_sc[rows, :]
        cw = c_sc[rows, :]
        hidden = []
        for j in range(EXPERTS_PER_STEP):
            gate = jnp.dot(x, wg_ref[j], preferred_element_type=f32)
            up = jnp.dot(x, wu_ref[j], preferred_element_type=f32)
            c = jnp.sum(jnp.where(lane == e * EXPERTS_PER_STEP + j, cw, 0.0), axis=1, keepdims=True)
            hidden.append((gate * _sigmoid(gate) * up * c).astype(bf16))
        ys_sc[rows, :] += jnp.dot(jnp.concatenate(hidden, axis=1), wd_ref[...], preferred_element_type=f32)
        return carry

    lax.fori_loop(off // MOE_CHUNK, (off + cnt + MOE_CHUNK - 1) // MOE_CHUNK, chunk, 0)

    @pl.when(e == pl.num_programs(1) - 1)
    def _():
        y = jnp.dot(pt_sc[...], ys_sc[...].astype(bf16), preferred_element_type=f32)
        y_ref[...] = y.astype(y_ref.dtype)


def _cast_kernel(x_ref, o_ref):
    o_ref[...] = x_ref[...].astype(o_ref.dtype)


def _to_bf16(x, rows):
    n, width = x.shape
    spec = pl.BlockSpec((rows, width), lambda i: (i, 0))
    return pl.pallas_call(
        _cast_kernel, grid=(n // rows,), in_specs=[spec], out_specs=spec,
        out_shape=jax.ShapeDtypeStruct((n, width), bf16),
        compiler_params=_cparams(("parallel",)), name="to_bf16",
    )(x)


def _moe(h2, comb, wg, wu, wd_rows, tm):
    n = h2.shape[0]
    per = EXPERTS_PER_STEP
    assert EXPERTS_PER_GROUP % per == 0 and tm % MOE_CHUNK == 0
    return pl.pallas_call(
        _moe_kernel,
        grid=(n // tm, N_EXPERTS // per),
        in_specs=[pl.BlockSpec((tm, D_MODEL), lambda i, e: (i, 0)),
                  pl.BlockSpec((tm, LANES), lambda i, e: (i, 0)),
                  pl.BlockSpec((per, D_MODEL, D_EXPERT), lambda i, e: (e, 0, 0)),
                  pl.BlockSpec((per, D_MODEL, D_EXPERT), lambda i, e: (e, 0, 0)),
                  pl.BlockSpec((per * D_EXPERT, D_MODEL), lambda i, e: (e, 0))],
        out_specs=pl.BlockSpec((tm, D_MODEL), lambda i, e: (i, 0)),
        out_shape=jax.ShapeDtypeStruct((n, D_MODEL), bf16),
        scratch_shapes=[pltpu.VMEM((tm, tm), bf16), pltpu.VMEM((tm, D_MODEL), bf16),
                        pltpu.VMEM((tm, LANES), f32), pltpu.VMEM((tm, D_MODEL), f32),
                        pltpu.VMEM((SUBLANES, LANES), jnp.int32)],
        compiler_params=_cparams(("arbitrary", "arbitrary")),
        name="moe",
    )(h2, comb, wg, wu, wd_rows)


def _ple_kernel(x1_ref, y_ref, p_ref, gp_ref, wpg_ref, bpg_ref, wpi_ref, gfin_ref, o_ref):
    x2 = x1_ref[...] + y_ref[...]
    hp = _rms(x2, gp_ref[...]).astype(bf16)
    pg = _sigmoid(jnp.dot(hp, wpg_ref[...], preferred_element_type=f32) + bpg_ref[...])
    pin = jnp.dot(p_ref[...].astype(bf16), wpi_ref[...], preferred_element_type=f32)
    x3 = x2 + pg * pin
    o_ref[...] = _rms(x3, gfin_ref[...])


def _ple(x1, y, p, g_ple, wpg, bpg, wpi, g_final, tm):
    n = x1.shape[0]
    row = lambda w: pl.BlockSpec((tm, w), lambda i: (i, 0))
    vec = _const_spec((1, D_MODEL))
    return pl.pallas_call(
        _ple_kernel,
        grid=(n // tm,),
        in_specs=[row(D_MODEL), row(D_MODEL), row(PLE_DIM), vec, _const_spec((D_MODEL, D_MODEL)), vec,
                  _const_spec((PLE_DIM, D_MODEL)), vec],
        out_specs=row(D_MODEL),
        out_shape=jax.ShapeDtypeStruct((n, D_MODEL), f32),
        compiler_params=_cparams(("parallel",)),
        name="ple_final",
    )(x1, y, p, g_ple, wpg, bpg, wpi, g_final)


def _tile(n, pref):
    return pref if n % pref == 0 else n


def _stream(x, p, conv0, h0, cache, li, rel_bias, w):
    nb_, tlen, _ = x.shape
    n = nb_ * tlen
    xf = x.reshape(n, D_MODEL)
    lam_init = 0.8 - 0.6 * math.exp(-0.3 * li)
    tk = None if cache is not None else 512
    q, k, kb, v, vb, lx, gl, gt = _in_proj(xf, w["g_mix"], w["w_in"], w["b_merge"], _tile(n, 256), tk)
    if cache is None:
        att = _attn_prompt(q, kb, vb, rel_bias, w["lam_vecs"], w["g_subln"], lam_init, 2 * tk, tk)
    else:
        att = _attn_sample(q, kb, vb, cache[0], cache[1], rel_bias, w["lam_vecs"], w["g_subln"], lam_init,
                           nb_, tlen)
    conv_pad = jnp.pad(conv0, ((0, 0), (SUBLANES - (CONV_W - 1), 0), (0, 0)))
    hsg, cout, hout = _conv_lru(lx, gl, conv_pad, h0[:, None, :], w["w_conv"], w["b_conv"], w["w_rg_a"],
                                w["b_rg_a"], w["w_rg_x"], w["b_rg_x"], w["lru_lambda"], nb_, tlen,
                                _tile(tlen, 256), cache is None)
    tm = _tile(n, 512)
    x1, h2, comb = _merge(xf, att, hsg, gt, w["w_attn_br"], w["w_lru_br"], w["w_out"], w["g_ffn"],
                          w["w_router"], w["b_router"], tm)
    y = _moe(h2, comb, w["w_e_gate"], w["w_e_up"], w["w_e_down"], _tile(n, 1024))
    out = _ple(x1, y, p.reshape(n, PLE_DIM), w["g_ple"], w["w_ple_gate"], w["b_ple_gate"], w["w_ple_in"],
               w["g_final"], tm)
    return out, k, v, cout[:, SUBLANES - (CONV_W - 1):, :], hout[:, 0, :]


def kernel(x_prompt, x_sample, cache_k, cache_v, state_conv, state_lru, p_prompt, p_sample, rel_bias, g_mix, w_in, b_merge, lam_q1, lam_k1, lam_q2, lam_k2, g_subln, w_attn_br, w_conv, b_conv, w_rg_a, b_rg_a, w_rg_x, b_rg_x, lru_lambda, w_lru_br, w_out, g_ffn, w_grp, b_grp, w_rt, b_rt, w_e_gate, w_e_up, w_e_down, g_ple, w_ple_gate, b_ple_gate, w_ple_in, g_final):
    depth = w_in.shape[0]
    assert depth == 1, "the final norm is fused into the single layer's last stage"
    bp, tp, _ = x_prompt.shape
    bs, ts, _ = x_sample.shape
    past = cache_k.shape[2]
    li = 0
    row = lambda a: a[li].reshape(1, -1).astype(f32)
    w_router = jnp.concatenate(
        [jnp.transpose(w_rt[li], (1, 0, 2)).reshape(D_MODEL, N_EXPERTS), w_grp[li]], axis=1).astype(f32)
    w_router = jnp.pad(w_router, ((0, 0), (0, LANES - N_EXPERTS - N_GROUPS)))
    w_router_hi = w_router.astype(bf16)
    w_router = jnp.concatenate([w_router_hi, (w_router - w_router_hi.astype(f32)).astype(bf16)], axis=1)
    b_router = jnp.pad(jnp.concatenate([b_rt[li].reshape(-1), b_grp[li]]).astype(f32),
                       (0, LANES - N_EXPERTS - N_GROUPS)).reshape(1, LANES)
    w = dict(
        g_mix=row(g_mix), w_in=w_in[li].astype(bf16), b_merge=row(b_merge),
        lam_vecs=(row(lam_q1), row(lam_k1), row(lam_q2), row(lam_k2)), g_subln=row(g_subln),
        w_attn_br=w_attn_br[li].astype(bf16), w_conv=w_conv[li].astype(f32), b_conv=row(b_conv),
        w_rg_a=w_rg_a[li].astype(bf16), b_rg_a=row(b_rg_a), w_rg_x=w_rg_x[li].astype(bf16), b_rg_x=row(b_rg_x),
        lru_lambda=row(lru_lambda), w_lru_br=w_lru_br[li].astype(bf16), w_out=w_out[li].astype(bf16),
        g_ffn=row(g_ffn), w_router=w_router, b_router=b_router,
        w_e_gate=w_e_gate[li].astype(bf16), w_e_up=w_e_up[li].astype(bf16), w_e_down=_to_bf16(w_e_down[li].reshape(N_EXPERTS * D_EXPERT, D_MODEL), 1024),
        g_ple=row(g_ple), w_ple_gate=w_ple_gate[li].astype(bf16), b_ple_gate=row(b_ple_gate),
        w_ple_in=w_ple_in[li].astype(bf16), g_final=g_final.reshape(1, -1).astype(f32),
    )
    conv0 = jnp.zeros((bp, CONV_W - 1, LRU_WIDTH), f32)
    h0 = jnp.zeros((bp, LRU_WIDTH), f32)
    yp, kp, vp, cp, hp = _stream(x_prompt, p_prompt[li], conv0, h0, None, li, rel_bias, w)
    cache = (cache_k[li].reshape(bs, past, QK_COLS), cache_v[li].reshape(bs, past, V_COLS))
    ys, ks, vs, cs, hs = _stream(x_sample, p_sample[li], state_conv[li], state_lru[li], cache, li, rel_bias, w)
    return (yp.reshape(bp, tp, D_MODEL), ys.reshape(bs, ts, D_MODEL),
            kp.reshape(1, bp, tp, N_HEADS, 2, HEAD_DIM), vp.reshape(1, bp, tp, N_HEADS, V_DIM),
            cp[None], hp[None],
            ks.reshape(1, bs, ts, N_HEADS, 2, HEAD_DIM), vs.reshape(1, bs, ts, N_HEADS, V_DIM),
            cs[None], hs[None])
```

```python
import functools
import math

import jax
import jax.numpy as jnp
from jax import lax
from jax.experimental import pallas as pl
from jax.experimental.pallas import tpu as pltpu

f32 = jnp.float32
bf16 = jnp.bfloat16

D_MODEL = 1024
N_HEADS = 8
HEAD_DIM = 64
V_DIM = 128
QK_COLS = N_HEADS * 2 * HEAD_DIM
V_COLS = N_HEADS * V_DIM
LRU_WIDTH = 1024
LRU_BLOCKS = 8
LRU_BLOCK_W = LRU_WIDTH // LRU_BLOCKS
CONV_W = 4
LRU_C = 8.0
CHUNK = 64
N_BUCKETS = 32
MAX_DISTANCE = 128
N_GROUPS = 4
EXPERTS_PER_GROUP = 8
N_EXPERTS = N_GROUPS * EXPERTS_PER_GROUP
D_EXPERT = 256
PLE_DIM = 256
EPS = 1e-6
NEG_INF = -1e30
LOG2E = math.log2(math.e)
ONES_ROWS = 16
LANES = 128
SUBLANES = 8
VMEM_LIMIT = 56 * 1024 * 1024

C_Q, C_K, C_V = 0, QK_COLS, 2 * QK_COLS
C_LX = 2 * QK_COLS + V_COLS
C_LY = C_LX + LRU_WIDTH
C_G = C_LY + LRU_WIDTH
IN_COLS = C_G + 2 * D_MODEL


def _cparams(sem):
    return pltpu.CompilerParams(dimension_semantics=sem, vmem_limit_bytes=VMEM_LIMIT)


def _const_spec(shape):
    nd = len(shape)
    return pl.BlockSpec(shape, lambda *_: (0,) * nd, pipeline_mode=pl.Buffered(1))


def _rms(x, g):
    return x * lax.rsqrt(jnp.mean(x * x, axis=-1, keepdims=True) + EPS) * g


def _sigmoid(x):
    return 1.0 / (1.0 + jnp.exp(-x))


def _gelu_tanh(x):
    c = math.sqrt(2.0 / math.pi)
    return 0.5 * x * (1.0 + jnp.tanh(c * (x + 0.044715 * (x * x * x))))


def _in_proj_kernel(x_ref, g_ref, w_ref, bm_ref, q_ref, k_ref, kb_ref, v_ref, vb_ref, lx_ref, gl_ref, gt_ref,
                    *, transposed_v):
    h = _rms(x_ref[...], g_ref[...]).astype(bf16)

    def proj(lo, width):
        return jnp.dot(h, w_ref[:, lo:lo + width], preferred_element_type=f32)

    q_ref[...] = (proj(C_Q, QK_COLS) * (HEAD_DIM ** -0.5 * LOG2E)).astype(bf16)
    k = proj(C_K, QK_COLS)
    k_ref[...] = k
    kb_ref[...] = k.astype(bf16)
    v = proj(C_V, V_COLS)
    v_ref[...] = v
    if transposed_v:
        for hd in range(N_HEADS):
            vb_ref[hd, :V_DIM, :] = v[:, hd * V_DIM:(hd + 1) * V_DIM].T.astype(bf16)
            vb_ref[hd, V_DIM:, :] = jnp.ones((ONES_ROWS, v.shape[0]), bf16)
    else:
        vb_ref[...] = v.astype(bf16)
    lx_ref[...] = proj(C_LX, LRU_WIDTH)
    gl_ref[...] = _gelu_tanh(proj(C_LY, LRU_WIDTH))
    gt_ref[...] = _sigmoid(proj(C_G, 2 * D_MODEL) + bm_ref[...])


def _in_proj(x, g_mix, w_in_b, b_merge, tm, tq=None):
    n = x.shape[0]
    row = lambda w: pl.BlockSpec((tm, w), lambda i: (i, 0))
    if tq is None:
        vb_shape, vb_spec = jax.ShapeDtypeStruct((n, V_COLS), bf16), row(V_COLS)
    else:
        per = tq // tm
        vb_shape = jax.ShapeDtypeStruct((N_HEADS, n // tq, V_DIM + ONES_ROWS, tq), bf16)
        vb_spec = pl.BlockSpec((N_HEADS, None, V_DIM + ONES_ROWS, tm), lambda i: (0, i // per, 0, i % per))
    outs = [
        jax.ShapeDtypeStruct((n, QK_COLS), bf16),
        jax.ShapeDtypeStruct((n, QK_COLS), f32),
        jax.ShapeDtypeStruct((n, QK_COLS), bf16),
        jax.ShapeDtypeStruct((n, V_COLS), f32),
        vb_shape,
        jax.ShapeDtypeStruct((n, LRU_WIDTH), f32),
        jax.ShapeDtypeStruct((n, LRU_WIDTH), f32),
        jax.ShapeDtypeStruct((n, 2 * D_MODEL), f32),
    ]
    return pl.pallas_call(
        functools.partial(_in_proj_kernel, transposed_v=tq is not None),
        grid=(n // tm,),
        in_specs=[row(D_MODEL), _const_spec((1, D_MODEL)), _const_spec((D_MODEL, IN_COLS)),
                  _const_spec((1, 2 * D_MODEL))],
        out_specs=[row(QK_COLS), row(QK_COLS), row(QK_COLS), row(V_COLS), vb_spec,
                   row(LRU_WIDTH), row(LRU_WIDTH), row(2 * D_MODEL)],
        out_shape=outs,
        compiler_params=_cparams(("parallel",)),
        name="in_proj",
    )(x, g_mix, w_in_b, b_merge)


def _rel_bucket(rel):
    half = N_BUCKETS // 2
    max_exact = half // 2
    ret = jnp.where(rel > 0, half, 0)
    n = jnp.abs(rel)
    nf = jnp.maximum(n, 1).astype(f32)
    large = max_exact + (jnp.log(nf / max_exact) / math.log(MAX_DISTANCE / max_exact)
                         * (half - max_exact)).astype(jnp.int32)
    large = jnp.minimum(large, half - 1)
    return ret + jnp.where(n < max_exact, n, large)


def _toeplitz(t, nq, nk):
    nh, length = t.shape
    a = jnp.broadcast_to(t[:, None, :], (nh, nq, length))
    a = jnp.pad(a, ((0, 0), (0, 0), (0, 1))).reshape(nh, nq * (length + 1))
    a = a[:, :nq * length].reshape(nh, nq, length)
    return a[:, :, nq - 1:nq - 1 + nk]


def _bias_table(rel_bias, qpos0, nq, kpos0, nk, shift):
    rel = jnp.arange(-(nq - 1), nk) + (kpos0 - qpos0)
    t = (rel_bias.astype(f32)[_rel_bucket(rel)].T - shift[:, None]) * LOG2E
    qpos = qpos0 + jnp.arange(nq)
    kpos = kpos0 + jnp.arange(nk)
    mask = (kpos[None, :] // CHUNK) <= (qpos[:, None] // CHUNK)
    return jnp.where(mask[None], _toeplitz(t, nq, nk), NEG_INF)


def _bias_ring(rel_bias, offset, tq, shift):
    r = jnp.arange(2 * tq)
    x = jnp.where(r < tq, r, r - 2 * tq)
    t = (rel_bias.astype(f32)[_rel_bucket(offset - x)].T - shift[:, None]) * LOG2E
    return t[:, None, :]


def _lam_value(lq1, lk1, lq2, lk2, lam_init):
    s1 = jnp.sum(lq1[...] * lk1[...], axis=-1, keepdims=True)
    s2 = jnp.sum(lq2[...] * lk2[...], axis=-1, keepdims=True)
    return jnp.exp(s1) - jnp.exp(s2) + lam_init


def _split_maps(q):
    lane = lax.broadcasted_iota(jnp.int32, q.shape, 1)
    zero = jnp.zeros_like(q)
    return jnp.where(lane < HEAD_DIM, q, zero), jnp.where(lane >= HEAD_DIM, q, zero)


def _qk(qz, kt):
    return lax.dot_general(qz, kt, (((1,), (1,)), ((), ())), preferred_element_type=f32)


def _finish_heads(o1, o2, lam, gs, lam_init):
    att = o1 - lam * o2
    return _rms(att, gs) * (1.0 - lam_init)


def _attn_prompt_kernel(q_ref, k_ref, vt_ref, ring_ref, lq1, lk1, lq2, lk2, gs_ref, o_ref,
                        nb_sc, s_sc, p_sc, al_sc, tm_sc, m_sc, acc_sc, *, tq, tk, lam_init):
    i = pl.program_id(1)
    ring_len = ring_ref.shape[-1]

    def bias_tile(ring):
        return pltpu.roll(jnp.broadcast_to(ring, (tk, ring_len)), 0, 1, stride=1, stride_axis=0)[:, :tq]

    @pl.when(i == 0)
    def _():
        kk = lax.broadcasted_iota(jnp.int32, (tk, tq), 0)
        qq = lax.broadcasted_iota(jnp.int32, (tk, tq), 1)
        nb_sc[0] = jnp.where((kk // CHUNK) <= (qq // CHUNK), bias_tile(ring_ref[1]), NEG_INF)
        nb_sc[1] = jnp.where(((kk + tk) // CHUNK) <= (qq // CHUNK), bias_tile(ring_ref[2]), NEG_INF)
        nb_sc[2] = jnp.full((tk, tq), NEG_INF, f32)

    @pl.when(i == 1)
    def _():
        nb_sc[2] = nb_sc[1]
        nb_sc[1] = nb_sc[0]
        nb_sc[0] = bias_tile(ring_ref[0])

    qt = q_ref[...].astype(f32).T
    row = lax.broadcasted_iota(jnp.int32, qt.shape, 0)
    qz = (jnp.where(row < HEAD_DIM, qt, 0.0).astype(bf16), jnp.where(row >= HEAD_DIM, qt, 0.0).astype(bf16))

    m_sc[...] = jnp.full(m_sc.shape, NEG_INF, f32)
    acc_sc[...] = jnp.zeros(acc_sc.shape, f32)

    def qk(slot, tile, bias):
        kt = k_ref[pl.ds(pl.multiple_of(tile * tk, tk), tk), :]
        for mi in range(2):
            s = jnp.dot(kt, qz[mi], preferred_element_type=f32)
            s = s if bias is None else s + bias
            s_sc[mi, slot] = s
            tm_sc[mi, slot] = jnp.max(s, axis=0, keepdims=True)

    def softmax(slot):
        for mi in range(2):
            s = s_sc[mi, slot]
            m_prev = m_sc[mi]
            m_new = jnp.maximum(m_prev, tm_sc[mi, slot])
            al_sc[mi, slot] = jnp.exp2(m_prev - m_new)
            p_sc[mi, slot] = jnp.exp2(s - m_new).astype(bf16)
            m_sc[mi] = m_new

    def pv(slot, vtile):
        va = vt_ref[vtile]
        for mi in range(2):
            acc_sc[mi] = acc_sc[mi] * al_sc[mi, slot] + jnp.dot(va, p_sc[mi, slot], preferred_element_type=f32)

    per = tq // tk
    a0 = jnp.maximum(per * i - 1, 0)
    n_far = jnp.maximum(per * i - 1, 0)
    last_tile = vt_ref.shape[0] - 1

    def far_k(f):
        return jnp.clip(f, 0, jnp.maximum(n_far - 1, 0))

    qk(0, a0, nb_sc[0])
    qk(1, a0 + 1, nb_sc[1])
    softmax(0)
    qk(0, jnp.minimum(a0 + 2, last_tile), nb_sc[2])
    softmax(1)
    pv(0, a0)

    @pl.when(i == 0)
    def _():
        pv(1, a0 + 1)

    @pl.when(i > 0)
    def _():
        qk(1, 0, None)
        softmax(0)
        pv(1, a0 + 1)
        qk(0, far_k(1), None)
        softmax(1)
        pv(0, a0 + 2)

        def far_pair(p, carry):
            qk(1, 2 * p + 2, None)
            softmax(0)
            pv(1, 2 * p)
            qk(0, far_k(2 * p + 3), None)
            softmax(1)
            pv(0, 2 * p + 1)
            return carry

        lax.fori_loop(0, (n_far - 1) // 2, far_pair, 0)
        pv(1, n_far - 1)

    a1 = acc_sc[0]
    a2 = acc_sc[1]
    o1 = a1[:V_DIM] / a1[V_DIM:V_DIM + 1]
    o2 = a2[:V_DIM] / a2[V_DIM:V_DIM + 1]
    lam = _lam_value(lq1, lk1, lq2, lk2, lam_init)
    att = (o1 - lam * o2).T
    o_ref[...] = (_rms(att, gs_ref[...]) * (1.0 - lam_init)).astype(o_ref.dtype)


def _attn_prompt(q, kb, vt, rel_bias, lam_vecs, g_subln, lam_init, tq, tk):
    t = q.shape[0]
    nq = t // tq
    assert tq == 2 * tk and t % tq == 0
    far = rel_bias.astype(f32)[_rel_bucket(jnp.asarray(-4 * MAX_DISTANCE))]
    rings = jnp.stack([_bias_ring(rel_bias, d, tq, far) for d in (-tk, 0, tk)], axis=1)
    vec = _const_spec((1, HEAD_DIM))
    return pl.pallas_call(
        functools.partial(_attn_prompt_kernel, tq=tq, tk=tk, lam_init=lam_init),
        grid=(N_HEADS, nq),
        in_specs=[
            pl.BlockSpec((tq, 2 * HEAD_DIM), lambda h, i: (i, h)),
            pl.BlockSpec((t, 2 * HEAD_DIM), lambda h, i: (0, h)),
            pl.BlockSpec((None, t // tk, V_DIM + ONES_ROWS, tk), lambda h, i: (h, 0, 0, 0)),
            pl.BlockSpec((None, 3, 1, 2 * tq), lambda h, i: (h, 0, 0, 0)),
            vec, vec, vec, vec, _const_spec((1, V_DIM)),
        ],
        out_specs=pl.BlockSpec((tq, V_DIM), lambda h, i: (i, h)),
        out_shape=jax.ShapeDtypeStruct((t, V_COLS), bf16),
        scratch_shapes=[pltpu.VMEM((3, tk, tq), f32), pltpu.VMEM((2, 2, tk, tq), f32),
                        pltpu.VMEM((2, 2, tk, tq), bf16), pltpu.VMEM((2, 2, 1, tq), f32),
                        pltpu.VMEM((2, 2, 1, tq), f32), pltpu.VMEM((2, 1, tq), f32),
                        pltpu.VMEM((2, V_DIM + ONES_ROWS, tq), f32)],
        compiler_params=_cparams(("arbitrary", "arbitrary")),
        name="attn_prompt",
    )(q, kb, vt, rings, *lam_vecs, g_subln)


def _attn_sample_kernel(q_ref, kp_ref, vp_ref, kn_ref, vn_ref, nbp_ref, nbn_ref, lq1, lk1, lq2, lk2, gs_ref,
                        o_ref, *, lam_init):
    lam = _lam_value(lq1, lk1, lq2, lk2, lam_init)
    for hd in range(N_HEADS):
        qk_cols = slice(hd * 2 * HEAD_DIM, (hd + 1) * 2 * HEAD_DIM)
        v_cols = slice(hd * V_DIM, (hd + 1) * V_DIM)
        qz = _split_maps(q_ref[:, qk_cols])
        kp = kp_ref[:, qk_cols].astype(bf16)
        vp = vp_ref[:, v_cols].astype(bf16)
        kn = kn_ref[:, qk_cols]
        vn = vn_ref[:, v_cols]
        outs = []
        for mi in range(2):
            sp = _qk(qz[mi], kp) + nbp_ref[hd]
            sn = _qk(qz[mi], kn) + nbn_ref[hd]
            m = jnp.maximum(jnp.max(sp, axis=1, keepdims=True), jnp.max(sn, axis=1, keepdims=True))
            pp = jnp.exp2(sp - m).astype(bf16)
            pn = jnp.exp2(sn - m).astype(bf16)
            den = (jnp.sum(pp.astype(f32), axis=1, keepdims=True)
                   + jnp.sum(pn.astype(f32), axis=1, keepdims=True))
            num = jnp.dot(pp, vp, preferred_element_type=f32) + jnp.dot(pn, vn, preferred_element_type=f32)
            outs.append(num / den)
        o_ref[:, v_cols] = _finish_heads(outs[0], outs[1], lam, gs_ref[...], lam_init).astype(o_ref.dtype)


def _attn_sample(q, kb, vb, cache_k, cache_v, rel_bias, lam_vecs, g_subln, lam_init, nb_, ts):
    past = cache_k.shape[1]
    zero = jnp.zeros((N_HEADS,), f32)
    nbp = _bias_table(rel_bias, past, ts, 0, past, zero)
    nbn = _bias_table(rel_bias, past, ts, past, ts, zero)
    vec = _const_spec((1, HEAD_DIM))
    return pl.pallas_call(
        functools.partial(_attn_sample_kernel, lam_init=lam_init),
        grid=(nb_,),
        in_specs=[
            pl.BlockSpec((ts, QK_COLS), lambda b: (b, 0)),
            pl.BlockSpec((None, past, QK_COLS), lambda b: (b, 0, 0)),
            pl.BlockSpec((None, past, V_COLS), lambda b: (b, 0, 0)),
            pl.BlockSpec((ts, QK_COLS), lambda b: (b, 0)),
            pl.BlockSpec((ts, V_COLS), lambda b: (b, 0)),
            _const_spec((N_HEADS, ts, past)), _const_spec((N_HEADS, ts, ts)),
            vec, vec, vec, vec, _const_spec((1, V_DIM)),
        ],
        out_specs=pl.BlockSpec((ts, V_COLS), lambda b: (b, 0)),
        out_shape=jax.ShapeDtypeStruct((nb_ * ts, V_COLS), bf16),
        compiler_params=_cparams(("parallel",)),
        name="attn_sample",
    )(q, cache_k, cache_v, kb, vb, nbp, nbn, *lam_vecs, g_subln)


def _conv_lru_kernel(lx_ref, gl_ref, c0_ref, h0_ref, wc_ref, bc_ref, wa_ref, ba_ref, wx_ref, bx_ref, lam_ref,
                     hsg_ref, cout_ref, hout_ref, xbuf, a_sc, b_sc, hcar, *, tt, first_pos_is_zero):
    t = pl.program_id(1)

    @pl.when(t == 0)
    def _():
        xbuf[0:SUBLANES] = c0_ref[...]
        hcar[...] = jnp.broadcast_to(h0_ref[...], hcar.shape)

    xbuf[SUBLANES:SUBLANES + tt] = lx_ref[...]
    xc = jnp.broadcast_to(bc_ref[...], (tt, LRU_WIDTH))
    for j in range(CONV_W):
        lo = SUBLANES - (CONV_W - 1) + j
        xc = xc + xbuf[lo:lo + tt, :] * wc_ref[j:j + 1, :]
    tail = xbuf[tt:tt + SUBLANES, :]
    xbuf[0:SUBLANES] = tail
    cout_ref[...] = tail

    xcb = xc.astype(bf16)

    def block_diag(w_ref, b_ref):
        cols = [jnp.dot(xcb[:, n * LRU_BLOCK_W:(n + 1) * LRU_BLOCK_W], w_ref[n], preferred_element_type=f32)
                for n in range(LRU_BLOCKS)]
        return jnp.concatenate(cols, axis=1) + b_ref[...]

    r = _sigmoid(block_diag(wa_ref, ba_ref))
    ig = _sigmoid(block_diag(wx_ref, bx_ref))
    z = -lam_ref[...]
    softplus = jnp.maximum(z, 0.0) + jnp.log1p(jnp.exp(-jnp.abs(z)))
    log_a = -LRU_C * r * softplus
    a = jnp.exp(log_a)
    th = jnp.tanh(log_a)
    m2 = -2.0 * th / (1.0 - th)
    mult = jnp.where(m2 > 0.0, m2 * lax.rsqrt(m2), 0.0)
    if first_pos_is_zero:
        row = lax.broadcasted_iota(jnp.int32, (tt, LRU_WIDTH), 0)
        mult = jnp.where((row == 0) & (t == 0), 1.0, mult)
    bv = mult * ig * xc

    groups = (tt // SUBLANES, SUBLANES, LRU_WIDTH)
    a = a.reshape(groups)
    bv = bv.reshape(groups)
    sub = lax.broadcasted_iota(jnp.int32, groups, 1)
    d = 1
    while d < SUBLANES:
        valid = sub >= d
        a_s = pltpu.roll(a, d, axis=1)
        b_s = pltpu.roll(bv, d, axis=1)
        bv = jnp.where(valid, a * b_s + bv, bv)
        a = jnp.where(valid, a * a_s, a)
        d *= 2
    a_sc[...] = a.reshape(tt, LRU_WIDTH)
    b_sc[...] = bv.reshape(tt, LRU_WIDTH)

    def group(g, h):
        sl = pl.ds(pl.multiple_of(g * SUBLANES, SUBLANES), SUBLANES)
        hg = a_sc[sl, :] * h + b_sc[sl, :]
        b_sc[sl, :] = hg
        return jnp.broadcast_to(hg[SUBLANES - 1:SUBLANES, :], hg.shape)

    h = lax.fori_loop(0, tt // SUBLANES, group, hcar[...], unroll=4)
    hcar[...] = h
    hout_ref[...] = h[0:1, :]
    hsg_ref[...] = (b_sc[...] * gl_ref[...]).astype(hsg_ref.dtype)


def _conv_lru(lx, gl, conv0, h0, w_conv, b_conv, wa, ba, wx, bx, lru_lambda, nb_, tlen, tt, first_pos_is_zero):
    nt = tlen // tt
    rows = pl.BlockSpec((tt, LRU_WIDTH), lambda b, t: (b * nt + t, 0))
    vecw = _const_spec((1, LRU_WIDTH))
    wblk = _const_spec((LRU_BLOCKS, LRU_BLOCK_W, LRU_BLOCK_W))
    return pl.pallas_call(
        functools.partial(_conv_lru_kernel, tt=tt, first_pos_is_zero=first_pos_is_zero),
        grid=(nb_, nt),
        in_specs=[rows, rows,
                  pl.BlockSpec((None, SUBLANES, LRU_WIDTH), lambda b, t: (b, 0, 0)),
                  pl.BlockSpec((None, 1, LRU_WIDTH), lambda b, t: (b, 0, 0)),
                  _const_spec((CONV_W, LRU_WIDTH)), vecw, wblk, vecw, wblk, vecw, vecw],
        out_specs=[rows,
                   pl.BlockSpec((None, SUBLANES, LRU_WIDTH), lambda b, t: (b, 0, 0)),
                   pl.BlockSpec((None, 1, LRU_WIDTH), lambda b, t: (b, 0, 0))],
        out_shape=[jax.ShapeDtypeStruct((nb_ * tlen, LRU_WIDTH), bf16),
                   jax.ShapeDtypeStruct((nb_, SUBLANES, LRU_WIDTH), f32),
                   jax.ShapeDtypeStruct((nb_, 1, LRU_WIDTH), f32)],
        scratch_shapes=[pltpu.VMEM((tt + SUBLANES, LRU_WIDTH), f32), pltpu.VMEM((tt, LRU_WIDTH), f32),
                        pltpu.VMEM((tt, LRU_WIDTH), f32), pltpu.VMEM((SUBLANES, LRU_WIDTH), f32)],
        compiler_params=_cparams(("arbitrary", "arbitrary")),
        name="conv_lru",
    )(lx, gl, conv0, h0, w_conv, b_conv, wa, ba, wx, bx, lru_lambda)


def _merge_kernel(x_ref, att_ref, hsg_ref, gt_ref, wa_ref, wb_ref, wo_ref, gf_ref, wr_ref, br_ref,
                  x1_ref, h2_ref, comb_ref, *, parts):
    rows_per = x_ref.shape[0] // parts
    for part in range(parts):
        rows = slice(part * rows_per, (part + 1) * rows_per)
        bra = jnp.dot(att_ref[rows, :], wa_ref[...], preferred_element_type=f32)
        brb = jnp.dot(hsg_ref[rows, :], wb_ref[...], preferred_element_type=f32)
        gt = gt_ref[rows, :]
        m = (gt[:, :D_MODEL] * bra + gt[:, D_MODEL:] * brb).astype(bf16)
        x1 = x_ref[rows, :] + jnp.dot(m, wo_ref[...], preferred_element_type=f32)
        x1_ref[rows, :] = x1
        h2 = _rms(x1, gf_ref[...])
        h2_hi = h2.astype(bf16)
        h2_ref[rows, :] = h2_hi

        h2_lo = (h2 - h2_hi.astype(f32)).astype(bf16)
        prods = (jnp.dot(h2_hi, wr_ref[...], preferred_element_type=f32)
                 + jnp.dot(h2_lo, wr_ref[...], preferred_element_type=f32))
        logits = prods[:, :LANES] + prods[:, LANES:] + br_ref[...]
        lane = lax.broadcasted_iota(jnp.int32, logits.shape, 1)
        lanef = lane.astype(f32)
        low = jnp.float32(-3.0e38)
        is_grp = (lane >= N_EXPERTS) & (lane < N_EXPERTS + N_GROUPS)
        gl = jnp.where(is_grp, logits, low)
        gmax = jnp.max(gl, axis=1, keepdims=True)
        gidx = jnp.min(jnp.where(is_grp & (gl == gmax), lanef, 1.0e3), axis=1, keepdims=True) - N_EXPERTS
        gden = jnp.sum(jnp.where(is_grp, jnp.exp(gl - gmax), 0.0), axis=1, keepdims=True)
        g_w = 1.0 / gden
        lo = gidx * EXPERTS_PER_GROUP
        in_sel = (lanef >= lo) & (lanef < lo + EXPERTS_PER_GROUP)
        sel = jnp.where(in_sel, logits, low)
        v1 = jnp.max(sel, axis=1, keepdims=True)
        i1 = jnp.min(jnp.where(in_sel & (sel == v1), lanef, 1.0e3), axis=1, keepdims=True)
        in_sel2 = in_sel & (lanef != i1)
        sel2 = jnp.where(in_sel2, logits, low)
        v2 = jnp.max(sel2, axis=1, keepdims=True)
        i2 = jnp.min(jnp.where(in_sel2 & (sel2 == v2), lanef, 1.0e3), axis=1, keepdims=True)
        e2 = jnp.exp(v2 - v1)
        w1 = g_w / (1.0 + e2)
        w2 = g_w * e2 / (1.0 + e2)
        comb_ref[rows, :] = jnp.where(lanef == i1, w1, 0.0) + jnp.where(lanef == i2, w2, 0.0)


def _merge(x, att, hsg, gt, wa, wb, wo, g_ffn, w_router, b_router, tm):
    n = x.shape[0]
    row = lambda w: pl.BlockSpec((tm, w), lambda i: (i, 0))
    sq = _const_spec((D_MODEL, D_MODEL))
    return pl.pallas_call(
        functools.partial(_merge_kernel, parts=2 if tm % 512 == 0 else 1),
        grid=(n // tm,),
        in_specs=[row(D_MODEL), row(V_COLS), row(LRU_WIDTH), row(2 * D_MODEL), sq, sq, sq,
                  _const_spec((1, D_MODEL)), _const_spec((D_MODEL, 2 * LANES)), _const_spec((1, LANES))],
        out_specs=[row(D_MODEL), row(D_MODEL), row(LANES)],
        out_shape=[jax.ShapeDtypeStruct((n, D_MODEL), f32), jax.ShapeDtypeStruct((n, D_MODEL), bf16),
                   jax.ShapeDtypeStruct((n, LANES), f32)],
        compiler_params=_cparams(("parallel",)),
        name="merge_router",
    )(x, att, hsg, gt, wa, wb, wo, g_ffn, w_router, b_router)


EXPERTS_PER_STEP = 8


MOE_CHUNK = 128


def _moe_kernel(h_ref, comb_ref, wg_ref, wu_ref, wd_ref, y_ref, pt_sc, x_sc, c_sc, ys_sc, seg_sc):
    e = pl.program_id(1)
    tm = h_ref.shape[0]
    lane1 = lax.broadcasted_iota(jnp.int32, (1, LANES), 1)

    @pl.when(e == 0)
    def _():
        comb = comb_ref[...]
        ex = lax.broadcasted_iota(jnp.int32, (LANES, LANES), 0)
        gr = lax.broadcasted_iota(jnp.int32, (LANES, LANES), 1)
        in_group = jnp.where((ex // EXPERTS_PER_GROUP == gr) & (ex < N_EXPERTS), 1.0, 0.0).astype(bf16)
        gsum = jnp.dot(comb.astype(bf16), in_group, preferred_element_type=f32)
        member = jnp.where(gsum > 0.0, 1.0, 0.0)
        r_i = lax.broadcasted_iota(jnp.int32, (tm, tm), 0)
        c_i = lax.broadcasted_iota(jnp.int32, (tm, tm), 1)
        tri = jnp.where(gr <= ex, 1.0, 0.0).astype(bf16)
        member_b = member.astype(bf16)
        totals = jnp.zeros((1, LANES), f32)
        blocks = []
        for blk in range(tm // LANES):
            part = jnp.dot(tri, member_b[blk * LANES:(blk + 1) * LANES], preferred_element_type=f32) + totals
            blocks.append(part)
            totals = part[LANES - 1:LANES, :]
        rank = jnp.concatenate(blocks, axis=0)
        offs = jnp.zeros((1, LANES), f32)
        for g in range(1, N_GROUPS):
            prev = jnp.sum(jnp.where(lane1 == g - 1, totals, 0.0), axis=1, keepdims=True)
            offs = offs + jnp.where(lane1 >= g, prev, 0.0)
        seg_sc[0:1, :] = offs.astype(jnp.int32)
        seg_sc[1:2, :] = totals.astype(jnp.int32)
        pos = jnp.sum(member * (offs + rank - 1.0), axis=1, keepdims=True)
        pos_row = jnp.broadcast_to(pos, (tm, LANES)).T[0:1, :].astype(jnp.int32)
        perm = jnp.where(r_i == pos_row, 1.0, 0.0).astype(bf16)
        pt_sc[...] = jnp.where(c_i == pos.astype(jnp.int32), 1.0, 0.0).astype(bf16)
        comb_hi = comb.astype(bf16)
        comb_lo = (comb - comb_hi.astype(f32)).astype(bf16)
        srt = jnp.dot(perm, jnp.concatenate([h_ref[...], comb_hi, comb_lo], axis=1), preferred_element_type=f32)
        x_sc[...] = srt[:, :D_MODEL].astype(bf16)
        c_sc[...] = srt[:, D_MODEL:D_MODEL + LANES] + srt[:, D_MODEL + LANES:]
        ys_sc[...] = jnp.zeros(ys_sc.shape, f32)

    g = (e * EXPERTS_PER_STEP) // EXPERTS_PER_GROUP
    off = jnp.sum(jnp.where(lane1 == g, seg_sc[0:1, :], 0))
    cnt = jnp.sum(jnp.where(lane1 == g, seg_sc[1:2, :], 0))
    lane = lax.broadcasted_iota(jnp.int32, (MOE_CHUNK, LANES), 1)

    def chunk(k, carry):
        rows = pl.ds(pl.multiple_of(k * MOE_CHUNK, MOE_CHUNK), MOE_CHUNK)
        x = x_sc[rows, :]
        cw = c_sc[rows, :]
        hidden = []
        for j in range(EXPERTS_PER_STEP):
            gate = jnp.dot(x, wg_ref[j], preferred_element_type=f32)
            up = jnp.dot(x, wu_ref[j], preferred_element_type=f32)
            c = jnp.sum(jnp.where(lane == e * EXPERTS_PER_STEP + j, cw, 0.0), axis=1, keepdims=True)
            hidden.append((gate * _sigmoid(gate) * up * c).astype(bf16))
        ys_sc[rows, :] += jnp.dot(jnp.concatenate(hidden, axis=1), wd_ref[...], preferred_element_type=f32)
        return carry

    lax.fori_loop(off // MOE_CHUNK, (off + cnt + MOE_CHUNK - 1) // MOE_CHUNK, chunk, 0)

    @pl.when(e == pl.num_programs(1) - 1)
    def _():
        y = jnp.dot(pt_sc[...], ys_sc[...].astype(bf16), preferred_element_type=f32)
        y_ref[...] = y.astype(y_ref.dtype)


def _moe(h2, comb, wg, wu, wd_rows, tm):
    n = h2.shape[0]
    per = EXPERTS_PER_STEP
    assert EXPERTS_PER_GROUP % per == 0 and tm % MOE_CHUNK == 0
    return pl.pallas_call(
        _moe_kernel,
        grid=(n // tm, N_EXPERTS // per),
        in_specs=[pl.BlockSpec((tm, D_MODEL), lambda i, e: (i, 0)),
                  pl.BlockSpec((tm, LANES), lambda i, e: (i, 0)),
                  pl.BlockSpec((per, D_MODEL, D_EXPERT), lambda i, e: (e, 0, 0)),
                  pl.BlockSpec((per, D_MODEL, D_EXPERT), lambda i, e: (e, 0, 0)),
                  pl.BlockSpec((per * D_EXPERT, D_MODEL), lambda i, e: (e, 0))],
        out_specs=pl.BlockSpec((tm, D_MODEL), lambda i, e: (i, 0)),
        out_shape=jax.ShapeDtypeStruct((n, D_MODEL), bf16),
        scratch_shapes=[pltpu.VMEM((tm, tm), bf16), pltpu.VMEM((tm, D_MODEL), bf16),
                        pltpu.VMEM((tm, LANES), f32), pltpu.VMEM((tm, D_MODEL), f32),
                        pltpu.VMEM((SUBLANES, LANES), jnp.int32)],
        compiler_params=_cparams(("arbitrary", "arbitrary")),
        name="moe",
    )(h2, comb, wg, wu, wd_rows)


def _ple_kernel(x1_ref, y_ref, p_ref, gp_ref, wpg_ref, bpg_ref, wpi_ref, gfin_ref, o_ref):
    x2 = x1_ref[...] + y_ref[...]
    hp = _rms(x2, gp_ref[...]).astype(bf16)
    pg = _sigmoid(jnp.dot(hp, wpg_ref[...], preferred_element_type=f32) + bpg_ref[...])
    pin = jnp.dot(p_ref[...].astype(bf16), wpi_ref[...], preferred_element_type=f32)
    x3 = x2 + pg * pin
    o_ref[...] = _rms(x3, gfin_ref[...])


def _ple(x1, y, p, g_ple, wpg, bpg, wpi, g_final, tm):
    n = x1.shape[0]
    row = lambda w: pl.BlockSpec((tm, w), lambda i: (i, 0))
    vec = _const_spec((1, D_MODEL))
    return pl.pallas_call(
        _ple_kernel,
        grid=(n // tm,),
        in_specs=[row(D_MODEL), row(D_MODEL), row(PLE_DIM), vec, _const_spec((D_MODEL, D_MODEL)), vec,
                  _const_spec((PLE_DIM, D_MODEL)), vec],
        out_specs=row(D_MODEL),
        out_shape=jax.ShapeDtypeStruct((n, D_MODEL), f32),
        compiler_params=_cparams(("parallel",)),
        name="ple_final",
    )(x1, y, p, g_ple, wpg, bpg, wpi, g_final)


PROJ_ROWS = 256
ATTN_KEY_TILE = 512
MERGE_ROWS = 512
MOE_ROWS = 1024


def _tile(n, pref):
    return pref if n % pref == 0 else n


def _stream(x, p, conv0, h0, cache, li, rel_bias, w):
    nb_, tlen, _ = x.shape
    n = nb_ * tlen
    xf = x.reshape(n, D_MODEL)
    lam_init = 0.8 - 0.6 * math.exp(-0.3 * li)
    tk = None if cache is not None else ATTN_KEY_TILE
    q, k, kb, v, vb, lx, gl, gt = _in_proj(xf, w["g_mix"], w["w_in"], w["b_merge"], _tile(n, PROJ_ROWS), tk)
    if cache is None:
        att = _attn_prompt(q, kb, vb, rel_bias, w["lam_vecs"], w["g_subln"], lam_init, 2 * tk, tk)
    else:
        att = _attn_sample(q, kb, vb, cache[0], cache[1], rel_bias, w["lam_vecs"], w["g_subln"], lam_init,
                           nb_, tlen)
    conv_pad = jnp.pad(conv0, ((0, 0), (SUBLANES - (CONV_W - 1), 0), (0, 0)))
    hsg, cout, hout = _conv_lru(lx, gl, conv_pad, h0[:, None, :], w["w_conv"], w["b_conv"], w["w_rg_a"],
                                w["b_rg_a"], w["w_rg_x"], w["b_rg_x"], w["lru_lambda"], nb_, tlen,
                                _tile(tlen, PROJ_ROWS), cache is None)
    tm = _tile(n, MERGE_ROWS)
    x1, h2, comb = _merge(xf, att, hsg, gt, w["w_attn_br"], w["w_lru_br"], w["w_out"], w["g_ffn"],
                          w["w_router"], w["b_router"], tm)
    y = _moe(h2, comb, w["w_e_gate"], w["w_e_up"], w["w_e_down"], _tile(n, MOE_ROWS))
    out = _ple(x1, y, p.reshape(n, PLE_DIM), w["g_ple"], w["w_ple_gate"], w["b_ple_gate"], w["w_ple_in"],
               w["g_final"], tm)
    return out, k, v, cout[:, SUBLANES - (CONV_W - 1):, :], hout[:, 0, :]


def kernel(x_prompt, x_sample, cache_k, cache_v, state_conv, state_lru, p_prompt, p_sample, rel_bias, g_mix, w_in, b_merge, lam_q1, lam_k1, lam_q2, lam_k2, g_subln, w_attn_br, w_conv, b_conv, w_rg_a, b_rg_a, w_rg_x, b_rg_x, lru_lambda, w_lru_br, w_out, g_ffn, w_grp, b_grp, w_rt, b_rt, w_e_gate, w_e_up, w_e_down, g_ple, w_ple_gate, b_ple_gate, w_ple_in, g_final):
    depth = w_in.shape[0]
    assert depth == 1, "the final norm is fused into the single layer's last stage"
    bp, tp, _ = x_prompt.shape
    bs, ts, _ = x_sample.shape
    past = cache_k.shape[2]
    li = 0
    row = lambda a: a[li].reshape(1, -1).astype(f32)
    w_router = jnp.concatenate(
        [jnp.transpose(w_rt[li], (1, 0, 2)).reshape(D_MODEL, N_EXPERTS), w_grp[li]], axis=1).astype(f32)
    w_router = jnp.pad(w_router, ((0, 0), (0, LANES - N_EXPERTS - N_GROUPS)))
    w_router_hi = w_router.astype(bf16)
    w_router = jnp.concatenate([w_router_hi, (w_router - w_router_hi.astype(f32)).astype(bf16)], axis=1)
    b_router = jnp.pad(jnp.concatenate([b_rt[li].reshape(-1), b_grp[li]]).astype(f32),
                       (0, LANES - N_EXPERTS - N_GROUPS)).reshape(1, LANES)
    w = dict(
        g_mix=row(g_mix), w_in=w_in[li].astype(bf16), b_merge=row(b_merge),
        lam_vecs=(row(lam_q1), row(lam_k1), row(lam_q2), row(lam_k2)), g_subln=row(g_subln),
        w_attn_br=w_attn_br[li].astype(bf16), w_conv=w_conv[li].astype(f32), b_conv=row(b_conv),
        w_rg_a=w_rg_a[li].astype(bf16), b_rg_a=row(b_rg_a), w_rg_x=w_rg_x[li].astype(bf16), b_rg_x=row(b_rg_x),
        lru_lambda=row(lru_lambda), w_lru_br=w_lru_br[li].astype(bf16), w_out=w_out[li].astype(bf16),
        g_ffn=row(g_ffn), w_router=w_router, b_router=b_router,
        w_e_gate=w_e_gate[li].astype(bf16), w_e_up=w_e_up[li].astype(bf16), w_e_down=w_e_down[li].reshape(N_EXPERTS * D_EXPERT, D_MODEL).astype(bf16),
        g_ple=row(g_ple), w_ple_gate=w_ple_gate[li].astype(bf16), b_ple_gate=row(b_ple_gate),
        w_ple_in=w_ple_in[li].astype(bf16), g_final=g_final.reshape(1, -1).astype(f32),
    )
    conv0 = jnp.zeros((bp, CONV_W - 1, LRU_WIDTH), f32)
    h0 = jnp.zeros((bp, LRU_WIDTH), f32)
    yp, kp, vp, cp, hp = _stream(x_prompt, p_prompt[li], conv0, h0, None, li, rel_bias, w)
    cache = (cache_k[li].reshape(bs, past, QK_COLS), cache_v[li].reshape(bs, past, V_COLS))
    ys, ks, vs, cs, hs = _stream(x_sample, p_sample[li], state_conv[li], state_lru[li], cache, li, rel_bias, w)
    return (yp.reshape(bp, tp, D_MODEL), ys.reshape(bs, ts, D_MODEL),
            kp.reshape(1, bp, tp, N_HEADS, 2, HEAD_DIM), vp.reshape(1, bp, tp, N_HEADS, V_DIM),
            cp[None], hp[None],
            ks.reshape(1, bs, ts, N_HEADS, 2, HEAD_DIM), vs.reshape(1, bs, ts, N_HEADS, V_DIM),
            cs[None], hs[None])
```

```python
import functools
import math

import jax
import jax.numpy as jnp
from jax import lax
from jax.experimental import pallas as pl
from jax.experimental.pallas import tpu as pltpu

f32 = jnp.float32
bf16 = jnp.bfloat16

D_MODEL = 1024
N_HEADS = 8
HEAD_DIM = 64
V_DIM = 128
QK_COLS = N_HEADS * 2 * HEAD_DIM
V_COLS = N_HEADS * V_DIM
LRU_WIDTH = 1024
LRU_BLOCKS = 8
LRU_BLOCK_W = LRU_WIDTH // LRU_BLOCKS
CONV_W = 4
LRU_C = 8.0
CHUNK = 64
N_BUCKETS = 32
MAX_DISTANCE = 128
N_GROUPS = 4
EXPERTS_PER_GROUP = 8
N_EXPERTS = N_GROUPS * EXPERTS_PER_GROUP
D_EXPERT = 256
PLE_DIM = 256
EPS = 1e-6
NEG_INF = -1e30
LOG2E = math.log2(math.e)
ONES_ROWS = 16
LANES = 128
SUBLANES = 8
VMEM_LIMIT = 56 * 1024 * 1024

C_Q, C_K, C_V = 0, QK_COLS, 2 * QK_COLS
C_LX = 2 * QK_COLS + V_COLS
C_LY = C_LX + LRU_WIDTH
C_G = C_LY + LRU_WIDTH
IN_COLS = C_G + 2 * D_MODEL


def _cparams(sem):
    return pltpu.CompilerParams(dimension_semantics=sem, vmem_limit_bytes=VMEM_LIMIT)


def _const_spec(shape):
    nd = len(shape)
    return pl.BlockSpec(shape, lambda *_: (0,) * nd, pipeline_mode=pl.Buffered(1))


def _rms(x, g):
    return x * lax.rsqrt(jnp.mean(x * x, axis=-1, keepdims=True) + EPS) * g


def _sigmoid(x):
    return 1.0 / (1.0 + jnp.exp(-x))


def _gelu_tanh(x):
    c = math.sqrt(2.0 / math.pi)
    return 0.5 * x * (1.0 + jnp.tanh(c * (x + 0.044715 * (x * x * x))))


def _in_proj_kernel(x_ref, g_ref, w_ref, bm_ref, q_ref, k_ref, kb_ref, v_ref, vb_ref, lx_ref, gl_ref, gt_ref,
                    *, transposed_v):
    h = _rms(x_ref[...], g_ref[...]).astype(bf16)

    def proj(lo, width):
        return jnp.dot(h, w_ref[:, lo:lo + width], preferred_element_type=f32)

    q_ref[...] = (proj(C_Q, QK_COLS) * (HEAD_DIM ** -0.5 * LOG2E)).astype(bf16)
    k = proj(C_K, QK_COLS)
    k_ref[...] = k
    kb_ref[...] = k.astype(bf16)
    v = proj(C_V, V_COLS)
    v_ref[...] = v
    if transposed_v:
        for hd in range(N_HEADS):
            vb_ref[hd, :V_DIM, :] = v[:, hd * V_DIM:(hd + 1) * V_DIM].T.astype(bf16)
            vb_ref[hd, V_DIM:, :] = jnp.ones((ONES_ROWS, v.shape[0]), bf16)
    else:
        vb_ref[...] = v.astype(bf16)
    lx_ref[...] = proj(C_LX, LRU_WIDTH)
    gl_ref[...] = _gelu_tanh(proj(C_LY, LRU_WIDTH))
    gt_ref[...] = _sigmoid(proj(C_G, 2 * D_MODEL) + bm_ref[...])


def _in_proj(x, g_mix, w_in_b, b_merge, tm, tq=None):
    n = x.shape[0]
    row = lambda w: pl.BlockSpec((tm, w), lambda i: (i, 0))
    if tq is None:
        vb_shape, vb_spec = jax.ShapeDtypeStruct((n, V_COLS), bf16), row(V_COLS)
    else:
        per = tq // tm
        vb_shape = jax.ShapeDtypeStruct((N_HEADS, n // tq, V_DIM + ONES_ROWS, tq), bf16)
        vb_spec = pl.BlockSpec((N_HEADS, None, V_DIM + ONES_ROWS, tm), lambda i: (0, i // per, 0, i % per))
    outs = [
        jax.ShapeDtypeStruct((n, QK_COLS), bf16),
        jax.ShapeDtypeStruct((n, QK_COLS), f32),
        jax.ShapeDtypeStruct((n, QK_COLS), bf16),
        jax.ShapeDtypeStruct((n, V_COLS), f32),
        vb_shape,
        jax.ShapeDtypeStruct((n, LRU_WIDTH), f32),
        jax.ShapeDtypeStruct((n, LRU_WIDTH), f32),
        jax.ShapeDtypeStruct((n, 2 * D_MODEL), f32),
    ]
    return pl.pallas_call(
        functools.partial(_in_proj_kernel, transposed_v=tq is not None),
        grid=(n // tm,),
        in_specs=[row(D_MODEL), _const_spec((1, D_MODEL)), _const_spec((D_MODEL, IN_COLS)),
                  _const_spec((1, 2 * D_MODEL))],
        out_specs=[row(QK_COLS), row(QK_COLS), row(QK_COLS), row(V_COLS), vb_spec,
                   row(LRU_WIDTH), row(LRU_WIDTH), row(2 * D_MODEL)],
        out_shape=outs,
        compiler_params=_cparams(("parallel",)),
        name="in_proj",
    )(x, g_mix, w_in_b, b_merge)


def _rel_bucket(rel):
    half = N_BUCKETS // 2
    max_exact = half // 2
    ret = jnp.where(rel > 0, half, 0)
    n = jnp.abs(rel)
    nf = jnp.maximum(n, 1).astype(f32)
    large = max_exact + (jnp.log(nf / max_exact) / math.log(MAX_DISTANCE / max_exact)
                         * (half - max_exact)).astype(jnp.int32)
    large = jnp.minimum(large, half - 1)
    return ret + jnp.where(n < max_exact, n, large)


def _toeplitz(t, nq, nk):
    nh, length = t.shape
    a = jnp.broadcast_to(t[:, None, :], (nh, nq, length))
    a = jnp.pad(a, ((0, 0), (0, 0), (0, 1))).reshape(nh, nq * (length + 1))
    a = a[:, :nq * length].reshape(nh, nq, length)
    return a[:, :, nq - 1:nq - 1 + nk]


def _bias_table(rel_bias, qpos0, nq, kpos0, nk, shift):
    rel = jnp.arange(-(nq - 1), nk) + (kpos0 - qpos0)
    t = (rel_bias.astype(f32)[_rel_bucket(rel)].T - shift[:, None]) * LOG2E
    qpos = qpos0 + jnp.arange(nq)
    kpos = kpos0 + jnp.arange(nk)
    mask = (kpos[None, :] // CHUNK) <= (qpos[:, None] // CHUNK)
    return jnp.where(mask[None], _toeplitz(t, nq, nk), NEG_INF)


def _bias_ring(rel_bias, offset, tq, shift):
    r = jnp.arange(2 * tq)
    x = jnp.where(r < tq, r, r - 2 * tq)
    t = (rel_bias.astype(f32)[_rel_bucket(offset - x)].T - shift[:, None]) * LOG2E
    return t[:, None, :]


def _lam_value(lq1, lk1, lq2, lk2, lam_init):
    s1 = jnp.sum(lq1[...] * lk1[...], axis=-1, keepdims=True)
    s2 = jnp.sum(lq2[...] * lk2[...], axis=-1, keepdims=True)
    return jnp.exp(s1) - jnp.exp(s2) + lam_init


def _split_maps(q):
    lane = lax.broadcasted_iota(jnp.int32, q.shape, 1)
    zero = jnp.zeros_like(q)
    return jnp.where(lane < HEAD_DIM, q, zero), jnp.where(lane >= HEAD_DIM, q, zero)


def _qk(qz, kt):
    return lax.dot_general(qz, kt, (((1,), (1,)), ((), ())), preferred_element_type=f32)


def _finish_heads(o1, o2, lam, gs, lam_init):
    att = o1 - lam * o2
    return _rms(att, gs) * (1.0 - lam_init)


def _attn_prompt_kernel(q_ref, k_ref, vt_ref, ring_ref, lq1, lk1, lq2, lk2, gs_ref, o_ref,
                        nb_sc, s_sc, p_sc, al_sc, tm_sc, m_sc, acc_sc, *, tq, tk, lam_init):
    i = pl.program_id(1)
    ring_len = ring_ref.shape[-1]

    def bias_tile(ring):
        return pltpu.roll(jnp.broadcast_to(ring, (tk, ring_len)), 0, 1, stride=1, stride_axis=0)[:, :tq]

    @pl.when(i == 0)
    def _():
        kk = lax.broadcasted_iota(jnp.int32, (tk, tq), 0)
        qq = lax.broadcasted_iota(jnp.int32, (tk, tq), 1)
        nb_sc[0] = jnp.where((kk // CHUNK) <= (qq // CHUNK), bias_tile(ring_ref[1]), NEG_INF)
        nb_sc[1] = jnp.where(((kk + tk) // CHUNK) <= (qq // CHUNK), bias_tile(ring_ref[2]), NEG_INF)
        nb_sc[2] = jnp.full((tk, tq), NEG_INF, f32)

    @pl.when(i == 1)
    def _():
        nb_sc[2] = nb_sc[1]
        nb_sc[1] = nb_sc[0]
        nb_sc[0] = bias_tile(ring_ref[0])

    qt = q_ref[...].astype(f32).T
    row = lax.broadcasted_iota(jnp.int32, qt.shape, 0)
    qz = (jnp.where(row < HEAD_DIM, qt, 0.0).astype(bf16), jnp.where(row >= HEAD_DIM, qt, 0.0).astype(bf16))

    m_sc[...] = jnp.full(m_sc.shape, NEG_INF, f32)
    acc_sc[...] = jnp.zeros(acc_sc.shape, f32)

    def qk(slot, tile, bias, cols=slice(None)):
        kt = k_ref[pl.ds(pl.multiple_of(tile * tk, tk), tk), :]
        for mi in range(2):
            s = jnp.dot(kt, qz[mi][:, cols], preferred_element_type=f32)
            s = s if bias is None else s + bias[:, cols]
            s_sc[mi, slot, :, cols] = s
            tm_sc[mi, slot, :, cols] = jnp.max(s, axis=0, keepdims=True)

    def softmax(slot, cols=slice(None)):
        for mi in range(2):
            s = s_sc[mi, slot, :, cols]
            m_prev = m_sc[mi, :, cols]
            m_new = jnp.maximum(m_prev, tm_sc[mi, slot, :, cols])
            al_sc[mi, slot, :, cols] = jnp.exp2(m_prev - m_new)
            p_sc[mi, slot, :, cols] = jnp.exp2(s - m_new).astype(bf16)
            m_sc[mi, :, cols] = m_new

    def pv(slot, vtile, cols=slice(None)):
        va = vt_ref[vtile]
        for mi in range(2):
            acc_sc[mi, :, cols] = (acc_sc[mi, :, cols] * al_sc[mi, slot, :, cols]
                                   + jnp.dot(va, p_sc[mi, slot, :, cols], preferred_element_type=f32))

    late = slice(tq // 2, tq)

    per = tq // tk
    a0 = jnp.maximum(per * i - 1, 0)
    n_far = jnp.maximum(per * i - 1, 0)
    last_tile = vt_ref.shape[0] - 1

    def far_k(f):
        return jnp.clip(f, 0, jnp.maximum(n_far - 1, 0))

    qk(0, a0, nb_sc[0])
    qk(1, a0 + 1, nb_sc[1])
    softmax(0)
    qk(0, jnp.minimum(a0 + 2, last_tile), nb_sc[2], late)
    softmax(1)
    pv(0, a0)

    @pl.when(i == 0)
    def _():
        pv(1, a0 + 1)

    @pl.when(i > 0)
    def _():
        qk(1, 0, None)
        softmax(0, late)
        pv(1, a0 + 1)
        qk(0, far_k(1), None)
        softmax(1)
        pv(0, a0 + 2, late)

        def far_pair(p, carry):
            qk(1, 2 * p + 2, None)
            softmax(0)
            pv(1, 2 * p)
            qk(0, far_k(2 * p + 3), None)
            softmax(1)
            pv(0, 2 * p + 1)
            return carry

        lax.fori_loop(0, (n_far - 1) // 2, far_pair, 0)
        pv(1, n_far - 1)

    a1 = acc_sc[0]
    a2 = acc_sc[1]
    o1 = a1[:V_DIM] / a1[V_DIM:V_DIM + 1]
    o2 = a2[:V_DIM] / a2[V_DIM:V_DIM + 1]
    lam = _lam_value(lq1, lk1, lq2, lk2, lam_init)
    att = (o1 - lam * o2).T
    o_ref[...] = (_rms(att, gs_ref[...]) * (1.0 - lam_init)).astype(o_ref.dtype)


def _attn_prompt(q, kb, vt, rel_bias, lam_vecs, g_subln, lam_init, tq, tk):
    t = q.shape[0]
    nq = t // tq
    assert tq == 2 * tk and t % tq == 0
    far = rel_bias.astype(f32)[_rel_bucket(jnp.asarray(-4 * MAX_DISTANCE))]
    rings = jnp.stack([_bias_ring(rel_bias, d, tq, far) for d in (-tk, 0, tk)], axis=1)
    vec = _const_spec((1, HEAD_DIM))
    return pl.pallas_call(
        functools.partial(_attn_prompt_kernel, tq=tq, tk=tk, lam_init=lam_init),
        grid=(N_HEADS, nq),
        in_specs=[
            pl.BlockSpec((tq, 2 * HEAD_DIM), lambda h, i: (i, h)),
            pl.BlockSpec((t, 2 * HEAD_DIM), lambda h, i: (0, h)),
            pl.BlockSpec((None, t // tk, V_DIM + ONES_ROWS, tk), lambda h, i: (h, 0, 0, 0)),
            pl.BlockSpec((None, 3, 1, 2 * tq), lambda h, i: (h, 0, 0, 0)),
            vec, vec, vec, vec, _const_spec((1, V_DIM)),
        ],
        out_specs=pl.BlockSpec((tq, V_DIM), lambda h, i: (i, h)),
        out_shape=jax.ShapeDtypeStruct((t, V_COLS), bf16),
        scratch_shapes=[pltpu.VMEM((3, tk, tq), f32), pltpu.VMEM((2, 2, tk, tq), f32),
                        pltpu.VMEM((2, 2, tk, tq), bf16), pltpu.VMEM((2, 2, 1, tq), f32),
                        pltpu.VMEM((2, 2, 1, tq), f32), pltpu.VMEM((2, 1, tq), f32),
                        pltpu.VMEM((2, V_DIM + ONES_ROWS, tq), f32)],
        compiler_params=_cparams(("arbitrary", "arbitrary")),
        name="attn_prompt",
    )(q, kb, vt, rings, *lam_vecs, g_subln)


def _attn_sample_kernel(q_ref, kp_ref, vp_ref, kn_ref, vn_ref, nbp_ref, nbn_ref, lq1, lk1, lq2, lk2, gs_ref,
                        o_ref, *, lam_init):
    lam = _lam_value(lq1, lk1, lq2, lk2, lam_init)
    for hd in range(N_HEADS):
        qk_cols = slice(hd * 2 * HEAD_DIM, (hd + 1) * 2 * HEAD_DIM)
        v_cols = slice(hd * V_DIM, (hd + 1) * V_DIM)
        qz = _split_maps(q_ref[:, qk_cols])
        kp = kp_ref[:, qk_cols].astype(bf16)
        vp = vp_ref[:, v_cols].astype(bf16)
        kn = kn_ref[:, qk_cols]
        vn = vn_ref[:, v_cols]
        outs = []
        for mi in range(2):
            sp = _qk(qz[mi], kp) + nbp_ref[hd]
            sn = _qk(qz[mi], kn) + nbn_ref[hd]
            m = jnp.maximum(jnp.max(sp, axis=1, keepdims=True), jnp.max(sn, axis=1, keepdims=True))
            pp = jnp.exp2(sp - m).astype(bf16)
            pn = jnp.exp2(sn - m).astype(bf16)
            den = (jnp.sum(pp.astype(f32), axis=1, keepdims=True)
                   + jnp.sum(pn.astype(f32), axis=1, keepdims=True))
            num = jnp.dot(pp, vp, preferred_element_type=f32) + jnp.dot(pn, vn, preferred_element_type=f32)
            outs.append(num / den)
        o_ref[:, v_cols] = _finish_heads(outs[0], outs[1], lam, gs_ref[...], lam_init).astype(o_ref.dtype)


def _attn_sample(q, kb, vb, cache_k, cache_v, rel_bias, lam_vecs, g_subln, lam_init, nb_, ts):
    past = cache_k.shape[1]
    zero = jnp.zeros((N_HEADS,), f32)
    nbp = _bias_table(rel_bias, past, ts, 0, past, zero)
    nbn = _bias_table(rel_bias, past, ts, past, ts, zero)
    vec = _const_spec((1, HEAD_DIM))
    return pl.pallas_call(
        functools.partial(_attn_sample_kernel, lam_init=lam_init),
        grid=(nb_,),
        in_specs=[
            pl.BlockSpec((ts, QK_COLS), lambda b: (b, 0)),
            pl.BlockSpec((None, past, QK_COLS), lambda b: (b, 0, 0)),
            pl.BlockSpec((None, past, V_COLS), lambda b: (b, 0, 0)),
            pl.BlockSpec((ts, QK_COLS), lambda b: (b, 0)),
            pl.BlockSpec((ts, V_COLS), lambda b: (b, 0)),
            _const_spec((N_HEADS, ts, past)), _const_spec((N_HEADS, ts, ts)),
            vec, vec, vec, vec, _const_spec((1, V_DIM)),
        ],
        out_specs=pl.BlockSpec((ts, V_COLS), lambda b: (b, 0)),
        out_shape=jax.ShapeDtypeStruct((nb_ * ts, V_COLS), bf16),
        compiler_params=_cparams(("parallel",)),
        name="attn_sample",
    )(q, cache_k, cache_v, kb, vb, nbp, nbn, *lam_vecs, g_subln)


def _conv_lru_kernel(lx_ref, gl_ref, c0_ref, h0_ref, wc_ref, bc_ref, wa_ref, ba_ref, wx_ref, bx_ref, lam_ref,
                     hsg_ref, cout_ref, hout_ref, xbuf, a_sc, b_sc, hcar, *, tt, first_pos_is_zero):
    t = pl.program_id(1)

    @pl.when(t == 0)
    def _():
        xbuf[0:SUBLANES] = c0_ref[...]
        hcar[...] = jnp.broadcast_to(h0_ref[...], hcar.shape)

    xbuf[SUBLANES:SUBLANES + tt] = lx_ref[...]
    xc = jnp.broadcast_to(bc_ref[...], (tt, LRU_WIDTH))
    for j in range(CONV_W):
        lo = SUBLANES - (CONV_W - 1) + j
        xc = xc + xbuf[lo:lo + tt, :] * wc_ref[j:j + 1, :]
    tail = xbuf[tt:tt + SUBLANES, :]
    xbuf[0:SUBLANES] = tail
    cout_ref[...] = tail

    xcb = xc.astype(bf16)

    def block_diag(w_ref, b_ref):
        cols = [jnp.dot(xcb[:, n * LRU_BLOCK_W:(n + 1) * LRU_BLOCK_W], w_ref[n], preferred_element_type=f32)
                for n in range(LRU_BLOCKS)]
        return jnp.concatenate(cols, axis=1) + b_ref[...]

    r = _sigmoid(block_diag(wa_ref, ba_ref))
    ig = _sigmoid(block_diag(wx_ref, bx_ref))
    z = -lam_ref[...]
    softplus = jnp.maximum(z, 0.0) + jnp.log1p(jnp.exp(-jnp.abs(z)))
    log_a = -LRU_C * r * softplus
    a = jnp.exp(log_a)
    th = jnp.tanh(log_a)
    m2 = -2.0 * th / (1.0 - th)
    mult = jnp.where(m2 > 0.0, m2 * lax.rsqrt(m2), 0.0)
    if first_pos_is_zero:
        row = lax.broadcasted_iota(jnp.int32, (tt, LRU_WIDTH), 0)
        mult = jnp.where((row == 0) & (t == 0), 1.0, mult)
    bv = mult * ig * xc

    groups = (tt // SUBLANES, SUBLANES, LRU_WIDTH)
    a = a.reshape(groups)
    bv = bv.reshape(groups)
    sub = lax.broadcasted_iota(jnp.int32, groups, 1)
    d = 1
    while d < SUBLANES:
        valid = sub >= d
        a_s = pltpu.roll(a, d, axis=1)
        b_s = pltpu.roll(bv, d, axis=1)
        bv = jnp.where(valid, a * b_s + bv, bv)
        a = jnp.where(valid, a * a_s, a)
        d *= 2
    a_sc[...] = a.reshape(tt, LRU_WIDTH)
    b_sc[...] = bv.reshape(tt, LRU_WIDTH)

    def group(g, h):
        sl = pl.ds(pl.multiple_of(g * SUBLANES, SUBLANES), SUBLANES)
        hg = a_sc[sl, :] * h + b_sc[sl, :]
        b_sc[sl, :] = hg
        return jnp.broadcast_to(hg[SUBLANES - 1:SUBLANES, :], hg.shape)

    h = lax.fori_loop(0, tt // SUBLANES, group, hcar[...], unroll=4)
    hcar[...] = h
    hout_ref[...] = h[0:1, :]
    hsg_ref[...] = (b_sc[...] * gl_ref[...]).astype(hsg_ref.dtype)


def _conv_lru(lx, gl, conv0, h0, w_conv, b_conv, wa, ba, wx, bx, lru_lambda, nb_, tlen, tt, first_pos_is_zero):
    nt = tlen // tt
    rows = pl.BlockSpec((tt, LRU_WIDTH), lambda b, t: (b * nt + t, 0))
    vecw = _const_spec((1, LRU_WIDTH))
    wblk = _const_spec((LRU_BLOCKS, LRU_BLOCK_W, LRU_BLOCK_W))
    return pl.pallas_call(
        functools.partial(_conv_lru_kernel, tt=tt, first_pos_is_zero=first_pos_is_zero),
        grid=(nb_, nt),
        in_specs=[rows, rows,
                  pl.BlockSpec((None, SUBLANES, LRU_WIDTH), lambda b, t: (b, 0, 0)),
                  pl.BlockSpec((None, 1, LRU_WIDTH), lambda b, t: (b, 0, 0)),
                  _const_spec((CONV_W, LRU_WIDTH)), vecw, wblk, vecw, wblk, vecw, vecw],
        out_specs=[rows,
                   pl.BlockSpec((None, SUBLANES, LRU_WIDTH), lambda b, t: (b, 0, 0)),
                   pl.BlockSpec((None, 1, LRU_WIDTH), lambda b, t: (b, 0, 0))],
        out_shape=[jax.ShapeDtypeStruct((nb_ * tlen, LRU_WIDTH), bf16),
                   jax.ShapeDtypeStruct((nb_, SUBLANES, LRU_WIDTH), f32),
                   jax.ShapeDtypeStruct((nb_, 1, LRU_WIDTH), f32)],
        scratch_shapes=[pltpu.VMEM((tt + SUBLANES, LRU_WIDTH), f32), pltpu.VMEM((tt, LRU_WIDTH), f32),
                        pltpu.VMEM((tt, LRU_WIDTH), f32), pltpu.VMEM((SUBLANES, LRU_WIDTH), f32)],
        compiler_params=_cparams(("arbitrary", "arbitrary")),
        name="conv_lru",
    )(lx, gl, conv0, h0, w_conv, b_conv, wa, ba, wx, bx, lru_lambda)


def _merge_kernel(x_ref, att_ref, hsg_ref, gt_ref, wa_ref, wb_ref, wo_ref, gf_ref, wr_ref, br_ref,
                  x1_ref, h2_ref, comb_ref, *, parts):
    rows_per = x_ref.shape[0] // parts
    for part in range(parts):
        rows = slice(part * rows_per, (part + 1) * rows_per)
        bra = jnp.dot(att_ref[rows, :], wa_ref[...], preferred_element_type=f32)
        brb = jnp.dot(hsg_ref[rows, :], wb_ref[...], preferred_element_type=f32)
        gt = gt_ref[rows, :]
        m = (gt[:, :D_MODEL] * bra + gt[:, D_MODEL:] * brb).astype(bf16)
        x1 = x_ref[rows, :] + jnp.dot(m, wo_ref[...], preferred_element_type=f32)
        x1_ref[rows, :] = x1
        h2 = _rms(x1, gf_ref[...])
        h2_hi = h2.astype(bf16)
        h2_ref[rows, :] = h2_hi

        h2_lo = (h2 - h2_hi.astype(f32)).astype(bf16)
        prods = (jnp.dot(h2_hi, wr_ref[...], preferred_element_type=f32)
                 + jnp.dot(h2_lo, wr_ref[...], preferred_element_type=f32))
        logits = prods[:, :LANES] + prods[:, LANES:] + br_ref[...]
        lane = lax.broadcasted_iota(jnp.int32, logits.shape, 1)
        lanef = lane.astype(f32)
        low = jnp.float32(-3.0e38)
        is_grp = (lane >= N_EXPERTS) & (lane < N_EXPERTS + N_GROUPS)
        gl = jnp.where(is_grp, logits, low)
        gmax = jnp.max(gl, axis=1, keepdims=True)
        gidx = jnp.min(jnp.where(is_grp & (gl == gmax), lanef, 1.0e3), axis=1, keepdims=True) - N_EXPERTS
        gden = jnp.sum(jnp.where(is_grp, jnp.exp(gl - gmax), 0.0), axis=1, keepdims=True)
        g_w = 1.0 / gden
        lo = gidx * EXPERTS_PER_GROUP
        in_sel = (lanef >= lo) & (lanef < lo + EXPERTS_PER_GROUP)
        sel = jnp.where(in_sel, logits, low)
        v1 = jnp.max(sel, axis=1, keepdims=True)
        i1 = jnp.min(jnp.where(in_sel & (sel == v1), lanef, 1.0e3), axis=1, keepdims=True)
        in_sel2 = in_sel & (lanef != i1)
        sel2 = jnp.where(in_sel2, logits, low)
        v2 = jnp.max(sel2, axis=1, keepdims=True)
        i2 = jnp.min(jnp.where(in_sel2 & (sel2 == v2), lanef, 1.0e3), axis=1, keepdims=True)
        e2 = jnp.exp(v2 - v1)
        w1 = g_w / (1.0 + e2)
        w2 = g_w * e2 / (1.0 + e2)
        comb_ref[rows, :] = jnp.where(lanef == i1, w1, 0.0) + jnp.where(lanef == i2, w2, 0.0)


def _merge(x, att, hsg, gt, wa, wb, wo, g_ffn, w_router, b_router, tm):
    n = x.shape[0]
    row = lambda w: pl.BlockSpec((tm, w), lambda i: (i, 0))
    sq = _const_spec((D_MODEL, D_MODEL))
    return pl.pallas_call(
        functools.partial(_merge_kernel, parts=2 if tm % 512 == 0 else 1),
        grid=(n // tm,),
        in_specs=[row(D_MODEL), row(V_COLS), row(LRU_WIDTH), row(2 * D_MODEL), sq, sq, sq,
                  _const_spec((1, D_MODEL)), _const_spec((D_MODEL, 2 * LANES)), _const_spec((1, LANES))],
        out_specs=[row(D_MODEL), row(D_MODEL), row(LANES)],
        out_shape=[jax.ShapeDtypeStruct((n, D_MODEL), f32), jax.ShapeDtypeStruct((n, D_MODEL), bf16),
                   jax.ShapeDtypeStruct((n, LANES), f32)],
        compiler_params=_cparams(("parallel",)),
        name="merge_router",
    )(x, att, hsg, gt, wa, wb, wo, g_ffn, w_router, b_router)


EXPERTS_PER_STEP = 8


MOE_CHUNK = 128


def _moe_kernel(h_ref, comb_ref, wg_ref, wu_ref, wd_ref, y_ref, pt_sc, x_sc, c_sc, ys_sc, seg_sc):
    e = pl.program_id(1)
    tm = h_ref.shape[0]
    lane1 = lax.broadcasted_iota(jnp.int32, (1, LANES), 1)

    @pl.when(e == 0)
    def _():
        comb = comb_ref[...]
        ex = lax.broadcasted_iota(jnp.int32, (LANES, LANES), 0)
        gr = lax.broadcasted_iota(jnp.int32, (LANES, LANES), 1)
        in_group = jnp.where((ex // EXPERTS_PER_GROUP == gr) & (ex < N_EXPERTS), 1.0, 0.0).astype(bf16)
        gsum = jnp.dot(comb.astype(bf16), in_group, preferred_element_type=f32)
        member = jnp.where(gsum > 0.0, 1.0, 0.0)
        r_i = lax.broadcasted_iota(jnp.int32, (tm, tm), 0)
        c_i = lax.broadcasted_iota(jnp.int32, (tm, tm), 1)
        tri = jnp.where(gr <= ex, 1.0, 0.0).astype(bf16)
        member_b = member.astype(bf16)
        totals = jnp.zeros((1, LANES), f32)
        blocks = []
        for blk in range(tm // LANES):
            part = jnp.dot(tri, member_b[blk * LANES:(blk + 1) * LANES], preferred_element_type=f32) + totals
            blocks.append(part)
            totals = part[LANES - 1:LANES, :]
        rank = jnp.concatenate(blocks, axis=0)
        offs = jnp.zeros((1, LANES), f32)
        for g in range(1, N_GROUPS):
            prev = jnp.sum(jnp.where(lane1 == g - 1, totals, 0.0), axis=1, keepdims=True)
            offs = offs + jnp.where(lane1 >= g, prev, 0.0)
        seg_sc[0:1, :] = offs.astype(jnp.int32)
        seg_sc[1:2, :] = totals.astype(jnp.int32)
        pos = jnp.sum(member * (offs + rank - 1.0), axis=1, keepdims=True)
        pos_row = jnp.broadcast_to(pos, (tm, LANES)).T[0:1, :].astype(jnp.int32)
        perm = jnp.where(r_i == pos_row, 1.0, 0.0).astype(bf16)
        pt_sc[...] = jnp.where(c_i == pos.astype(jnp.int32), 1.0, 0.0).astype(bf16)
        comb_hi = comb.astype(bf16)
        comb_lo = (comb - comb_hi.astype(f32)).astype(bf16)
        srt = jnp.dot(perm, jnp.concatenate([h_ref[...], comb_hi, comb_lo], axis=1), preferred_element_type=f32)
        x_sc[...] = srt[:, :D_MODEL].astype(bf16)
        c_sc[...] = srt[:, D_MODEL:D_MODEL + LANES] + srt[:, D_MODEL + LANES:]
        ys_sc[...] = jnp.zeros(ys_sc.shape, f32)

    g = (e * EXPERTS_PER_STEP) // EXPERTS_PER_GROUP
    off = jnp.sum(jnp.where(lane1 == g, seg_sc[0:1, :], 0))
    cnt = jnp.sum(jnp.where(lane1 == g, seg_sc[1:2, :], 0))
    lane = lax.broadcasted_iota(jnp.int32, (MOE_CHUNK, LANES), 1)

    def chunk(k, carry):
        rows = pl.ds(pl.multiple_of(k * MOE_CHUNK, MOE_CHUNK), MOE_CHUNK)
        x = x_sc[rows, :]
        cw = c_sc[rows, :]
        hidden = []
        for j in range(EXPERTS_PER_STEP):
            gate = jnp.dot(x, wg_ref[j], preferred_element_type=f32)
            up = jnp.dot(x, wu_ref[j], preferred_element_type=f32)
            c = jnp.sum(jnp.where(lane == e * EXPERTS_PER_STEP + j, cw, 0.0), axis=1, keepdims=True)
            hidden.append((gate * _sigmoid(gate) * up * c).astype(bf16))
        ys_sc[rows, :] += jnp.dot(jnp.concatenate(hidden, axis=1), wd_ref[...], preferred_element_type=f32)
        return carry

    lax.fori_loop(off // MOE_CHUNK, (off + cnt + MOE_CHUNK - 1) // MOE_CHUNK, chunk, 0)

    @pl.when(e == pl.num_programs(1) - 1)
    def _():
        y = jnp.dot(pt_sc[...], ys_sc[...].astype(bf16), preferred_element_type=f32)
        y_ref[...] = y.astype(y_ref.dtype)


def _moe(h2, comb, wg, wu, wd_rows, tm):
    n = h2.shape[0]
    per = EXPERTS_PER_STEP
    assert EXPERTS_PER_GROUP % per == 0 and tm % MOE_CHUNK == 0
    return pl.pallas_call(
        _moe_kernel,
        grid=(n // tm, N_EXPERTS // per),
        in_specs=[pl.BlockSpec((tm, D_MODEL), lambda i, e: (i, 0)),
                  pl.BlockSpec((tm, LANES), lambda i, e: (i, 0)),
                  pl.BlockSpec((per, D_MODEL, D_EXPERT), lambda i, e: (e, 0, 0)),
                  pl.BlockSpec((per, D_MODEL, D_EXPERT), lambda i, e: (e, 0, 0)),
                  pl.BlockSpec((per * D_EXPERT, D_MODEL), lambda i, e: (e, 0))],
        out_specs=pl.BlockSpec((tm, D_MODEL), lambda i, e: (i, 0)),
        out_shape=jax.ShapeDtypeStruct((n, D_MODEL), bf16),
        scratch_shapes=[pltpu.VMEM((tm, tm), bf16), pltpu.VMEM((tm, D_MODEL), bf16),
                        pltpu.VMEM((tm, LANES), f32), pltpu.VMEM((tm, D_MODEL), f32),
                        pltpu.VMEM((SUBLANES, LANES), jnp.int32)],
        compiler_params=_cparams(("arbitrary", "arbitrary")),
        name="moe",
    )(h2, comb, wg, wu, wd_rows)


def _ple_kernel(x1_ref, y_ref, p_ref, gp_ref, wpg_ref, bpg_ref, wpi_ref, gfin_ref, o_ref):
    x2 = x1_ref[...] + y_ref[...]
    hp = _rms(x2, gp_ref[...]).astype(bf16)
    pg = _sigmoid(jnp.dot(hp, wpg_ref[...], preferred_element_type=f32) + bpg_ref[...])
    pin = jnp.dot(p_ref[...].astype(bf16), wpi_ref[...], preferred_element_type=f32)
    x3 = x2 + pg * pin
    o_ref[...] = _rms(x3, gfin_ref[...])


def _ple(x1, y, p, g_ple, wpg, bpg, wpi, g_final, tm):
    n = x1.shape[0]
    row = lambda w: pl.BlockSpec((tm, w), lambda i: (i, 0))
    vec = _const_spec((1, D_MODEL))
    return pl.pallas_call(
        _ple_kernel,
        grid=(n // tm,),
        in_specs=[row(D_MODEL), row(D_MODEL), row(PLE_DIM), vec, _const_spec((D_MODEL, D_MODEL)), vec,
                  _const_spec((PLE_DIM, D_MODEL)), vec],
        out_specs=row(D_MODEL),
        out_shape=jax.ShapeDtypeStruct((n, D_MODEL), f32),
        compiler_params=_cparams(("parallel",)),
        name="ple_final",
    )(x1, y, p, g_ple, wpg, bpg, wpi, g_final)


PROJ_ROWS = 256
ATTN_KEY_TILE = 512
MERGE_ROWS = 512
MOE_ROWS = 1024


def _tile(n, pref):
    return pref if n % pref == 0 else n


def _stream(x, p, conv0, h0, cache, li, rel_bias, w):
    nb_, tlen, _ = x.shape
    n = nb_ * tlen
    xf = x.reshape(n, D_MODEL)
    lam_init = 0.8 - 0.6 * math.exp(-0.3 * li)
    tk = None if cache is not None else ATTN_KEY_TILE
    q, k, kb, v, vb, lx, gl, gt = _in_proj(xf, w["g_mix"], w["w_in"], w["b_merge"], _tile(n, PROJ_ROWS), tk)
    if cache is None:
        att = _attn_prompt(q, kb, vb, rel_bias, w["lam_vecs"], w["g_subln"], lam_init, 2 * tk, tk)
    else:
        att = _attn_sample(q, kb, vb, cache[0], cache[1], rel_bias, w["lam_vecs"], w["g_subln"], lam_init,
                           nb_, tlen)
    conv_pad = jnp.pad(conv0, ((0, 0), (SUBLANES - (CONV_W - 1), 0), (0, 0)))
    hsg, cout, hout = _conv_lru(lx, gl, conv_pad, h0[:, None, :], w["w_conv"], w["b_conv"], w["w_rg_a"],
                                w["b_rg_a"], w["w_rg_x"], w["b_rg_x"], w["lru_lambda"], nb_, tlen,
                                _tile(tlen, PROJ_ROWS), cache is None)
    tm = _tile(n, MERGE_ROWS)
    x1, h2, comb = _merge(xf, att, hsg, gt, w["w_attn_br"], w["w_lru_br"], w["w_out"], w["g_ffn"],
                          w["w_router"], w["b_router"], tm)
    y = _moe(h2, comb, w["w_e_gate"], w["w_e_up"], w["w_e_down"], _tile(n, MOE_ROWS))
    out = _ple(x1, y, p.reshape(n, PLE_DIM), w["g_ple"], w["w_ple_gate"], w["b_ple_gate"], w["w_ple_in"],
               w["g_final"], tm)
    return out, k, v, cout[:, SUBLANES - (CONV_W - 1):, :], hout[:, 0, :]


def kernel(x_prompt, x_sample, cache_k, cache_v, state_conv, state_lru, p_prompt, p_sample, rel_bias, g_mix, w_in, b_merge, lam_q1, lam_k1, lam_q2, lam_k2, g_subln, w_attn_br, w_conv, b_conv, w_rg_a, b_rg_a, w_rg_x, b_rg_x, lru_lambda, w_lru_br, w_out, g_ffn, w_grp, b_grp, w_rt, b_rt, w_e_gate, w_e_up, w_e_down, g_ple, w_ple_gate, b_ple_gate, w_ple_in, g_final):
    depth = w_in.shape[0]
    assert depth == 1, "the final norm is fused into the single layer's last stage"
    bp, tp, _ = x_prompt.shape
    bs, ts, _ = x_sample.shape
    past = cache_k.shape[2]
    li = 0
    row = lambda a: a[li].reshape(1, -1).astype(f32)
    w_router = jnp.concatenate(
        [jnp.transpose(w_rt[li], (1, 0, 2)).reshape(D_MODEL, N_EXPERTS), w_grp[li]], axis=1).astype(f32)
    w_router = jnp.pad(w_router, ((0, 0), (0, LANES - N_EXPERTS - N_GROUPS)))
    w_router_hi = w_router.astype(bf16)
    w_router = jnp.concatenate([w_router_hi, (w_router - w_router_hi.astype(f32)).astype(bf16)], axis=1)
    b_router = jnp.pad(jnp.concatenate([b_rt[li].reshape(-1), b_grp[li]]).astype(f32),
                       (0, LANES - N_EXPERTS - N_GROUPS)).reshape(1, LANES)
    w = dict(
        g_mix=row(g_mix), w_in=w_in[li].astype(bf16), b_merge=row(b_merge),
        lam_vecs=(row(lam_q1), row(lam_k1), row(lam_q2), row(lam_k2)), g_subln=row(g_subln),
        w_attn_br=w_attn_br[li].astype(bf16), w_conv=w_conv[li].astype(f32), b_conv=row(b_conv),
        w_rg_a=w_rg_a[li].astype(bf16), b_rg_a=row(b_rg_a), w_rg_x=w_rg_x[li].astype(bf16), b_rg_x=row(b_rg_x),
        lru_lambda=row(lru_lambda), w_lru_br=w_lru_br[li].astype(bf16), w_out=w_out[li].astype(bf16),
        g_ffn=row(g_ffn), w_router=w_router, b_router=b_router,
        w_e_gate=w_e_gate[li].astype(bf16), w_e_up=w_e_up[li].astype(bf16), w_e_down=w_e_down[li].reshape(N_EXPERTS * D_EXPERT, D_MODEL).astype(bf16),
        g_ple=row(g_ple), w_ple_gate=w_ple_gate[li].astype(bf16), b_ple_gate=row(b_ple_gate),
        w_ple_in=w_ple_in[li].astype(bf16), g_final=g_final.reshape(1, -1).astype(f32),
    )
    conv0 = jnp.zeros((bp, CONV_W - 1, LRU_WIDTH), f32)
    h0 = jnp.zeros((bp, LRU_WIDTH), f32)
    yp, kp, vp, cp, hp = _stream(x_prompt, p_prompt[li], conv0, h0, None, li, rel_bias, w)
    cache = (cache_k[li].reshape(bs, past, QK_COLS), cache_v[li].reshape(bs, past, V_COLS))
    ys, ks, vs, cs, hs = _stream(x_sample, p_sample[li], state_conv[li], state_lru[li], cache, li, rel_bias, w)
    return (yp.reshape(bp, tp, D_MODEL), ys.reshape(bs, ts, D_MODEL),
            kp.reshape(1, bp, tp, N_HEADS, 2, HEAD_DIM), vp.reshape(1, bp, tp, N_HEADS, V_DIM),
            cp[None], hp[None],
            ks.reshape(1, bs, ts, N_HEADS, 2, HEAD_DIM), vs.reshape(1, bs, ts, N_HEADS, V_DIM),
            cs[None], hs[None])
```

```python
import functools
import math

import jax
import jax.numpy as jnp
from jax import lax
from jax.experimental import pallas as pl
from jax.experimental.pallas import tpu as pltpu

f32 = jnp.float32
bf16 = jnp.bfloat16

D_MODEL = 1024
N_HEADS = 8
HEAD_DIM = 64
V_DIM = 128
QK_COLS = N_HEADS * 2 * HEAD_DIM
V_COLS = N_HEADS * V_DIM
LRU_WIDTH = 1024
LRU_BLOCKS = 8
LRU_BLOCK_W = LRU_WIDTH // LRU_BLOCKS
CONV_W = 4
LRU_C = 8.0
CHUNK = 64
N_BUCKETS = 32
MAX_DISTANCE = 128
N_GROUPS = 4
EXPERTS_PER_GROUP = 8
N_EXPERTS = N_GROUPS * EXPERTS_PER_GROUP
D_EXPERT = 256
PLE_DIM = 256
EPS = 1e-6
NEG_INF = -1e30
LOG2E = math.log2(math.e)
ONES_ROWS = 16
LANES = 128
SUBLANES = 8
VMEM_LIMIT = 56 * 1024 * 1024

C_Q, C_K, C_V = 0, QK_COLS, 2 * QK_COLS
C_LX = 2 * QK_COLS + V_COLS
C_LY = C_LX + LRU_WIDTH
C_G = C_LY + LRU_WIDTH
IN_COLS = C_G + 2 * D_MODEL


def _cparams(sem):
    return pltpu.CompilerParams(dimension_semantics=sem, vmem_limit_bytes=VMEM_LIMIT)


def _const_spec(shape):
    nd = len(shape)
    return pl.BlockSpec(shape, lambda *_: (0,) * nd, pipeline_mode=pl.Buffered(1))


def _rms(x, g):
    return x * lax.rsqrt(jnp.mean(x * x, axis=-1, keepdims=True) + EPS) * g


def _sigmoid(x):
    return 1.0 / (1.0 + jnp.exp(-x))


def _gelu_tanh(x):
    c = math.sqrt(2.0 / math.pi)
    return 0.5 * x * (1.0 + jnp.tanh(c * (x + 0.044715 * (x * x * x))))


def _in_proj_kernel(x_ref, g_ref, w_ref, bm_ref, q_ref, k_ref, kb_ref, v_ref, vb_ref, lx_ref, gl_ref, gt_ref,
                    *, transposed_v):
    h = _rms(x_ref[...], g_ref[...]).astype(bf16)

    def proj(lo, width):
        return jnp.dot(h, w_ref[:, lo:lo + width], preferred_element_type=f32)

    q_ref[...] = (proj(C_Q, QK_COLS) * (HEAD_DIM ** -0.5 * LOG2E)).astype(bf16)
    k = proj(C_K, QK_COLS)
    k_ref[...] = k
    kb_ref[...] = k.astype(bf16)
    v = proj(C_V, V_COLS)
    v_ref[...] = v
    if transposed_v:
        for hd in range(N_HEADS):
            vb_ref[hd, :V_DIM, :] = v[:, hd * V_DIM:(hd + 1) * V_DIM].T.astype(bf16)
            vb_ref[hd, V_DIM:, :] = jnp.ones((ONES_ROWS, v.shape[0]), bf16)
    else:
        vb_ref[...] = v.astype(bf16)
    lx_ref[...] = proj(C_LX, LRU_WIDTH)
    gl_ref[...] = _gelu_tanh(proj(C_LY, LRU_WIDTH))
    gt_ref[...] = _sigmoid(proj(C_G, 2 * D_MODEL) + bm_ref[...])


def _in_proj(x, g_mix, w_in_b, b_merge, tm, tq=None):
    n = x.shape[0]
    row = lambda w: pl.BlockSpec((tm, w), lambda i: (i, 0))
    if tq is None:
        vb_shape, vb_spec = jax.ShapeDtypeStruct((n, V_COLS), bf16), row(V_COLS)
    else:
        per = tq // tm
        vb_shape = jax.ShapeDtypeStruct((N_HEADS, n // tq, V_DIM + ONES_ROWS, tq), bf16)
        vb_spec = pl.BlockSpec((N_HEADS, None, V_DIM + ONES_ROWS, tm), lambda i: (0, i // per, 0, i % per))
    outs = [
        jax.ShapeDtypeStruct((n, QK_COLS), bf16),
        jax.ShapeDtypeStruct((n, QK_COLS), f32),
        jax.ShapeDtypeStruct((n, QK_COLS), bf16),
        jax.ShapeDtypeStruct((n, V_COLS), f32),
        vb_shape,
        jax.ShapeDtypeStruct((n, LRU_WIDTH), f32),
        jax.ShapeDtypeStruct((n, LRU_WIDTH), f32),
        jax.ShapeDtypeStruct((n, 2 * D_MODEL), f32),
    ]
    return pl.pallas_call(
        functools.partial(_in_proj_kernel, transposed_v=tq is not None),
        grid=(n // tm,),
        in_specs=[row(D_MODEL), _const_spec((1, D_MODEL)), _const_spec((D_MODEL, IN_COLS)),
                  _const_spec((1, 2 * D_MODEL))],
        out_specs=[row(QK_COLS), row(QK_COLS), row(QK_COLS), row(V_COLS), vb_spec,
                   row(LRU_WIDTH), row(LRU_WIDTH), row(2 * D_MODEL)],
        out_shape=outs,
        compiler_params=_cparams(("parallel",)),
        name="in_proj",
    )(x, g_mix, w_in_b, b_merge)


def _rel_bucket(rel):
    half = N_BUCKETS // 2
    max_exact = half // 2
    ret = jnp.where(rel > 0, half, 0)
    n = jnp.abs(rel)
    nf = jnp.maximum(n, 1).astype(f32)
    large = max_exact + (jnp.log(nf / max_exact) / math.log(MAX_DISTANCE / max_exact)
                         * (half - max_exact)).astype(jnp.int32)
    large = jnp.minimum(large, half - 1)
    return ret + jnp.where(n < max_exact, n, large)


def _toeplitz(t, nq, nk):
    nh, length = t.shape
    a = jnp.broadcast_to(t[:, None, :], (nh, nq, length))
    a = jnp.pad(a, ((0, 0), (0, 0), (0, 1))).reshape(nh, nq * (length + 1))
    a = a[:, :nq * length].reshape(nh, nq, length)
    return a[:, :, nq - 1:nq - 1 + nk]


def _bias_table(rel_bias, qpos0, nq, kpos0, nk, shift):
    rel = jnp.arange(-(nq - 1), nk) + (kpos0 - qpos0)
    t = (rel_bias.astype(f32)[_rel_bucket(rel)].T - shift[:, None]) * LOG2E
    qpos = qpos0 + jnp.arange(nq)
    kpos = kpos0 + jnp.arange(nk)
    mask = (kpos[None, :] // CHUNK) <= (qpos[:, None] // CHUNK)
    return jnp.where(mask[None], _toeplitz(t, nq, nk), NEG_INF)


def _bias_ring(rel_bias, offset, tq, shift):
    r = jnp.arange(2 * tq)
    x = jnp.where(r < tq, r, r - 2 * tq)
    t = (rel_bias.astype(f32)[_rel_bucket(offset - x)].T - shift[:, None]) * LOG2E
    return t[:, None, :]


def _lam_value(lq1, lk1, lq2, lk2, lam_init):
    s1 = jnp.sum(lq1[...] * lk1[...], axis=-1, keepdims=True)
    s2 = jnp.sum(lq2[...] * lk2[...], axis=-1, keepdims=True)
    return jnp.exp(s1) - jnp.exp(s2) + lam_init


def _split_maps(q):
    lane = lax.broadcasted_iota(jnp.int32, q.shape, 1)
    zero = jnp.zeros_like(q)
    return jnp.where(lane < HEAD_DIM, q, zero), jnp.where(lane >= HEAD_DIM, q, zero)


def _qk(qz, kt):
    return lax.dot_general(qz, kt, (((1,), (1,)), ((), ())), preferred_element_type=f32)


def _finish_heads(o1, o2, lam, gs, lam_init):
    att = o1 - lam * o2
    return _rms(att, gs) * (1.0 - lam_init)


def _attn_prompt_kernel(q_ref, k_ref, vt_ref, ring_ref, lq1, lk1, lq2, lk2, gs_ref, o_ref,
                        nb_sc, s_sc, p_sc, al_sc, tm_sc, m_sc, acc_sc, *, tq, tk, lam_init):
    i = pl.program_id(1)
    ring_len = ring_ref.shape[-1]

    def bias_tile(ring):
        return pltpu.roll(jnp.broadcast_to(ring, (tk, ring_len)), 0, 1, stride=1, stride_axis=0)[:, :tq]

    @pl.when(i == 0)
    def _():
        kk = lax.broadcasted_iota(jnp.int32, (tk, tq), 0)
        qq = lax.broadcasted_iota(jnp.int32, (tk, tq), 1)
        nb_sc[0] = jnp.where((kk // CHUNK) <= (qq // CHUNK), bias_tile(ring_ref[1]), NEG_INF)
        nb_sc[1] = jnp.where(((kk + tk) // CHUNK) <= (qq // CHUNK), bias_tile(ring_ref[2]), NEG_INF)
        nb_sc[2] = jnp.full((tk, tq), NEG_INF, f32)

    @pl.when(i == 1)
    def _():
        nb_sc[2] = nb_sc[1]
        nb_sc[1] = nb_sc[0]
        nb_sc[0] = bias_tile(ring_ref[0])

    qt = q_ref[...].astype(f32).T
    row = lax.broadcasted_iota(jnp.int32, qt.shape, 0)
    qz = (jnp.where(row < HEAD_DIM, qt, 0.0).astype(bf16), jnp.where(row >= HEAD_DIM, qt, 0.0).astype(bf16))

    m_sc[...] = jnp.full(m_sc.shape, NEG_INF, f32)
    acc_sc[...] = jnp.zeros(acc_sc.shape, f32)

    def qk(slot, tile, bias, cols=slice(None)):
        kt = k_ref[pl.ds(pl.multiple_of(tile * tk, tk), tk), :]
        for mi in range(2):
            s = jnp.dot(kt, qz[mi][:, cols], preferred_element_type=f32)
            s = s if bias is None else s + bias[:, cols]
            s_sc[mi, slot, :, cols] = s
            tm_sc[mi, slot, :, cols] = jnp.max(s, axis=0, keepdims=True)

    def softmax(slot, cols=slice(None)):
        for mi in range(2):
            s = s_sc[mi, slot, :, cols]
            m_prev = m_sc[mi, :, cols]
            m_new = jnp.maximum(m_prev, tm_sc[mi, slot, :, cols])
            al_sc[mi, slot, :, cols] = jnp.exp2(m_prev - m_new)
            p_sc[mi, slot, :, cols] = jnp.exp2(s - m_new).astype(bf16)
            m_sc[mi, :, cols] = m_new

    def pv(slot, vtile, cols=slice(None)):
        va = vt_ref[vtile]
        for mi in range(2):
            acc_sc[mi, :, cols] = (acc_sc[mi, :, cols] * al_sc[mi, slot, :, cols]
                                   + jnp.dot(va, p_sc[mi, slot, :, cols], preferred_element_type=f32))

    late = slice(tq // 2, tq)

    per = tq // tk
    a0 = jnp.maximum(per * i - 1, 0)
    n_far = jnp.maximum(per * i - 1, 0)
    last_tile = vt_ref.shape[0] - 1

    def far_k(f):
        return jnp.clip(f, 0, jnp.maximum(n_far - 1, 0))

    qk(0, a0, nb_sc[0])
    qk(1, a0 + 1, nb_sc[1])
    softmax(0)
    qk(0, jnp.minimum(a0 + 2, last_tile), nb_sc[2], late)
    softmax(1)
    pv(0, a0)

    @pl.when(i == 0)
    def _():
        pv(1, a0 + 1)

    @pl.when(i > 0)
    def _():
        qk(1, 0, None)
        softmax(0, late)
        pv(1, a0 + 1)
        qk(0, far_k(1), None)
        softmax(1)
        pv(0, a0 + 2, late)

        def far_pair(p, carry):
            qk(1, 2 * p + 2, None)
            softmax(0)
            pv(1, 2 * p)
            qk(0, far_k(2 * p + 3), None)
            softmax(1)
            pv(0, 2 * p + 1)
            return carry

        lax.fori_loop(0, (n_far - 1) // 2, far_pair, 0)
        pv(1, n_far - 1)

    a1 = acc_sc[0]
    a2 = acc_sc[1]
    o1 = a1[:V_DIM] / a1[V_DIM:V_DIM + 1]
    o2 = a2[:V_DIM] / a2[V_DIM:V_DIM + 1]
    lam = _lam_value(lq1, lk1, lq2, lk2, lam_init)
    att = (o1 - lam * o2).T
    o_ref[...] = (_rms(att, gs_ref[...]) * (1.0 - lam_init)).astype(o_ref.dtype)


def _attn_prompt(q, kb, vt, rel_bias, lam_vecs, g_subln, lam_init, tq, tk):
    t = q.shape[0]
    nq = t // tq
    assert tq == 2 * tk and t % tq == 0
    far = rel_bias.astype(f32)[_rel_bucket(jnp.asarray(-4 * MAX_DISTANCE))]
    rings = jnp.stack([_bias_ring(rel_bias, d, tq, far) for d in (-tk, 0, tk)], axis=1)
    vec = _const_spec((1, HEAD_DIM))
    return pl.pallas_call(
        functools.partial(_attn_prompt_kernel, tq=tq, tk=tk, lam_init=lam_init),
        grid=(N_HEADS, nq),
        in_specs=[
            pl.BlockSpec((tq, 2 * HEAD_DIM), lambda h, i: (i, h)),
            pl.BlockSpec((t, 2 * HEAD_DIM), lambda h, i: (0, h)),
            pl.BlockSpec((None, t // tk, V_DIM + ONES_ROWS, tk), lambda h, i: (h, 0, 0, 0)),
            pl.BlockSpec((None, 3, 1, 2 * tq), lambda h, i: (h, 0, 0, 0)),
            vec, vec, vec, vec, _const_spec((1, V_DIM)),
        ],
        out_specs=pl.BlockSpec((tq, V_DIM), lambda h, i: (i, h)),
        out_shape=jax.ShapeDtypeStruct((t, V_COLS), bf16),
        scratch_shapes=[pltpu.VMEM((3, tk, tq), f32), pltpu.VMEM((2, 2, tk, tq), f32),
                        pltpu.VMEM((2, 2, tk, tq), bf16), pltpu.VMEM((2, 2, 1, tq), f32),
                        pltpu.VMEM((2, 2, 1, tq), f32), pltpu.VMEM((2, 1, tq), f32),
                        pltpu.VMEM((2, V_DIM + ONES_ROWS, tq), f32)],
        compiler_params=_cparams(("arbitrary", "arbitrary")),
        name="attn_prompt",
    )(q, kb, vt, rings, *lam_vecs, g_subln)


def _attn_sample_kernel(q_ref, kp_ref, vp_ref, kn_ref, vn_ref, nbp_ref, nbn_ref, lq1, lk1, lq2, lk2, gs_ref,
                        o_ref, *, lam_init):
    lam = _lam_value(lq1, lk1, lq2, lk2, lam_init)
    for hd in range(N_HEADS):
        qk_cols = slice(hd * 2 * HEAD_DIM, (hd + 1) * 2 * HEAD_DIM)
        v_cols = slice(hd * V_DIM, (hd + 1) * V_DIM)
        qz = _split_maps(q_ref[:, qk_cols])
        kp = kp_ref[:, qk_cols].astype(bf16)
        vp = vp_ref[:, v_cols].astype(bf16)
        kn = kn_ref[:, qk_cols]
        vn = vn_ref[:, v_cols]
        outs = []
        for mi in range(2):
            sp = _qk(qz[mi], kp) + nbp_ref[hd]
            sn = _qk(qz[mi], kn) + nbn_ref[hd]
            m = jnp.maximum(jnp.max(sp, axis=1, keepdims=True), jnp.max(sn, axis=1, keepdims=True))
            pp = jnp.exp2(sp - m).astype(bf16)
            pn = jnp.exp2(sn - m).astype(bf16)
            den = (jnp.sum(pp.astype(f32), axis=1, keepdims=True)
                   + jnp.sum(pn.astype(f32), axis=1, keepdims=True))
            num = jnp.dot(pp, vp, preferred_element_type=f32) + jnp.dot(pn, vn, preferred_element_type=f32)
            outs.append(num / den)
        o_ref[:, v_cols] = _finish_heads(outs[0], outs[1], lam, gs_ref[...], lam_init).astype(o_ref.dtype)


def _attn_sample(q, kb, vb, cache_k, cache_v, rel_bias, lam_vecs, g_subln, lam_init, nb_, ts):
    past = cache_k.shape[1]
    zero = jnp.zeros((N_HEADS,), f32)
    nbp = _bias_table(rel_bias, past, ts, 0, past, zero)
    nbn = _bias_table(rel_bias, past, ts, past, ts, zero)
    vec = _const_spec((1, HEAD_DIM))
    return pl.pallas_call(
        functools.partial(_attn_sample_kernel, lam_init=lam_init),
        grid=(nb_,),
        in_specs=[
            pl.BlockSpec((ts, QK_COLS), lambda b: (b, 0)),
            pl.BlockSpec((None, past, QK_COLS), lambda b: (b, 0, 0)),
            pl.BlockSpec((None, past, V_COLS), lambda b: (b, 0, 0)),
            pl.BlockSpec((ts, QK_COLS), lambda b: (b, 0)),
            pl.BlockSpec((ts, V_COLS), lambda b: (b, 0)),
            _const_spec((N_HEADS, ts, past)), _const_spec((N_HEADS, ts, ts)),
            vec, vec, vec, vec, _const_spec((1, V_DIM)),
        ],
        out_specs=pl.BlockSpec((ts, V_COLS), lambda b: (b, 0)),
        out_shape=jax.ShapeDtypeStruct((nb_ * ts, V_COLS), bf16),
        compiler_params=_cparams(("parallel",)),
        name="attn_sample",
    )(q, cache_k, cache_v, kb, vb, nbp, nbn, *lam_vecs, g_subln)


def _conv_lru_kernel(lx_ref, gl_ref, c0_ref, h0_ref, wc_ref, bc_ref, wa_ref, ba_ref, wx_ref, bx_ref, lam_ref,
                     hsg_ref, cout_ref, hout_ref, xbuf, a_sc, b_sc, hcar, *, tt, first_pos_is_zero):
    t = pl.program_id(1)

    @pl.when(t == 0)
    def _():
        xbuf[0:SUBLANES] = c0_ref[...]
        hcar[...] = jnp.broadcast_to(h0_ref[...], hcar.shape)

    xbuf[SUBLANES:SUBLANES + tt] = lx_ref[...]
    ngrp = tt // SUBLANES
    xg = xbuf[...].reshape(ngrp + 1, SUBLANES, LRU_WIDTH)
    sub = lax.broadcasted_iota(jnp.int32, (ngrp, SUBLANES, LRU_WIDTH), 1)
    xc = xg[1:] * wc_ref[CONV_W - 1:CONV_W, :] + bc_ref[...]
    for s in range(1, CONV_W):
        rolled = pltpu.roll(xg, s, axis=1)
        shifted = jnp.where(sub >= s, rolled[1:], rolled[:-1])
        xc = xc + shifted * wc_ref[CONV_W - 1 - s:CONV_W - s, :]
    xc = xc.reshape(tt, LRU_WIDTH)
    tail = xbuf[tt:tt + SUBLANES, :]
    xbuf[0:SUBLANES] = tail
    cout_ref[...] = tail

    xcb = xc.astype(bf16)

    def block_diag(w_ref, b_ref):
        cols = [jnp.dot(xcb[:, n * LRU_BLOCK_W:(n + 1) * LRU_BLOCK_W], w_ref[n], preferred_element_type=f32)
                for n in range(LRU_BLOCKS)]
        return jnp.concatenate(cols, axis=1) + b_ref[...]

    r = _sigmoid(block_diag(wa_ref, ba_ref))
    ig = _sigmoid(block_diag(wx_ref, bx_ref))
    z = -lam_ref[...]
    softplus = jnp.maximum(z, 0.0) + jnp.log1p(jnp.exp(-jnp.abs(z)))
    log_a = -LRU_C * r * softplus
    a = jnp.exp(log_a)
    th = jnp.tanh(log_a)
    m2 = -2.0 * th / (1.0 - th)
    mult = jnp.where(m2 > 0.0, m2 * lax.rsqrt(m2), 0.0)
    if first_pos_is_zero:
        row = lax.broadcasted_iota(jnp.int32, (tt, LRU_WIDTH), 0)
        mult = jnp.where((row == 0) & (t == 0), 1.0, mult)
    bv = mult * ig * xc

    groups = (tt // SUBLANES, SUBLANES, LRU_WIDTH)
    a = a.reshape(groups)
    bv = bv.reshape(groups)
    sub = lax.broadcasted_iota(jnp.int32, groups, 1)
    d = 1
    while d < SUBLANES:
        valid = sub >= d
        a_s = pltpu.roll(a, d, axis=1)
        b_s = pltpu.roll(bv, d, axis=1)
        bv = jnp.where(valid, a * b_s + bv, bv)
        a = jnp.where(valid, a * a_s, a)
        d *= 2
    a_sc[...] = a.reshape(tt, LRU_WIDTH)
    b_sc[...] = bv.reshape(tt, LRU_WIDTH)

    def group(g, h):
        sl = pl.ds(pl.multiple_of(g * SUBLANES, SUBLANES), SUBLANES)
        hg = a_sc[sl, :] * h + b_sc[sl, :]
        b_sc[sl, :] = hg
        return jnp.broadcast_to(hg[SUBLANES - 1:SUBLANES, :], hg.shape)

    h = lax.fori_loop(0, tt // SUBLANES, group, hcar[...], unroll=4)
    hcar[...] = h
    hout_ref[...] = h[0:1, :]
    hsg_ref[...] = (b_sc[...] * gl_ref[...]).astype(hsg_ref.dtype)


def _conv_lru(lx, gl, conv0, h0, w_conv, b_conv, wa, ba, wx, bx, lru_lambda, nb_, tlen, tt, first_pos_is_zero):
    nt = tlen // tt
    rows = pl.BlockSpec((tt, LRU_WIDTH), lambda b, t: (b * nt + t, 0))
    vecw = _const_spec((1, LRU_WIDTH))
    wblk = _const_spec((LRU_BLOCKS, LRU_BLOCK_W, LRU_BLOCK_W))
    return pl.pallas_call(
        functools.partial(_conv_lru_kernel, tt=tt, first_pos_is_zero=first_pos_is_zero),
        grid=(nb_, nt),
        in_specs=[rows, rows,
                  pl.BlockSpec((None, SUBLANES, LRU_WIDTH), lambda b, t: (b, 0, 0)),
                  pl.BlockSpec((None, 1, LRU_WIDTH), lambda b, t: (b, 0, 0)),
                  _const_spec((CONV_W, LRU_WIDTH)), vecw, wblk, vecw, wblk, vecw, vecw],
        out_specs=[rows,
                   pl.BlockSpec((None, SUBLANES, LRU_WIDTH), lambda b, t: (b, 0, 0)),
                   pl.BlockSpec((None, 1, LRU_WIDTH), lambda b, t: (b, 0, 0))],
        out_shape=[jax.ShapeDtypeStruct((nb_ * tlen, LRU_WIDTH), bf16),
                   jax.ShapeDtypeStruct((nb_, SUBLANES, LRU_WIDTH), f32),
                   jax.ShapeDtypeStruct((nb_, 1, LRU_WIDTH), f32)],
        scratch_shapes=[pltpu.VMEM((tt + SUBLANES, LRU_WIDTH), f32), pltpu.VMEM((tt, LRU_WIDTH), f32),
                        pltpu.VMEM((tt, LRU_WIDTH), f32), pltpu.VMEM((SUBLANES, LRU_WIDTH), f32)],
        compiler_params=_cparams(("arbitrary", "arbitrary")),
        name="conv_lru",
    )(lx, gl, conv0, h0, w_conv, b_conv, wa, ba, wx, bx, lru_lambda)


def _merge_kernel(x_ref, att_ref, hsg_ref, gt_ref, wa_ref, wb_ref, wo_ref, gf_ref, wr_ref, br_ref,
                  x1_ref, h2_ref, comb_ref, *, parts):
    rows_per = x_ref.shape[0] // parts
    for part in range(parts):
        rows = slice(part * rows_per, (part + 1) * rows_per)
        bra = jnp.dot(att_ref[rows, :], wa_ref[...], preferred_element_type=f32)
        brb = jnp.dot(hsg_ref[rows, :], wb_ref[...], preferred_element_type=f32)
        gt = gt_ref[rows, :]
        m = (gt[:, :D_MODEL] * bra + gt[:, D_MODEL:] * brb).astype(bf16)
        x1 = x_ref[rows, :] + jnp.dot(m, wo_ref[...], preferred_element_type=f32)
        x1_ref[rows, :] = x1
        h2 = _rms(x1, gf_ref[...])
        h2_hi = h2.astype(bf16)
        h2_ref[rows, :] = h2_hi

        h2_lo = (h2 - h2_hi.astype(f32)).astype(bf16)
        prods = (jnp.dot(h2_hi, wr_ref[...], preferred_element_type=f32)
                 + jnp.dot(h2_lo, wr_ref[...], preferred_element_type=f32))
        logits = prods[:, :LANES] + prods[:, LANES:] + br_ref[...]
        lane = lax.broadcasted_iota(jnp.int32, logits.shape, 1)
        lanef = lane.astype(f32)
        low = jnp.float32(-3.0e38)
        is_grp = (lane >= N_EXPERTS) & (lane < N_EXPERTS + N_GROUPS)
        gl = jnp.where(is_grp, logits, low)
        gmax = jnp.max(gl, axis=1, keepdims=True)
        gidx = jnp.min(jnp.where(is_grp & (gl == gmax), lanef, 1.0e3), axis=1, keepdims=True) - N_EXPERTS
        gden = jnp.sum(jnp.where(is_grp, jnp.exp(gl - gmax), 0.0), axis=1, keepdims=True)
        g_w = 1.0 / gden
        lo = gidx * EXPERTS_PER_GROUP
        in_sel = (lanef >= lo) & (lanef < lo + EXPERTS_PER_GROUP)
        sel = jnp.where(in_sel, logits, low)
        v1 = jnp.max(sel, axis=1, keepdims=True)
        i1 = jnp.min(jnp.where(in_sel & (sel == v1), lanef, 1.0e3), axis=1, keepdims=True)
        in_sel2 = in_sel & (lanef != i1)
        sel2 = jnp.where(in_sel2, logits, low)
        v2 = jnp.max(sel2, axis=1, keepdims=True)
        i2 = jnp.min(jnp.where(in_sel2 & (sel2 == v2), lanef, 1.0e3), axis=1, keepdims=True)
        e2 = jnp.exp(v2 - v1)
        w1 = g_w / (1.0 + e2)
        w2 = g_w * e2 / (1.0 + e2)
        comb_ref[rows, :] = jnp.where(lanef == i1, w1, 0.0) + jnp.where(lanef == i2, w2, 0.0)


def _merge(x, att, hsg, gt, wa, wb, wo, g_ffn, w_router, b_router, tm):
    n = x.shape[0]
    row = lambda w: pl.BlockSpec((tm, w), lambda i: (i, 0))
    sq = _const_spec((D_MODEL, D_MODEL))
    return pl.pallas_call(
        functools.partial(_merge_kernel, parts=2 if tm % 512 == 0 else 1),
        grid=(n // tm,),
        in_specs=[row(D_MODEL), row(V_COLS), row(LRU_WIDTH), row(2 * D_MODEL), sq, sq, sq,
                  _const_spec((1, D_MODEL)), _const_spec((D_MODEL, 2 * LANES)), _const_spec((1, LANES))],
        out_specs=[row(D_MODEL), row(D_MODEL), row(LANES)],
        out_shape=[jax.ShapeDtypeStruct((n, D_MODEL), f32), jax.ShapeDtypeStruct((n, D_MODEL), bf16),
                   jax.ShapeDtypeStruct((n, LANES), f32)],
        compiler_params=_cparams(("parallel",)),
        name="merge_router",
    )(x, att, hsg, gt, wa, wb, wo, g_ffn, w_router, b_router)


EXPERTS_PER_STEP = 8


MOE_CHUNK = 128


def _moe_kernel(h_ref, comb_ref, wg_ref, wu_ref, wd_ref, y_ref, pt_sc, x_sc, c_sc, ys_sc, seg_sc):
    e = pl.program_id(1)
    tm = h_ref.shape[0]
    lane1 = lax.broadcasted_iota(jnp.int32, (1, LANES), 1)

    @pl.when(e == 0)
    def _():
        comb = comb_ref[...]
        ex = lax.broadcasted_iota(jnp.int32, (LANES, LANES), 0)
        gr = lax.broadcasted_iota(jnp.int32, (LANES, LANES), 1)
        in_group = jnp.where((ex // EXPERTS_PER_GROUP == gr) & (ex < N_EXPERTS), 1.0, 0.0).astype(bf16)
        gsum = jnp.dot(comb.astype(bf16), in_group, preferred_element_type=f32)
        member = jnp.where(gsum > 0.0, 1.0, 0.0)
        r_i = lax.broadcasted_iota(jnp.int32, (tm, tm), 0)
        c_i = lax.broadcasted_iota(jnp.int32, (tm, tm), 1)
        tri = jnp.where(gr <= ex, 1.0, 0.0).astype(bf16)
        member_b = member.astype(bf16)
        totals = jnp.zeros((1, LANES), f32)
        blocks = []
        for blk in range(tm // LANES):
            part = jnp.dot(tri, member_b[blk * LANES:(blk + 1) * LANES], preferred_element_type=f32) + totals
            blocks.append(part)
            totals = part[LANES - 1:LANES, :]
        rank = jnp.concatenate(blocks, axis=0)
        offs = jnp.zeros((1, LANES), f32)
        for g in range(1, N_GROUPS):
            prev = jnp.sum(jnp.where(lane1 == g - 1, totals, 0.0), axis=1, keepdims=True)
            offs = offs + jnp.where(lane1 >= g, prev, 0.0)
        seg_sc[0:1, :] = offs.astype(jnp.int32)
        seg_sc[1:2, :] = totals.astype(jnp.int32)
        pos = jnp.sum(member * (offs + rank - 1.0), axis=1, keepdims=True)
        pos_row = jnp.broadcast_to(pos, (tm, LANES)).T[0:1, :].astype(jnp.int32)
        perm = jnp.where(r_i == pos_row, 1.0, 0.0).astype(bf16)
        pt_sc[...] = jnp.where(c_i == pos.astype(jnp.int32), 1.0, 0.0).astype(bf16)
        comb_hi = comb.astype(bf16)
        comb_lo = (comb - comb_hi.astype(f32)).astype(bf16)
        srt = jnp.dot(perm, jnp.concatenate([h_ref[...], comb_hi, comb_lo], axis=1), preferred_element_type=f32)
        x_sc[...] = srt[:, :D_MODEL].astype(bf16)
        c_sc[...] = srt[:, D_MODEL:D_MODEL + LANES] + srt[:, D_MODEL + LANES:]
        ys_sc[...] = jnp.zeros(ys_sc.shape, f32)

    g = (e * EXPERTS_PER_STEP) // EXPERTS_PER_GROUP
    off = jnp.sum(jnp.where(lane1 == g, seg_sc[0:1, :], 0))
    cnt = jnp.sum(jnp.where(lane1 == g, seg_sc[1:2, :], 0))
    lane = lax.broadcasted_iota(jnp.int32, (MOE_CHUNK, LANES), 1)

    def chunk(k, carry):
        rows = pl.ds(pl.multiple_of(k * MOE_CHUNK, MOE_CHUNK), MOE_CHUNK)
        x = x_sc[rows, :]
        cw = c_sc[rows, :]
        hidden = []
        for j in range(EXPERTS_PER_STEP):
            gate = jnp.dot(x, wg_ref[j], preferred_element_type=f32)
            up = jnp.dot(x, wu_ref[j], preferred_element_type=f32)
            c = jnp.sum(jnp.where(lane == e * EXPERTS_PER_STEP + j, cw, 0.0), axis=1, keepdims=True)
            hidden.append((gate * _sigmoid(gate) * up * c).astype(bf16))
        ys_sc[rows, :] += jnp.dot(jnp.concatenate(hidden, axis=1), wd_ref[...], preferred_element_type=f32)
        return carry

    lax.fori_loop(off // MOE_CHUNK, (off + cnt + MOE_CHUNK - 1) // MOE_CHUNK, chunk, 0)

    @pl.when(e == pl.num_programs(1) - 1)
    def _():
        y = jnp.dot(pt_sc[...], ys_sc[...].astype(bf16), preferred_element_type=f32)
        y_ref[...] = y.astype(y_ref.dtype)


def _moe(h2, comb, wg, wu, wd_rows, tm):
    n = h2.shape[0]
    per = EXPERTS_PER_STEP
    assert EXPERTS_PER_GROUP % per == 0 and tm % MOE_CHUNK == 0
    return pl.pallas_call(
        _moe_kernel,
        grid=(n // tm, N_EXPERTS // per),
        in_specs=[pl.BlockSpec((tm, D_MODEL), lambda i, e: (i, 0)),
                  pl.BlockSpec((tm, LANES), lambda i, e: (i, 0)),
                  pl.BlockSpec((per, D_MODEL, D_EXPERT), lambda i, e: (e, 0, 0)),
                  pl.BlockSpec((per, D_MODEL, D_EXPERT), lambda i, e: (e, 0, 0)),
                  pl.BlockSpec((per * D_EXPERT, D_MODEL), lambda i, e: (e, 0))],
        out_specs=pl.BlockSpec((tm, D_MODEL), lambda i, e: (i, 0)),
        out_shape=jax.ShapeDtypeStruct((n, D_MODEL), bf16),
        scratch_shapes=[pltpu.VMEM((tm, tm), bf16), pltpu.VMEM((tm, D_MODEL), bf16),
                        pltpu.VMEM((tm, LANES), f32), pltpu.VMEM((tm, D_MODEL), f32),
                        pltpu.VMEM((SUBLANES, LANES), jnp.int32)],
        compiler_params=_cparams(("arbitrary", "arbitrary")),
        name="moe",
    )(h2, comb, wg, wu, wd_rows)


def _ple_kernel(x1_ref, y_ref, p_ref, gp_ref, wpg_ref, bpg_ref, wpi_ref, gfin_ref, o_ref):
    x2 = x1_ref[...] + y_ref[...]
    hp = _rms(x2, gp_ref[...]).astype(bf16)
    pg = _sigmoid(jnp.dot(hp, wpg_ref[...], preferred_element_type=f32) + bpg_ref[...])
    pin = jnp.dot(p_ref[...].astype(bf16), wpi_ref[...], preferred_element_type=f32)
    x3 = x2 + pg * pin
    o_ref[...] = _rms(x3, gfin_ref[...])


def _ple(x1, y, p, g_ple, wpg, bpg, wpi, g_final, tm):
    n = x1.shape[0]
    row = lambda w: pl.BlockSpec((tm, w), lambda i: (i, 0))
    vec = _const_spec((1, D_MODEL))
    return pl.pallas_call(
        _ple_kernel,
        grid=(n // tm,),
        in_specs=[row(D_MODEL), row(D_MODEL), row(PLE_DIM), vec, _const_spec((D_MODEL, D_MODEL)), vec,
                  _const_spec((PLE_DIM, D_MODEL)), vec],
        out_specs=row(D_MODEL),
        out_shape=jax.ShapeDtypeStruct((n, D_MODEL), f32),
        compiler_params=_cparams(("parallel",)),
        name="ple_final",
    )(x1, y, p, g_ple, wpg, bpg, wpi, g_final)


PROJ_ROWS = 256
ATTN_KEY_TILE = 512
MERGE_ROWS = 512
MOE_ROWS = 1024


def _tile(n, pref):
    return pref if n % pref == 0 else n


def _stream(x, p, conv0, h0, cache, li, rel_bias, w):
    nb_, tlen, _ = x.shape
    n = nb_ * tlen
    xf = x.reshape(n, D_MODEL)
    lam_init = 0.8 - 0.6 * math.exp(-0.3 * li)
    tk = None if cache is not None else ATTN_KEY_TILE
    q, k, kb, v, vb, lx, gl, gt = _in_proj(xf, w["g_mix"], w["w_in"], w["b_merge"], _tile(n, PROJ_ROWS), tk)
    if cache is None:
        att = _attn_prompt(q, kb, vb, rel_bias, w["lam_vecs"], w["g_subln"], lam_init, 2 * tk, tk)
    else:
        att = _attn_sample(q, kb, vb, cache[0], cache[1], rel_bias, w["lam_vecs"], w["g_subln"], lam_init,
                           nb_, tlen)
    conv_pad = jnp.pad(conv0, ((0, 0), (SUBLANES - (CONV_W - 1), 0), (0, 0)))
    hsg, cout, hout = _conv_lru(lx, gl, conv_pad, h0[:, None, :], w["w_conv"], w["b_conv"], w["w_rg_a"],
                                w["b_rg_a"], w["w_rg_x"], w["b_rg_x"], w["lru_lambda"], nb_, tlen,
                                _tile(tlen, PROJ_ROWS), cache is None)
    tm = _tile(n, MERGE_ROWS)
    x1, h2, comb = _merge(xf, att, hsg, gt, w["w_attn_br"], w["w_lru_br"], w["w_out"], w["g_ffn"],
                          w["w_router"], w["b_router"], tm)
    y = _moe(h2, comb, w["w_e_gate"], w["w_e_up"], w["w_e_down"], _tile(n, MOE_ROWS))
    out = _ple(x1, y, p.reshape(n, PLE_DIM), w["g_ple"], w["w_ple_gate"], w["b_ple_gate"], w["w_ple_in"],
               w["g_final"], tm)
    return out, k, v, cout[:, SUBLANES - (CONV_W - 1):, :], hout[:, 0, :]


def kernel(x_prompt, x_sample, cache_k, cache_v, state_conv, state_lru, p_prompt, p_sample, rel_bias, g_mix, w_in, b_merge, lam_q1, lam_k1, lam_q2, lam_k2, g_subln, w_attn_br, w_conv, b_conv, w_rg_a, b_rg_a, w_rg_x, b_rg_x, lru_lambda, w_lru_br, w_out, g_ffn, w_grp, b_grp, w_rt, b_rt, w_e_gate, w_e_up, w_e_down, g_ple, w_ple_gate, b_ple_gate, w_ple_in, g_final):
    depth = w_in.shape[0]
    assert depth == 1, "the final norm is fused into the single layer's last stage"
    bp, tp, _ = x_prompt.shape
    bs, ts, _ = x_sample.shape
    past = cache_k.shape[2]
    li = 0
    row = lambda a: a[li].reshape(1, -1).astype(f32)
    w_router = jnp.concatenate(
        [jnp.transpose(w_rt[li], (1, 0, 2)).reshape(D_MODEL, N_EXPERTS), w_grp[li]], axis=1).astype(f32)
    w_router = jnp.pad(w_router, ((0, 0), (0, LANES - N_EXPERTS - N_GROUPS)))
    w_router_hi = w_router.astype(bf16)
    w_router = jnp.concatenate([w_router_hi, (w_router - w_router_hi.astype(f32)).astype(bf16)], axis=1)
    b_router = jnp.pad(jnp.concatenate([b_rt[li].reshape(-1), b_grp[li]]).astype(f32),
                       (0, LANES - N_EXPERTS - N_GROUPS)).reshape(1, LANES)
    w = dict(
        g_mix=row(g_mix), w_in=w_in[li].astype(bf16), b_merge=row(b_merge),
        lam_vecs=(row(lam_q1), row(lam_k1), row(lam_q2), row(lam_k2)), g_subln=row(g_subln),
        w_attn_br=w_attn_br[li].astype(bf16), w_conv=w_conv[li].astype(f32), b_conv=row(b_conv),
        w_rg_a=w_rg_a[li].astype(bf16), b_rg_a=row(b_rg_a), w_rg_x=w_rg_x[li].astype(bf16), b_rg_x=row(b_rg_x),
        lru_lambda=row(lru_lambda), w_lru_br=w_lru_br[li].astype(bf16), w_out=w_out[li].astype(bf16),
        g_ffn=row(g_ffn), w_router=w_router, b_router=b_router,
        w_e_gate=w_e_gate[li].astype(bf16), w_e_up=w_e_up[li].astype(bf16), w_e_down=w_e_down[li].reshape(N_EXPERTS * D_EXPERT, D_MODEL).astype(bf16),
        g_ple=row(g_ple), w_ple_gate=w_ple_gate[li].astype(bf16), b_ple_gate=row(b_ple_gate),
        w_ple_in=w_ple_in[li].astype(bf16), g_final=g_final.reshape(1, -1).astype(f32),
    )
    conv0 = jnp.zeros((bp, CONV_W - 1, LRU_WIDTH), f32)
    h0 = jnp.zeros((bp, LRU_WIDTH), f32)
    yp, kp, vp, cp, hp = _stream(x_prompt, p_prompt[li], conv0, h0, None, li, rel_bias, w)
    cache = (cache_k[li].reshape(bs, past, QK_COLS), cache_v[li].reshape(bs, past, V_COLS))
    ys, ks, vs, cs, hs = _stream(x_sample, p_sample[li], state_conv[li], state_lru[li], cache, li, rel_bias, w)
    return (yp.reshape(bp, tp, D_MODEL), ys.reshape(bs, ts, D_MODEL),
            kp.reshape(1, bp, tp, N_HEADS, 2, HEAD_DIM), vp.reshape(1, bp, tp, N_HEADS, V_DIM),
            cp[None], hp[None],
            ks.reshape(1, bs, ts, N_HEADS, 2, HEAD_DIM), vs.reshape(1, bs, ts, N_HEADS, V_DIM),
            cs[None], hs[None])
```

```python
import functools
import math

import jax
import jax.numpy as jnp
from jax import lax
from jax.experimental import pallas as pl
from jax.experimental.pallas import tpu as pltpu

f32 = jnp.float32
bf16 = jnp.bfloat16

D_MODEL = 1024
N_HEADS = 8
HEAD_DIM = 64
V_DIM = 128
QK_COLS = N_HEADS * 2 * HEAD_DIM
V_COLS = N_HEADS * V_DIM
LRU_WIDTH = 1024
LRU_BLOCKS = 8
LRU_BLOCK_W = LRU_WIDTH // LRU_BLOCKS
CONV_W = 4
LRU_C = 8.0
CHUNK = 64
N_BUCKETS = 32
MAX_DISTANCE = 128
N_GROUPS = 4
EXPERTS_PER_GROUP = 8
N_EXPERTS = N_GROUPS * EXPERTS_PER_GROUP
D_EXPERT = 256
PLE_DIM = 256
EPS = 1e-6
NEG_INF = -1e30
LOG2E = math.log2(math.e)
ONES_ROWS = 16
LANES = 128
SUBLANES = 8
VMEM_LIMIT = 56 * 1024 * 1024

C_Q, C_K, C_V = 0, QK_COLS, 2 * QK_COLS
C_LX = 2 * QK_COLS + V_COLS
C_LY = C_LX + LRU_WIDTH
C_G = C_LY + LRU_WIDTH
IN_COLS = C_G + 2 * D_MODEL


def _cparams(sem):
    return pltpu.CompilerParams(dimension_semantics=sem, vmem_limit_bytes=VMEM_LIMIT)


def _const_spec(shape):
    nd = len(shape)
    return pl.BlockSpec(shape, lambda *_: (0,) * nd, pipeline_mode=pl.Buffered(1))


def _rms(x, g):
    return x * lax.rsqrt(jnp.mean(x * x, axis=-1, keepdims=True) + EPS) * g


def _sigmoid(x):
    return 1.0 / (1.0 + jnp.exp(-x))


def _gelu_tanh(x):
    c = math.sqrt(2.0 / math.pi)
    return 0.5 * x * (1.0 + jnp.tanh(c * (x + 0.044715 * (x * x * x))))


def _in_proj_kernel(x_ref, g_ref, w_ref, bm_ref, q_ref, k_ref, kb_ref, v_ref, vb_ref, lx_ref, gl_ref, gt_ref,
                    *, transposed_v):
    h = _rms(x_ref[...], g_ref[...]).astype(bf16)

    def proj(lo, width):
        return jnp.dot(h, w_ref[:, lo:lo + width], preferred_element_type=f32)

    q_ref[...] = (proj(C_Q, QK_COLS) * (HEAD_DIM ** -0.5 * LOG2E)).astype(bf16)
    k = proj(C_K, QK_COLS)
    k_ref[...] = k
    kb_ref[...] = k.astype(bf16)
    v = proj(C_V, V_COLS)
    v_ref[...] = v
    if transposed_v:
        for hd in range(N_HEADS):
            vb_ref[hd, :V_DIM, :] = v[:, hd * V_DIM:(hd + 1) * V_DIM].T.astype(bf16)
            vb_ref[hd, V_DIM:, :] = jnp.ones((ONES_ROWS, v.shape[0]), bf16)
    else:
        vb_ref[...] = v.astype(bf16)
    lx_ref[...] = proj(C_LX, LRU_WIDTH)
    gl_ref[...] = _gelu_tanh(proj(C_LY, LRU_WIDTH))
    gt_ref[...] = _sigmoid(proj(C_G, 2 * D_MODEL) + bm_ref[...])


def _in_proj(x, g_mix, w_in_b, b_merge, tm, tq=None):
    n = x.shape[0]
    row = lambda w: pl.BlockSpec((tm, w), lambda i: (i, 0))
    if tq is None:
        vb_shape, vb_spec = jax.ShapeDtypeStruct((n, V_COLS), bf16), row(V_COLS)
    else:
        per = tq // tm
        vb_shape = jax.ShapeDtypeStruct((N_HEADS, n // tq, V_DIM + ONES_ROWS, tq), bf16)
        vb_spec = pl.BlockSpec((N_HEADS, None, V_DIM + ONES_ROWS, tm), lambda i: (0, i // per, 0, i % per))
    outs = [
        jax.ShapeDtypeStruct((n, QK_COLS), bf16),
        jax.ShapeDtypeStruct((n, QK_COLS), f32),
        jax.ShapeDtypeStruct((n, QK_COLS), bf16),
        jax.ShapeDtypeStruct((n, V_COLS), f32),
        vb_shape,
        jax.ShapeDtypeStruct((n, LRU_WIDTH), f32),
        jax.ShapeDtypeStruct((n, LRU_WIDTH), f32),
        jax.ShapeDtypeStruct((n, 2 * D_MODEL), f32),
    ]
    return pl.pallas_call(
        functools.partial(_in_proj_kernel, transposed_v=tq is not None),
        grid=(n // tm,),
        in_specs=[row(D_MODEL), _const_spec((1, D_MODEL)), _const_spec((D_MODEL, IN_COLS)),
                  _const_spec((1, 2 * D_MODEL))],
        out_specs=[row(QK_COLS), row(QK_COLS), row(QK_COLS), row(V_COLS), vb_spec,
                   row(LRU_WIDTH), row(LRU_WIDTH), row(2 * D_MODEL)],
        out_shape=outs,
        compiler_params=_cparams(("parallel",)),
        name="in_proj",
    )(x, g_mix, w_in_b, b_merge)


def _rel_bucket(rel):
    half = N_BUCKETS // 2
    max_exact = half // 2
    ret = jnp.where(rel > 0, half, 0)
    n = jnp.abs(rel)
    nf = jnp.maximum(n, 1).astype(f32)
    large = max_exact + (jnp.log(nf / max_exact) / math.log(MAX_DISTANCE / max_exact)
                         * (half - max_exact)).astype(jnp.int32)
    large = jnp.minimum(large, half - 1)
    return ret + jnp.where(n < max_exact, n, large)


def _toeplitz(t, nq, nk):
    nh, length = t.shape
    a = jnp.broadcast_to(t[:, None, :], (nh, nq, length))
    a = jnp.pad(a, ((0, 0), (0, 0), (0, 1))).reshape(nh, nq * (length + 1))
    a = a[:, :nq * length].reshape(nh, nq, length)
    return a[:, :, nq - 1:nq - 1 + nk]


def _bias_table(rel_bias, qpos0, nq, kpos0, nk, shift):
    rel = jnp.arange(-(nq - 1), nk) + (kpos0 - qpos0)
    t = (rel_bias.astype(f32)[_rel_bucket(rel)].T - shift[:, None]) * LOG2E
    qpos = qpos0 + jnp.arange(nq)
    kpos = kpos0 + jnp.arange(nk)
    mask = (kpos[None, :] // CHUNK) <= (qpos[:, None] // CHUNK)
    return jnp.where(mask[None], _toeplitz(t, nq, nk), NEG_INF)


def _bias_ring(rel_bias, offset, tq, shift):
    r = jnp.arange(2 * tq)
    x = jnp.where(r < tq, r, r - 2 * tq)
    t = (rel_bias.astype(f32)[_rel_bucket(offset - x)].T - shift[:, None]) * LOG2E
    return t[:, None, :]


def _lam_value(lq1, lk1, lq2, lk2, lam_init):
    s1 = jnp.sum(lq1[...] * lk1[...], axis=-1, keepdims=True)
    s2 = jnp.sum(lq2[...] * lk2[...], axis=-1, keepdims=True)
    return jnp.exp(s1) - jnp.exp(s2) + lam_init


def _split_maps(q):
    lane = lax.broadcasted_iota(jnp.int32, q.shape, 1)
    zero = jnp.zeros_like(q)
    return jnp.where(lane < HEAD_DIM, q, zero), jnp.where(lane >= HEAD_DIM, q, zero)


def _qk(qz, kt):
    return lax.dot_general(qz, kt, (((1,), (1,)), ((), ())), preferred_element_type=f32)


def _finish_heads(o1, o2, lam, gs, lam_init):
    att = o1 - lam * o2
    return _rms(att, gs) * (1.0 - lam_init)


def _attn_prompt_kernel(q_ref, k_ref, vt_ref, ring_ref, lq1, lk1, lq2, lk2, gs_ref, o_ref,
                        nb_sc, s_sc, p_sc, al_sc, tm_sc, m_sc, acc_sc, *, tq, tk, lam_init):
    i = pl.program_id(1)
    ring_len = ring_ref.shape[-1]

    def bias_tile(ring):
        return pltpu.roll(jnp.broadcast_to(ring, (tk, ring_len)), 0, 1, stride=1, stride_axis=0)[:, :tq]

    @pl.when(i == 0)
    def _():
        kk = lax.broadcasted_iota(jnp.int32, (tk, tq), 0)
        qq = lax.broadcasted_iota(jnp.int32, (tk, tq), 1)
        nb_sc[0] = jnp.where((kk // CHUNK) <= (qq // CHUNK), bias_tile(ring_ref[1]), NEG_INF)
        nb_sc[1] = jnp.where(((kk + tk) // CHUNK) <= (qq // CHUNK), bias_tile(ring_ref[2]), NEG_INF)
        nb_sc[2] = jnp.full((tk, tq), NEG_INF, f32)

    @pl.when(i == 1)
    def _():
        nb_sc[2] = nb_sc[1]
        nb_sc[1] = nb_sc[0]
        nb_sc[0] = bias_tile(ring_ref[0])

    qt = q_ref[...].astype(f32).T
    row = lax.broadcasted_iota(jnp.int32, qt.shape, 0)
    qz = (jnp.where(row < HEAD_DIM, qt, 0.0).astype(bf16), jnp.where(row >= HEAD_DIM, qt, 0.0).astype(bf16))

    m_sc[...] = jnp.full(m_sc.shape, NEG_INF, f32)
    acc_sc[...] = jnp.zeros(acc_sc.shape, f32)

    def qk(slot, tile, bias, cols=slice(None)):
        kt = k_ref[pl.ds(pl.multiple_of(tile * tk, tk), tk), :]
        for mi in range(2):
            s = jnp.dot(kt, qz[mi][:, cols], preferred_element_type=f32)
            s = s if bias is None else s + bias[:, cols]
            s_sc[mi, slot, :, cols] = s
            tm_sc[mi, slot, :, cols] = jnp.max(s, axis=0, keepdims=True)

    def softmax(slot, cols=slice(None)):
        for mi in range(2):
            s = s_sc[mi, slot, :, cols]
            m_prev = m_sc[mi, :, cols]
            m_new = jnp.maximum(m_prev, tm_sc[mi, slot, :, cols])
            al_sc[mi, slot, :, cols] = jnp.exp2(m_prev - m_new)
            p_sc[mi, slot, :, cols] = jnp.exp2(s - m_new).astype(bf16)
            m_sc[mi, :, cols] = m_new

    def pv(slot, vtile, cols=slice(None)):
        va = vt_ref[vtile]
        for mi in range(2):
            acc_sc[mi, :, cols] = (acc_sc[mi, :, cols] * al_sc[mi, slot, :, cols]
                                   + jnp.dot(va, p_sc[mi, slot, :, cols], preferred_element_type=f32))

    late = slice(tq // 2, tq)

    per = tq // tk
    a0 = jnp.maximum(per * i - 1, 0)
    n_far = jnp.maximum(per * i - 1, 0)
    last_tile = vt_ref.shape[0] - 1

    def far_k(f):
        return jnp.clip(f, 0, jnp.maximum(n_far - 1, 0))

    qk(0, a0, nb_sc[0])
    qk(1, a0 + 1, nb_sc[1])
    softmax(0)
    qk(0, jnp.minimum(a0 + 2, last_tile), nb_sc[2], late)
    softmax(1)
    pv(0, a0)

    @pl.when(i == 0)
    def _():
        pv(1, a0 + 1)

    @pl.when(i > 0)
    def _():
        qk(1, 0, None)
        softmax(0, late)
        pv(1, a0 + 1)
        qk(0, far_k(1), None)
        softmax(1)
        pv(0, a0 + 2, late)

        def far_pair(p, carry):
            qk(1, 2 * p + 2, None)
            softmax(0)
            pv(1, 2 * p)
            qk(0, far_k(2 * p + 3), None)
            softmax(1)
            pv(0, 2 * p + 1)
            return carry

        lax.fori_loop(0, (n_far - 1) // 2, far_pair, 0)
        pv(1, n_far - 1)

    a1 = acc_sc[0]
    a2 = acc_sc[1]
    o1 = a1[:V_DIM] / a1[V_DIM:V_DIM + 1]
    o2 = a2[:V_DIM] / a2[V_DIM:V_DIM + 1]
    lam = _lam_value(lq1, lk1, lq2, lk2, lam_init)
    att = (o1 - lam * o2).T
    o_ref[...] = (_rms(att, gs_ref[...]) * (1.0 - lam_init)).astype(o_ref.dtype)


def _attn_prompt(q, kb, vt, rel_bias, lam_vecs, g_subln, lam_init, tq, tk):
    t = q.shape[0]
    nq = t // tq
    assert tq == 2 * tk and t % tq == 0
    far = rel_bias.astype(f32)[_rel_bucket(jnp.asarray(-4 * MAX_DISTANCE))]
    rings = jnp.stack([_bias_ring(rel_bias, d, tq, far) for d in (-tk, 0, tk)], axis=1)
    vec = _const_spec((1, HEAD_DIM))
    return pl.pallas_call(
        functools.partial(_attn_prompt_kernel, tq=tq, tk=tk, lam_init=lam_init),
        grid=(N_HEADS, nq),
        in_specs=[
            pl.BlockSpec((tq, 2 * HEAD_DIM), lambda h, i: (i, h)),
            pl.BlockSpec((t, 2 * HEAD_DIM), lambda h, i: (0, h)),
            pl.BlockSpec((None, t // tk, V_DIM + ONES_ROWS, tk), lambda h, i: (h, 0, 0, 0)),
            pl.BlockSpec((None, 3, 1, 2 * tq), lambda h, i: (h, 0, 0, 0)),
            vec, vec, vec, vec, _const_spec((1, V_DIM)),
        ],
        out_specs=pl.BlockSpec((tq, V_DIM), lambda h, i: (i, h)),
        out_shape=jax.ShapeDtypeStruct((t, V_COLS), bf16),
        scratch_shapes=[pltpu.VMEM((3, tk, tq), f32), pltpu.VMEM((2, 2, tk, tq), f32),
                        pltpu.VMEM((2, 2, tk, tq), bf16), pltpu.VMEM((2, 2, 1, tq), f32),
                        pltpu.VMEM((2, 2, 1, tq), f32), pltpu.VMEM((2, 1, tq), f32),
                        pltpu.VMEM((2, V_DIM + ONES_ROWS, tq), f32)],
        compiler_params=_cparams(("arbitrary", "arbitrary")),
        name="attn_prompt",
    )(q, kb, vt, rings, *lam_vecs, g_subln)


def _attn_sample_kernel(q_ref, kp_ref, vp_ref, kn_ref, vn_ref, nbp_ref, nbn_ref, lq1, lk1, lq2, lk2, gs_ref,
                        o_ref, *, lam_init):
    lam = _lam_value(lq1, lk1, lq2, lk2, lam_init)
    for hd in range(N_HEADS):
        qk_cols = slice(hd * 2 * HEAD_DIM, (hd + 1) * 2 * HEAD_DIM)
        v_cols = slice(hd * V_DIM, (hd + 1) * V_DIM)
        qz = jnp.concatenate(_split_maps(q_ref[:, qk_cols]), axis=0)
        kp = kp_ref[:, qk_cols].astype(bf16)
        vp = vp_ref[:, v_cols].astype(bf16)
        kn = kn_ref[:, qk_cols]
        vn = vn_ref[:, v_cols]
        ts = q_ref.shape[0]
        sp = _qk(qz, kp) + jnp.concatenate([nbp_ref[hd], nbp_ref[hd]], axis=0)
        sn = _qk(qz, kn) + jnp.concatenate([nbn_ref[hd], nbn_ref[hd]], axis=0)
        m = jnp.maximum(jnp.max(sp, axis=1, keepdims=True), jnp.max(sn, axis=1, keepdims=True))
        pp = jnp.exp2(sp - m).astype(bf16)
        pn = jnp.exp2(sn - m).astype(bf16)
        den = jnp.sum(pp.astype(f32), axis=1, keepdims=True) + jnp.sum(pn.astype(f32), axis=1, keepdims=True)
        out = (jnp.dot(pp, vp, preferred_element_type=f32) + jnp.dot(pn, vn, preferred_element_type=f32)) / den
        o_ref[:, v_cols] = _finish_heads(out[:ts], out[ts:], lam, gs_ref[...], lam_init).astype(o_ref.dtype)


def _attn_sample(q, kb, vb, cache_k, cache_v, rel_bias, lam_vecs, g_subln, lam_init, nb_, ts):
    past = cache_k.shape[1]
    zero = jnp.zeros((N_HEADS,), f32)
    nbp = _bias_table(rel_bias, past, ts, 0, past, zero)
    nbn = _bias_table(rel_bias, past, ts, past, ts, zero)
    vec = _const_spec((1, HEAD_DIM))
    return pl.pallas_call(
        functools.partial(_attn_sample_kernel, lam_init=lam_init),
        grid=(nb_,),
        in_specs=[
            pl.BlockSpec((ts, QK_COLS), lambda b: (b, 0)),
            pl.BlockSpec((None, past, QK_COLS), lambda b: (b, 0, 0)),
            pl.BlockSpec((None, past, V_COLS), lambda b: (b, 0, 0)),
            pl.BlockSpec((ts, QK_COLS), lambda b: (b, 0)),
            pl.BlockSpec((ts, V_COLS), lambda b: (b, 0)),
            _const_spec((N_HEADS, ts, past)), _const_spec((N_HEADS, ts, ts)),
            vec, vec, vec, vec, _const_spec((1, V_DIM)),
        ],
        out_specs=pl.BlockSpec((ts, V_COLS), lambda b: (b, 0)),
        out_shape=jax.ShapeDtypeStruct((nb_ * ts, V_COLS), bf16),
        compiler_params=_cparams(("parallel",)),
        name="attn_sample",
    )(q, cache_k, cache_v, kb, vb, nbp, nbn, *lam_vecs, g_subln)


def _conv_lru_kernel(lx_ref, gl_ref, c0_ref, h0_ref, wc_ref, bc_ref, wa_ref, ba_ref, wx_ref, bx_ref, lam_ref,
                     hsg_ref, cout_ref, hout_ref, xbuf, a_sc, b_sc, hcar, *, tt, first_pos_is_zero):
    t = pl.program_id(1)

    @pl.when(t == 0)
    def _():
        xbuf[0:SUBLANES] = c0_ref[...]
        hcar[...] = jnp.broadcast_to(h0_ref[...], hcar.shape)

    xbuf[SUBLANES:SUBLANES + tt] = lx_ref[...]
    ngrp = tt // SUBLANES
    xg = xbuf[...].reshape(ngrp + 1, SUBLANES, LRU_WIDTH)
    sub = lax.broadcasted_iota(jnp.int32, (ngrp, SUBLANES, LRU_WIDTH), 1)
    xc = xg[1:] * wc_ref[CONV_W - 1:CONV_W, :] + bc_ref[...]
    for s in range(1, CONV_W):
        rolled = pltpu.roll(xg, s, axis=1)
        shifted = jnp.where(sub >= s, rolled[1:], rolled[:-1])
        xc = xc + shifted * wc_ref[CONV_W - 1 - s:CONV_W - s, :]
    xc = xc.reshape(tt, LRU_WIDTH)
    tail = xbuf[tt:tt + SUBLANES, :]
    xbuf[0:SUBLANES] = tail
    cout_ref[...] = tail

    xcb = xc.astype(bf16)

    def block_diag(w_ref, b_ref):
        cols = [jnp.dot(xcb[:, n * LRU_BLOCK_W:(n + 1) * LRU_BLOCK_W], w_ref[n], preferred_element_type=f32)
                for n in range(LRU_BLOCKS)]
        return jnp.concatenate(cols, axis=1) + b_ref[...]

    r = _sigmoid(block_diag(wa_ref, ba_ref))
    ig = _sigmoid(block_diag(wx_ref, bx_ref))
    z = -lam_ref[...]
    softplus = jnp.maximum(z, 0.0) + jnp.log1p(jnp.exp(-jnp.abs(z)))
    log_a = -LRU_C * r * softplus
    a = jnp.exp(log_a)
    th = jnp.tanh(log_a)
    m2 = -2.0 * th / (1.0 - th)
    mult = jnp.where(m2 > 0.0, m2 * lax.rsqrt(m2), 0.0)
    if first_pos_is_zero:
        row = lax.broadcasted_iota(jnp.int32, (tt, LRU_WIDTH), 0)
        mult = jnp.where((row == 0) & (t == 0), 1.0, mult)
    bv = mult * ig * xc

    groups = (tt // SUBLANES, SUBLANES, LRU_WIDTH)
    a = a.reshape(groups)
    bv = bv.reshape(groups)
    sub = lax.broadcasted_iota(jnp.int32, groups, 1)
    d = 1
    while d < SUBLANES:
        valid = sub >= d
        a_s = pltpu.roll(a, d, axis=1)
        b_s = pltpu.roll(bv, d, axis=1)
        bv = jnp.where(valid, a * b_s + bv, bv)
        a = jnp.where(valid, a * a_s, a)
        d *= 2
    a_sc[...] = a.reshape(tt, LRU_WIDTH)
    b_sc[...] = bv.reshape(tt, LRU_WIDTH)

    def group(g, h):
        sl = pl.ds(pl.multiple_of(g * SUBLANES, SUBLANES), SUBLANES)
        hg = a_sc[sl, :] * h + b_sc[sl, :]
        b_sc[sl, :] = hg
        return jnp.broadcast_to(hg[SUBLANES - 1:SUBLANES, :], hg.shape)

    h = lax.fori_loop(0, tt // SUBLANES, group, hcar[...], unroll=4)
    hcar[...] = h
    hout_ref[...] = h[0:1, :]
    hsg_ref[...] = (b_sc[...] * gl_ref[...]).astype(hsg_ref.dtype)


def _conv_lru(lx, gl, conv0, h0, w_conv, b_conv, wa, ba, wx, bx, lru_lambda, nb_, tlen, tt, first_pos_is_zero):
    nt = tlen // tt
    rows = pl.BlockSpec((tt, LRU_WIDTH), lambda b, t: (b * nt + t, 0))
    vecw = _const_spec((1, LRU_WIDTH))
    wblk = _const_spec((LRU_BLOCKS, LRU_BLOCK_W, LRU_BLOCK_W))
    return pl.pallas_call(
        functools.partial(_conv_lru_kernel, tt=tt, first_pos_is_zero=first_pos_is_zero),
        grid=(nb_, nt),
        in_specs=[rows, rows,
                  pl.BlockSpec((None, SUBLANES, LRU_WIDTH), lambda b, t: (b, 0, 0)),
                  pl.BlockSpec((None, 1, LRU_WIDTH), lambda b, t: (b, 0, 0)),
                  _const_spec((CONV_W, LRU_WIDTH)), vecw, wblk, vecw, wblk, vecw, vecw],
        out_specs=[rows,
                   pl.BlockSpec((None, SUBLANES, LRU_WIDTH), lambda b, t: (b, 0, 0)),
                   pl.BlockSpec((None, 1, LRU_WIDTH), lambda b, t: (b, 0, 0))],
        out_shape=[jax.ShapeDtypeStruct((nb_ * tlen, LRU_WIDTH), bf16),
                   jax.ShapeDtypeStruct((nb_, SUBLANES, LRU_WIDTH), f32),
                   jax.ShapeDtypeStruct((nb_, 1, LRU_WIDTH), f32)],
        scratch_shapes=[pltpu.VMEM((tt + SUBLANES, LRU_WIDTH), f32), pltpu.VMEM((tt, LRU_WIDTH), f32),
                        pltpu.VMEM((tt, LRU_WIDTH), f32), pltpu.VMEM((SUBLANES, LRU_WIDTH), f32)],
        compiler_params=_cparams(("arbitrary", "arbitrary")),
        name="conv_lru",
    )(lx, gl, conv0, h0, w_conv, b_conv, wa, ba, wx, bx, lru_lambda)


def _merge_kernel(x_ref, att_ref, hsg_ref, gt_ref, wa_ref, wb_ref, wo_ref, gf_ref, wr_ref, br_ref,
                  x1_ref, h2_ref, comb_ref, *, parts):
    rows_per = x_ref.shape[0] // parts
    for part in range(parts):
        rows = slice(part * rows_per, (part + 1) * rows_per)
        bra = jnp.dot(att_ref[rows, :], wa_ref[...], preferred_element_type=f32)
        brb = jnp.dot(hsg_ref[rows, :], wb_ref[...], preferred_element_type=f32)
        gt = gt_ref[rows, :]
        m = (gt[:, :D_MODEL] * bra + gt[:, D_MODEL:] * brb).astype(bf16)
        x1 = x_ref[rows, :] + jnp.dot(m, wo_ref[...], preferred_element_type=f32)
        x1_ref[rows, :] = x1
        h2 = _rms(x1, gf_ref[...])
        h2_hi = h2.astype(bf16)
        h2_ref[rows, :] = h2_hi

        h2_lo = (h2 - h2_hi.astype(f32)).astype(bf16)
        prods = (jnp.dot(h2_hi, wr_ref[...], preferred_element_type=f32)
                 + jnp.dot(h2_lo, wr_ref[...], preferred_element_type=f32))
        logits = prods[:, :LANES] + prods[:, LANES:] + br_ref[...]
        lane = lax.broadcasted_iota(jnp.int32, logits.shape, 1)
        lanef = lane.astype(f32)
        low = jnp.float32(-3.0e38)
        is_grp = (lane >= N_EXPERTS) & (lane < N_EXPERTS + N_GROUPS)
        gl = jnp.where(is_grp, logits, low)
        gmax = jnp.max(gl, axis=1, keepdims=True)
        gidx = jnp.min(jnp.where(is_grp & (gl == gmax), lanef, 1.0e3), axis=1, keepdims=True) - N_EXPERTS
        gden = jnp.sum(jnp.where(is_grp, jnp.exp(gl - gmax), 0.0), axis=1, keepdims=True)
        g_w = 1.0 / gden
        lo = gidx * EXPERTS_PER_GROUP
        in_sel = (lanef >= lo) & (lanef < lo + EXPERTS_PER_GROUP)
        sel = jnp.where(in_sel, logits, low)
        v1 = jnp.max(sel, axis=1, keepdims=True)
        i1 = jnp.min(jnp.where(in_sel & (sel == v1), lanef, 1.0e3), axis=1, keepdims=True)
        in_sel2 = in_sel & (lanef != i1)
        sel2 = jnp.where(in_sel2, logits, low)
        v2 = jnp.max(sel2, axis=1, keepdims=True)
        i2 = jnp.min(jnp.where(in_sel2 & (sel2 == v2), lanef, 1.0e3), axis=1, keepdims=True)
        e2 = jnp.exp(v2 - v1)
        w1 = g_w / (1.0 + e2)
        w2 = g_w * e2 / (1.0 + e2)
        comb_ref[rows, :] = jnp.where(lanef == i1, w1, 0.0) + jnp.where(lanef == i2, w2, 0.0)


def _merge(x, att, hsg, gt, wa, wb, wo, g_ffn, w_router, b_router, tm):
    n = x.shape[0]
    row = lambda w: pl.BlockSpec((tm, w), lambda i: (i, 0))
    sq = _const_spec((D_MODEL, D_MODEL))
    return pl.pallas_call(
        functools.partial(_merge_kernel, parts=2 if tm % 512 == 0 else 1),
        grid=(n // tm,),
        in_specs=[row(D_MODEL), row(V_COLS), row(LRU_WIDTH), row(2 * D_MODEL), sq, sq, sq,
                  _const_spec((1, D_MODEL)), _const_spec((D_MODEL, 2 * LANES)), _const_spec((1, LANES))],
        out_specs=[row(D_MODEL), row(D_MODEL), row(LANES)],
        out_shape=[jax.ShapeDtypeStruct((n, D_MODEL), f32), jax.ShapeDtypeStruct((n, D_MODEL), bf16),
                   jax.ShapeDtypeStruct((n, LANES), f32)],
        compiler_params=_cparams(("parallel",)),
        name="merge_router",
    )(x, att, hsg, gt, wa, wb, wo, g_ffn, w_router, b_router)


EXPERTS_PER_STEP = 8


MOE_CHUNK = 128


def _moe_kernel(h_ref, comb_ref, wg_ref, wu_ref, wd_ref, y_ref, pt_sc, x_sc, c_sc, ys_sc, seg_sc):
    e = pl.program_id(1)
    tm = h_ref.shape[0]
    lane1 = lax.broadcasted_iota(jnp.int32, (1, LANES), 1)

    @pl.when(e == 0)
    def _():
        comb = comb_ref[...]
        ex = lax.broadcasted_iota(jnp.int32, (LANES, LANES), 0)
        gr = lax.broadcasted_iota(jnp.int32, (LANES, LANES), 1)
        in_group = jnp.where((ex // EXPERTS_PER_GROUP == gr) & (ex < N_EXPERTS), 1.0, 0.0).astype(bf16)
        gsum = jnp.dot(comb.astype(bf16), in_group, preferred_element_type=f32)
        member = jnp.where(gsum > 0.0, 1.0, 0.0)
        r_i = lax.broadcasted_iota(jnp.int32, (tm, tm), 0)
        c_i = lax.broadcasted_iota(jnp.int32, (tm, tm), 1)
        tri = jnp.where(gr <= ex, 1.0, 0.0).astype(bf16)
        member_b = member.astype(bf16)
        totals = jnp.zeros((1, LANES), f32)
        blocks = []
        for blk in range(tm // LANES):
            part = jnp.dot(tri, member_b[blk * LANES:(blk + 1) * LANES], preferred_element_type=f32) + totals
            blocks.append(part)
            totals = part[LANES - 1:LANES, :]
        rank = jnp.concatenate(blocks, axis=0)
        offs = jnp.zeros((1, LANES), f32)
        for g in range(1, N_GROUPS):
            prev = jnp.sum(jnp.where(lane1 == g - 1, totals, 0.0), axis=1, keepdims=True)
            offs = offs + jnp.where(lane1 >= g, prev, 0.0)
        seg_sc[0:1, :] = offs.astype(jnp.int32)
        seg_sc[1:2, :] = totals.astype(jnp.int32)
        pos = jnp.sum(member * (offs + rank - 1.0), axis=1, keepdims=True)
        pos_row = jnp.broadcast_to(pos, (tm, LANES)).T[0:1, :].astype(jnp.int32)
        perm = jnp.where(r_i == pos_row, 1.0, 0.0).astype(bf16)
        pt_sc[...] = jnp.where(c_i == pos.astype(jnp.int32), 1.0, 0.0).astype(bf16)
        comb_hi = comb.astype(bf16)
        comb_lo = (comb - comb_hi.astype(f32)).astype(bf16)
        srt = jnp.dot(perm, jnp.concatenate([h_ref[...], comb_hi, comb_lo], axis=1), preferred_element_type=f32)
        x_sc[...] = srt[:, :D_MODEL].astype(bf16)
        c_sc[...] = srt[:, D_MODEL:D_MODEL + LANES] + srt[:, D_MODEL + LANES:]
        ys_sc[...] = jnp.zeros(ys_sc.shape, f32)

    g = (e * EXPERTS_PER_STEP) // EXPERTS_PER_GROUP
    off = jnp.sum(jnp.where(lane1 == g, seg_sc[0:1, :], 0))
    cnt = jnp.sum(jnp.where(lane1 == g, seg_sc[1:2, :], 0))
    lane = lax.broadcasted_iota(jnp.int32, (MOE_CHUNK, LANES), 1)

    def chunk(k, carry):
        rows = pl.ds(pl.multiple_of(k * MOE_CHUNK, MOE_CHUNK), MOE_CHUNK)
        x = x_sc[rows, :]
        cw = c_sc[rows, :]
        hidden = []
        for j in range(EXPERTS_PER_STEP):
            gate = jnp.dot(x, wg_ref[j], preferred_element_type=f32)
            up = jnp.dot(x, wu_ref[j], preferred_element_type=f32)
            c = jnp.sum(jnp.where(lane == e * EXPERTS_PER_STEP + j, cw, 0.0), axis=1, keepdims=True)
            hidden.append((gate * _sigmoid(gate) * up * c).astype(bf16))
        ys_sc[rows, :] += jnp.dot(jnp.concatenate(hidden, axis=1), wd_ref[...], preferred_element_type=f32)
        return carry

    lax.fori_loop(off // MOE_CHUNK, (off + cnt + MOE_CHUNK - 1) // MOE_CHUNK, chunk, 0)

    @pl.when(e == pl.num_programs(1) - 1)
    def _():
        y = jnp.dot(pt_sc[...], ys_sc[...].astype(bf16), preferred_element_type=f32)
        y_ref[...] = y.astype(y_ref.dtype)


def _moe(h2, comb, wg, wu, wd_rows, tm):
    n = h2.shape[0]
    per = EXPERTS_PER_STEP
    assert EXPERTS_PER_GROUP % per == 0 and tm % MOE_CHUNK == 0
    return pl.pallas_call(
        _moe_kernel,
        grid=(n // tm, N_EXPERTS // per),
        in_specs=[pl.BlockSpec((tm, D_MODEL), lambda i, e: (i, 0)),
                  pl.BlockSpec((tm, LANES), lambda i, e: (i, 0)),
                  pl.BlockSpec((per, D_MODEL, D_EXPERT), lambda i, e: (e, 0, 0)),
                  pl.BlockSpec((per, D_MODEL, D_EXPERT), lambda i, e: (e, 0, 0)),
                  pl.BlockSpec((per * D_EXPERT, D_MODEL), lambda i, e: (e, 0))],
        out_specs=pl.BlockSpec((tm, D_MODEL), lambda i, e: (i, 0)),
        out_shape=jax.ShapeDtypeStruct((n, D_MODEL), bf16),
        scratch_shapes=[pltpu.VMEM((tm, tm), bf16), pltpu.VMEM((tm, D_MODEL), bf16),
                        pltpu.VMEM((tm, LANES), f32), pltpu.VMEM((tm, D_MODEL), f32),
                        pltpu.VMEM((SUBLANES, LANES), jnp.int32)],
        compiler_params=_cparams(("arbitrary", "arbitrary")),
        name="moe",
    )(h2, comb, wg, wu, wd_rows)


def _ple_kernel(x1_ref, y_ref, p_ref, gp_ref, wpg_ref, bpg_ref, wpi_ref, gfin_ref, o_ref):
    x2 = x1_ref[...] + y_ref[...]
    hp = _rms(x2, gp_ref[...]).astype(bf16)
    pg = _sigmoid(jnp.dot(hp, wpg_ref[...], preferred_element_type=f32) + bpg_ref[...])
    pin = jnp.dot(p_ref[...].astype(bf16), wpi_ref[...], preferred_element_type=f32)
    x3 = x2 + pg * pin
    o_ref[...] = _rms(x3, gfin_ref[...])


def _ple(x1, y, p, g_ple, wpg, bpg, wpi, g_final, tm):
    n = x1.shape[0]
    row = lambda w: pl.BlockSpec((tm, w), lambda i: (i, 0))
    vec = _const_spec((1, D_MODEL))
    return pl.pallas_call(
        _ple_kernel,
        grid=(n // tm,),
        in_specs=[row(D_MODEL), row(D_MODEL), row(PLE_DIM), vec, _const_spec((D_MODEL, D_MODEL)), vec,
                  _const_spec((PLE_DIM, D_MODEL)), vec],
        out_specs=row(D_MODEL),
        out_shape=jax.ShapeDtypeStruct((n, D_MODEL), f32),
        compiler_params=_cparams(("parallel",)),
        name="ple_final",
    )(x1, y, p, g_ple, wpg, bpg, wpi, g_final)


PROJ_ROWS = 256
ATTN_KEY_TILE = 512
MERGE_ROWS = 512
MOE_ROWS = 1024


def _tile(n, pref):
    return pref if n % pref == 0 else n


def _stream(x, p, conv0, h0, cache, li, rel_bias, w):
    nb_, tlen, _ = x.shape
    n = nb_ * tlen
    xf = x.reshape(n, D_MODEL)
    lam_init = 0.8 - 0.6 * math.exp(-0.3 * li)
    tk = None if cache is not None else ATTN_KEY_TILE
    q, k, kb, v, vb, lx, gl, gt = _in_proj(xf, w["g_mix"], w["w_in"], w["b_merge"], _tile(n, PROJ_ROWS), tk)
    if cache is None:
        att = _attn_prompt(q, kb, vb, rel_bias, w["lam_vecs"], w["g_subln"], lam_init, 2 * tk, tk)
    else:
        att = _attn_sample(q, kb, vb, cache[0], cache[1], rel_bias, w["lam_vecs"], w["g_subln"], lam_init,
                           nb_, tlen)
    conv_pad = jnp.pad(conv0, ((0, 0), (SUBLANES - (CONV_W - 1), 0), (0, 0)))
    hsg, cout, hout = _conv_lru(lx, gl, conv_pad, h0[:, None, :], w["w_conv"], w["b_conv"], w["w_rg_a"],
                                w["b_rg_a"], w["w_rg_x"], w["b_rg_x"], w["lru_lambda"], nb_, tlen,
                                _tile(tlen, PROJ_ROWS), cache is None)
    tm = _tile(n, MERGE_ROWS)
    x1, h2, comb = _merge(xf, att, hsg, gt, w["w_attn_br"], w["w_lru_br"], w["w_out"], w["g_ffn"],
                          w["w_router"], w["b_router"], tm)
    y = _moe(h2, comb, w["w_e_gate"], w["w_e_up"], w["w_e_down"], _tile(n, MOE_ROWS))
    out = _ple(x1, y, p.reshape(n, PLE_DIM), w["g_ple"], w["w_ple_gate"], w["b_ple_gate"], w["w_ple_in"],
               w["g_final"], tm)
    return out, k, v, cout[:, SUBLANES - (CONV_W - 1):, :], hout[:, 0, :]


def kernel(x_prompt, x_sample, cache_k, cache_v, state_conv, state_lru, p_prompt, p_sample, rel_bias, g_mix, w_in, b_merge, lam_q1, lam_k1, lam_q2, lam_k2, g_subln, w_attn_br, w_conv, b_conv, w_rg_a, b_rg_a, w_rg_x, b_rg_x, lru_lambda, w_lru_br, w_out, g_ffn, w_grp, b_grp, w_rt, b_rt, w_e_gate, w_e_up, w_e_down, g_ple, w_ple_gate, b_ple_gate, w_ple_in, g_final):
    depth = w_in.shape[0]
    assert depth == 1, "the final norm is fused into the single layer's last stage"
    bp, tp, _ = x_prompt.shape
    bs, ts, _ = x_sample.shape
    past = cache_k.shape[2]
    li = 0
    row = lambda a: a[li].reshape(1, -1).astype(f32)
    w_router = jnp.concatenate(
        [jnp.transpose(w_rt[li], (1, 0, 2)).reshape(D_MODEL, N_EXPERTS), w_grp[li]], axis=1).astype(f32)
    w_router = jnp.pad(w_router, ((0, 0), (0, LANES - N_EXPERTS - N_GROUPS)))
    w_router_hi = w_router.astype(bf16)
    w_router = jnp.concatenate([w_router_hi, (w_router - w_router_hi.astype(f32)).astype(bf16)], axis=1)
    b_router = jnp.pad(jnp.concatenate([b_rt[li].reshape(-1), b_grp[li]]).astype(f32),
                       (0, LANES - N_EXPERTS - N_GROUPS)).reshape(1, LANES)
    w = dict(
        g_mix=row(g_mix), w_in=w_in[li].astype(bf16), b_merge=row(b_merge),
        lam_vecs=(row(lam_q1), row(lam_k1), row(lam_q2), row(lam_k2)), g_subln=row(g_subln),
        w_attn_br=w_attn_br[li].astype(bf16), w_conv=w_conv[li].astype(f32), b_conv=row(b_conv),
        w_rg_a=w_rg_a[li].astype(bf16), b_rg_a=row(b_rg_a), w_rg_x=w_rg_x[li].astype(bf16), b_rg_x=row(b_rg_x),
        lru_lambda=row(lru_lambda), w_lru_br=w_lru_br[li].astype(bf16), w_out=w_out[li].astype(bf16),
        g_ffn=row(g_ffn), w_router=w_router, b_router=b_router,
        w_e_gate=w_e_gate[li].astype(bf16), w_e_up=w_e_up[li].astype(bf16), w_e_down=w_e_down[li].reshape(N_EXPERTS * D_EXPERT, D_MODEL).astype(bf16),
        g_ple=row(g_ple), w_ple_gate=w_ple_gate[li].astype(bf16), b_ple_gate=row(b_ple_gate),
        w_ple_in=w_ple_in[li].astype(bf16), g_final=g_final.reshape(1, -1).astype(f32),
    )
    conv0 = jnp.zeros((bp, CONV_W - 1, LRU_WIDTH), f32)
    h0 = jnp.zeros((bp, LRU_WIDTH), f32)
    yp, kp, vp, cp, hp = _stream(x_prompt, p_prompt[li], conv0, h0, None, li, rel_bias, w)
    cache = (cache_k[li].reshape(bs, past, QK_COLS), cache_v[li].reshape(bs, past, V_COLS))
    ys, ks, vs, cs, hs = _stream(x_sample, p_sample[li], state_conv[li], state_lru[li], cache, li, rel_bias, w)
    return (yp.reshape(bp, tp, D_MODEL), ys.reshape(bs, ts, D_MODEL),
            kp.reshape(1, bp, tp, N_HEADS, 2, HEAD_DIM), vp.reshape(1, bp, tp, N_HEADS, V_DIM),
            cp[None], hp[None],
            ks.reshape(1, bs, ts, N_HEADS, 2, HEAD_DIM), vs.reshape(1, bs, ts, N_HEADS, V_DIM),
            cs[None], hs[None])
```

```python
import functools
import math

import jax
import jax.numpy as jnp
from jax import lax
from jax.experimental import pallas as pl
from jax.experimental.pallas import tpu as pltpu

f32 = jnp.float32
bf16 = jnp.bfloat16

D_MODEL = 1024
N_HEADS = 8
HEAD_DIM = 64
V_DIM = 128
QK_COLS = N_HEADS * 2 * HEAD_DIM
V_COLS = N_HEADS * V_DIM
LRU_WIDTH = 1024
LRU_BLOCKS = 8
LRU_BLOCK_W = LRU_WIDTH // LRU_BLOCKS
CONV_W = 4
LRU_C = 8.0
CHUNK = 64
N_BUCKETS = 32
MAX_DISTANCE = 128
N_GROUPS = 4
EXPERTS_PER_GROUP = 8
N_EXPERTS = N_GROUPS * EXPERTS_PER_GROUP
D_EXPERT = 256
PLE_DIM = 256
EPS = 1e-6
NEG_INF = -1e30
LOG2E = math.log2(math.e)
ONES_ROWS = 16
LANES = 128
SUBLANES = 8
VMEM_LIMIT = 56 * 1024 * 1024

C_Q, C_K, C_V = 0, QK_COLS, 2 * QK_COLS
C_LX = 2 * QK_COLS + V_COLS
C_LY = C_LX + LRU_WIDTH
C_G = C_LY + LRU_WIDTH
IN_COLS = C_G + 2 * D_MODEL


def _cparams(sem):
    return pltpu.CompilerParams(dimension_semantics=sem, vmem_limit_bytes=VMEM_LIMIT)


def _const_spec(shape):
    nd = len(shape)
    return pl.BlockSpec(shape, lambda *_: (0,) * nd, pipeline_mode=pl.Buffered(1))


def _rms(x, g):
    return x * lax.rsqrt(jnp.mean(x * x, axis=-1, keepdims=True) + EPS) * g


def _sigmoid(x):
    return 1.0 / (1.0 + jnp.exp(-x))


def _gelu_tanh(x):
    c = math.sqrt(2.0 / math.pi)
    return 0.5 * x * (1.0 + jnp.tanh(c * (x + 0.044715 * (x * x * x))))


def _in_proj_kernel(x_ref, g_ref, w_ref, bm_ref, q_ref, k_ref, kb_ref, v_ref, vb_ref, lx_ref, gl_ref, gt_ref,
                    *, transposed_v):
    h = _rms(x_ref[...], g_ref[...]).astype(bf16)

    def proj(lo, width):
        return jnp.dot(h, w_ref[:, lo:lo + width], preferred_element_type=f32)

    q_ref[...] = (proj(C_Q, QK_COLS) * (HEAD_DIM ** -0.5 * LOG2E)).astype(bf16)
    k = proj(C_K, QK_COLS)
    k_ref[...] = k
    kb_ref[...] = k.astype(bf16)
    v = proj(C_V, V_COLS)
    v_ref[...] = v
    if transposed_v:
        for hd in range(N_HEADS):
            vb_ref[hd, :V_DIM, :] = v[:, hd * V_DIM:(hd + 1) * V_DIM].T.astype(bf16)
            vb_ref[hd, V_DIM:, :] = jnp.ones((ONES_ROWS, v.shape[0]), bf16)
    else:
        vb_ref[...] = v.astype(bf16)
    lx_ref[...] = proj(C_LX, LRU_WIDTH)
    gl_ref[...] = _gelu_tanh(proj(C_LY, LRU_WIDTH))
    gt_ref[...] = _sigmoid(proj(C_G, 2 * D_MODEL) + bm_ref[...])


def _in_proj(x, g_mix, w_in_b, b_merge, tm, tq=None):
    n = x.shape[0]
    row = lambda w: pl.BlockSpec((tm, w), lambda i: (i, 0))
    if tq is None:
        vb_shape, vb_spec = jax.ShapeDtypeStruct((n, V_COLS), bf16), row(V_COLS)
    else:
        per = tq // tm
        vb_shape = jax.ShapeDtypeStruct((N_HEADS, n // tq, V_DIM + ONES_ROWS, tq), bf16)
        vb_spec = pl.BlockSpec((N_HEADS, None, V_DIM + ONES_ROWS, tm), lambda i: (0, i // per, 0, i % per))
    outs = [
        jax.ShapeDtypeStruct((n, QK_COLS), bf16),
        jax.ShapeDtypeStruct((n, QK_COLS), f32),
        jax.ShapeDtypeStruct((n, QK_COLS), bf16),
        jax.ShapeDtypeStruct((n, V_COLS), f32),
        vb_shape,
        jax.ShapeDtypeStruct((n, LRU_WIDTH), f32),
        jax.ShapeDtypeStruct((n, LRU_WIDTH), f32),
        jax.ShapeDtypeStruct((n, 2 * D_MODEL), f32),
    ]
    return pl.pallas_call(
        functools.partial(_in_proj_kernel, transposed_v=tq is not None),
        grid=(n // tm,),
        in_specs=[row(D_MODEL), _const_spec((1, D_MODEL)), _const_spec((D_MODEL, IN_COLS)),
                  _const_spec((1, 2 * D_MODEL))],
        out_specs=[row(QK_COLS), row(QK_COLS), row(QK_COLS), row(V_COLS), vb_spec,
                   row(LRU_WIDTH), row(LRU_WIDTH), row(2 * D_MODEL)],
        out_shape=outs,
        compiler_params=_cparams(("parallel",)),
        name="in_proj",
    )(x, g_mix, w_in_b, b_merge)


def _rel_bucket(rel):
    half = N_BUCKETS // 2
    max_exact = half // 2
    ret = jnp.where(rel > 0, half, 0)
    n = jnp.abs(rel)
    nf = jnp.maximum(n, 1).astype(f32)
    large = max_exact + (jnp.log(nf / max_exact) / math.log(MAX_DISTANCE / max_exact)
                         * (half - max_exact)).astype(jnp.int32)
    large = jnp.minimum(large, half - 1)
    return ret + jnp.where(n < max_exact, n, large)


def _toeplitz(t, nq, nk):
    nh, length = t.shape
    a = jnp.broadcast_to(t[:, None, :], (nh, nq, length))
    a = jnp.pad(a, ((0, 0), (0, 0), (0, 1))).reshape(nh, nq * (length + 1))
    a = a[:, :nq * length].reshape(nh, nq, length)
    return a[:, :, nq - 1:nq - 1 + nk]


def _bias_table(rel_bias, qpos0, nq, kpos0, nk, shift):
    rel = jnp.arange(-(nq - 1), nk) + (kpos0 - qpos0)
    t = (rel_bias.astype(f32)[_rel_bucket(rel)].T - shift[:, None]) * LOG2E
    qpos = qpos0 + jnp.arange(nq)
    kpos = kpos0 + jnp.arange(nk)
    mask = (kpos[None, :] // CHUNK) <= (qpos[:, None] // CHUNK)
    return jnp.where(mask[None], _toeplitz(t, nq, nk), NEG_INF)


def _bias_ring(rel_bias, offset, tq, shift):
    r = jnp.arange(2 * tq)
    x = jnp.where(r < tq, r, r - 2 * tq)
    t = (rel_bias.astype(f32)[_rel_bucket(offset - x)].T - shift[:, None]) * LOG2E
    return t[:, None, :]


def _lam_value(lq1, lk1, lq2, lk2, lam_init):
    s1 = jnp.sum(lq1[...] * lk1[...], axis=-1, keepdims=True)
    s2 = jnp.sum(lq2[...] * lk2[...], axis=-1, keepdims=True)
    return jnp.exp(s1) - jnp.exp(s2) + lam_init


def _split_maps(q):
    lane = lax.broadcasted_iota(jnp.int32, q.shape, 1)
    zero = jnp.zeros_like(q)
    return jnp.where(lane < HEAD_DIM, q, zero), jnp.where(lane >= HEAD_DIM, q, zero)


def _qk(qz, kt):
    return lax.dot_general(qz, kt, (((1,), (1,)), ((), ())), preferred_element_type=f32)


def _finish_heads(o1, o2, lam, gs, lam_init):
    att = o1 - lam * o2
    return _rms(att, gs) * (1.0 - lam_init)


def _attn_prompt_kernel(q_ref, k_ref, vt_ref, ring_ref, lq1, lk1, lq2, lk2, gs_ref, o_ref,
                        nb_sc, s_sc, p_sc, al_sc, tm_sc, m_sc, acc_sc, *, tq, tk, lam_init):
    i = pl.program_id(1)
    ring_len = ring_ref.shape[-1]

    def bias_tile(ring):
        return pltpu.roll(jnp.broadcast_to(ring, (tk, ring_len)), 0, 1, stride=1, stride_axis=0)[:, :tq]

    @pl.when(i == 0)
    def _():
        kk = lax.broadcasted_iota(jnp.int32, (tk, tq), 0)
        qq = lax.broadcasted_iota(jnp.int32, (tk, tq), 1)
        nb_sc[0] = jnp.where((kk // CHUNK) <= (qq // CHUNK), bias_tile(ring_ref[1]), NEG_INF)
        nb_sc[1] = jnp.where(((kk + tk) // CHUNK) <= (qq // CHUNK), bias_tile(ring_ref[2]), NEG_INF)
        nb_sc[2] = jnp.full((tk, tq), NEG_INF, f32)

    @pl.when(i == 1)
    def _():
        nb_sc[2] = nb_sc[1]
        nb_sc[1] = nb_sc[0]
        nb_sc[0] = bias_tile(ring_ref[0])

    qt = q_ref[...].astype(f32).T
    row = lax.broadcasted_iota(jnp.int32, qt.shape, 0)
    qz = (jnp.where(row < HEAD_DIM, qt, 0.0).astype(bf16), jnp.where(row >= HEAD_DIM, qt, 0.0).astype(bf16))

    m_sc[...] = jnp.full(m_sc.shape, NEG_INF, f32)
    acc_sc[...] = jnp.zeros(acc_sc.shape, f32)

    def qk(slot, tile, bias, cols=slice(None)):
        kt = k_ref[pl.ds(pl.multiple_of(tile * tk, tk), tk), :]
        for mi in range(2):
            s = jnp.dot(kt, qz[mi][:, cols], preferred_element_type=f32)
            s = s if bias is None else s + bias[:, cols]
            s_sc[mi, slot, :, cols] = s
            tm_sc[mi, slot, :, cols] = jnp.max(s, axis=0, keepdims=True)

    def softmax(slot, cols=slice(None)):
        for mi in range(2):
            s = s_sc[mi, slot, :, cols]
            m_prev = m_sc[mi, :, cols]
            m_new = jnp.maximum(m_prev, tm_sc[mi, slot, :, cols])
            al_sc[mi, slot, :, cols] = jnp.exp2(m_prev - m_new)
            p_sc[mi, slot, :, cols] = jnp.exp2(s - m_new).astype(bf16)
            m_sc[mi, :, cols] = m_new

    def pv(slot, vtile, cols=slice(None)):
        va = vt_ref[vtile]
        for mi in range(2):
            acc_sc[mi, :, cols] = (acc_sc[mi, :, cols] * al_sc[mi, slot, :, cols]
                                   + jnp.dot(va, p_sc[mi, slot, :, cols], preferred_element_type=f32))

    late = slice(tq // 2, tq)

    per = tq // tk
    a0 = jnp.maximum(per * i - 1, 0)
    n_far = jnp.maximum(per * i - 1, 0)
    last_tile = vt_ref.shape[0] - 1

    def far_k(f):
        return jnp.clip(f, 0, jnp.maximum(n_far - 1, 0))

    qk(0, a0, nb_sc[0])
    qk(1, a0 + 1, nb_sc[1])
    softmax(0)
    qk(0, jnp.minimum(a0 + 2, last_tile), nb_sc[2], late)
    softmax(1)
    pv(0, a0)

    @pl.when(i == 0)
    def _():
        pv(1, a0 + 1)

    @pl.when(i > 0)
    def _():
        qk(1, 0, None)
        softmax(0, late)
        pv(1, a0 + 1)
        qk(0, far_k(1), None)
        softmax(1)
        pv(0, a0 + 2, late)

        def far_pair(p, carry):
            qk(1, 2 * p + 2, None)
            softmax(0)
            pv(1, 2 * p)
            qk(0, far_k(2 * p + 3), None)
            softmax(1)
            pv(0, 2 * p + 1)
            return carry

        lax.fori_loop(0, (n_far - 1) // 2, far_pair, 0)
        pv(1, n_far - 1)

    a1 = acc_sc[0]
    a2 = acc_sc[1]
    o1 = a1[:V_DIM] / a1[V_DIM:V_DIM + 1]
    o2 = a2[:V_DIM] / a2[V_DIM:V_DIM + 1]
    lam = _lam_value(lq1, lk1, lq2, lk2, lam_init)
    att = (o1 - lam * o2).T
    o_ref[...] = (_rms(att, gs_ref[...]) * (1.0 - lam_init)).astype(o_ref.dtype)


def _attn_prompt(q, kb, vt, rel_bias, lam_vecs, g_subln, lam_init, tq, tk):
    t = q.shape[0]
    nq = t // tq
    assert tq == 2 * tk and t % tq == 0
    far = rel_bias.astype(f32)[_rel_bucket(jnp.asarray(-4 * MAX_DISTANCE))]
    rings = jnp.stack([_bias_ring(rel_bias, d, tq, far) for d in (-tk, 0, tk)], axis=1)
    vec = _const_spec((1, HEAD_DIM))
    return pl.pallas_call(
        functools.partial(_attn_prompt_kernel, tq=tq, tk=tk, lam_init=lam_init),
        grid=(N_HEADS, nq),
        in_specs=[
            pl.BlockSpec((tq, 2 * HEAD_DIM), lambda h, i: (i, h)),
            pl.BlockSpec((t, 2 * HEAD_DIM), lambda h, i: (0, h)),
            pl.BlockSpec((None, t // tk, V_DIM + ONES_ROWS, tk), lambda h, i: (h, 0, 0, 0)),
            pl.BlockSpec((None, 3, 1, 2 * tq), lambda h, i: (h, 0, 0, 0)),
            vec, vec, vec, vec, _const_spec((1, V_DIM)),
        ],
        out_specs=pl.BlockSpec((tq, V_DIM), lambda h, i: (i, h)),
        out_shape=jax.ShapeDtypeStruct((t, V_COLS), bf16),
        scratch_shapes=[pltpu.VMEM((3, tk, tq), f32), pltpu.VMEM((2, 2, tk, tq), f32),
                        pltpu.VMEM((2, 2, tk, tq), bf16), pltpu.VMEM((2, 2, 1, tq), f32),
                        pltpu.VMEM((2, 2, 1, tq), f32), pltpu.VMEM((2, 1, tq), f32),
                        pltpu.VMEM((2, V_DIM + ONES_ROWS, tq), f32)],
        compiler_params=_cparams(("arbitrary", "arbitrary")),
        name="attn_prompt",
    )(q, kb, vt, rings, *lam_vecs, g_subln)


def _attn_sample_kernel(q_ref, kp_ref, vp_ref, kn_ref, vn_ref, nbp_ref, nbn_ref, lq1, lk1, lq2, lk2, gs_ref,
                        o_ref, *, lam_init):
    lam = _lam_value(lq1, lk1, lq2, lk2, lam_init)
    for hd in range(N_HEADS):
        qk_cols = slice(hd * 2 * HEAD_DIM, (hd + 1) * 2 * HEAD_DIM)
        v_cols = slice(hd * V_DIM, (hd + 1) * V_DIM)
        qz = jnp.concatenate(_split_maps(q_ref[:, qk_cols]), axis=0)
        kp_t = kp_ref[qk_cols, :].astype(bf16)
        vp = vp_ref[:, v_cols].astype(bf16)
        kn = kn_ref[:, qk_cols]
        vn = vn_ref[:, v_cols]
        ts = q_ref.shape[0]
        sp = jnp.dot(qz, kp_t, preferred_element_type=f32) + jnp.concatenate([nbp_ref[hd], nbp_ref[hd]], axis=0)
        sn = _qk(qz, kn) + jnp.concatenate([nbn_ref[hd], nbn_ref[hd]], axis=0)
        m = jnp.maximum(jnp.max(sp, axis=1, keepdims=True), jnp.max(sn, axis=1, keepdims=True))
        pp = jnp.exp2(sp - m).astype(bf16)
        pn = jnp.exp2(sn - m).astype(bf16)
        den = jnp.sum(pp.astype(f32), axis=1, keepdims=True) + jnp.sum(pn.astype(f32), axis=1, keepdims=True)
        out = (jnp.dot(pp, vp, preferred_element_type=f32) + jnp.dot(pn, vn, preferred_element_type=f32)) / den
        o_ref[:, v_cols] = _finish_heads(out[:ts], out[ts:], lam, gs_ref[...], lam_init).astype(o_ref.dtype)


def _attn_sample(q, kb, vb, cache_k, cache_v, rel_bias, lam_vecs, g_subln, lam_init, nb_, ts):
    past = cache_k.shape[2]
    zero = jnp.zeros((N_HEADS,), f32)
    nbp = _bias_table(rel_bias, past, ts, 0, past, zero)
    nbn = _bias_table(rel_bias, past, ts, past, ts, zero)
    vec = _const_spec((1, HEAD_DIM))
    return pl.pallas_call(
        functools.partial(_attn_sample_kernel, lam_init=lam_init),
        grid=(nb_,),
        in_specs=[
            pl.BlockSpec((ts, QK_COLS), lambda b: (b, 0)),
            pl.BlockSpec((None, QK_COLS, past), lambda b: (b, 0, 0)),
            pl.BlockSpec((None, past, V_COLS), lambda b: (b, 0, 0)),
            pl.BlockSpec((ts, QK_COLS), lambda b: (b, 0)),
            pl.BlockSpec((ts, V_COLS), lambda b: (b, 0)),
            _const_spec((N_HEADS, ts, past)), _const_spec((N_HEADS, ts, ts)),
            vec, vec, vec, vec, _const_spec((1, V_DIM)),
        ],
        out_specs=pl.BlockSpec((ts, V_COLS), lambda b: (b, 0)),
        out_shape=jax.ShapeDtypeStruct((nb_ * ts, V_COLS), bf16),
        compiler_params=_cparams(("parallel",)),
        name="attn_sample",
    )(q, cache_k, cache_v, kb, vb, nbp, nbn, *lam_vecs, g_subln)


def _conv_lru_kernel(lx_ref, gl_ref, c0_ref, h0_ref, wc_ref, bc_ref, wa_ref, ba_ref, wx_ref, bx_ref, lam_ref,
                     hsg_ref, cout_ref, hout_ref, xbuf, a_sc, b_sc, hcar, *, tt, first_pos_is_zero):
    t = pl.program_id(1)

    @pl.when(t == 0)
    def _():
        xbuf[0:SUBLANES] = c0_ref[...]
        hcar[...] = jnp.broadcast_to(h0_ref[...], hcar.shape)

    xbuf[SUBLANES:SUBLANES + tt] = lx_ref[...]
    ngrp = tt // SUBLANES
    xg = xbuf[...].reshape(ngrp + 1, SUBLANES, LRU_WIDTH)
    sub = lax.broadcasted_iota(jnp.int32, (ngrp, SUBLANES, LRU_WIDTH), 1)
    xc = xg[1:] * wc_ref[CONV_W - 1:CONV_W, :] + bc_ref[...]
    for s in range(1, CONV_W):
        rolled = pltpu.roll(xg, s, axis=1)
        shifted = jnp.where(sub >= s, rolled[1:], rolled[:-1])
        xc = xc + shifted * wc_ref[CONV_W - 1 - s:CONV_W - s, :]
    xc = xc.reshape(tt, LRU_WIDTH)
    tail = xbuf[tt:tt + SUBLANES, :]
    xbuf[0:SUBLANES] = tail
    cout_ref[...] = tail

    xcb = xc.astype(bf16)

    def block_diag(w_ref, b_ref):
        cols = [jnp.dot(xcb[:, n * LRU_BLOCK_W:(n + 1) * LRU_BLOCK_W], w_ref[n], preferred_element_type=f32)
                for n in range(LRU_BLOCKS)]
        return jnp.concatenate(cols, axis=1) + b_ref[...]

    r = _sigmoid(block_diag(wa_ref, ba_ref))
    ig = _sigmoid(block_diag(wx_ref, bx_ref))
    z = -lam_ref[...]
    softplus = jnp.maximum(z, 0.0) + jnp.log1p(jnp.exp(-jnp.abs(z)))
    log_a = -LRU_C * r * softplus
    a = jnp.exp(log_a)
    th = jnp.tanh(log_a)
    m2 = -2.0 * th / (1.0 - th)
    mult = jnp.where(m2 > 0.0, m2 * lax.rsqrt(m2), 0.0)
    if first_pos_is_zero:
        row = lax.broadcasted_iota(jnp.int32, (tt, LRU_WIDTH), 0)
        mult = jnp.where((row == 0) & (t == 0), 1.0, mult)
    bv = mult * ig * xc

    groups = (tt // SUBLANES, SUBLANES, LRU_WIDTH)
    a = a.reshape(groups)
    bv = bv.reshape(groups)
    sub = lax.broadcasted_iota(jnp.int32, groups, 1)
    d = 1
    while d < SUBLANES:
        valid = sub >= d
        a_s = pltpu.roll(a, d, axis=1)
        b_s = pltpu.roll(bv, d, axis=1)
        bv = jnp.where(valid, a * b_s + bv, bv)
        a = jnp.where(valid, a * a_s, a)
        d *= 2
    a_sc[...] = a.reshape(tt, LRU_WIDTH)
    b_sc[...] = bv.reshape(tt, LRU_WIDTH)

    def group(g, h):
        sl = pl.ds(pl.multiple_of(g * SUBLANES, SUBLANES), SUBLANES)
        hg = a_sc[sl, :] * h + b_sc[sl, :]
        b_sc[sl, :] = hg
        return jnp.broadcast_to(hg[SUBLANES - 1:SUBLANES, :], hg.shape)

    h = lax.fori_loop(0, tt // SUBLANES, group, hcar[...], unroll=4)
    hcar[...] = h
    hout_ref[...] = h[0:1, :]
    hsg_ref[...] = (b_sc[...] * gl_ref[...]).astype(hsg_ref.dtype)


def _conv_lru(lx, gl, conv0, h0, w_conv, b_conv, wa, ba, wx, bx, lru_lambda, nb_, tlen, tt, first_pos_is_zero):
    nt = tlen // tt
    rows = pl.BlockSpec((tt, LRU_WIDTH), lambda b, t: (b * nt + t, 0))
    vecw = _const_spec((1, LRU_WIDTH))
    wblk = _const_spec((LRU_BLOCKS, LRU_BLOCK_W, LRU_BLOCK_W))
    return pl.pallas_call(
        functools.partial(_conv_lru_kernel, tt=tt, first_pos_is_zero=first_pos_is_zero),
        grid=(nb_, nt),
        in_specs=[rows, rows,
                  pl.BlockSpec((None, SUBLANES, LRU_WIDTH), lambda b, t: (b, 0, 0)),
                  pl.BlockSpec((None, 1, LRU_WIDTH), lambda b, t: (b, 0, 0)),
                  _const_spec((CONV_W, LRU_WIDTH)), vecw, wblk, vecw, wblk, vecw, vecw],
        out_specs=[rows,
                   pl.BlockSpec((None, SUBLANES, LRU_WIDTH), lambda b, t: (b, 0, 0)),
                   pl.BlockSpec((None, 1, LRU_WIDTH), lambda b, t: (b, 0, 0))],
        out_shape=[jax.ShapeDtypeStruct((nb_ * tlen, LRU_WIDTH), bf16),
                   jax.ShapeDtypeStruct((nb_, SUBLANES, LRU_WIDTH), f32),
                   jax.ShapeDtypeStruct((nb_, 1, LRU_WIDTH), f32)],
        scratch_shapes=[pltpu.VMEM((tt + SUBLANES, LRU_WIDTH), f32), pltpu.VMEM((tt, LRU_WIDTH), f32),
                        pltpu.VMEM((tt, LRU_WIDTH), f32), pltpu.VMEM((SUBLANES, LRU_WIDTH), f32)],
        compiler_params=_cparams(("arbitrary", "arbitrary")),
        name="conv_lru",
    )(lx, gl, conv0, h0, w_conv, b_conv, wa, ba, wx, bx, lru_lambda)


def _merge_kernel(x_ref, att_ref, hsg_ref, gt_ref, wa_ref, wb_ref, wo_ref, gf_ref, wr_ref, br_ref,
                  x1_ref, h2_ref, comb_ref, *, parts):
    rows_per = x_ref.shape[0] // parts
    for part in range(parts):
        rows = slice(part * rows_per, (part + 1) * rows_per)
        bra = jnp.dot(att_ref[rows, :], wa_ref[...], preferred_element_type=f32)
        brb = jnp.dot(hsg_ref[rows, :], wb_ref[...], preferred_element_type=f32)
        gt = gt_ref[rows, :]
        m = (gt[:, :D_MODEL] * bra + gt[:, D_MODEL:] * brb).astype(bf16)
        x1 = x_ref[rows, :] + jnp.dot(m, wo_ref[...], preferred_element_type=f32)
        x1_ref[rows, :] = x1
        h2 = _rms(x1, gf_ref[...])
        h2_hi = h2.astype(bf16)
        h2_ref[rows, :] = h2_hi

        h2_lo = (h2 - h2_hi.astype(f32)).astype(bf16)
        prods = (jnp.dot(h2_hi, wr_ref[...], preferred_element_type=f32)
                 + jnp.dot(h2_lo, wr_ref[...], preferred_element_type=f32))
        logits = prods[:, :LANES] + prods[:, LANES:] + br_ref[...]
        lane = lax.broadcasted_iota(jnp.int32, logits.shape, 1)
        lanef = lane.astype(f32)
        low = jnp.float32(-3.0e38)
        is_grp = (lane >= N_EXPERTS) & (lane < N_EXPERTS + N_GROUPS)
        gl = jnp.where(is_grp, logits, low)
        gmax = jnp.max(gl, axis=1, keepdims=True)
        gidx = jnp.min(jnp.where(is_grp & (gl == gmax), lanef, 1.0e3), axis=1, keepdims=True) - N_EXPERTS
        gden = jnp.sum(jnp.where(is_grp, jnp.exp(gl - gmax), 0.0), axis=1, keepdims=True)
        g_w = 1.0 / gden
        lo = gidx * EXPERTS_PER_GROUP
        in_sel = (lanef >= lo) & (lanef < lo + EXPERTS_PER_GROUP)
        sel = jnp.where(in_sel, logits, low)
        v1 = jnp.max(sel, axis=1, keepdims=True)
        i1 = jnp.min(jnp.where(in_sel & (sel == v1), lanef, 1.0e3), axis=1, keepdims=True)
        in_sel2 = in_sel & (lanef != i1)
        sel2 = jnp.where(in_sel2, logits, low)
        v2 = jnp.max(sel2, axis=1, keepdims=True)
        i2 = jnp.min(jnp.where(in_sel2 & (sel2 == v2), lanef, 1.0e3), axis=1, keepdims=True)
        e2 = jnp.exp(v2 - v1)
        w1 = g_w / (1.0 + e2)
        w2 = g_w * e2 / (1.0 + e2)
        comb_ref[rows, :] = jnp.where(lanef == i1, w1, 0.0) + jnp.where(lanef == i2, w2, 0.0)


def _merge(x, att, hsg, gt, wa, wb, wo, g_ffn, w_router, b_router, tm):
    n = x.shape[0]
    row = lambda w: pl.BlockSpec((tm, w), lambda i: (i, 0))
    sq = _const_spec((D_MODEL, D_MODEL))
    return pl.pallas_call(
        functools.partial(_merge_kernel, parts=2 if tm % 512 == 0 else 1),
        grid=(n // tm,),
        in_specs=[row(D_MODEL), row(V_COLS), row(LRU_WIDTH), row(2 * D_MODEL), sq, sq, sq,
                  _const_spec((1, D_MODEL)), _const_spec((D_MODEL, 2 * LANES)), _const_spec((1, LANES))],
        out_specs=[row(D_MODEL), row(D_MODEL), row(LANES)],
        out_shape=[jax.ShapeDtypeStruct((n, D_MODEL), f32), jax.ShapeDtypeStruct((n, D_MODEL), bf16),
                   jax.ShapeDtypeStruct((n, LANES), f32)],
        compiler_params=_cparams(("parallel",)),
        name="merge_router",
    )(x, att, hsg, gt, wa, wb, wo, g_ffn, w_router, b_router)


EXPERTS_PER_STEP = 8


MOE_CHUNK = 128


def _moe_kernel(h_ref, comb_ref, wg_ref, wu_ref, wd_ref, y_ref, pt_sc, x_sc, c_sc, ys_sc, seg_sc):
    e = pl.program_id(1)
    tm = h_ref.shape[0]
    lane1 = lax.broadcasted_iota(jnp.int32, (1, LANES), 1)

    @pl.when(e == 0)
    def _():
        comb = comb_ref[...]
        ex = lax.broadcasted_iota(jnp.int32, (LANES, LANES), 0)
        gr = lax.broadcasted_iota(jnp.int32, (LANES, LANES), 1)
        in_group = jnp.where((ex // EXPERTS_PER_GROUP == gr) & (ex < N_EXPERTS), 1.0, 0.0).astype(bf16)
        gsum = jnp.dot(comb.astype(bf16), in_group, preferred_element_type=f32)
        member = jnp.where(gsum > 0.0, 1.0, 0.0)
        r_i = lax.broadcasted_iota(jnp.int32, (tm, tm), 0)
        c_i = lax.broadcasted_iota(jnp.int32, (tm, tm), 1)
        tri = jnp.where(gr <= ex, 1.0, 0.0).astype(bf16)
        member_b = member.astype(bf16)
        totals = jnp.zeros((1, LANES), f32)
        blocks = []
        for blk in range(tm // LANES):
            part = jnp.dot(tri, member_b[blk * LANES:(blk + 1) * LANES], preferred_element_type=f32) + totals
            blocks.append(part)
            totals = part[LANES - 1:LANES, :]
        rank = jnp.concatenate(blocks, axis=0)
        offs = jnp.zeros((1, LANES), f32)
        for g in range(1, N_GROUPS):
            prev = jnp.sum(jnp.where(lane1 == g - 1, totals, 0.0), axis=1, keepdims=True)
            offs = offs + jnp.where(lane1 >= g, prev, 0.0)
        seg_sc[0:1, :] = offs.astype(jnp.int32)
        seg_sc[1:2, :] = totals.astype(jnp.int32)
        pos = jnp.sum(member * (offs + rank - 1.0), axis=1, keepdims=True)
        pos_row = jnp.broadcast_to(pos, (tm, LANES)).T[0:1, :].astype(jnp.int32)
        perm = jnp.where(r_i == pos_row, 1.0, 0.0).astype(bf16)
        pt_sc[...] = jnp.where(c_i == pos.astype(jnp.int32), 1.0, 0.0).astype(bf16)
        comb_hi = comb.astype(bf16)
        comb_lo = (comb - comb_hi.astype(f32)).astype(bf16)
        srt = jnp.dot(perm, jnp.concatenate([h_ref[...], comb_hi, comb_lo], axis=1), preferred_element_type=f32)
        x_sc[...] = srt[:, :D_MODEL].astype(bf16)
        c_sc[...] = srt[:, D_MODEL:D_MODEL + LANES] + srt[:, D_MODEL + LANES:]
        ys_sc[...] = jnp.zeros(ys_sc.shape, f32)

    g = (e * EXPERTS_PER_STEP) // EXPERTS_PER_GROUP
    off = jnp.sum(jnp.where(lane1 == g, seg_sc[0:1, :], 0))
    cnt = jnp.sum(jnp.where(lane1 == g, seg_sc[1:2, :], 0))
    lane = lax.broadcasted_iota(jnp.int32, (MOE_CHUNK, LANES), 1)

    def chunk(k, carry):
        rows = pl.ds(pl.multiple_of(k * MOE_CHUNK, MOE_CHUNK), MOE_CHUNK)
        x = x_sc[rows, :]
        cw = c_sc[rows, :]
        hidden = []
        for j in range(EXPERTS_PER_STEP):
            gate = jnp.dot(x, wg_ref[j], preferred_element_type=f32)
            up = jnp.dot(x, wu_ref[j], preferred_element_type=f32)
            c = jnp.sum(jnp.where(lane == e * EXPERTS_PER_STEP + j, cw, 0.0), axis=1, keepdims=True)
            hidden.append((gate * _sigmoid(gate) * up * c).astype(bf16))
        ys_sc[rows, :] += jnp.dot(jnp.concatenate(hidden, axis=1), wd_ref[...], preferred_element_type=f32)
        return carry

    lax.fori_loop(off // MOE_CHUNK, (off + cnt + MOE_CHUNK - 1) // MOE_CHUNK, chunk, 0)

    @pl.when(e == pl.num_programs(1) - 1)
    def _():
        y = jnp.dot(pt_sc[...], ys_sc[...].astype(bf16), preferred_element_type=f32)
        y_ref[...] = y.astype(y_ref.dtype)


def _moe(h2, comb, wg, wu, wd_rows, tm):
    n = h2.shape[0]
    per = EXPERTS_PER_STEP
    assert EXPERTS_PER_GROUP % per == 0 and tm % MOE_CHUNK == 0
    return pl.pallas_call(
        _moe_kernel,
        grid=(n // tm, N_EXPERTS // per),
        in_specs=[pl.BlockSpec((tm, D_MODEL), lambda i, e: (i, 0)),
                  pl.BlockSpec((tm, LANES), lambda i, e: (i, 0)),
                  pl.BlockSpec((per, D_MODEL, D_EXPERT), lambda i, e: (e, 0, 0)),
                  pl.BlockSpec((per, D_MODEL, D_EXPERT), lambda i, e: (e, 0, 0)),
                  pl.BlockSpec((per * D_EXPERT, D_MODEL), lambda i, e: (e, 0))],
        out_specs=pl.BlockSpec((tm, D_MODEL), lambda i, e: (i, 0)),
        out_shape=jax.ShapeDtypeStruct((n, D_MODEL), bf16),
        scratch_shapes=[pltpu.VMEM((tm, tm), bf16), pltpu.VMEM((tm, D_MODEL), bf16),
                        pltpu.VMEM((tm, LANES), f32), pltpu.VMEM((tm, D_MODEL), f32),
                        pltpu.VMEM((SUBLANES, LANES), jnp.int32)],
        compiler_params=_cparams(("arbitrary", "arbitrary")),
        name="moe",
    )(h2, comb, wg, wu, wd_rows)


def _ple_kernel(x1_ref, y_ref, p_ref, gp_ref, wpg_ref, bpg_ref, wpi_ref, gfin_ref, o_ref):
    x2 = x1_ref[...] + y_ref[...]
    hp = _rms(x2, gp_ref[...]).astype(bf16)
    pg = _sigmoid(jnp.dot(hp, wpg_ref[...], preferred_element_type=f32) + bpg_ref[...])
    pin = jnp.dot(p_ref[...].astype(bf16), wpi_ref[...], preferred_element_type=f32)
    x3 = x2 + pg * pin
    o_ref[...] = _rms(x3, gfin_ref[...])


def _ple(x1, y, p, g_ple, wpg, bpg, wpi, g_final, tm):
    n = x1.shape[0]
    row = lambda w: pl.BlockSpec((tm, w), lambda i: (i, 0))
    vec = _const_spec((1, D_MODEL))
    return pl.pallas_call(
        _ple_kernel,
        grid=(n // tm,),
        in_specs=[row(D_MODEL), row(D_MODEL), row(PLE_DIM), vec, _const_spec((D_MODEL, D_MODEL)), vec,
                  _const_spec((PLE_DIM, D_MODEL)), vec],
        out_specs=row(D_MODEL),
        out_shape=jax.ShapeDtypeStruct((n, D_MODEL), f32),
        compiler_params=_cparams(("parallel",)),
        name="ple_final",
    )(x1, y, p, g_ple, wpg, bpg, wpi, g_final)


PROJ_ROWS = 256
ATTN_KEY_TILE = 512
MERGE_ROWS = 512
MOE_ROWS = 1024


def _tile(n, pref):
    return pref if n % pref == 0 else n


def _stream(x, p, conv0, h0, cache, li, rel_bias, w):
    nb_, tlen, _ = x.shape
    n = nb_ * tlen
    xf = x.reshape(n, D_MODEL)
    lam_init = 0.8 - 0.6 * math.exp(-0.3 * li)
    tk = None if cache is not None else ATTN_KEY_TILE
    q, k, kb, v, vb, lx, gl, gt = _in_proj(xf, w["g_mix"], w["w_in"], w["b_merge"], _tile(n, PROJ_ROWS), tk)
    if cache is None:
        att = _attn_prompt(q, kb, vb, rel_bias, w["lam_vecs"], w["g_subln"], lam_init, 2 * tk, tk)
    else:
        att = _attn_sample(q, kb, vb, cache[0], cache[1], rel_bias, w["lam_vecs"], w["g_subln"], lam_init,
                           nb_, tlen)
    conv_pad = jnp.pad(conv0, ((0, 0), (SUBLANES - (CONV_W - 1), 0), (0, 0)))
    hsg, cout, hout = _conv_lru(lx, gl, conv_pad, h0[:, None, :], w["w_conv"], w["b_conv"], w["w_rg_a"],
                                w["b_rg_a"], w["w_rg_x"], w["b_rg_x"], w["lru_lambda"], nb_, tlen,
                                _tile(tlen, PROJ_ROWS), cache is None)
    tm = _tile(n, MERGE_ROWS)
    x1, h2, comb = _merge(xf, att, hsg, gt, w["w_attn_br"], w["w_lru_br"], w["w_out"], w["g_ffn"],
                          w["w_router"], w["b_router"], tm)
    y = _moe(h2, comb, w["w_e_gate"], w["w_e_up"], w["w_e_down"], _tile(n, MOE_ROWS))
    out = _ple(x1, y, p.reshape(n, PLE_DIM), w["g_ple"], w["w_ple_gate"], w["b_ple_gate"], w["w_ple_in"],
               w["g_final"], tm)
    return out, k, v, cout[:, SUBLANES - (CONV_W - 1):, :], hout[:, 0, :]


def kernel(x_prompt, x_sample, cache_k, cache_v, state_conv, state_lru, p_prompt, p_sample, rel_bias, g_mix, w_in, b_merge, lam_q1, lam_k1, lam_q2, lam_k2, g_subln, w_attn_br, w_conv, b_conv, w_rg_a, b_rg_a, w_rg_x, b_rg_x, lru_lambda, w_lru_br, w_out, g_ffn, w_grp, b_grp, w_rt, b_rt, w_e_gate, w_e_up, w_e_down, g_ple, w_ple_gate, b_ple_gate, w_ple_in, g_final):
    depth = w_in.shape[0]
    assert depth == 1, "the final norm is fused into the single layer's last stage"
    bp, tp, _ = x_prompt.shape
    bs, ts, _ = x_sample.shape
    past = cache_k.shape[2]
    li = 0
    row = lambda a: a[li].reshape(1, -1).astype(f32)
    w_router = jnp.concatenate(
        [jnp.transpose(w_rt[li], (1, 0, 2)).reshape(D_MODEL, N_EXPERTS), w_grp[li]], axis=1).astype(f32)
    w_router = jnp.pad(w_router, ((0, 0), (0, LANES - N_EXPERTS - N_GROUPS)))
    w_router_hi = w_router.astype(bf16)
    w_router = jnp.concatenate([w_router_hi, (w_router - w_router_hi.astype(f32)).astype(bf16)], axis=1)
    b_router = jnp.pad(jnp.concatenate([b_rt[li].reshape(-1), b_grp[li]]).astype(f32),
                       (0, LANES - N_EXPERTS - N_GROUPS)).reshape(1, LANES)
    w = dict(
        g_mix=row(g_mix), w_in=w_in[li].astype(bf16), b_merge=row(b_merge),
        lam_vecs=(row(lam_q1), row(lam_k1), row(lam_q2), row(lam_k2)), g_subln=row(g_subln),
        w_attn_br=w_attn_br[li].astype(bf16), w_conv=w_conv[li].astype(f32), b_conv=row(b_conv),
        w_rg_a=w_rg_a[li].astype(bf16), b_rg_a=row(b_rg_a), w_rg_x=w_rg_x[li].astype(bf16), b_rg_x=row(b_rg_x),
        lru_lambda=row(lru_lambda), w_lru_br=w_lru_br[li].astype(bf16), w_out=w_out[li].astype(bf16),
        g_ffn=row(g_ffn), w_router=w_router, b_router=b_router,
        w_e_gate=w_e_gate[li].astype(bf16), w_e_up=w_e_up[li].astype(bf16), w_e_down=w_e_down[li].reshape(N_EXPERTS * D_EXPERT, D_MODEL).astype(bf16),
        g_ple=row(g_ple), w_ple_gate=w_ple_gate[li].astype(bf16), b_ple_gate=row(b_ple_gate),
        w_ple_in=w_ple_in[li].astype(bf16), g_final=g_final.reshape(1, -1).astype(f32),
    )
    conv0 = jnp.zeros((bp, CONV_W - 1, LRU_WIDTH), f32)
    h0 = jnp.zeros((bp, LRU_WIDTH), f32)
    yp, kp, vp, cp, hp = _stream(x_prompt, p_prompt[li], conv0, h0, None, li, rel_bias, w)
    cache = (jnp.transpose(cache_k[li], (0, 2, 3, 4, 1)).reshape(bs, QK_COLS, past), cache_v[li].reshape(bs, past, V_COLS))
    ys, ks, vs, cs, hs = _stream(x_sample, p_sample[li], state_conv[li], state_lru[li], cache, li, rel_bias, w)
    return (yp.reshape(bp, tp, D_MODEL), ys.reshape(bs, ts, D_MODEL),
            kp.reshape(1, bp, tp, N_HEADS, 2, HEAD_DIM), vp.reshape(1, bp, tp, N_HEADS, V_DIM),
            cp[None], hp[None],
            ks.reshape(1, bs, ts, N_HEADS, 2, HEAD_DIM), vs.reshape(1, bs, ts, N_HEADS, V_DIM),
            cs[None], hs[None])
```

```python
import functools
import math

import jax
import jax.numpy as jnp
from jax import lax
from jax.experimental import pallas as pl
from jax.experimental.pallas import tpu as pltpu

f32 = jnp.float32
bf16 = jnp.bfloat16

D_MODEL = 1024
N_HEADS = 8
HEAD_DIM = 64
V_DIM = 128
QK_COLS = N_HEADS * 2 * HEAD_DIM
V_COLS = N_HEADS * V_DIM
LRU_WIDTH = 1024
LRU_BLOCKS = 8
LRU_BLOCK_W = LRU_WIDTH // LRU_BLOCKS
CONV_W = 4
LRU_C = 8.0
CHUNK = 64
N_BUCKETS = 32
MAX_DISTANCE = 128
N_GROUPS = 4
EXPERTS_PER_GROUP = 8
N_EXPERTS = N_GROUPS * EXPERTS_PER_GROUP
D_EXPERT = 256
PLE_DIM = 256
EPS = 1e-6
NEG_INF = -1e30
LOG2E = math.log2(math.e)
ONES_ROWS = 16
LANES = 128
SUBLANES = 8
VMEM_LIMIT = 56 * 1024 * 1024

C_Q, C_K, C_V = 0, QK_COLS, 2 * QK_COLS
C_LX = 2 * QK_COLS + V_COLS
C_LY = C_LX + LRU_WIDTH
C_G = C_LY + LRU_WIDTH
IN_COLS = C_G + 2 * D_MODEL


def _cparams(sem):
    return pltpu.CompilerParams(dimension_semantics=sem, vmem_limit_bytes=VMEM_LIMIT)


def _const_spec(shape):
    nd = len(shape)
    return pl.BlockSpec(shape, lambda *_: (0,) * nd, pipeline_mode=pl.Buffered(1))


def _rms(x, g):
    return x * lax.rsqrt(jnp.mean(x * x, axis=-1, keepdims=True) + EPS) * g


def _sigmoid(x):
    return 1.0 / (1.0 + jnp.exp(-x))


def _gelu_tanh(x):
    c = math.sqrt(2.0 / math.pi)
    return 0.5 * x * (1.0 + jnp.tanh(c * (x + 0.044715 * (x * x * x))))


def _in_proj_kernel(x_ref, g_ref, w_ref, bm_ref, q_ref, k_ref, kb_ref, v_ref, vb_ref, lx_ref, gl_ref, gt_ref,
                    *, transposed_v):
    h = _rms(x_ref[...], g_ref[...]).astype(bf16)

    def proj(lo, width):
        return jnp.dot(h, w_ref[:, lo:lo + width], preferred_element_type=f32)

    q_ref[...] = (proj(C_Q, QK_COLS) * (HEAD_DIM ** -0.5 * LOG2E)).astype(bf16)
    k = proj(C_K, QK_COLS)
    k_ref[...] = k
    kb_ref[...] = k.astype(bf16)
    v = proj(C_V, V_COLS)
    v_ref[...] = v
    if transposed_v:
        for hd in range(N_HEADS):
            vb_ref[hd, :V_DIM, :] = v[:, hd * V_DIM:(hd + 1) * V_DIM].T.astype(bf16)
            vb_ref[hd, V_DIM:, :] = jnp.ones((ONES_ROWS, v.shape[0]), bf16)
    else:
        vb_ref[...] = v.astype(bf16)
    lx_ref[...] = proj(C_LX, LRU_WIDTH)
    gl_ref[...] = _gelu_tanh(proj(C_LY, LRU_WIDTH))
    gt_ref[...] = _sigmoid(proj(C_G, 2 * D_MODEL) + bm_ref[...])


def _in_proj(x, g_mix, w_in_b, b_merge, tm, tq=None):
    n = x.shape[0]
    row = lambda w: pl.BlockSpec((tm, w), lambda i: (i, 0))
    if tq is None:
        vb_shape, vb_spec = jax.ShapeDtypeStruct((n, V_COLS), bf16), row(V_COLS)
    else:
        per = tq // tm
        vb_shape = jax.ShapeDtypeStruct((N_HEADS, n // tq, V_DIM + ONES_ROWS, tq), bf16)
        vb_spec = pl.BlockSpec((N_HEADS, None, V_DIM + ONES_ROWS, tm), lambda i: (0, i // per, 0, i % per))
    outs = [
        jax.ShapeDtypeStruct((n, QK_COLS), bf16),
        jax.ShapeDtypeStruct((n, QK_COLS), f32),
        jax.ShapeDtypeStruct((n, QK_COLS), bf16),
        jax.ShapeDtypeStruct((n, V_COLS), f32),
        vb_shape,
        jax.ShapeDtypeStruct((n, LRU_WIDTH), f32),
        jax.ShapeDtypeStruct((n, LRU_WIDTH), f32),
        jax.ShapeDtypeStruct((n, 2 * D_MODEL), f32),
    ]
    return pl.pallas_call(
        functools.partial(_in_proj_kernel, transposed_v=tq is not None),
        grid=(n // tm,),
        in_specs=[row(D_MODEL), _const_spec((1, D_MODEL)), _const_spec((D_MODEL, IN_COLS)),
                  _const_spec((1, 2 * D_MODEL))],
        out_specs=[row(QK_COLS), row(QK_COLS), row(QK_COLS), row(V_COLS), vb_spec,
                   row(LRU_WIDTH), row(LRU_WIDTH), row(2 * D_MODEL)],
        out_shape=outs,
        compiler_params=_cparams(("parallel",)),
        name="in_proj",
    )(x, g_mix, w_in_b, b_merge)


def _rel_bucket(rel):
    half = N_BUCKETS // 2
    max_exact = half // 2
    ret = jnp.where(rel > 0, half, 0)
    n = jnp.abs(rel)
    nf = jnp.maximum(n, 1).astype(f32)
    large = max_exact + (jnp.log(nf / max_exact) / math.log(MAX_DISTANCE / max_exact)
                         * (half - max_exact)).astype(jnp.int32)
    large = jnp.minimum(large, half - 1)
    return ret + jnp.where(n < max_exact, n, large)


def _toeplitz(t, nq, nk):
    nh, length = t.shape
    a = jnp.broadcast_to(t[:, None, :], (nh, nq, length))
    a = jnp.pad(a, ((0, 0), (0, 0), (0, 1))).reshape(nh, nq * (length + 1))
    a = a[:, :nq * length].reshape(nh, nq, length)
    return a[:, :, nq - 1:nq - 1 + nk]


def _bias_table(rel_bias, qpos0, nq, kpos0, nk, shift):
    rel = jnp.arange(-(nq - 1), nk) + (kpos0 - qpos0)
    t = (rel_bias.astype(f32)[_rel_bucket(rel)].T - shift[:, None]) * LOG2E
    qpos = qpos0 + jnp.arange(nq)
    kpos = kpos0 + jnp.arange(nk)
    mask = (kpos[None, :] // CHUNK) <= (qpos[:, None] // CHUNK)
    return jnp.where(mask[None], _toeplitz(t, nq, nk), NEG_INF)


def _bias_ring(rel_bias, offset, tq, shift):
    r = jnp.arange(2 * tq)
    x = jnp.where(r < tq, r, r - 2 * tq)
    t = (rel_bias.astype(f32)[_rel_bucket(offset - x)].T - shift[:, None]) * LOG2E
    return t[:, None, :]


def _lam_value(lq1, lk1, lq2, lk2, lam_init):
    s1 = jnp.sum(lq1[...] * lk1[...], axis=-1, keepdims=True)
    s2 = jnp.sum(lq2[...] * lk2[...], axis=-1, keepdims=True)
    return jnp.exp(s1) - jnp.exp(s2) + lam_init


def _split_maps(q):
    lane = lax.broadcasted_iota(jnp.int32, q.shape, 1)
    zero = jnp.zeros_like(q)
    return jnp.where(lane < HEAD_DIM, q, zero), jnp.where(lane >= HEAD_DIM, q, zero)


def _qk(qz, kt):
    return lax.dot_general(qz, kt, (((1,), (1,)), ((), ())), preferred_element_type=f32)


def _finish_heads(o1, o2, lam, gs, lam_init):
    att = o1 - lam * o2
    return _rms(att, gs) * (1.0 - lam_init)


def _attn_prompt_kernel(q_ref, k_ref, vt_ref, ring_ref, lq1, lk1, lq2, lk2, gs_ref, o_ref,
                        nb_sc, s_sc, p_sc, al_sc, tm_sc, m_sc, acc_sc, *, tq, tk, lam_init):
    i = pl.program_id(1)
    ring_len = ring_ref.shape[-1]

    def bias_tile(ring):
        return pltpu.roll(jnp.broadcast_to(ring, (tk, ring_len)), 0, 1, stride=1, stride_axis=0)[:, :tq]

    @pl.when(i == 0)
    def _():
        kk = lax.broadcasted_iota(jnp.int32, (tk, tq), 0)
        qq = lax.broadcasted_iota(jnp.int32, (tk, tq), 1)
        nb_sc[0] = jnp.where((kk // CHUNK) <= (qq // CHUNK), bias_tile(ring_ref[1]), NEG_INF)
        nb_sc[1] = jnp.where(((kk + tk) // CHUNK) <= (qq // CHUNK), bias_tile(ring_ref[2]), NEG_INF)
        nb_sc[2] = jnp.full((tk, tq), NEG_INF, f32)

    @pl.when(i == 1)
    def _():
        nb_sc[2] = nb_sc[1]
        nb_sc[1] = nb_sc[0]
        nb_sc[0] = bias_tile(ring_ref[0])

    qt = q_ref[...].astype(f32).T
    row = lax.broadcasted_iota(jnp.int32, qt.shape, 0)
    qz = (jnp.where(row < HEAD_DIM, qt, 0.0).astype(bf16), jnp.where(row >= HEAD_DIM, qt, 0.0).astype(bf16))

    m_sc[...] = jnp.full(m_sc.shape, NEG_INF, f32)
    acc_sc[...] = jnp.zeros(acc_sc.shape, f32)

    def qk(slot, tile, bias, cols=slice(None)):
        kt = k_ref[pl.ds(pl.multiple_of(tile * tk, tk), tk), :]
        for mi in range(2):
            s = jnp.dot(kt, qz[mi][:, cols], preferred_element_type=f32)
            s = s if bias is None else s + bias[:, cols]
            s_sc[mi, slot, :, cols] = s
            tm_sc[mi, slot, :, cols] = jnp.max(s, axis=0, keepdims=True)

    def softmax(slot, cols=slice(None)):
        for mi in range(2):
            s = s_sc[mi, slot, :, cols]
            m_prev = m_sc[mi, :, cols]
            m_new = jnp.maximum(m_prev, tm_sc[mi, slot, :, cols])
            al_sc[mi, slot, :, cols] = jnp.exp2(m_prev - m_new)
            p_sc[mi, slot, :, cols] = jnp.exp2(s - m_new).astype(bf16)
            m_sc[mi, :, cols] = m_new

    def pv(slot, vtile, cols=slice(None)):
        va = vt_ref[vtile]
        for mi in range(2):
            acc_sc[mi, :, cols] = (acc_sc[mi, :, cols] * al_sc[mi, slot, :, cols]
                                   + jnp.dot(va, p_sc[mi, slot, :, cols], preferred_element_type=f32))

    late = slice(tq // 2, tq)

    per = tq // tk
    a0 = jnp.maximum(per * i - 1, 0)
    n_far = jnp.maximum(per * i - 1, 0)
    last_tile = vt_ref.shape[0] - 1

    def far_k(f):
        return jnp.clip(f, 0, jnp.maximum(n_far - 1, 0))

    qk(0, a0, nb_sc[0])
    qk(1, a0 + 1, nb_sc[1])
    softmax(0)
    qk(0, jnp.minimum(a0 + 2, last_tile), nb_sc[2], late)
    softmax(1)
    pv(0, a0)

    @pl.when(i == 0)
    def _():
        pv(1, a0 + 1)

    @pl.when(i > 0)
    def _():
        qk(1, 0, None)
        softmax(0, late)
        pv(1, a0 + 1)
        qk(0, far_k(1), None)
        softmax(1)
        pv(0, a0 + 2, late)

        def far_pair(p, carry):
            qk(1, 2 * p + 2, None)
            softmax(0)
            pv(1, 2 * p)
            qk(0, far_k(2 * p + 3), None)
            softmax(1)
            pv(0, 2 * p + 1)
            return carry

        lax.fori_loop(0, (n_far - 1) // 2, far_pair, 0)
        pv(1, n_far - 1)

    a1 = acc_sc[0]
    a2 = acc_sc[1]
    o1 = a1[:V_DIM] / a1[V_DIM:V_DIM + 1]
    o2 = a2[:V_DIM] / a2[V_DIM:V_DIM + 1]
    lam = _lam_value(lq1, lk1, lq2, lk2, lam_init)
    att = (o1 - lam * o2).T
    o_ref[...] = (_rms(att, gs_ref[...]) * (1.0 - lam_init)).astype(o_ref.dtype)


def _attn_prompt(q, kb, vt, rel_bias, lam_vecs, g_subln, lam_init, tq, tk):
    t = q.shape[0]
    nq = t // tq
    assert tq == 2 * tk and t % tq == 0
    far = rel_bias.astype(f32)[_rel_bucket(jnp.asarray(-4 * MAX_DISTANCE))]
    rings = jnp.stack([_bias_ring(rel_bias, d, tq, far) for d in (-tk, 0, tk)], axis=1)
    vec = _const_spec((1, HEAD_DIM))
    return pl.pallas_call(
        functools.partial(_attn_prompt_kernel, tq=tq, tk=tk, lam_init=lam_init),
        grid=(N_HEADS, nq),
        in_specs=[
            pl.BlockSpec((tq, 2 * HEAD_DIM), lambda h, i: (i, h)),
            pl.BlockSpec((t, 2 * HEAD_DIM), lambda h, i: (0, h)),
            pl.BlockSpec((None, t // tk, V_DIM + ONES_ROWS, tk), lambda h, i: (h, 0, 0, 0)),
            pl.BlockSpec((None, 3, 1, 2 * tq), lambda h, i: (h, 0, 0, 0)),
            vec, vec, vec, vec, _const_spec((1, V_DIM)),
        ],
        out_specs=pl.BlockSpec((tq, V_DIM), lambda h, i: (i, h)),
        out_shape=jax.ShapeDtypeStruct((t, V_COLS), bf16),
        scratch_shapes=[pltpu.VMEM((3, tk, tq), f32), pltpu.VMEM((2, 2, tk, tq), f32),
                        pltpu.VMEM((2, 2, tk, tq), bf16), pltpu.VMEM((2, 2, 1, tq), f32),
                        pltpu.VMEM((2, 2, 1, tq), f32), pltpu.VMEM((2, 1, tq), f32),
                        pltpu.VMEM((2, V_DIM + ONES_ROWS, tq), f32)],
        compiler_params=_cparams(("arbitrary", "arbitrary")),
        name="attn_prompt",
    )(q, kb, vt, rings, *lam_vecs, g_subln)


def _attn_sample_kernel(q_ref, kp_ref, vp_ref, kn_ref, vn_ref, nbp_ref, nbn_ref, lq1, lk1, lq2, lk2, gs_ref,
                        o_ref, *, lam_init):
    lam = _lam_value(lq1, lk1, lq2, lk2, lam_init)
    for hd in range(N_HEADS):
        qk_cols = slice(hd * 2 * HEAD_DIM, (hd + 1) * 2 * HEAD_DIM)
        v_cols = slice(hd * V_DIM, (hd + 1) * V_DIM)
        qz = jnp.concatenate(_split_maps(q_ref[:, qk_cols]), axis=0)
        kp_t = kp_ref[qk_cols, :].astype(bf16)
        vp = vp_ref[:, hd, :].astype(bf16)
        kn = kn_ref[:, qk_cols]
        vn = vn_ref[:, v_cols]
        ts = q_ref.shape[0]
        sp = jnp.dot(qz, kp_t, preferred_element_type=f32) + jnp.concatenate([nbp_ref[hd], nbp_ref[hd]], axis=0)
        sn = _qk(qz, kn) + jnp.concatenate([nbn_ref[hd], nbn_ref[hd]], axis=0)
        m = jnp.maximum(jnp.max(sp, axis=1, keepdims=True), jnp.max(sn, axis=1, keepdims=True))
        pp = jnp.exp2(sp - m).astype(bf16)
        pn = jnp.exp2(sn - m).astype(bf16)
        den = jnp.sum(pp.astype(f32), axis=1, keepdims=True) + jnp.sum(pn.astype(f32), axis=1, keepdims=True)
        out = (jnp.dot(pp, vp, preferred_element_type=f32) + jnp.dot(pn, vn, preferred_element_type=f32)) / den
        o_ref[:, v_cols] = _finish_heads(out[:ts], out[ts:], lam, gs_ref[...], lam_init).astype(o_ref.dtype)


def _attn_sample(q, kb, vb, cache_k, cache_v, rel_bias, lam_vecs, g_subln, lam_init, nb_, ts):
    past = cache_k.shape[2]
    zero = jnp.zeros((N_HEADS,), f32)
    nbp = _bias_table(rel_bias, past, ts, 0, past, zero)
    nbn = _bias_table(rel_bias, past, ts, past, ts, zero)
    vec = _const_spec((1, HEAD_DIM))
    return pl.pallas_call(
        functools.partial(_attn_sample_kernel, lam_init=lam_init),
        grid=(nb_,),
        in_specs=[
            pl.BlockSpec((ts, QK_COLS), lambda b: (b, 0)),
            pl.BlockSpec((None, QK_COLS, past), lambda b: (b, 0, 0)),
            pl.BlockSpec((None, past, N_HEADS, V_DIM), lambda b: (b, 0, 0, 0)),
            pl.BlockSpec((ts, QK_COLS), lambda b: (b, 0)),
            pl.BlockSpec((ts, V_COLS), lambda b: (b, 0)),
            _const_spec((N_HEADS, ts, past)), _const_spec((N_HEADS, ts, ts)),
            vec, vec, vec, vec, _const_spec((1, V_DIM)),
        ],
        out_specs=pl.BlockSpec((ts, V_COLS), lambda b: (b, 0)),
        out_shape=jax.ShapeDtypeStruct((nb_ * ts, V_COLS), bf16),
        compiler_params=_cparams(("parallel",)),
        name="attn_sample",
    )(q, cache_k, cache_v, kb, vb, nbp, nbn, *lam_vecs, g_subln)


def _conv_lru_kernel(lx_ref, gl_ref, c0_ref, h0_ref, wc_ref, bc_ref, wa_ref, ba_ref, wx_ref, bx_ref, lam_ref,
                     hsg_ref, cout_ref, hout_ref, xbuf, a_sc, b_sc, hcar, *, tt, first_pos_is_zero):
    t = pl.program_id(1)

    @pl.when(t == 0)
    def _():
        xbuf[0:SUBLANES] = c0_ref[...]
        hcar[...] = jnp.broadcast_to(h0_ref[...], hcar.shape)

    xbuf[SUBLANES:SUBLANES + tt] = lx_ref[...]
    ngrp = tt // SUBLANES
    xg = xbuf[...].reshape(ngrp + 1, SUBLANES, LRU_WIDTH)
    sub = lax.broadcasted_iota(jnp.int32, (ngrp, SUBLANES, LRU_WIDTH), 1)
    xc = xg[1:] * wc_ref[CONV_W - 1:CONV_W, :] + bc_ref[...]
    for s in range(1, CONV_W):
        rolled = pltpu.roll(xg, s, axis=1)
        shifted = jnp.where(sub >= s, rolled[1:], rolled[:-1])
        xc = xc + shifted * wc_ref[CONV_W - 1 - s:CONV_W - s, :]
    xc = xc.reshape(tt, LRU_WIDTH)
    tail = xbuf[tt:tt + SUBLANES, :]
    xbuf[0:SUBLANES] = tail
    cout_ref[...] = tail

    xcb = xc.astype(bf16)

    def block_diag(w_ref, b_ref):
        cols = [jnp.dot(xcb[:, n * LRU_BLOCK_W:(n + 1) * LRU_BLOCK_W], w_ref[n], preferred_element_type=f32)
                for n in range(LRU_BLOCKS)]
        return jnp.concatenate(cols, axis=1) + b_ref[...]

    r = _sigmoid(block_diag(wa_ref, ba_ref))
    ig = _sigmoid(block_diag(wx_ref, bx_ref))
    z = -lam_ref[...]
    softplus = jnp.maximum(z, 0.0) + jnp.log1p(jnp.exp(-jnp.abs(z)))
    log_a = -LRU_C * r * softplus
    a = jnp.exp(log_a)
    th = jnp.tanh(log_a)
    m2 = -2.0 * th / (1.0 - th)
    mult = jnp.where(m2 > 0.0, m2 * lax.rsqrt(m2), 0.0)
    if first_pos_is_zero:
        row = lax.broadcasted_iota(jnp.int32, (tt, LRU_WIDTH), 0)
        mult = jnp.where((row == 0) & (t == 0), 1.0, mult)
    bv = mult * ig * xc

    groups = (tt // SUBLANES, SUBLANES, LRU_WIDTH)
    a = a.reshape(groups)
    bv = bv.reshape(groups)
    sub = lax.broadcasted_iota(jnp.int32, groups, 1)
    d = 1
    while d < SUBLANES:
        valid = sub >= d
        a_s = pltpu.roll(a, d, axis=1)
        b_s = pltpu.roll(bv, d, axis=1)
        bv = jnp.where(valid, a * b_s + bv, bv)
        a = jnp.where(valid, a * a_s, a)
        d *= 2
    a_sc[...] = a.reshape(tt, LRU_WIDTH)
    b_sc[...] = bv.reshape(tt, LRU_WIDTH)

    def group(g, h):
        sl = pl.ds(pl.multiple_of(g * SUBLANES, SUBLANES), SUBLANES)
        hg = a_sc[sl, :] * h + b_sc[sl, :]
        b_sc[sl, :] = hg
        return jnp.broadcast_to(hg[SUBLANES - 1:SUBLANES, :], hg.shape)

    h = lax.fori_loop(0, tt // SUBLANES, group, hcar[...], unroll=4)
    hcar[...] = h
    hout_ref[...] = h[0:1, :]
    hsg_ref[...] = (b_sc[...] * gl_ref[...]).astype(hsg_ref.dtype)


def _conv_lru(lx, gl, conv0, h0, w_conv, b_conv, wa, ba, wx, bx, lru_lambda, nb_, tlen, tt, first_pos_is_zero):
    nt = tlen // tt
    rows = pl.BlockSpec((tt, LRU_WIDTH), lambda b, t: (b * nt + t, 0))
    vecw = _const_spec((1, LRU_WIDTH))
    wblk = _const_spec((LRU_BLOCKS, LRU_BLOCK_W, LRU_BLOCK_W))
    return pl.pallas_call(
        functools.partial(_conv_lru_kernel, tt=tt, first_pos_is_zero=first_pos_is_zero),
        grid=(nb_, nt),
        in_specs=[rows, rows,
                  pl.BlockSpec((None, SUBLANES, LRU_WIDTH), lambda b, t: (b, 0, 0)),
                  pl.BlockSpec((None, 1, LRU_WIDTH), lambda b, t: (b, 0, 0)),
                  _const_spec((CONV_W, LRU_WIDTH)), vecw, wblk, vecw, wblk, vecw, vecw],
        out_specs=[rows,
                   pl.BlockSpec((None, SUBLANES, LRU_WIDTH), lambda b, t: (b, 0, 0)),
                   pl.BlockSpec((None, 1, LRU_WIDTH), lambda b, t: (b, 0, 0))],
        out_shape=[jax.ShapeDtypeStruct((nb_ * tlen, LRU_WIDTH), bf16),
                   jax.ShapeDtypeStruct((nb_, SUBLANES, LRU_WIDTH), f32),
                   jax.ShapeDtypeStruct((nb_, 1, LRU_WIDTH), f32)],
        scratch_shapes=[pltpu.VMEM((tt + SUBLANES, LRU_WIDTH), f32), pltpu.VMEM((tt, LRU_WIDTH), f32),
                        pltpu.VMEM((tt, LRU_WIDTH), f32), pltpu.VMEM((SUBLANES, LRU_WIDTH), f32)],
        compiler_params=_cparams(("arbitrary", "arbitrary")),
        name="conv_lru",
    )(lx, gl, conv0, h0, w_conv, b_conv, wa, ba, wx, bx, lru_lambda)


def _merge_kernel(x_ref, att_ref, hsg_ref, gt_ref, wa_ref, wb_ref, wo_ref, gf_ref, wr_ref, br_ref,
                  x1_ref, h2_ref, comb_ref, *, parts):
    rows_per = x_ref.shape[0] // parts
    for part in range(parts):
        rows = slice(part * rows_per, (part + 1) * rows_per)
        bra = jnp.dot(att_ref[rows, :], wa_ref[...], preferred_element_type=f32)
        brb = jnp.dot(hsg_ref[rows, :], wb_ref[...], preferred_element_type=f32)
        gt = gt_ref[rows, :]
        m = (gt[:, :D_MODEL] * bra + gt[:, D_MODEL:] * brb).astype(bf16)
        x1 = x_ref[rows, :] + jnp.dot(m, wo_ref[...], preferred_element_type=f32)
        x1_ref[rows, :] = x1
        h2 = _rms(x1, gf_ref[...])
        h2_hi = h2.astype(bf16)
        h2_ref[rows, :] = h2_hi

        h2_lo = (h2 - h2_hi.astype(f32)).astype(bf16)
        prods = (jnp.dot(h2_hi, wr_ref[...], preferred_element_type=f32)
                 + jnp.dot(h2_lo, wr_ref[...], preferred_element_type=f32))
        logits = prods[:, :LANES] + prods[:, LANES:] + br_ref[...]
        lane = lax.broadcasted_iota(jnp.int32, logits.shape, 1)
        lanef = lane.astype(f32)
        low = jnp.float32(-3.0e38)
        is_grp = (lane >= N_EXPERTS) & (lane < N_EXPERTS + N_GROUPS)
        gl = jnp.where(is_grp, logits, low)
        gmax = jnp.max(gl, axis=1, keepdims=True)
        gidx = jnp.min(jnp.where(is_grp & (gl == gmax), lanef, 1.0e3), axis=1, keepdims=True) - N_EXPERTS
        gden = jnp.sum(jnp.where(is_grp, jnp.exp(gl - gmax), 0.0), axis=1, keepdims=True)
        g_w = 1.0 / gden
        lo = gidx * EXPERTS_PER_GROUP
        in_sel = (lanef >= lo) & (lanef < lo + EXPERTS_PER_GROUP)
        sel = jnp.where(in_sel, logits, low)
        v1 = jnp.max(sel, axis=1, keepdims=True)
        i1 = jnp.min(jnp.where(in_sel & (sel == v1), lanef, 1.0e3), axis=1, keepdims=True)
        in_sel2 = in_sel & (lanef != i1)
        sel2 = jnp.where(in_sel2, logits, low)
        v2 = jnp.max(sel2, axis=1, keepdims=True)
        i2 = jnp.min(jnp.where(in_sel2 & (sel2 == v2), lanef, 1.0e3), axis=1, keepdims=True)
        e2 = jnp.exp(v2 - v1)
        w1 = g_w / (1.0 + e2)
        w2 = g_w * e2 / (1.0 + e2)
        comb_ref[rows, :] = jnp.where(lanef == i1, w1, 0.0) + jnp.where(lanef == i2, w2, 0.0)


def _merge(x, att, hsg, gt, wa, wb, wo, g_ffn, w_router, b_router, tm):
    n = x.shape[0]
    row = lambda w: pl.BlockSpec((tm, w), lambda i: (i, 0))
    sq = _const_spec((D_MODEL, D_MODEL))
    return pl.pallas_call(
        functools.partial(_merge_kernel, parts=2 if tm % 512 == 0 else 1),
        grid=(n // tm,),
        in_specs=[row(D_MODEL), row(V_COLS), row(LRU_WIDTH), row(2 * D_MODEL), sq, sq, sq,
                  _const_spec((1, D_MODEL)), _const_spec((D_MODEL, 2 * LANES)), _const_spec((1, LANES))],
        out_specs=[row(D_MODEL), row(D_MODEL), row(LANES)],
        out_shape=[jax.ShapeDtypeStruct((n, D_MODEL), f32), jax.ShapeDtypeStruct((n, D_MODEL), bf16),
                   jax.ShapeDtypeStruct((n, LANES), f32)],
        compiler_params=_cparams(("parallel",)),
        name="merge_router",
    )(x, att, hsg, gt, wa, wb, wo, g_ffn, w_router, b_router)


EXPERTS_PER_STEP = 8


MOE_CHUNK = 128


def _moe_kernel(h_ref, comb_ref, wg_ref, wu_ref, wd_ref, y_ref, pt_sc, x_sc, c_sc, ys_sc, seg_sc):
    e = pl.program_id(1)
    tm = h_ref.shape[0]
    lane1 = lax.broadcasted_iota(jnp.int32, (1, LANES), 1)

    @pl.when(e == 0)
    def _():
        comb = comb_ref[...]
        ex = lax.broadcasted_iota(jnp.int32, (LANES, LANES), 0)
        gr = lax.broadcasted_iota(jnp.int32, (LANES, LANES), 1)
        in_group = jnp.where((ex // EXPERTS_PER_GROUP == gr) & (ex < N_EXPERTS), 1.0, 0.0).astype(bf16)
        gsum = jnp.dot(comb.astype(bf16), in_group, preferred_element_type=f32)
        member = jnp.where(gsum > 0.0, 1.0, 0.0)
        r_i = lax.broadcasted_iota(jnp.int32, (tm, tm), 0)
        c_i = lax.broadcasted_iota(jnp.int32, (tm, tm), 1)
        tri = jnp.where(gr <= ex, 1.0, 0.0).astype(bf16)
        member_b = member.astype(bf16)
        totals = jnp.zeros((1, LANES), f32)
        blocks = []
        for blk in range(tm // LANES):
            part = jnp.dot(tri, member_b[blk * LANES:(blk + 1) * LANES], preferred_element_type=f32) + totals
            blocks.append(part)
            totals = part[LANES - 1:LANES, :]
        rank = jnp.concatenate(blocks, axis=0)
        offs = jnp.zeros((1, LANES), f32)
        for g in range(1, N_GROUPS):
            prev = jnp.sum(jnp.where(lane1 == g - 1, totals, 0.0), axis=1, keepdims=True)
            offs = offs + jnp.where(lane1 >= g, prev, 0.0)
        seg_sc[0:1, :] = offs.astype(jnp.int32)
        seg_sc[1:2, :] = totals.astype(jnp.int32)
        pos = jnp.sum(member * (offs + rank - 1.0), axis=1, keepdims=True)
        pos_row = jnp.broadcast_to(pos, (tm, LANES)).T[0:1, :].astype(jnp.int32)
        perm = jnp.where(r_i == pos_row, 1.0, 0.0).astype(bf16)
        pt_sc[...] = jnp.where(c_i == pos.astype(jnp.int32), 1.0, 0.0).astype(bf16)
        comb_hi = comb.astype(bf16)
        comb_lo = (comb - comb_hi.astype(f32)).astype(bf16)
        srt = jnp.dot(perm, jnp.concatenate([h_ref[...], comb_hi, comb_lo], axis=1), preferred_element_type=f32)
        x_sc[...] = srt[:, :D_MODEL].astype(bf16)
        c_sc[...] = srt[:, D_MODEL:D_MODEL + LANES] + srt[:, D_MODEL + LANES:]
        ys_sc[...] = jnp.zeros(ys_sc.shape, f32)

    g = (e * EXPERTS_PER_STEP) // EXPERTS_PER_GROUP
    off = jnp.sum(jnp.where(lane1 == g, seg_sc[0:1, :], 0))
    cnt = jnp.sum(jnp.where(lane1 == g, seg_sc[1:2, :], 0))
    lane = lax.broadcasted_iota(jnp.int32, (MOE_CHUNK, LANES), 1)

    def chunk(k, carry):
        rows = pl.ds(pl.multiple_of(k * MOE_CHUNK, MOE_CHUNK), MOE_CHUNK)
        x = x_sc[rows, :]
        cw = c_sc[rows, :]
        hidden = []
        for j in range(EXPERTS_PER_STEP):
            gate = jnp.dot(x, wg_ref[j], preferred_element_type=f32)
            up = jnp.dot(x, wu_ref[j], preferred_element_type=f32)
            c = jnp.sum(jnp.where(lane == e * EXPERTS_PER_STEP + j, cw, 0.0), axis=1, keepdims=True)
            hidden.append((gate * _sigmoid(gate) * up * c).astype(bf16))
        ys_sc[rows, :] += jnp.dot(jnp.concatenate(hidden, axis=1), wd_ref[...], preferred_element_type=f32)
        return carry

    lax.fori_loop(off // MOE_CHUNK, (off + cnt + MOE_CHUNK - 1) // MOE_CHUNK, chunk, 0)

    @pl.when(e == pl.num_programs(1) - 1)
    def _():
        y = jnp.dot(pt_sc[...], ys_sc[...].astype(bf16), preferred_element_type=f32)
        y_ref[...] = y.astype(y_ref.dtype)


def _moe(h2, comb, wg, wu, wd_rows, tm):
    n = h2.shape[0]
    per = EXPERTS_PER_STEP
    assert EXPERTS_PER_GROUP % per == 0 and tm % MOE_CHUNK == 0
    return pl.pallas_call(
        _moe_kernel,
        grid=(n // tm, N_EXPERTS // per),
        in_specs=[pl.BlockSpec((tm, D_MODEL), lambda i, e: (i, 0)),
                  pl.BlockSpec((tm, LANES), lambda i, e: (i, 0)),
                  pl.BlockSpec((per, D_MODEL, D_EXPERT), lambda i, e: (e, 0, 0)),
                  pl.BlockSpec((per, D_MODEL, D_EXPERT), lambda i, e: (e, 0, 0)),
                  pl.BlockSpec((per * D_EXPERT, D_MODEL), lambda i, e: (e, 0))],
        out_specs=pl.BlockSpec((tm, D_MODEL), lambda i, e: (i, 0)),
        out_shape=jax.ShapeDtypeStruct((n, D_MODEL), bf16),
        scratch_shapes=[pltpu.VMEM((tm, tm), bf16), pltpu.VMEM((tm, D_MODEL), bf16),
                        pltpu.VMEM((tm, LANES), f32), pltpu.VMEM((tm, D_MODEL), f32),
                        pltpu.VMEM((SUBLANES, LANES), jnp.int32)],
        compiler_params=_cparams(("arbitrary", "arbitrary")),
        name="moe",
    )(h2, comb, wg, wu, wd_rows)


def _ple_kernel(x1_ref, y_ref, p_ref, gp_ref, wpg_ref, bpg_ref, wpi_ref, gfin_ref, o_ref):
    x2 = x1_ref[...] + y_ref[...]
    hp = _rms(x2, gp_ref[...]).astype(bf16)
    pg = _sigmoid(jnp.dot(hp, wpg_ref[...], preferred_element_type=f32) + bpg_ref[...])
    pin = jnp.dot(p_ref[...].astype(bf16), wpi_ref[...], preferred_element_type=f32)
    x3 = x2 + pg * pin
    o_ref[...] = _rms(x3, gfin_ref[...])


def _ple(x1, y, p, g_ple, wpg, bpg, wpi, g_final, tm):
    n = x1.shape[0]
    row = lambda w: pl.BlockSpec((tm, w), lambda i: (i, 0))
    vec = _const_spec((1, D_MODEL))
    return pl.pallas_call(
        _ple_kernel,
        grid=(n // tm,),
        in_specs=[row(D_MODEL), row(D_MODEL), row(PLE_DIM), vec, _const_spec((D_MODEL, D_MODEL)), vec,
                  _const_spec((PLE_DIM, D_MODEL)), vec],
        out_specs=row(D_MODEL),
        out_shape=jax.ShapeDtypeStruct((n, D_MODEL), f32),
        compiler_params=_cparams(("parallel",)),
        name="ple_final",
    )(x1, y, p, g_ple, wpg, bpg, wpi, g_final)


PROJ_ROWS = 256
ATTN_KEY_TILE = 512
MERGE_ROWS = 512
MOE_ROWS = 1024


def _tile(n, pref):
    return pref if n % pref == 0 else n


def _stream(x, p, conv0, h0, cache, li, rel_bias, w):
    nb_, tlen, _ = x.shape
    n = nb_ * tlen
    xf = x.reshape(n, D_MODEL)
    lam_init = 0.8 - 0.6 * math.exp(-0.3 * li)
    tk = None if cache is not None else ATTN_KEY_TILE
    q, k, kb, v, vb, lx, gl, gt = _in_proj(xf, w["g_mix"], w["w_in"], w["b_merge"], _tile(n, PROJ_ROWS), tk)
    if cache is None:
        att = _attn_prompt(q, kb, vb, rel_bias, w["lam_vecs"], w["g_subln"], lam_init, 2 * tk, tk)
    else:
        att = _attn_sample(q, kb, vb, cache[0], cache[1], rel_bias, w["lam_vecs"], w["g_subln"], lam_init,
                           nb_, tlen)
    conv_pad = jnp.pad(conv0, ((0, 0), (SUBLANES - (CONV_W - 1), 0), (0, 0)))
    hsg, cout, hout = _conv_lru(lx, gl, conv_pad, h0[:, None, :], w["w_conv"], w["b_conv"], w["w_rg_a"],
                                w["b_rg_a"], w["w_rg_x"], w["b_rg_x"], w["lru_lambda"], nb_, tlen,
                                _tile(tlen, PROJ_ROWS), cache is None)
    tm = _tile(n, MERGE_ROWS)
    x1, h2, comb = _merge(xf, att, hsg, gt, w["w_attn_br"], w["w_lru_br"], w["w_out"], w["g_ffn"],
                          w["w_router"], w["b_router"], tm)
    y = _moe(h2, comb, w["w_e_gate"], w["w_e_up"], w["w_e_down"], _tile(n, MOE_ROWS))
    out = _ple(x1, y, p.reshape(n, PLE_DIM), w["g_ple"], w["w_ple_gate"], w["b_ple_gate"], w["w_ple_in"],
               w["g_final"], tm)
    return out, k, v, cout[:, SUBLANES - (CONV_W - 1):, :], hout[:, 0, :]


def kernel(x_prompt, x_sample, cache_k, cache_v, state_conv, state_lru, p_prompt, p_sample, rel_bias, g_mix, w_in, b_merge, lam_q1, lam_k1, lam_q2, lam_k2, g_subln, w_attn_br, w_conv, b_conv, w_rg_a, b_rg_a, w_rg_x, b_rg_x, lru_lambda, w_lru_br, w_out, g_ffn, w_grp, b_grp, w_rt, b_rt, w_e_gate, w_e_up, w_e_down, g_ple, w_ple_gate, b_ple_gate, w_ple_in, g_final):
    depth = w_in.shape[0]
    assert depth == 1, "the final norm is fused into the single layer's last stage"
    bp, tp, _ = x_prompt.shape
    bs, ts, _ = x_sample.shape
    past = cache_k.shape[2]
    li = 0
    row = lambda a: a[li].reshape(1, -1).astype(f32)
    w_router = jnp.concatenate(
        [jnp.transpose(w_rt[li], (1, 0, 2)).reshape(D_MODEL, N_EXPERTS), w_grp[li]], axis=1).astype(f32)
    w_router = jnp.pad(w_router, ((0, 0), (0, LANES - N_EXPERTS - N_GROUPS)))
    w_router_hi = w_router.astype(bf16)
    w_router = jnp.concatenate([w_router_hi, (w_router - w_router_hi.astype(f32)).astype(bf16)], axis=1)
    b_router = jnp.pad(jnp.concatenate([b_rt[li].reshape(-1), b_grp[li]]).astype(f32),
                       (0, LANES - N_EXPERTS - N_GROUPS)).reshape(1, LANES)
    w = dict(
        g_mix=row(g_mix), w_in=w_in[li].astype(bf16), b_merge=row(b_merge),
        lam_vecs=(row(lam_q1), row(lam_k1), row(lam_q2), row(lam_k2)), g_subln=row(g_subln),
        w_attn_br=w_attn_br[li].astype(bf16), w_conv=w_conv[li].astype(f32), b_conv=row(b_conv),
        w_rg_a=w_rg_a[li].astype(bf16), b_rg_a=row(b_rg_a), w_rg_x=w_rg_x[li].astype(bf16), b_rg_x=row(b_rg_x),
        lru_lambda=row(lru_lambda), w_lru_br=w_lru_br[li].astype(bf16), w_out=w_out[li].astype(bf16),
        g_ffn=row(g_ffn), w_router=w_router, b_router=b_router,
        w_e_gate=w_e_gate[li].astype(bf16), w_e_up=w_e_up[li].astype(bf16), w_e_down=w_e_down[li].reshape(N_EXPERTS * D_EXPERT, D_MODEL).astype(bf16),
        g_ple=row(g_ple), w_ple_gate=w_ple_gate[li].astype(bf16), b_ple_gate=row(b_ple_gate),
        w_ple_in=w_ple_in[li].astype(bf16), g_final=g_final.reshape(1, -1).astype(f32),
    )
    conv0 = jnp.zeros((bp, CONV_W - 1, LRU_WIDTH), f32)
    h0 = jnp.zeros((bp, LRU_WIDTH), f32)
    yp, kp, vp, cp, hp = _stream(x_prompt, p_prompt[li], conv0, h0, None, li, rel_bias, w)
    cache = (jnp.transpose(cache_k[li], (0, 2, 3, 4, 1)).reshape(bs, QK_COLS, past), cache_v[li])
    ys, ks, vs, cs, hs = _stream(x_sample, p_sample[li], state_conv[li], state_lru[li], cache, li, rel_bias, w)
    return (yp.reshape(bp, tp, D_MODEL), ys.reshape(bs, ts, D_MODEL),
            kp.reshape(1, bp, tp, N_HEADS, 2, HEAD_DIM), vp.reshape(1, bp, tp, N_HEADS, V_DIM),
            cp[None], hp[None],
            ks.reshape(1, bs, ts, N_HEADS, 2, HEAD_DIM), vs.reshape(1, bs, ts, N_HEADS, V_DIM),
            cs[None], hs[None])
```

```python
import functools
import math

import jax
import jax.numpy as jnp
from jax import lax
from jax.experimental import pallas as pl
from jax.experimental.pallas import tpu as pltpu

f32 = jnp.float32
bf16 = jnp.bfloat16

D_MODEL = 1024
N_HEADS = 8
HEAD_DIM = 64
V_DIM = 128
QK_COLS = N_HEADS * 2 * HEAD_DIM
V_COLS = N_HEADS * V_DIM
LRU_WIDTH = 1024
LRU_BLOCKS = 8
LRU_BLOCK_W = LRU_WIDTH // LRU_BLOCKS
CONV_W = 4
LRU_C = 8.0
CHUNK = 64
N_BUCKETS = 32
MAX_DISTANCE = 128
N_GROUPS = 4
EXPERTS_PER_GROUP = 8
N_EXPERTS = N_GROUPS * EXPERTS_PER_GROUP
D_EXPERT = 256
PLE_DIM = 256
EPS = 1e-6
NEG_INF = -1e30
LOG2E = math.log2(math.e)
ONES_ROWS = 16
LANES = 128
SUBLANES = 8
VMEM_LIMIT = 56 * 1024 * 1024

C_Q, C_K, C_V = 0, QK_COLS, 2 * QK_COLS
C_LX = 2 * QK_COLS + V_COLS
C_LY = C_LX + LRU_WIDTH
C_G = C_LY + LRU_WIDTH
IN_COLS = C_G + 2 * D_MODEL


def _cparams(sem):
    return pltpu.CompilerParams(dimension_semantics=sem, vmem_limit_bytes=VMEM_LIMIT)


def _const_spec(shape):
    nd = len(shape)
    return pl.BlockSpec(shape, lambda *_: (0,) * nd, pipeline_mode=pl.Buffered(1))


def _rms(x, g):
    return x * lax.rsqrt(jnp.mean(x * x, axis=-1, keepdims=True) + EPS) * g


def _sigmoid(x):
    return 1.0 / (1.0 + jnp.exp(-x))


def _gelu_tanh(x):
    c = math.sqrt(2.0 / math.pi)
    return 0.5 * x * (1.0 + jnp.tanh(c * (x + 0.044715 * (x * x * x))))


def _in_proj_kernel(x_ref, g_ref, w_ref, bm_ref, q_ref, k_ref, kb_ref, v_ref, vb_ref, lx_ref, gl_ref, gt_ref,
                    *, transposed_v):
    h = _rms(x_ref[...], g_ref[...]).astype(bf16)

    def proj(lo, width):
        return jnp.dot(h, w_ref[:, lo:lo + width], preferred_element_type=f32)

    q_ref[...] = (proj(C_Q, QK_COLS) * (HEAD_DIM ** -0.5 * LOG2E)).astype(bf16)
    k = proj(C_K, QK_COLS)
    k_ref[...] = k
    kb_ref[...] = k.astype(bf16)
    v = proj(C_V, V_COLS)
    v_ref[...] = v
    if transposed_v:
        for hd in range(N_HEADS):
            vb_ref[hd, :V_DIM, :] = v[:, hd * V_DIM:(hd + 1) * V_DIM].T.astype(bf16)
            vb_ref[hd, V_DIM:, :] = jnp.ones((ONES_ROWS, v.shape[0]), bf16)
    else:
        vb_ref[...] = v.astype(bf16)
    lx_ref[...] = proj(C_LX, LRU_WIDTH)
    gl_ref[...] = _gelu_tanh(proj(C_LY, LRU_WIDTH))
    gt_ref[...] = _sigmoid(proj(C_G, 2 * D_MODEL) + bm_ref[...])


def _in_proj(x, g_mix, w_in_b, b_merge, tm, tq=None):
    n = x.shape[0]
    row = lambda w: pl.BlockSpec((tm, w), lambda i: (i, 0))
    if tq is None:
        vb_shape, vb_spec = jax.ShapeDtypeStruct((n, V_COLS), bf16), row(V_COLS)
    else:
        per = tq // tm
        vb_shape = jax.ShapeDtypeStruct((N_HEADS, n // tq, V_DIM + ONES_ROWS, tq), bf16)
        vb_spec = pl.BlockSpec((N_HEADS, None, V_DIM + ONES_ROWS, tm), lambda i: (0, i // per, 0, i % per))
    outs = [
        jax.ShapeDtypeStruct((n, QK_COLS), bf16),
        jax.ShapeDtypeStruct((n, QK_COLS), f32),
        jax.ShapeDtypeStruct((n, QK_COLS), bf16),
        jax.ShapeDtypeStruct((n, V_COLS), f32),
        vb_shape,
        jax.ShapeDtypeStruct((n, LRU_WIDTH), f32),
        jax.ShapeDtypeStruct((n, LRU_WIDTH), f32),
        jax.ShapeDtypeStruct((n, 2 * D_MODEL), f32),
    ]
    return pl.pallas_call(
        functools.partial(_in_proj_kernel, transposed_v=tq is not None),
        grid=(n // tm,),
        in_specs=[row(D_MODEL), _const_spec((1, D_MODEL)), _const_spec((D_MODEL, IN_COLS)),
                  _const_spec((1, 2 * D_MODEL))],
        out_specs=[row(QK_COLS), row(QK_COLS), row(QK_COLS), row(V_COLS), vb_spec,
                   row(LRU_WIDTH), row(LRU_WIDTH), row(2 * D_MODEL)],
        out_shape=outs,
        compiler_params=_cparams(("parallel",)),
        name="in_proj",
    )(x, g_mix, w_in_b, b_merge)


def _rel_bucket(rel):
    half = N_BUCKETS // 2
    max_exact = half // 2
    ret = jnp.where(rel > 0, half, 0)
    n = jnp.abs(rel)
    nf = jnp.maximum(n, 1).astype(f32)
    large = max_exact + (jnp.log(nf / max_exact) / math.log(MAX_DISTANCE / max_exact)
                         * (half - max_exact)).astype(jnp.int32)
    large = jnp.minimum(large, half - 1)
    return ret + jnp.where(n < max_exact, n, large)


def _toeplitz(t, nq, nk):
    nh, length = t.shape
    a = jnp.broadcast_to(t[:, None, :], (nh, nq, length))
    a = jnp.pad(a, ((0, 0), (0, 0), (0, 1))).reshape(nh, nq * (length + 1))
    a = a[:, :nq * length].reshape(nh, nq, length)
    return a[:, :, nq - 1:nq - 1 + nk]


def _bias_table(rel_bias, qpos0, nq, kpos0, nk, shift):
    rel = jnp.arange(-(nq - 1), nk) + (kpos0 - qpos0)
    t = (rel_bias.astype(f32)[_rel_bucket(rel)].T - shift[:, None]) * LOG2E
    qpos = qpos0 + jnp.arange(nq)
    kpos = kpos0 + jnp.arange(nk)
    mask = (kpos[None, :] // CHUNK) <= (qpos[:, None] // CHUNK)
    return jnp.where(mask[None], _toeplitz(t, nq, nk), NEG_INF)


def _bias_ring(rel_bias, offset, tq, shift):
    r = jnp.arange(2 * tq)
    x = jnp.where(r < tq, r, r - 2 * tq)
    t = (rel_bias.astype(f32)[_rel_bucket(offset - x)].T - shift[:, None]) * LOG2E
    return t[:, None, :]


def _lam_value(lq1, lk1, lq2, lk2, lam_init):
    s1 = jnp.sum(lq1[...] * lk1[...], axis=-1, keepdims=True)
    s2 = jnp.sum(lq2[...] * lk2[...], axis=-1, keepdims=True)
    return jnp.exp(s1) - jnp.exp(s2) + lam_init


def _split_maps(q):
    lane = lax.broadcasted_iota(jnp.int32, q.shape, 1)
    zero = jnp.zeros_like(q)
    return jnp.where(lane < HEAD_DIM, q, zero), jnp.where(lane >= HEAD_DIM, q, zero)


def _qk(qz, kt):
    return lax.dot_general(qz, kt, (((1,), (1,)), ((), ())), preferred_element_type=f32)


def _finish_heads(o1, o2, lam, gs, lam_init):
    att = o1 - lam * o2
    return _rms(att, gs) * (1.0 - lam_init)


def _attn_prompt_kernel(q_ref, k_ref, vt_ref, ring_ref, lq1, lk1, lq2, lk2, gs_ref, o_ref,
                        nb_sc, s_sc, p_sc, al_sc, tm_sc, m_sc, acc_sc, *, tq, tk, lam_init):
    i = pl.program_id(1)
    ring_len = ring_ref.shape[-1]

    def bias_tile(ring):
        return pltpu.roll(jnp.broadcast_to(ring, (tk, ring_len)), 0, 1, stride=1, stride_axis=0)[:, :tq]

    @pl.when(i == 0)
    def _():
        kk = lax.broadcasted_iota(jnp.int32, (tk, tq), 0)
        qq = lax.broadcasted_iota(jnp.int32, (tk, tq), 1)
        nb_sc[0] = jnp.where((kk // CHUNK) <= (qq // CHUNK), bias_tile(ring_ref[1]), NEG_INF)
        nb_sc[1] = jnp.where(((kk + tk) // CHUNK) <= (qq // CHUNK), bias_tile(ring_ref[2]), NEG_INF)
        nb_sc[2] = jnp.full((tk, tq), NEG_INF, f32)

    @pl.when(i == 1)
    def _():
        nb_sc[2] = nb_sc[1]
        nb_sc[1] = nb_sc[0]
        nb_sc[0] = bias_tile(ring_ref[0])

    qt = q_ref[...].astype(f32).T
    row = lax.broadcasted_iota(jnp.int32, qt.shape, 0)
    qz = (jnp.where(row < HEAD_DIM, qt, 0.0).astype(bf16), jnp.where(row >= HEAD_DIM, qt, 0.0).astype(bf16))

    m_sc[...] = jnp.full(m_sc.shape, NEG_INF, f32)
    acc_sc[...] = jnp.zeros(acc_sc.shape, f32)

    def qk(slot, tile, bias, cols=slice(None)):
        kt = k_ref[pl.ds(pl.multiple_of(tile * tk, tk), tk), :]
        for mi in range(2):
            s = jnp.dot(kt, qz[mi][:, cols], preferred_element_type=f32)
            s = s if bias is None else s + bias[:, cols]
            s_sc[mi, slot, :, cols] = s
            tm_sc[mi, slot, :, cols] = jnp.max(s, axis=0, keepdims=True)

    def softmax(slot, cols=slice(None)):
        for mi in range(2):
            s = s_sc[mi, slot, :, cols]
            m_prev = m_sc[mi, :, cols]
            m_new = jnp.maximum(m_prev, tm_sc[mi, slot, :, cols])
            al_sc[mi, slot, :, cols] = jnp.exp2(m_prev - m_new)
            p_sc[mi, slot, :, cols] = jnp.exp2(s - m_new).astype(bf16)
            m_sc[mi, :, cols] = m_new

    def pv(slot, vtile, cols=slice(None)):
        va = vt_ref[vtile]
        for mi in range(2):
            acc_sc[mi, :, cols] = (acc_sc[mi, :, cols] * al_sc[mi, slot, :, cols]
                                   + jnp.dot(va, p_sc[mi, slot, :, cols], preferred_element_type=f32))

    late = slice(tq // 2, tq)

    per = tq // tk
    a0 = jnp.maximum(per * i - 1, 0)
    n_far = jnp.maximum(per * i - 1, 0)
    last_tile = vt_ref.shape[0] - 1

    def far_k(f):
        return jnp.clip(f, 0, jnp.maximum(n_far - 1, 0))

    qk(0, a0, nb_sc[0])
    qk(1, a0 + 1, nb_sc[1])
    softmax(0)
    qk(0, jnp.minimum(a0 + 2, last_tile), nb_sc[2], late)
    softmax(1)
    pv(0, a0)

    @pl.when(i == 0)
    def _():
        pv(1, a0 + 1)

    @pl.when(i > 0)
    def _():
        qk(1, 0, None)
        softmax(0, late)
        pv(1, a0 + 1)
        qk(0, far_k(1), None)
        softmax(1)
        pv(0, a0 + 2, late)

        def far_pair(p, carry):
            qk(1, 2 * p + 2, None)
            softmax(0)
            pv(1, 2 * p)
            qk(0, far_k(2 * p + 3), None)
            softmax(1)
            pv(0, 2 * p + 1)
            return carry

        lax.fori_loop(0, (n_far - 1) // 2, far_pair, 0)
        pv(1, n_far - 1)

    a1 = acc_sc[0]
    a2 = acc_sc[1]
    lam = _lam_value(lq1, lk1, lq2, lk2, lam_init)
    att = a1[:V_DIM] * (1.0 / a1[V_DIM:V_DIM + 1]) - a2[:V_DIM] * (lam / a2[V_DIM:V_DIM + 1])
    scale = lax.rsqrt(jnp.mean(att * att, axis=0, keepdims=True) + EPS) * (1.0 - lam_init)
    o_ref[...] = ((att * scale).T * gs_ref[...]).astype(o_ref.dtype)


def _attn_prompt(q, kb, vt, rel_bias, lam_vecs, g_subln, lam_init, tq, tk):
    t = q.shape[0]
    nq = t // tq
    assert tq == 2 * tk and t % tq == 0
    far = rel_bias.astype(f32)[_rel_bucket(jnp.asarray(-4 * MAX_DISTANCE))]
    rings = jnp.stack([_bias_ring(rel_bias, d, tq, far) for d in (-tk, 0, tk)], axis=1)
    vec = _const_spec((1, HEAD_DIM))
    return pl.pallas_call(
        functools.partial(_attn_prompt_kernel, tq=tq, tk=tk, lam_init=lam_init),
        grid=(N_HEADS, nq),
        in_specs=[
            pl.BlockSpec((tq, 2 * HEAD_DIM), lambda h, i: (i, h)),
            pl.BlockSpec((t, 2 * HEAD_DIM), lambda h, i: (0, h)),
            pl.BlockSpec((None, t // tk, V_DIM + ONES_ROWS, tk), lambda h, i: (h, 0, 0, 0)),
            pl.BlockSpec((None, 3, 1, 2 * tq), lambda h, i: (h, 0, 0, 0)),
            vec, vec, vec, vec, _const_spec((1, V_DIM)),
        ],
        out_specs=pl.BlockSpec((tq, V_DIM), lambda h, i: (i, h)),
        out_shape=jax.ShapeDtypeStruct((t, V_COLS), bf16),
        scratch_shapes=[pltpu.VMEM((3, tk, tq), f32), pltpu.VMEM((2, 2, tk, tq), f32),
                        pltpu.VMEM((2, 2, tk, tq), bf16), pltpu.VMEM((2, 2, 1, tq), f32),
                        pltpu.VMEM((2, 2, 1, tq), f32), pltpu.VMEM((2, 1, tq), f32),
                        pltpu.VMEM((2, V_DIM + ONES_ROWS, tq), f32)],
        compiler_params=_cparams(("arbitrary", "arbitrary")),
        name="attn_prompt",
    )(q, kb, vt, rings, *lam_vecs, g_subln)


def _attn_sample_kernel(q_ref, kp_ref, vp_ref, kn_ref, vn_ref, nbp_ref, nbn_ref, lq1, lk1, lq2, lk2, gs_ref,
                        o_ref, *, lam_init):
    lam = _lam_value(lq1, lk1, lq2, lk2, lam_init)
    for hd in range(N_HEADS):
        qk_cols = slice(hd * 2 * HEAD_DIM, (hd + 1) * 2 * HEAD_DIM)
        v_cols = slice(hd * V_DIM, (hd + 1) * V_DIM)
        qz = jnp.concatenate(_split_maps(q_ref[:, qk_cols]), axis=0)
        kp_t = kp_ref[qk_cols, :].astype(bf16)
        vp = vp_ref[:, hd, :].astype(bf16)
        kn = kn_ref[:, qk_cols]
        vn = vn_ref[:, v_cols]
        ts = q_ref.shape[0]
        sp = jnp.dot(qz, kp_t, preferred_element_type=f32) + jnp.concatenate([nbp_ref[hd], nbp_ref[hd]], axis=0)
        sn = _qk(qz, kn) + jnp.concatenate([nbn_ref[hd], nbn_ref[hd]], axis=0)
        m = jnp.maximum(jnp.max(sp, axis=1, keepdims=True), jnp.max(sn, axis=1, keepdims=True))
        pp = jnp.exp2(sp - m).astype(bf16)
        pn = jnp.exp2(sn - m).astype(bf16)
        den = jnp.sum(pp.astype(f32), axis=1, keepdims=True) + jnp.sum(pn.astype(f32), axis=1, keepdims=True)
        out = (jnp.dot(pp, vp, preferred_element_type=f32) + jnp.dot(pn, vn, preferred_element_type=f32)) / den
        o_ref[:, v_cols] = _finish_heads(out[:ts], out[ts:], lam, gs_ref[...], lam_init).astype(o_ref.dtype)


def _attn_sample(q, kb, vb, cache_k, cache_v, rel_bias, lam_vecs, g_subln, lam_init, nb_, ts):
    past = cache_k.shape[2]
    zero = jnp.zeros((N_HEADS,), f32)
    nbp = _bias_table(rel_bias, past, ts, 0, past, zero)
    nbn = _bias_table(rel_bias, past, ts, past, ts, zero)
    vec = _const_spec((1, HEAD_DIM))
    return pl.pallas_call(
        functools.partial(_attn_sample_kernel, lam_init=lam_init),
        grid=(nb_,),
        in_specs=[
            pl.BlockSpec((ts, QK_COLS), lambda b: (b, 0)),
            pl.BlockSpec((None, QK_COLS, past), lambda b: (b, 0, 0)),
            pl.BlockSpec((None, past, N_HEADS, V_DIM), lambda b: (b, 0, 0, 0)),
            pl.BlockSpec((ts, QK_COLS), lambda b: (b, 0)),
            pl.BlockSpec((ts, V_COLS), lambda b: (b, 0)),
            _const_spec((N_HEADS, ts, past)), _const_spec((N_HEADS, ts, ts)),
            vec, vec, vec, vec, _const_spec((1, V_DIM)),
        ],
        out_specs=pl.BlockSpec((ts, V_COLS), lambda b: (b, 0)),
        out_shape=jax.ShapeDtypeStruct((nb_ * ts, V_COLS), bf16),
        compiler_params=_cparams(("parallel",)),
        name="attn_sample",
    )(q, cache_k, cache_v, kb, vb, nbp, nbn, *lam_vecs, g_subln)


def _conv_lru_kernel(lx_ref, gl_ref, c0_ref, h0_ref, wc_ref, bc_ref, wa_ref, ba_ref, wx_ref, bx_ref, lam_ref,
                     hsg_ref, cout_ref, hout_ref, xbuf, a_sc, b_sc, hcar, *, tt, first_pos_is_zero):
    t = pl.program_id(1)

    @pl.when(t == 0)
    def _():
        xbuf[0:SUBLANES] = c0_ref[...]
        hcar[...] = jnp.broadcast_to(h0_ref[...], hcar.shape)

    xbuf[SUBLANES:SUBLANES + tt] = lx_ref[...]
    ngrp = tt // SUBLANES
    xg = xbuf[...].reshape(ngrp + 1, SUBLANES, LRU_WIDTH)
    sub = lax.broadcasted_iota(jnp.int32, (ngrp, SUBLANES, LRU_WIDTH), 1)
    xc = xg[1:] * wc_ref[CONV_W - 1:CONV_W, :] + bc_ref[...]
    for s in range(1, CONV_W):
        rolled = pltpu.roll(xg, s, axis=1)
        shifted = jnp.where(sub >= s, rolled[1:], rolled[:-1])
        xc = xc + shifted * wc_ref[CONV_W - 1 - s:CONV_W - s, :]
    xc = xc.reshape(tt, LRU_WIDTH)
    tail = xbuf[tt:tt + SUBLANES, :]
    xbuf[0:SUBLANES] = tail
    cout_ref[...] = tail

    xcb = xc.astype(bf16)

    def block_diag(w_ref, b_ref):
        cols = [jnp.dot(xcb[:, n * LRU_BLOCK_W:(n + 1) * LRU_BLOCK_W], w_ref[n], preferred_element_type=f32)
                for n in range(LRU_BLOCKS)]
        return jnp.concatenate(cols, axis=1) + b_ref[...]

    r = _sigmoid(block_diag(wa_ref, ba_ref))
    ig = _sigmoid(block_diag(wx_ref, bx_ref))
    z = -lam_ref[...]
    softplus = jnp.maximum(z, 0.0) + jnp.log1p(jnp.exp(-jnp.abs(z)))
    log_a = -LRU_C * r * softplus
    a = jnp.exp(log_a)
    th = jnp.tanh(log_a)
    m2 = -2.0 * th / (1.0 - th)
    mult = jnp.where(m2 > 0.0, m2 * lax.rsqrt(m2), 0.0)
    if first_pos_is_zero:
        row = lax.broadcasted_iota(jnp.int32, (tt, LRU_WIDTH), 0)
        mult = jnp.where((row == 0) & (t == 0), 1.0, mult)
    bv = mult * ig * xc

    groups = (tt // SUBLANES, SUBLANES, LRU_WIDTH)
    a = a.reshape(groups)
    bv = bv.reshape(groups)
    sub = lax.broadcasted_iota(jnp.int32, groups, 1)
    d = 1
    while d < SUBLANES:
        valid = sub >= d
        a_s = pltpu.roll(a, d, axis=1)
        b_s = pltpu.roll(bv, d, axis=1)
        bv = jnp.where(valid, a * b_s + bv, bv)
        a = jnp.where(valid, a * a_s, a)
        d *= 2
    a_sc[...] = a.reshape(tt, LRU_WIDTH)
    b_sc[...] = bv.reshape(tt, LRU_WIDTH)

    def group(g, h):
        sl = pl.ds(pl.multiple_of(g * SUBLANES, SUBLANES), SUBLANES)
        hg = a_sc[sl, :] * h + b_sc[sl, :]
        b_sc[sl, :] = hg
        return jnp.broadcast_to(hg[SUBLANES - 1:SUBLANES, :], hg.shape)

    h = lax.fori_loop(0, tt // SUBLANES, group, hcar[...], unroll=4)
    hcar[...] = h
    hout_ref[...] = h[0:1, :]
    hsg_ref[...] = (b_sc[...] * gl_ref[...]).astype(hsg_ref.dtype)


def _conv_lru(lx, gl, conv0, h0, w_conv, b_conv, wa, ba, wx, bx, lru_lambda, nb_, tlen, tt, first_pos_is_zero):
    nt = tlen // tt
    rows = pl.BlockSpec((tt, LRU_WIDTH), lambda b, t: (b * nt + t, 0))
    vecw = _const_spec((1, LRU_WIDTH))
    wblk = _const_spec((LRU_BLOCKS, LRU_BLOCK_W, LRU_BLOCK_W))
    return pl.pallas_call(
        functools.partial(_conv_lru_kernel, tt=tt, first_pos_is_zero=first_pos_is_zero),
        grid=(nb_, nt),
        in_specs=[rows, rows,
                  pl.BlockSpec((None, SUBLANES, LRU_WIDTH), lambda b, t: (b, 0, 0)),
                  pl.BlockSpec((None, 1, LRU_WIDTH), lambda b, t: (b, 0, 0)),
                  _const_spec((CONV_W, LRU_WIDTH)), vecw, wblk, vecw, wblk, vecw, vecw],
        out_specs=[rows,
                   pl.BlockSpec((None, SUBLANES, LRU_WIDTH), lambda b, t: (b, 0, 0)),
                   pl.BlockSpec((None, 1, LRU_WIDTH), lambda b, t: (b, 0, 0))],
        out_shape=[jax.ShapeDtypeStruct((nb_ * tlen, LRU_WIDTH), bf16),
                   jax.ShapeDtypeStruct((nb_, SUBLANES, LRU_WIDTH), f32),
                   jax.ShapeDtypeStruct((nb_, 1, LRU_WIDTH), f32)],
        scratch_shapes=[pltpu.VMEM((tt + SUBLANES, LRU_WIDTH), f32), pltpu.VMEM((tt, LRU_WIDTH), f32),
                        pltpu.VMEM((tt, LRU_WIDTH), f32), pltpu.VMEM((SUBLANES, LRU_WIDTH), f32)],
        compiler_params=_cparams(("arbitrary", "arbitrary")),
        name="conv_lru",
    )(lx, gl, conv0, h0, w_conv, b_conv, wa, ba, wx, bx, lru_lambda)


def _merge_kernel(x_ref, att_ref, hsg_ref, gt_ref, wa_ref, wb_ref, wo_ref, gf_ref, wr_ref, br_ref,
                  x1_ref, h2_ref, comb_ref, *, parts):
    rows_per = x_ref.shape[0] // parts
    for part in range(parts):
        rows = slice(part * rows_per, (part + 1) * rows_per)
        bra = jnp.dot(att_ref[rows, :], wa_ref[...], preferred_element_type=f32)
        brb = jnp.dot(hsg_ref[rows, :], wb_ref[...], preferred_element_type=f32)
        gt = gt_ref[rows, :]
        m = (gt[:, :D_MODEL] * bra + gt[:, D_MODEL:] * brb).astype(bf16)
        x1 = x_ref[rows, :] + jnp.dot(m, wo_ref[...], preferred_element_type=f32)
        x1_ref[rows, :] = x1
        h2 = _rms(x1, gf_ref[...])
        h2_hi = h2.astype(bf16)
        h2_ref[rows, :] = h2_hi

        h2_lo = (h2 - h2_hi.astype(f32)).astype(bf16)
        prods = (jnp.dot(h2_hi, wr_ref[...], preferred_element_type=f32)
                 + jnp.dot(h2_lo, wr_ref[...], preferred_element_type=f32))
        logits = prods[:, :LANES] + prods[:, LANES:] + br_ref[...]
        lane = lax.broadcasted_iota(jnp.int32, logits.shape, 1)
        lanef = lane.astype(f32)
        low = jnp.float32(-3.0e38)
        is_grp = (lane >= N_EXPERTS) & (lane < N_EXPERTS + N_GROUPS)
        gl = jnp.where(is_grp, logits, low)
        gmax = jnp.max(gl, axis=1, keepdims=True)
        gidx = jnp.min(jnp.where(is_grp & (gl == gmax), lanef, 1.0e3), axis=1, keepdims=True) - N_EXPERTS
        gden = jnp.sum(jnp.where(is_grp, jnp.exp(gl - gmax), 0.0), axis=1, keepdims=True)
        g_w = 1.0 / gden
        lo = gidx * EXPERTS_PER_GROUP
        in_sel = (lanef >= lo) & (lanef < lo + EXPERTS_PER_GROUP)
        sel = jnp.where(in_sel, logits, low)
        v1 = jnp.max(sel, axis=1, keepdims=True)
        i1 = jnp.min(jnp.where(in_sel & (sel == v1), lanef, 1.0e3), axis=1, keepdims=True)
        in_sel2 = in_sel & (lanef != i1)
        sel2 = jnp.where(in_sel2, logits, low)
        v2 = jnp.max(sel2, axis=1, keepdims=True)
        i2 = jnp.min(jnp.where(in_sel2 & (sel2 == v2), lanef, 1.0e3), axis=1, keepdims=True)
        e2 = jnp.exp(v2 - v1)
        w1 = g_w / (1.0 + e2)
        w2 = g_w * e2 / (1.0 + e2)
        comb_ref[rows, :] = jnp.where(lanef == i1, w1, 0.0) + jnp.where(lanef == i2, w2, 0.0)


def _merge(x, att, hsg, gt, wa, wb, wo, g_ffn, w_router, b_router, tm):
    n = x.shape[0]
    row = lambda w: pl.BlockSpec((tm, w), lambda i: (i, 0))
    sq = _const_spec((D_MODEL, D_MODEL))
    return pl.pallas_call(
        functools.partial(_merge_kernel, parts=2 if tm % 512 == 0 else 1),
        grid=(n // tm,),
        in_specs=[row(D_MODEL), row(V_COLS), row(LRU_WIDTH), row(2 * D_MODEL), sq, sq, sq,
                  _const_spec((1, D_MODEL)), _const_spec((D_MODEL, 2 * LANES)), _const_spec((1, LANES))],
        out_specs=[row(D_MODEL), row(D_MODEL), row(LANES)],
        out_shape=[jax.ShapeDtypeStruct((n, D_MODEL), f32), jax.ShapeDtypeStruct((n, D_MODEL), bf16),
                   jax.ShapeDtypeStruct((n, LANES), f32)],
        compiler_params=_cparams(("parallel",)),
        name="merge_router",
    )(x, att, hsg, gt, wa, wb, wo, g_ffn, w_router, b_router)


EXPERTS_PER_STEP = 8


MOE_CHUNK = 128


def _moe_kernel(h_ref, comb_ref, wg_ref, wu_ref, wd_ref, y_ref, pt_sc, x_sc, c_sc, ys_sc, seg_sc):
    e = pl.program_id(1)
    tm = h_ref.shape[0]
    lane1 = lax.broadcasted_iota(jnp.int32, (1, LANES), 1)

    @pl.when(e == 0)
    def _():
        comb = comb_ref[...]
        ex = lax.broadcasted_iota(jnp.int32, (LANES, LANES), 0)
        gr = lax.broadcasted_iota(jnp.int32, (LANES, LANES), 1)
        in_group = jnp.where((ex // EXPERTS_PER_GROUP == gr) & (ex < N_EXPERTS), 1.0, 0.0).astype(bf16)
        gsum = jnp.dot(comb.astype(bf16), in_group, preferred_element_type=f32)
        member = jnp.where(gsum > 0.0, 1.0, 0.0)
        r_i = lax.broadcasted_iota(jnp.int32, (tm, tm), 0)
        c_i = lax.broadcasted_iota(jnp.int32, (tm, tm), 1)
        tri = jnp.where(gr <= ex, 1.0, 0.0).astype(bf16)
        member_b = member.astype(bf16)
        totals = jnp.zeros((1, LANES), f32)
        blocks = []
        for blk in range(tm // LANES):
            part = jnp.dot(tri, member_b[blk * LANES:(blk + 1) * LANES], preferred_element_type=f32) + totals
            blocks.append(part)
            totals = part[LANES - 1:LANES, :]
        rank = jnp.concatenate(blocks, axis=0)
        offs = jnp.zeros((1, LANES), f32)
        for g in range(1, N_GROUPS):
            prev = jnp.sum(jnp.where(lane1 == g - 1, totals, 0.0), axis=1, keepdims=True)
            offs = offs + jnp.where(lane1 >= g, prev, 0.0)
        seg_sc[0:1, :] = offs.astype(jnp.int32)
        seg_sc[1:2, :] = totals.astype(jnp.int32)
        pos = jnp.sum(member * (offs + rank - 1.0), axis=1, keepdims=True)
        pos_row = jnp.broadcast_to(pos, (tm, LANES)).T[0:1, :].astype(jnp.int32)
        perm = jnp.where(r_i == pos_row, 1.0, 0.0).astype(bf16)
        pt_sc[...] = jnp.where(c_i == pos.astype(jnp.int32), 1.0, 0.0).astype(bf16)
        comb_hi = comb.astype(bf16)
        comb_lo = (comb - comb_hi.astype(f32)).astype(bf16)
        srt = jnp.dot(perm, jnp.concatenate([h_ref[...], comb_hi, comb_lo], axis=1), preferred_element_type=f32)
        x_sc[...] = srt[:, :D_MODEL].astype(bf16)
        c_sc[...] = srt[:, D_MODEL:D_MODEL + LANES] + srt[:, D_MODEL + LANES:]
        ys_sc[...] = jnp.zeros(ys_sc.shape, f32)

    g = (e * EXPERTS_PER_STEP) // EXPERTS_PER_GROUP
    off = jnp.sum(jnp.where(lane1 == g, seg_sc[0:1, :], 0))
    cnt = jnp.sum(jnp.where(lane1 == g, seg_sc[1:2, :], 0))
    lane = lax.broadcasted_iota(jnp.int32, (MOE_CHUNK, LANES), 1)

    def chunk(k, carry):
        rows = pl.ds(pl.multiple_of(k * MOE_CHUNK, MOE_CHUNK), MOE_CHUNK)
        x = x_sc[rows, :]
        cw = c_sc[rows, :]
        hidden = []
        for j in range(EXPERTS_PER_STEP):
            gate = jnp.dot(x, wg_ref[j], preferred_element_type=f32)
            up = jnp.dot(x, wu_ref[j], preferred_element_type=f32)
            c = jnp.sum(jnp.where(lane == e * EXPERTS_PER_STEP + j, cw, 0.0), axis=1, keepdims=True)
            hidden.append((gate * _sigmoid(gate) * up * c).astype(bf16))
        ys_sc[rows, :] += jnp.dot(jnp.concatenate(hidden, axis=1), wd_ref[...], preferred_element_type=f32)
        return carry

    lax.fori_loop(off // MOE_CHUNK, (off + cnt + MOE_CHUNK - 1) // MOE_CHUNK, chunk, 0)

    @pl.when(e == pl.num_programs(1) - 1)
    def _():
        y = jnp.dot(pt_sc[...], ys_sc[...].astype(bf16), preferred_element_type=f32)
        y_ref[...] = y.astype(y_ref.dtype)


def _moe(h2, comb, wg, wu, wd_rows, tm):
    n = h2.shape[0]
    per = EXPERTS_PER_STEP
    assert EXPERTS_PER_GROUP % per == 0 and tm % MOE_CHUNK == 0
    return pl.pallas_call(
        _moe_kernel,
        grid=(n // tm, N_EXPERTS // per),
        in_specs=[pl.BlockSpec((tm, D_MODEL), lambda i, e: (i, 0)),
                  pl.BlockSpec((tm, LANES), lambda i, e: (i, 0)),
                  pl.BlockSpec((per, D_MODEL, D_EXPERT), lambda i, e: (e, 0, 0)),
                  pl.BlockSpec((per, D_MODEL, D_EXPERT), lambda i, e: (e, 0, 0)),
                  pl.BlockSpec((per * D_EXPERT, D_MODEL), lambda i, e: (e, 0))],
        out_specs=pl.BlockSpec((tm, D_MODEL), lambda i, e: (i, 0)),
        out_shape=jax.ShapeDtypeStruct((n, D_MODEL), bf16),
        scratch_shapes=[pltpu.VMEM((tm, tm), bf16), pltpu.VMEM((tm, D_MODEL), bf16),
                        pltpu.VMEM((tm, LANES), f32), pltpu.VMEM((tm, D_MODEL), f32),
                        pltpu.VMEM((SUBLANES, LANES), jnp.int32)],
        compiler_params=_cparams(("arbitrary", "arbitrary")),
        name="moe",
    )(h2, comb, wg, wu, wd_rows)


def _ple_kernel(x1_ref, y_ref, p_ref, gp_ref, wpg_ref, bpg_ref, wpi_ref, gfin_ref, o_ref):
    x2 = x1_ref[...] + y_ref[...]
    hp = _rms(x2, gp_ref[...]).astype(bf16)
    pg = _sigmoid(jnp.dot(hp, wpg_ref[...], preferred_element_type=f32) + bpg_ref[...])
    pin = jnp.dot(p_ref[...].astype(bf16), wpi_ref[...], preferred_element_type=f32)
    x3 = x2 + pg * pin
    o_ref[...] = _rms(x3, gfin_ref[...])


def _ple(x1, y, p, g_ple, wpg, bpg, wpi, g_final, tm):
    n = x1.shape[0]
    row = lambda w: pl.BlockSpec((tm, w), lambda i: (i, 0))
    vec = _const_spec((1, D_MODEL))
    return pl.pallas_call(
        _ple_kernel,
        grid=(n // tm,),
        in_specs=[row(D_MODEL), row(D_MODEL), row(PLE_DIM), vec, _const_spec((D_MODEL, D_MODEL)), vec,
                  _const_spec((PLE_DIM, D_MODEL)), vec],
        out_specs=row(D_MODEL),
        out_shape=jax.ShapeDtypeStruct((n, D_MODEL), f32),
        compiler_params=_cparams(("parallel",)),
        name="ple_final",
    )(x1, y, p, g_ple, wpg, bpg, wpi, g_final)


PROJ_ROWS = 256
ATTN_KEY_TILE = 512
MERGE_ROWS = 512
MOE_ROWS = 1024


def _tile(n, pref):
    return pref if n % pref == 0 else n


def _stream(x, p, conv0, h0, cache, li, rel_bias, w):
    nb_, tlen, _ = x.shape
    n = nb_ * tlen
    xf = x.reshape(n, D_MODEL)
    lam_init = 0.8 - 0.6 * math.exp(-0.3 * li)
    tk = None if cache is not None else ATTN_KEY_TILE
    q, k, kb, v, vb, lx, gl, gt = _in_proj(xf, w["g_mix"], w["w_in"], w["b_merge"], _tile(n, PROJ_ROWS), tk)
    if cache is None:
        att = _attn_prompt(q, kb, vb, rel_bias, w["lam_vecs"], w["g_subln"], lam_init, 2 * tk, tk)
    else:
        att = _attn_sample(q, kb, vb, cache[0], cache[1], rel_bias, w["lam_vecs"], w["g_subln"], lam_init,
                           nb_, tlen)
    conv_pad = jnp.pad(conv0, ((0, 0), (SUBLANES - (CONV_W - 1), 0), (0, 0)))
    hsg, cout, hout = _conv_lru(lx, gl, conv_pad, h0[:, None, :], w["w_conv"], w["b_conv"], w["w_rg_a"],
                                w["b_rg_a"], w["w_rg_x"], w["b_rg_x"], w["lru_lambda"], nb_, tlen,
                                _tile(tlen, PROJ_ROWS), cache is None)
    tm = _tile(n, MERGE_ROWS)
    x1, h2, comb = _merge(xf, att, hsg, gt, w["w_attn_br"], w["w_lru_br"], w["w_out"], w["g_ffn"],
                          w["w_router"], w["b_router"], tm)
    y = _moe(h2, comb, w["w_e_gate"], w["w_e_up"], w["w_e_down"], _tile(n, MOE_ROWS))
    out = _ple(x1, y, p.reshape(n, PLE_DIM), w["g_ple"], w["w_ple_gate"], w["b_ple_gate"], w["w_ple_in"],
               w["g_final"], tm)
    return out, k, v, cout[:, SUBLANES - (CONV_W - 1):, :], hout[:, 0, :]


def kernel(x_prompt, x_sample, cache_k, cache_v, state_conv, state_lru, p_prompt, p_sample, rel_bias, g_mix, w_in, b_merge, lam_q1, lam_k1, lam_q2, lam_k2, g_subln, w_attn_br, w_conv, b_conv, w_rg_a, b_rg_a, w_rg_x, b_rg_x, lru_lambda, w_lru_br, w_out, g_ffn, w_grp, b_grp, w_rt, b_rt, w_e_gate, w_e_up, w_e_down, g_ple, w_ple_gate, b_ple_gate, w_ple_in, g_final):
    depth = w_in.shape[0]
    assert depth == 1, "the final norm is fused into the single layer's last stage"
    bp, tp, _ = x_prompt.shape
    bs, ts, _ = x_sample.shape
    past = cache_k.shape[2]
    li = 0
    row = lambda a: a[li].reshape(1, -1).astype(f32)
    w_router = jnp.concatenate(
        [jnp.transpose(w_rt[li], (1, 0, 2)).reshape(D_MODEL, N_EXPERTS), w_grp[li]], axis=1).astype(f32)
    w_router = jnp.pad(w_router, ((0, 0), (0, LANES - N_EXPERTS - N_GROUPS)))
    w_router_hi = w_router.astype(bf16)
    w_router = jnp.concatenate([w_router_hi, (w_router - w_router_hi.astype(f32)).astype(bf16)], axis=1)
    b_router = jnp.pad(jnp.concatenate([b_rt[li].reshape(-1), b_grp[li]]).astype(f32),
                       (0, LANES - N_EXPERTS - N_GROUPS)).reshape(1, LANES)
    w = dict(
        g_mix=row(g_mix), w_in=w_in[li].astype(bf16), b_merge=row(b_merge),
        lam_vecs=(row(lam_q1), row(lam_k1), row(lam_q2), row(lam_k2)), g_subln=row(g_subln),
        w_attn_br=w_attn_br[li].astype(bf16), w_conv=w_conv[li].astype(f32), b_conv=row(b_conv),
        w_rg_a=w_rg_a[li].astype(bf16), b_rg_a=row(b_rg_a), w_rg_x=w_rg_x[li].astype(bf16), b_rg_x=row(b_rg_x),
        lru_lambda=row(lru_lambda), w_lru_br=w_lru_br[li].astype(bf16), w_out=w_out[li].astype(bf16),
        g_ffn=row(g_ffn), w_router=w_router, b_router=b_router,
        w_e_gate=w_e_gate[li].astype(bf16), w_e_up=w_e_up[li].astype(bf16), w_e_down=w_e_down[li].reshape(N_EXPERTS * D_EXPERT, D_MODEL).astype(bf16),
        g_ple=row(g_ple), w_ple_gate=w_ple_gate[li].astype(bf16), b_ple_gate=row(b_ple_gate),
        w_ple_in=w_ple_in[li].astype(bf16), g_final=g_final.reshape(1, -1).astype(f32),
    )
    conv0 = jnp.zeros((bp, CONV_W - 1, LRU_WIDTH), f32)
    h0 = jnp.zeros((bp, LRU_WIDTH), f32)
    yp, kp, vp, cp, hp = _stream(x_prompt, p_prompt[li], conv0, h0, None, li, rel_bias, w)
    cache = (jnp.transpose(cache_k[li], (0, 2, 3, 4, 1)).reshape(bs, QK_COLS, past), cache_v[li])
    ys, ks, vs, cs, hs = _stream(x_sample, p_sample[li], state_conv[li], state_lru[li], cache, li, rel_bias, w)
    return (yp.reshape(bp, tp, D_MODEL), ys.reshape(bs, ts, D_MODEL),
            kp.reshape(1, bp, tp, N_HEADS, 2, HEAD_DIM), vp.reshape(1, bp, tp, N_HEADS, V_DIM),
            cp[None], hp[None],
            ks.reshape(1, bs, ts, N_HEADS, 2, HEAD_DIM), vs.reshape(1, bs, ts, N_HEADS, V_DIM),
            cs[None], hs[None])
```
